```python
import math
import jax, jax.numpy as jnp
from jax import lax
import numpy as np

D_MODEL = 1024
BATCH = 8
SEQ = 2048
DEPTH = 2

CTX_LEN = 256
GRID_W = 64
EPS = 1e-6
F32 = jnp.float32

HY_WIDTH = D_MODEL // 2
HY_ORDER = 2
HY_IN = (HY_ORDER + 1) * HY_WIDTH
HY_SHORT = 3
HY_BANDS = 16
HY_EMB = 2 * HY_BANDS + 1
HY_HIDDEN = 64
HY_DECAY_SLOW = -math.log(1e-2) / 1.5
HY_DECAY_FAST = -math.log(1e-2) / 0.3
HEAD_DIM = 64
N_Q_HEADS = (D_MODEL // 2) // HEAD_DIM
N_KV_HEADS = N_Q_HEADS // 4
Q_PER_KV = N_Q_HEADS // N_KV_HEADS
ATTN_WIDTH = N_Q_HEADS * HEAD_DIM
KV_WIDTH = N_KV_HEADS * HEAD_DIM
IN_EVEN = HY_IN + ATTN_WIDTH + 2 * KV_WIDTH
MIX_EVEN = HY_WIDTH + ATTN_WIDTH
ROPE_THETA = 10000.0
Q_BLOCK = 128
D_FF = 256 * ((8 * D_MODEL // 3 + 255) // 256)

S5_WIDTH = D_MODEL
S5_GROUP = 16
S5_GROUPS = S5_WIDTH // S5_GROUP
S5_STATE = 64
S5_DT_MIN = 1e-3
S5_DT_MAX = 1e-1
N_EXPERTS = 8
TOP_K = 2
D_FF_EXPERT = 7 * D_MODEL // 2

N_EVEN = (DEPTH + 1) // 2
N_ODD = DEPTH // 2

kernel_name = 'hybrid_hyena_gqa_s5_moe_dit'


def rms_norm(x, gain):
    xf = x.astype(F32)
    xf = xf * lax.rsqrt(jnp.mean(xf * xf, axis=-1, keepdims=True) + EPS)
    return (xf * gain.astype(F32)).astype(x.dtype)


def ada_params(cond, w, b):
    m = jax.nn.silu(cond) @ w + b
    m = m.reshape(m.shape[:-1] + (1, 6, D_MODEL))
    return tuple(m[..., k, :] for k in range(6))


def modulate(h, shift, scale):
    return h * (1.0 + scale) + shift


def centred_short_conv(u, w, b):
    L = u.shape[1]
    pad = HY_SHORT // 2
    up = jnp.pad(u, ((0, 0), (pad, pad), (0, 0)))
    y = b
    for k in range(HY_SHORT):
        y = y + up[:, k:k + L] * w[k]
    return y


def hyena_filters(L, w1, b1, w2, b2, wout, freq):
    t = jnp.linspace(0.0, 1.0, L, dtype=F32)[:, None]
    bands = jnp.linspace(1e-4, HY_BANDS - 1, HY_BANDS, dtype=F32)
    phase = (2.0 * math.pi / L) * jnp.arange(L, dtype=F32)[:, None] * bands
    z = jnp.concatenate([t, jnp.cos(phase), -jnp.sin(phase)], axis=-1)
    fr = freq.astype(F32)
    h = jnp.sin(fr * (z @ w1.astype(F32) + b1.astype(F32)))
    h = jnp.sin(fr * (h @ w2.astype(F32) + b2.astype(F32)))
    h = (h @ wout.astype(F32)).reshape(L, 2, HY_ORDER, HY_WIDTH)
    deltas = jnp.linspace(HY_DECAY_SLOW, HY_DECAY_FAST, HY_WIDTH, dtype=F32)
    decay = jnp.exp(-t * deltas)
    return h * decay[:, None, None, :]


def two_sided_fftconv(u, h2, skip):
    L, C = u.shape[1], u.shape[2]
    k = jnp.concatenate([h2[:, 0], jnp.zeros((1, C), F32), h2[:0:-1, 1]], axis=0)
    kf = jnp.fft.rfft(k, axis=0)
    uf = jnp.fft.rfft(u.astype(F32), n=2 * L, axis=1)
    y = jnp.fft.irfft(uf * kf, n=2 * L, axis=1)[:, :L]
    return (y + u.astype(F32) * skip.astype(F32)).astype(u.dtype)


def hyena_mixer(p, filt, conv_w, conv_b, skip):
    z = centred_short_conv(p, conv_w, conv_b)
    parts = jnp.split(z, HY_ORDER + 1, axis=-1)
    v = parts[0]
    for o in range(HY_ORDER):
        v = parts[o + 1] * two_sided_fftconv(v, filt[:, :, o], skip[o])
    return v


def axial_rope(L):
    rows = L // GRID_W
    row = jnp.repeat(jnp.arange(rows, dtype=F32), GRID_W)
    col = jnp.tile(jnp.arange(GRID_W, dtype=F32), rows)
    n_freq = HEAD_DIM // 4
    inv = ROPE_THETA ** (-jnp.arange(n_freq, dtype=F32) / n_freq)
    ang = jnp.concatenate([row[:, None] * inv, col[:, None] * inv], axis=-1)
    return jnp.cos(ang), jnp.sin(ang)


def apply_rope(x, cos, sin):
    xf = x.astype(F32)
    x1, x2 = jnp.split(xf, 2, axis=-1)
    c = cos[None, :, None, :]
    s = sin[None, :, None, :]
    return jnp.concatenate([x1 * c - x2 * s, x1 * s + x2 * c], axis=-1).astype(x.dtype)


def gqa_core(q, k, v):
    s = jnp.einsum('bqhgd,bkhd->bhgqk', q, k).astype(F32) * (HEAD_DIM ** -0.5)
    p = jax.nn.softmax(s, axis=-1).astype(v.dtype)
    return jnp.einsum('bhgqk,bkhd->bqhgd', p, v)


def gqa_context(q, k, v):
    B, L = q.shape[:2]
    o = gqa_core(q.reshape(B, L, N_KV_HEADS, Q_PER_KV, HEAD_DIM), k, v)
    return o.reshape(B, L, ATTN_WIDTH)


def gqa_latent_blocked(q, k, v):
    B, L = q.shape[:2]
    nb = L // Q_BLOCK
    qb = q.reshape(B, nb, Q_BLOCK, N_KV_HEADS, Q_PER_KV, HEAD_DIM).transpose(1, 0, 2, 3, 4, 5)
    o = lax.map(lambda qi: gqa_core(qi, k, v), qb)
    return o.transpose(1, 0, 2, 3, 4, 5).reshape(B, L, ATTN_WIDTH)


def even_mixer(h_l, h_c, w_in, conv_w, conv_b, f_w1, f_b1, f_w2, f_b2, f_wout, f_freq, hy_skip,
               q_norm, k_norm, w_out, need_ctx):
    def project(h):
        B, L = h.shape[:2]
        p = h @ w_in
        hy = p[..., :HY_IN]
        q = p[..., HY_IN:HY_IN + ATTN_WIDTH].reshape(B, L, N_Q_HEADS, HEAD_DIM)
        k = p[..., HY_IN + ATTN_WIDTH:HY_IN + ATTN_WIDTH + KV_WIDTH].reshape(B, L, N_KV_HEADS, HEAD_DIM)
        v = p[..., HY_IN + ATTN_WIDTH + KV_WIDTH:].reshape(B, L, N_KV_HEADS, HEAD_DIM)
        return hy, rms_norm(q, q_norm), rms_norm(k, k_norm), v

    filt_args = (f_w1, f_b1, f_w2, f_b2, f_wout, f_freq)
    hy_c, q_c, k_c, v_c = project(h_c)
    hy_l, q_l, k_l, v_l = project(h_l)
    L = h_l.shape[1]
    cos, sin = axial_rope(L)
    q_l = apply_rope(q_l, cos, sin)
    k_l = apply_rope(k_l, cos, sin)
    y_hy_l = hyena_mixer(hy_l, hyena_filters(L, *filt_args), conv_w, conv_b, hy_skip)
    y_at_l = gqa_latent_blocked(q_l, jnp.concatenate([k_c, k_l], axis=1),
                                jnp.concatenate([v_c, v_l], axis=1))
    out_l = jnp.concatenate([y_hy_l, y_at_l], axis=-1) @ w_out
    out_c = None
    if need_ctx:
        y_hy_c = hyena_mixer(hy_c, hyena_filters(h_c.shape[1], *filt_args), conv_w, conv_b, hy_skip)
        y_at_c = gqa_context(q_c, k_c, v_c)
        out_c = jnp.concatenate([y_hy_c, y_at_c], axis=-1) @ w_out
    return out_l, out_c


def s5_discretise(lam_re, lam_im, log_step, b_re, b_im):
    lr = jnp.minimum(lam_re.astype(F32), -1e-4)
    li = lam_im.astype(F32)
    dt = jnp.exp(log_step.astype(F32))[:, None]
    mag = jnp.exp(lr * dt)
    ab_re = mag * jnp.cos(li * dt)
    ab_im = mag * jnp.sin(li * dt)
    den = lr * lr + li * li
    nr, ni = ab_re - 1.0, ab_im
    co_re = (nr * lr + ni * li) / den
    co_im = (ni * lr - nr * li) / den
    br, bi = b_re.astype(F32), b_im.astype(F32)
    bb_re = co_re[..., None] * br - co_im[..., None] * bi
    bb_im = co_re[..., None] * bi + co_im[..., None] * br
    return ab_re, ab_im, bb_re, bb_im


def s5_scan(bu_re, bu_im, ab_re, ab_im, s0_re, s0_im):
    bu_re = bu_re.at[:, 0].add(ab_re * s0_re - ab_im * s0_im)
    bu_im = bu_im.at[:, 0].add(ab_re * s0_im + ab_im * s0_re)
    L = bu_re.shape[1]
    a_re = jnp.broadcast_to(ab_re, (1, L) + ab_re.shape)
    a_im = jnp.broadcast_to(ab_im, (1, L) + ab_im.shape)

    def combine(e1, e2):
        a1r, a1i, b1r, b1i = e1
        a2r, a2i, b2r, b2i = e2
        return (a2r * a1r - a2i * a1i, a2r * a1i + a2i * a1r,
                a2r * b1r - a2i * b1i + b2r, a2r * b1i + a2i * b1r + b2i)

    _, _, s_re, s_im = lax.associative_scan(combine, (a_re, a_im, bu_re, bu_im), axis=1)
    return s_re, s_im


def s5_direction(u_c, u_l, lam_re, lam_im, log_step, b_re, b_im, c_re, c_im, need_ctx):
    ab_re, ab_im, bb_re, bb_im = s5_discretise(lam_re, lam_im, log_step, b_re, b_im)
    cr, ci = c_re.astype(F32), c_im.astype(F32)

    def drive(u):
        return (jnp.einsum('blgk,gnk->blgn', u, bb_re), jnp.einsum('blgk,gnk->blgn', u, bb_im))

    def readout(s_re, s_im):
        return jnp.einsum('blgn,gkn->blgk', s_re, cr) - jnp.einsum('blgn,gkn->blgk', s_im, ci)

    zero = jnp.zeros(u_c.shape[:1] + ab_re.shape, F32)
    sc_re, sc_im = s5_scan(*drive(u_c), ab_re, ab_im, zero, zero)
    sl_re, sl_im = s5_scan(*drive(u_l), ab_re, ab_im, sc_re[:, -1], sc_im[:, -1])
    y_l = readout(sl_re, sl_im)
    y_c = readout(sc_re, sc_im) if need_ctx else None
    return y_l, y_c


def s5_glu(y, w_a, w_b, dtype):
    z = jax.nn.gelu(y).astype(dtype)
    return (z @ w_a) * jax.nn.sigmoid(z @ w_b)


def odd_mixer(h_l, h_c, w_in, lam_re, lam_im, log_step, b_re, b_im, c_re, c_im, d_skip, w_a, w_b, need_ctx):
    B, L = h_l.shape[:2]
    Lc = h_c.shape[1]
    u_l = (h_l @ w_in).astype(F32)
    u_c = (h_c @ w_in).astype(F32)
    gl = u_l.reshape(B, L, S5_GROUPS, S5_GROUP)
    gc = u_c.reshape(B, Lc, S5_GROUPS, S5_GROUP)
    yf_l, yf_c = s5_direction(gc, gl, lam_re[0], lam_im[0], log_step[0], b_re[0], b_im[0],
                              c_re[0], c_im[0], need_ctx)
    yb_l, yb_c = s5_direction(gc[:, ::-1], gl[:, ::-1], lam_re[1], lam_im[1], log_step[1], b_re[1], b_im[1],
                              c_re[1], c_im[1], need_ctx)
    d = d_skip.astype(F32)
    y_l = (yf_l + yb_l[:, ::-1]).reshape(B, L, S5_WIDTH) + d * u_l
    out_l = s5_glu(y_l, w_a, w_b, h_l.dtype)
    out_c = None
    if need_ctx:
        y_c = (yf_c + yb_c[:, ::-1]).reshape(B, Lc, S5_WIDTH) + d * u_c
        out_c = s5_glu(y_c, w_a, w_b, h_c.dtype)
    return out_l, out_c


def dense_swiglu(h, w_gate, w_up, w_down):
    return (jax.nn.silu(h @ w_gate) * (h @ w_up)) @ w_down


def moe_swiglu(h, router, w_gate, w_up, w_down):
    shp = h.shape
    t = h.reshape(-1, D_MODEL)
    logits = (t @ router).astype(F32)
    top_val, top_idx = lax.top_k(logits, TOP_K)
    top_w = jax.nn.softmax(top_val, axis=-1)
    gates = jnp.sum(jax.nn.one_hot(top_idx, N_EXPERTS, dtype=F32) * top_w[..., None], axis=1)
    out = jnp.zeros_like(t)
    for e in range(N_EXPERTS):
        he = jax.nn.silu(t @ w_gate[e]) * (t @ w_up[e])
        out = out + gates[:, e:e + 1].astype(t.dtype) * (he @ w_down[e])
    return out.reshape(shp)


def setup_inputs(seed: int = 0) -> dict:
    key = jax.random.key(seed)
    ks = iter(jax.random.split(key, 64))

    def nrm(shape, scale):
        return jax.random.normal(next(ks), shape, F32) * scale

    def gain(shape):
        return 1.0 + nrm(shape, 0.01)

    g, n = S5_GROUPS, S5_STATE
    n_idx = jnp.arange(S5_STATE, dtype=F32)
    return {
        'x': nrm((BATCH, SEQ, D_MODEL), 1.0),
        'c': nrm((BATCH, D_MODEL), 1.0),
        'ctx': nrm((BATCH, CTX_LEN, D_MODEL), 1.0),
        'c_ctx': nrm((D_MODEL,), 1.0),
        'ada_w': nrm((DEPTH, D_MODEL, 6 * D_MODEL), 0.5 * D_MODEL ** -0.5),
        'ada_b': nrm((DEPTH, 6 * D_MODEL), 0.01),
        'norm_mix_pre': gain((DEPTH, D_MODEL)),
        'norm_mix_post': gain((DEPTH, D_MODEL)),
        'norm_ffn_pre': gain((DEPTH, D_MODEL)),
        'norm_ffn_post': gain((DEPTH, D_MODEL)),
        'ev_w_in': nrm((N_EVEN, D_MODEL, IN_EVEN), D_MODEL ** -0.5),
        'ev_hy_conv_w': nrm((N_EVEN, HY_SHORT, HY_IN), HY_SHORT ** -0.5),
        'ev_hy_conv_b': nrm((N_EVEN, HY_IN), 0.01),
        'ev_hy_f_w1': nrm((N_EVEN, HY_EMB, HY_HIDDEN), HY_EMB ** -0.5),
        'ev_hy_f_b1': nrm((N_EVEN, HY_HIDDEN), 0.01),
        'ev_hy_f_w2': nrm((N_EVEN, HY_HIDDEN, HY_HIDDEN), HY_HIDDEN ** -0.5),
        'ev_hy_f_b2': nrm((N_EVEN, HY_HIDDEN), 0.01),
        'ev_hy_f_wout': nrm((N_EVEN, HY_HIDDEN, 2 * HY_ORDER * HY_WIDTH), 0.02),
        'ev_hy_freq': gain((N_EVEN, HY_HIDDEN)),
        'ev_hy_skip': nrm((N_EVEN, HY_ORDER, HY_WIDTH), 0.5),
        'ev_q_norm': gain((N_EVEN, HEAD_DIM)),
        'ev_k_norm': gain((N_EVEN, HEAD_DIM)),
        'ev_w_out': nrm((N_EVEN, MIX_EVEN, D_MODEL), MIX_EVEN ** -0.5),
        'ev_ffn_w_gate': nrm((N_EVEN, D_MODEL, D_FF), D_MODEL ** -0.5),
        'ev_ffn_w_up': nrm((N_EVEN, D_MODEL, D_FF), D_MODEL ** -0.5),
        'ev_ffn_w_down': nrm((N_EVEN, D_FF, D_MODEL), D_FF ** -0.5),
        'od_w_in': nrm((N_ODD, D_MODEL, S5_WIDTH), D_MODEL ** -0.5),
        'od_s5_lambda_re': -0.5 + nrm((N_ODD, 2, g, n), 0.01),
        'od_s5_lambda_im': math.pi * n_idx + nrm((N_ODD, 2, g, n), 0.01),
        'od_s5_log_step': jax.random.uniform(next(ks), (N_ODD, 2, g), F32,
                                             math.log(S5_DT_MIN), math.log(S5_DT_MAX)),
        'od_s5_b_re': nrm((N_ODD, 2, g, n, S5_GROUP), (2 * S5_GROUP) ** -0.5),
        'od_s5_b_im': nrm((N_ODD, 2, g, n, S5_GROUP), (2 * S5_GROUP) ** -0.5),
        'od_s5_c_re': nrm((N_ODD, 2, g, S5_GROUP, n), S5_STATE ** -0.5),
        'od_s5_c_im': nrm((N_ODD, 2, g, S5_GROUP, n), S5_STATE ** -0.5),
        'od_s5_d': nrm((N_ODD, S5_WIDTH), 0.5),
        'od_glu_w_a': nrm((N_ODD, S5_WIDTH, D_MODEL), S5_WIDTH ** -0.5),
        'od_glu_w_b': nrm((N_ODD, S5_WIDTH, D_MODEL), S5_WIDTH ** -0.5),
        'od_router': nrm((N_ODD, D_MODEL, N_EXPERTS), D_MODEL ** -0.5),
        'od_moe_w_gate': nrm((N_ODD, N_EXPERTS, D_MODEL, D_FF_EXPERT), D_MODEL ** -0.5),
        'od_moe_w_up': nrm((N_ODD, N_EXPERTS, D_MODEL, D_FF_EXPERT), D_MODEL ** -0.5),
        'od_moe_w_down': nrm((N_ODD, N_EXPERTS, D_FF_EXPERT, D_MODEL), D_FF_EXPERT ** -0.5),
    }


def reference(x, c, ctx, c_ctx, ada_w, ada_b, norm_mix_pre, norm_mix_post, norm_ffn_pre, norm_ffn_post,
              ev_w_in, ev_hy_conv_w, ev_hy_conv_b, ev_hy_f_w1, ev_hy_f_b1, ev_hy_f_w2, ev_hy_f_b2,
              ev_hy_f_wout, ev_hy_freq, ev_hy_skip, ev_q_norm, ev_k_norm, ev_w_out,
              ev_ffn_w_gate, ev_ffn_w_up, ev_ffn_w_down,
              od_w_in, od_s5_lambda_re, od_s5_lambda_im, od_s5_log_step, od_s5_b_re, od_s5_b_im,
              od_s5_c_re, od_s5_c_im, od_s5_d, od_glu_w_a, od_glu_w_b,
              od_router, od_moe_w_gate, od_moe_w_up, od_moe_w_down):
    for i in range(DEPTH):
        last = i == DEPTH - 1
        j = i // 2
        sh_m, sc_m, g_m, sh_f, sc_f, g_f = ada_params(c, ada_w[i], ada_b[i])
        cm = ada_params(c_ctx, ada_w[i], ada_b[i])
        h_l = modulate(rms_norm(x, norm_mix_pre[i]), sh_m, sc_m)
        h_c = modulate(rms_norm(ctx, norm_mix_pre[i]), cm[0], cm[1])
        if i % 2 == 0:
            out_l, out_c = even_mixer(h_l, h_c, ev_w_in[j], ev_hy_conv_w[j], ev_hy_conv_b[j],
                                      ev_hy_f_w1[j], ev_hy_f_b1[j], ev_hy_f_w2[j], ev_hy_f_b2[j],
                                      ev_hy_f_wout[j], ev_hy_freq[j], ev_hy_skip[j],
                                      ev_q_norm[j], ev_k_norm[j], ev_w_out[j], not last)
            ffn = lambda h: dense_swiglu(h, ev_ffn_w_gate[j], ev_ffn_w_up[j], ev_ffn_w_down[j])
        else:
            out_l, out_c = odd_mixer(h_l, h_c, od_w_in[j], od_s5_lambda_re[j], od_s5_lambda_im[j],
                                     od_s5_log_step[j], od_s5_b_re[j], od_s5_b_im[j],
                                     od_s5_c_re[j], od_s5_c_im[j], od_s5_d[j],
                                     od_glu_w_a[j], od_glu_w_b[j], not last)
            ffn = lambda h: moe_swiglu(h, od_router[j], od_moe_w_gate[j], od_moe_w_up[j], od_moe_w_down[j])
        x = x + g_m * rms_norm(out_l, norm_mix_post[i])
        hf = modulate(rms_norm(x, norm_ffn_pre[i]), sh_f, sc_f)
        x = x + g_f * rms_norm(ffn(hf), norm_ffn_post[i])
        if not last:
            ctx = ctx + cm[2] * rms_norm(out_c, norm_mix_post[i])
            hf_c = modulate(rms_norm(ctx, norm_ffn_pre[i]), cm[3], cm[4])
            ctx = ctx + cm[5] * rms_norm(ffn(hf_c), norm_ffn_post[i])
    return x
```

```python
import functools
import math

import jax
import jax.numpy as jnp
from jax import lax
from jax.experimental import pallas as pl
from jax.experimental.pallas import tpu as pltpu

F32 = jnp.float32
BF16 = jnp.bfloat16

D_MODEL = 1024
EPS = 1e-6
GRID_W = 64

HY_WIDTH = 512
HY_ORDER = 2
HY_IN = (HY_ORDER + 1) * HY_WIDTH
HY_BANDS = 16
HY_EMB = 2 * HY_BANDS + 1
HY_DECAY_SLOW = -math.log(1e-2) / 1.5
HY_DECAY_FAST = -math.log(1e-2) / 0.3
HEAD_DIM = 64
N_Q_HEADS = 8
N_KV_HEADS = 2
ATTN_WIDTH = N_Q_HEADS * HEAD_DIM
KV_WIDTH = N_KV_HEADS * HEAD_DIM
ROPE_THETA = 10000.0

S5_GROUP = 16
S5_GROUPS = D_MODEL // S5_GROUP
S5_STATE = 64
S5_LANES = S5_GROUPS * S5_STATE
N_EXPERTS = 8

ROW_TILE = 256
FREQ_TILE = 256
S5_CHUNK = 32
LANE = 128
MIB = 1024 * 1024


def _cparams(sem, vmem_mib=48):
    return pltpu.CompilerParams(dimension_semantics=sem, vmem_limit_bytes=vmem_mib * MIB)


def _rms(x, gain):
    return x * lax.rsqrt(jnp.mean(x * x, axis=-1, keepdims=True) + EPS) * gain


def _bdot(a, b):
    return jnp.dot(a, b, preferred_element_type=F32)


def _full(shape):
    zeros = (0,) * len(shape)
    return pl.BlockSpec(shape, lambda *_: zeros)


def _ada_kernel(cond_ref, w_ref, b_ref, o_ref):
    c = cond_ref[...]
    s = (c * jax.nn.sigmoid(c)).astype(BF16)
    o_ref[...] = _bdot(s, w_ref[...].astype(BF16)) + b_ref[...]


def _ada_params(cond, ada_w, ada_b):
    depth, _, n6 = ada_w.shape
    rows = cond.shape[0]
    tn = 1536
    out = pl.pallas_call(
        _ada_kernel,
        grid=(depth, n6 // tn),
        in_specs=[pl.BlockSpec((rows, D_MODEL), lambda i, j: (0, 0)),
                  pl.BlockSpec((None, D_MODEL, tn), lambda i, j: (i, 0, j)),
                  pl.BlockSpec((None, 1, tn), lambda i, j: (i, 0, j))],
        out_specs=pl.BlockSpec((None, rows, tn), lambda i, j: (i, 0, j)),
        out_shape=jax.ShapeDtypeStruct((depth, rows, n6), F32),
        compiler_params=_cparams(("arbitrary", "arbitrary")),
        name="ada_params",
    )(cond, ada_w, ada_b.reshape(depth, 1, n6))
    return out.reshape(depth, rows, 6, D_MODEL)


def _rope_rotate(t):
    w = t.shape[-1]
    lane = lax.broadcasted_iota(jnp.int32, t.shape, 1)
    first = (lane & (HEAD_DIM - 1)) < HEAD_DIM // 2
    return jnp.where(first, pltpu.roll(t, w - HEAD_DIM // 2, 1), pltpu.roll(t, HEAD_DIM // 2, 1))


def _head_slots(t):
    lane = lax.broadcasted_iota(jnp.int32, t.shape, 1)
    lo = jnp.where(lane < HEAD_DIM, t, 0.0)
    hi = jnp.where(lane >= HEAD_DIM, t, 0.0)
    return jnp.concatenate([lo, pltpu.roll(lo, HEAD_DIM, 1), pltpu.roll(hi, HEAD_DIM, 1), hi], axis=-1)


def _inproj_even_kernel(x_ref, mod_ref, gain_ref, w_ref, qg_ref, kg_ref, e_ref, cos_ref, sin_ref,
                        hy_ref, q_ref, k4_ref, v4_ref):
    h = _rms(x_ref[...], gain_ref[...]) * (1.0 + mod_ref[1:2, :]) + mod_ref[0:1, :]
    p = _bdot(h.astype(BF16), w_ref[...])
    hy_ref[...] = p[:, :HY_IN].astype(hy_ref.dtype)
    q = p[:, HY_IN:HY_IN + ATTN_WIDTH]
    k = p[:, HY_IN + ATTN_WIDTH:HY_IN + ATTN_WIDTH + KV_WIDTH]
    v = p[:, HY_IN + ATTN_WIDTH + KV_WIDTH:]
    e = e_ref[...]
    qn = q * lax.rsqrt(_bdot((q * q).astype(BF16), e) + EPS) * qg_ref[...]
    kn = k * lax.rsqrt(_bdot((k * k).astype(BF16), e[:KV_WIDTH, :KV_WIDTH]) + EPS) * kg_ref[...]
    cos = cos_ref[...]
    sin = sin_ref[...]
    qr = (qn * cos + _rope_rotate(qn) * sin) * (HEAD_DIM ** -0.5)
    kr = kn * cos[:, :KV_WIDTH] + _rope_rotate(kn) * sin[:, :KV_WIDTH]
    q_ref[...] = qr.astype(q_ref.dtype)
    k4_ref[...] = _head_slots(kr).astype(k4_ref.dtype)
    v4_ref[...] = _head_slots(v).astype(v4_ref.dtype)


def _mod_spec(n_batch, nlat):
    return pl.BlockSpec((None, 6, D_MODEL), lambda b, j: (jnp.where(j >= nlat, n_batch, b), 0, 0))


def _row_spec(width):
    return pl.BlockSpec((None, ROW_TILE, width), lambda b, j: (b, j, 0))


def _inproj_even(xs, mod, gain, w_in, q_gain, k_gain, cos_t, sin_t, nlat):
    n_batch, lt, _ = xs.shape
    ntile = lt // ROW_TILE
    n_out = w_in.shape[1]
    head_avg = jnp.kron(jnp.eye(N_Q_HEADS, dtype=F32), jnp.full((HEAD_DIM, HEAD_DIM), 1.0 / HEAD_DIM, F32)).astype(BF16)
    table = pl.BlockSpec((ROW_TILE, ATTN_WIDTH), lambda b, j: (j, 0))
    outs = [jax.ShapeDtypeStruct((n_batch, lt, HY_IN), BF16)] + [jax.ShapeDtypeStruct((n_batch, lt, ATTN_WIDTH), BF16)] * 3
    return pl.pallas_call(
        _inproj_even_kernel,
        grid=(n_batch, ntile),
        in_specs=[_row_spec(D_MODEL), _mod_spec(n_batch, nlat), _full((1, D_MODEL)), _full((D_MODEL, n_out)),
                  _full((1, ATTN_WIDTH)), _full((1, KV_WIDTH)), _full((ATTN_WIDTH, ATTN_WIDTH)), table, table],
        out_specs=[_row_spec(HY_IN), _row_spec(ATTN_WIDTH), _row_spec(ATTN_WIDTH), _row_spec(ATTN_WIDTH)],
        out_shape=outs,
        compiler_params=_cparams(("parallel", "parallel")),
        name="inproj_even",
    )(xs, mod, gain, w_in, q_gain, k_gain, head_avg, cos_t, sin_t)


def _attn_kernel(q_ref, k4_ref, v4_ref, o_ref, *, n_lat, nlat_tiles):
    j = pl.program_id(1)
    n_keys = k4_ref.shape[0]

    def attend(lo):
        for g in range(N_KV_HEADS):
            for p in range(2):
                col = 256 * g + LANE * p
                qp = q_ref[:, col:col + LANE]
                acc = None
                for r in range(2):
                    slot = LANE * (2 * g + r)
                    kk = k4_ref[lo:n_keys, slot:slot + LANE]
                    vv = v4_ref[lo:n_keys, slot:slot + LANE]
                    s = lax.dot_general(qp, kk, (((1,), (1,)), ((), ())), preferred_element_type=F32)
                    e = jnp.exp(s - jnp.max(s, axis=-1, keepdims=True))
                    inv = 1.0 / jnp.sum(e, axis=-1, keepdims=True)
                    o = _bdot(e.astype(BF16), vv) * inv
                    acc = o if acc is None else acc + o
                o_ref[:, col:col + LANE] = acc.astype(o_ref.dtype)

    @pl.when(j < nlat_tiles)
    def _():
        attend(0)

    @pl.when(j >= nlat_tiles)
    def _():
        attend(n_lat)


def _attention(q, k4, v4, n_lat):
    n_batch, lt, _ = q.shape
    ntile = lt // ROW_TILE
    kv_spec = pl.BlockSpec((None, lt, ATTN_WIDTH), lambda b, j: (b, 0, 0))
    return pl.pallas_call(
        functools.partial(_attn_kernel, n_lat=n_lat, nlat_tiles=n_lat // ROW_TILE),
        grid=(n_batch, ntile),
        in_specs=[_row_spec(ATTN_WIDTH), kv_spec, kv_spec],
        out_specs=_row_spec(ATTN_WIDTH),
        out_shape=jax.ShapeDtypeStruct((n_batch, lt, ATTN_WIDTH), BF16),
        compiler_params=_cparams(("parallel", "parallel")),
        name="attention",
    )(q, k4, v4)


def _filter_kernel(z_ref, w1_ref, b1_ref, w2_ref, b2_ref, wo_ref, fr_ref, dl_ref, o_ref):
    hi = lax.Precision.HIGHEST
    z = z_ref[...]
    fr = fr_ref[...]
    h = jnp.sin(fr * (jnp.dot(z, w1_ref[...], precision=hi, preferred_element_type=F32) + b1_ref[...]))
    h = jnp.sin(fr * (jnp.dot(h, w2_ref[...], precision=hi, preferred_element_type=F32) + b2_ref[...]))
    h = jnp.dot(h, wo_ref[...], precision=hi, preferred_element_type=F32)
    decay = jnp.exp(-z[:, 0:1] * dl_ref[...])
    o_ref[...] = h * jnp.concatenate([decay] * (2 * HY_ORDER), axis=-1)


def _hyena_filters(seq, w1, b1, w2, b2, wout, freq):
    t = jnp.linspace(0.0, 1.0, seq, dtype=F32)[:, None]
    bands = jnp.linspace(1e-4, HY_BANDS - 1, HY_BANDS, dtype=F32)
    phase = (2.0 * math.pi / seq) * jnp.arange(seq, dtype=F32)[:, None] * bands
    z = jnp.concatenate([t, jnp.cos(phase), -jnp.sin(phase)], axis=-1)
    z = jnp.pad(z, ((0, 0), (0, LANE - HY_EMB)))
    w1p = jnp.pad(w1, ((0, LANE - HY_EMB), (0, 0)))
    deltas = jnp.linspace(HY_DECAY_SLOW, HY_DECAY_FAST, HY_WIDTH, dtype=F32)[None, :]
    hid = w1.shape[1]
    n_out = wout.shape[1]
    tl = ROW_TILE
    return pl.pallas_call(
        _filter_kernel,
        grid=(seq // tl,),
        in_specs=[pl.BlockSpec((tl, LANE), lambda i: (i, 0)), _full((LANE, hid)), _full((1, hid)), _full((hid, hid)),
                  _full((1, hid)), _full((hid, n_out)), _full((1, hid)), _full((1, HY_WIDTH))],
        out_specs=pl.BlockSpec((tl, n_out), lambda i: (i, 0)),
        out_shape=jax.ShapeDtypeStruct((seq, n_out), F32),
        compiler_params=_cparams(("parallel",)),
        name="hyena_filter",
    )(z, w1p, b1[None, :], w2, b2[None, :], wout, freq[None, :], deltas)


def _dft_tables(seq):
    n = 2 * seq
    f = jnp.arange(seq, dtype=jnp.int32)[:, None]
    t = jnp.arange(seq, dtype=jnp.int32)[None, :]
    ang = ((f * t) % n).astype(F32) * (2.0 * math.pi / n)
    cos = jnp.cos(ang)
    sin = jnp.where(f == 0, jnp.where(t % 2 == 0, 1.0, -1.0), jnp.sin(ang))
    nf = seq // FREQ_TILE
    fwd = jnp.concatenate([cos.reshape(nf, FREQ_TILE, seq), sin.reshape(nf, FREQ_TILE, seq)], axis=1)
    return fwd.astype(BF16), jnp.swapaxes(fwd, 1, 2).astype(BF16)


def _kfreq_kernel(hf_ref, hb_ref, f_ref, kre_ref, ks_ref):
    fi = pl.program_id(1)
    seq = hf_ref.shape[0]
    hf = hf_ref[...]
    row = lax.broadcasted_iota(jnp.int32, hf.shape, 0)
    hb = jnp.where(row == 0, 0.0, hb_ref[...])
    f = f_ref[...]
    a1 = _bdot(f, (hf + hb).astype(BF16))
    a2 = _bdot(f, (hf - hb).astype(BF16))
    tf = FREQ_TILE
    frow = lax.broadcasted_iota(jnp.int32, (tf, hf.shape[1]), 0)
    dc = jnp.logical_and(frow == 0, fi == 0)
    scale = jnp.where(dc, 1.0 / (2 * seq), 2.0 / (2 * seq))
    kre_ref[...] = a1[:tf] * scale
    ks_ref[...] = jnp.where(dc, a1[tf:], a2[tf:]) * scale


def _kfreq(hfilt, fwd_tab):
    seq = hfilt.shape[0]
    nf = seq // FREQ_TILE
    out = jax.ShapeDtypeStruct((HY_ORDER, seq, HY_WIDTH), F32)
    ospec = pl.BlockSpec((None, FREQ_TILE, HY_WIDTH), lambda o, fi: (o, fi, 0))
    return pl.pallas_call(
        _kfreq_kernel,
        grid=(HY_ORDER, nf),
        in_specs=[pl.BlockSpec((seq, HY_WIDTH), lambda o, fi: (0, o)),
                  pl.BlockSpec((seq, HY_WIDTH), lambda o, fi: (0, HY_ORDER + o)),
                  pl.BlockSpec((None, 2 * FREQ_TILE, seq), lambda o, fi: (fi, 0, 0))],
        out_specs=[ospec, ospec],
        out_shape=[out, out],
        compiler_params=_cparams(("parallel", "parallel")),
        name="hyena_kfreq",
    )(hfilt, hfilt, fwd_tab)


def _short_conv(p, w, b):
    n = p.shape[0]
    row = lax.broadcasted_iota(jnp.int32, p.shape, 0)
    prev = jnp.where(row == 0, 0.0, pltpu.roll(p, 1, 0))
    nxt = jnp.where(row == n - 1, 0.0, pltpu.roll(p, n - 1, 0))
    return b + prev * w[0:1, :] + p * w[1:2, :] + nxt * w[2:3, :]


def _hyena_kernel(vsrc_ref, gsrc_ref, cw_ref, cb_ref, skip_ref, f_ref, ft_ref, kre_ref, ks_ref, o_ref,
                  v_scr, vb_scr, acc_scr, *, conv_v):
    fi = pl.program_id(1)
    tf = FREQ_TILE

    @pl.when(fi == 0)
    def _():
        v = vsrc_ref[...].astype(F32)
        if conv_v:
            v = _short_conv(v, cw_ref[0], cb_ref[0])
        v_scr[...] = v
        vb_scr[...] = v.astype(BF16)
        acc_scr[...] = jnp.zeros_like(acc_scr)

    xf = _bdot(f_ref[...], vb_scr[...])
    xre, xs = xf[:tf], xf[tf:]
    kre, ks = kre_ref[...], ks_ref[...]
    row = lax.broadcasted_iota(jnp.int32, xre.shape, 0)
    dc = jnp.logical_and(row == 0, fi == 0)
    yre = jnp.where(dc, xre * kre, xre * kre - xs * ks)
    ys = jnp.where(dc, xs * ks, xre * ks + xs * kre)
    y = jnp.concatenate([yre, ys], axis=0).astype(BF16)
    acc_scr[...] += _bdot(ft_ref[...], y)

    @pl.when(fi == pl.num_programs(1) - 1)
    def _():
        gate = _short_conv(gsrc_ref[...].astype(F32), cw_ref[1], cb_ref[1])
        o_ref[...] = (gate * (acc_scr[...] + v_scr[...] * skip_ref[...])).astype(o_ref.dtype)


def _hyena_order(vsrc, v_blk, v_col, hy, row_blk, order, seq, conv_w, conv_b, skip, fwd_tab, inv_tab, kre, ks):
    n_batch = hy.shape[0]
    nf = seq // FREQ_TILE
    conv_v = order == 0
    cw = jnp.stack([conv_w[:, :HY_WIDTH], conv_w[:, (order + 1) * HY_WIDTH:(order + 2) * HY_WIDTH]])
    cb = jnp.stack([conv_b[None, :HY_WIDTH], conv_b[None, (order + 1) * HY_WIDTH:(order + 2) * HY_WIDTH]])
    kspec = pl.BlockSpec((None, FREQ_TILE, HY_WIDTH), lambda b, fi: (order, fi, 0))
    return pl.pallas_call(
        functools.partial(_hyena_kernel, conv_v=conv_v),
        grid=(n_batch, nf),
        in_specs=[pl.BlockSpec((None, seq, HY_WIDTH), lambda b, fi: (b, v_blk, v_col)),
                  pl.BlockSpec((None, seq, HY_WIDTH), lambda b, fi: (b, row_blk, order + 1)),
                  _full((2, 3, HY_WIDTH)), _full((2, 1, HY_WIDTH)), _full((1, HY_WIDTH)),
                  pl.BlockSpec((None, 2 * FREQ_TILE, seq), lambda b, fi: (fi, 0, 0)),
                  pl.BlockSpec((None, seq, 2 * FREQ_TILE), lambda b, fi: (fi, 0, 0)),
                  kspec, kspec],
        out_specs=pl.BlockSpec((None, seq, HY_WIDTH), lambda b, fi: (b, 0, 0)),
        out_shape=jax.ShapeDtypeStruct((n_batch, seq, HY_WIDTH), BF16),
        scratch_shapes=[pltpu.VMEM((seq, HY_WIDTH), F32), pltpu.VMEM((seq, HY_WIDTH), BF16),
                        pltpu.VMEM((seq, HY_WIDTH), F32)],
        compiler_params=_cparams(("parallel", "arbitrary")),
        name="hyena_order%d" % order,
    )(vsrc, hy, cw, cb, skip[order][None, :], fwd_tab, inv_tab, kre, ks)


def _hyena_mixer(hy, row_blk, seq, filt_args, conv_w, conv_b, skip):
    hfilt = _hyena_filters(seq, *filt_args)
    fwd_tab, inv_tab = _dft_tables(seq)
    kre, ks = _kfreq(hfilt, fwd_tab)
    v1 = _hyena_order(hy, row_blk, 0, hy, row_blk, 0, seq, conv_w, conv_b, skip, fwd_tab, inv_tab, kre, ks)
    return _hyena_order(v1, 0, 0, hy, row_blk, 1, seq, conv_w, conv_b, skip, fwd_tab, inv_tab, kre, ks)


def _outproj_kernel(yl_ref, yc_ref, ya_ref, x_ref, mod_ref, gpost_ref, w_ref, o_ref, *, nlat):
    j = pl.program_id(1)
    yh = jnp.where(j >= nlat, yc_ref[...], yl_ref[...])
    out = _bdot(yh, w_ref[:HY_WIDTH, :]) + _bdot(ya_ref[...], w_ref[HY_WIDTH:, :])
    o_ref[...] = x_ref[...] + mod_ref[2:3, :] * _rms(out, gpost_ref[...])


def _outproj(y_lat, y_ctx, y_att, xs, mod, gpost, w_out, nlat):
    n_batch, lt, _ = xs.shape
    ntile = lt // ROW_TILE
    nctx = ntile - nlat
    return pl.pallas_call(
        functools.partial(_outproj_kernel, nlat=nlat),
        grid=(n_batch, ntile),
        in_specs=[pl.BlockSpec((None, ROW_TILE, HY_WIDTH), lambda b, j: (b, jnp.minimum(j, nlat - 1), 0)),
                  pl.BlockSpec((None, ROW_TILE, HY_WIDTH), lambda b, j: (b, jnp.clip(j - nlat, 0, nctx - 1), 0)),
                  _row_spec(ATTN_WIDTH), _row_spec(D_MODEL), _mod_spec(n_batch, nlat), _full((1, D_MODEL)),
                  _full(w_out.shape)],
        out_specs=_row_spec(D_MODEL),
        out_shape=jax.ShapeDtypeStruct(xs.shape, F32),
        compiler_params=_cparams(("parallel", "parallel")),
        name="outproj_even",
    )(y_lat, y_ctx, y_att, xs, mod, gpost, w_out)


def _ffn_kernel(x_ref, mod_ref, gpre_ref, gpost_ref, wg_ref, wu_ref, wd_ref, o_ref):
    x = x_ref[...]
    h = (_rms(x, gpre_ref[...]) * (1.0 + mod_ref[4:5, :]) + mod_ref[3:4, :]).astype(BF16)
    g = _bdot(h, wg_ref[...])
    u = _bdot(h, wu_ref[...])
    a = (g * jax.nn.sigmoid(g) * u).astype(BF16)
    y = _bdot(a, wd_ref[...])
    o_ref[...] = x + mod_ref[5:6, :] * _rms(y, gpost_ref[...])


def _ffn_dense(xs, mod, gpre, gpost, wg, wu, wd, nlat):
    n_batch, lt, _ = xs.shape
    return pl.pallas_call(
        _ffn_kernel,
        grid=(n_batch, lt // ROW_TILE),
        in_specs=[_row_spec(D_MODEL), _mod_spec(n_batch, nlat), _full((1, D_MODEL)), _full((1, D_MODEL)),
                  _full(wg.shape), _full(wu.shape), _full(wd.shape)],
        out_specs=_row_spec(D_MODEL),
        out_shape=jax.ShapeDtypeStruct(xs.shape, F32),
        compiler_params=_cparams(("parallel", "parallel"), 56),
        name="ffn_dense",
    )(xs, mod, gpre, gpost, wg, wu, wd)


def _inproj_odd_kernel(x_ref, mod_ref, gain_ref, w_ref, u_ref):
    h = _rms(x_ref[...], gain_ref[...]) * (1.0 + mod_ref[1:2, :]) + mod_ref[0:1, :]
    u_ref[...] = _bdot(h.astype(BF16), w_ref[...])


def _inproj_odd(xs, mod, gain, w_in, nlat):
    n_batch, lt, _ = xs.shape
    return pl.pallas_call(
        _inproj_odd_kernel,
        grid=(n_batch, lt // ROW_TILE),
        in_specs=[_row_spec(D_MODEL), _mod_spec(n_batch, nlat), _full((1, D_MODEL)), _full(w_in.shape)],
        out_specs=pl.BlockSpec((ROW_TILE, D_MODEL), lambda b, j: (j, b)),
        out_shape=jax.ShapeDtypeStruct((lt, n_batch * D_MODEL), F32),
        compiler_params=_cparams(("parallel", "parallel")),
        name="inproj_odd",
    )(xs, mod, gain, w_in)


def _s5_param_kernel(lr_ref, li_ref, ls_ref, lrx_ref, lix_ref, lsx_ref, br_ref, bi_ref,
                     abr_ref, abi_ref, bbr_ref, bbi_ref):
    def zoh(lr_raw, li, log_step):
        lr = jnp.minimum(lr_raw, -1e-4)
        dt = jnp.exp(log_step)
        mag = jnp.exp(lr * dt)
        ab_re = mag * jnp.cos(li * dt)
        ab_im = mag * jnp.sin(li * dt)
        den = lr * lr + li * li
        nr, ni = ab_re - 1.0, ab_im
        return ab_re, ab_im, (nr * lr + ni * li) / den, (ni * lr - nr * li) / den

    ab_re, ab_im, _, _ = zoh(lr_ref[...], li_ref[...], ls_ref[...])
    abr_ref[...] = ab_re
    abi_ref[...] = ab_im
    _, _, co_re, co_im = zoh(lrx_ref[...], lix_ref[...], lsx_ref[...])
    br, bi = br_ref[...], bi_ref[...]
    bbr_ref[...] = co_re * br - co_im * bi
    bbi_ref[...] = co_re * bi + co_im * br


def _s5_params(lam_re, lam_im, log_step, b_re, b_im, c_re, c_im):
    nd, g, n = lam_re.shape
    k = S5_GROUP
    rep = lambda a: jnp.repeat(a, k, axis=1)
    ls = log_step[:, :, None]
    bt_re = jnp.swapaxes(b_re, 2, 3).reshape(nd, g * k, n)
    bt_im = jnp.swapaxes(b_im, 2, 3).reshape(nd, g * k, n)
    small = jax.ShapeDtypeStruct((nd, g, n), F32)
    big = jax.ShapeDtypeStruct((nd, g * k, n), F32)
    ab_re, ab_im, bb_re, bb_im = pl.pallas_call(
        _s5_param_kernel, out_shape=[small, small, big, big], name="s5_discretise",
    )(lam_re, lam_im, ls, rep(lam_re), rep(lam_im), rep(ls), bt_re, bt_im)
    a = jnp.stack([ab_re.reshape(nd, g * n), ab_im.reshape(nd, g * n)], axis=1)
    eye = jnp.eye(8, dtype=F32)
    nq = g // 8

    def drive_blocks(bb):
        return jnp.einsum('dqgkn,gh->dqgkhn', bb.reshape(nd, nq, 8, k, n), eye).reshape(nd, nq, 8 * k, 8 * n)

    def read_blocks(c):
        return jnp.einsum('dqgin,gh->dqhngi', c.reshape(nd, nq, 8, k, n), eye).reshape(nd, nq, 8 * n, 8 * k)

    w_drive = jnp.concatenate([drive_blocks(bb_re), drive_blocks(bb_im)], axis=-1).astype(BF16)
    w_read = jnp.stack([read_blocks(c_re), read_blocks(-c_im)], axis=2).astype(BF16)
    return a, w_drive, w_read


def _s5_scan_kernel(u_ref, a_ref, wd_ref, wr_ref, y_ref, sbuf, state, *, nctx_chunks):
    d = pl.program_id(0)
    i = pl.program_id(1)
    p_steps, n_batch, _ = u_ref.shape
    rows = p_steps * n_batch
    half = S5_LANES
    nq = wd_ref.shape[0]
    kq = wd_ref.shape[1]
    sq = wd_ref.shape[2] // 2

    @pl.when(i == 0)
    def _():
        state[...] = jnp.zeros_like(state)

    u = u_ref[...].reshape(rows, D_MODEL).astype(BF16)
    for q in range(nq):
        r = _bdot(u[:, kq * q:kq * (q + 1)], wd_ref[q])
        sbuf[:, :, sq * q:sq * (q + 1)] = r[:, :sq].reshape(p_steps, n_batch, sq)
        sbuf[:, :, half + sq * q:half + sq * (q + 1)] = r[:, sq:].reshape(p_steps, n_batch, sq)

    for q in range(nq):
        lo = sq * q
        ar = jnp.broadcast_to(a_ref[0:1, lo:lo + sq], (n_batch, sq))
        ai = jnp.broadcast_to(a_ref[1:2, lo:lo + sq], (n_batch, sq))

        def body(t, carry, lo=lo, ar=ar, ai=ai):
            sr, si = carry
            tt = jnp.where(d == 0, t, p_steps - 1 - t)
            nr = ar * sr - ai * si + sbuf[tt, :, lo:lo + sq]
            ni = ar * si + ai * sr + sbuf[tt, :, half + lo:half + lo + sq]
            sbuf[tt, :, lo:lo + sq] = nr
            sbuf[tt, :, half + lo:half + lo + sq] = ni
            return nr, ni

        sr, si = lax.fori_loop(0, p_steps, body, (state[:, lo:lo + sq], state[:, half + lo:half + lo + sq]))
        state[:, lo:lo + sq] = sr
        state[:, half + lo:half + lo + sq] = si

    @pl.when(i >= nctx_chunks)
    def _():
        s = sbuf[...].reshape(rows, 2 * half).astype(BF16)
        for q in range(nq):
            yq = (_bdot(s[:, sq * q:sq * (q + 1)], wr_ref[q, 0])
                  + _bdot(s[:, half + sq * q:half + sq * (q + 1)], wr_ref[q, 1]))
            y_ref[:, :, kq * q:kq * (q + 1)] = yq.reshape(p_steps, n_batch, kq).astype(y_ref.dtype)


def _s5_scan(u3, a, w_drive, w_read, n_lat):
    lt, n_batch, _ = u3.shape
    p = S5_CHUNK
    nchunk = lt // p
    nlatc = n_lat // p
    nctxc = nchunk - nlatc

    def u_map(d, i):
        return (jnp.where(d == 0, lax.rem(i + nlatc, nchunk), nchunk - 1 - i), 0, 0)

    def y_map(d, i):
        return (d, jnp.where(d == 0, jnp.maximum(i - nctxc, 0), jnp.minimum(nchunk - 1 - i, nlatc - 1)), 0, 0)

    return pl.pallas_call(
        functools.partial(_s5_scan_kernel, nctx_chunks=nctxc),
        grid=(2, nchunk),
        in_specs=[pl.BlockSpec((p, n_batch, D_MODEL), u_map),
                  pl.BlockSpec((None, 2, S5_LANES), lambda d, i: (d, 0, 0)),
                  pl.BlockSpec((None,) + w_drive.shape[1:], lambda d, i: (d, 0, 0, 0)),
                  pl.BlockSpec((None,) + w_read.shape[1:], lambda d, i: (d, 0, 0, 0, 0))],
        out_specs=pl.BlockSpec((None, p, n_batch, D_MODEL), y_map),
        out_shape=jax.ShapeDtypeStruct((2, n_lat, n_batch, D_MODEL), BF16),
        scratch_shapes=[pltpu.VMEM((p, n_batch, 2 * S5_LANES), F32), pltpu.VMEM((n_batch, 2 * S5_LANES), F32)],
        compiler_params=_cparams(("arbitrary", "arbitrary")),
        name="s5_scan",
    )(u3, a, w_drive, w_read)


def _glu_kernel(yf_ref, yb_ref, u_ref, dskip_ref, wa_ref, wb_ref, x_ref, mod_ref, gpost_ref, gpre_ref, router_ref,
                xo_ref, hf_ref, gates_ref):
    y = yf_ref[...].astype(F32) + yb_ref[...].astype(F32) + dskip_ref[...] * u_ref[...]
    z = jax.nn.gelu(y).astype(BF16)
    out = _bdot(z, wa_ref[...]) * jax.nn.sigmoid(_bdot(z, wb_ref[...]))
    xn = x_ref[...] + mod_ref[2:3, :] * _rms(out, gpost_ref[...])
    xo_ref[...] = xn
    hf = _rms(xn, gpre_ref[...]) * (1.0 + mod_ref[4:5, :]) + mod_ref[3:4, :]
    hf_ref[...] = hf.astype(hf_ref.dtype)
    logits = jnp.dot(hf, router_ref[...], precision=lax.Precision.HIGHEST, preferred_element_type=F32)
    lane = lax.broadcasted_iota(jnp.int32, logits.shape, 1)
    neg = jnp.float32(-jnp.inf)
    lg = jnp.where(lane < N_EXPERTS, logits, neg)
    m1 = jnp.max(lg, axis=-1, keepdims=True)
    i1 = jnp.min(jnp.where(lg == m1, lane, LANE), axis=-1, keepdims=True)
    lg2 = jnp.where(lane == i1, neg, lg)
    m2 = jnp.max(lg2, axis=-1, keepdims=True)
    i2 = jnp.min(jnp.where(lg2 == m2, lane, LANE), axis=-1, keepdims=True)
    e2 = jnp.exp(m2 - m1)
    w1 = 1.0 / (1.0 + e2)
    gates_ref[...] = jnp.where(lane == i1, w1, 0.0) + jnp.where(lane == i2, e2 * w1, 0.0)


def _glu(y2, u2d, d_skip, w_a, w_b, xs, mod, gpost, gpre, router, n_lat):
    n_batch = xs.shape[0]
    nlat = n_lat // ROW_TILE
    router_p = jnp.pad(router, ((0, 0), (0, LANE - router.shape[1])))
    tm_spec = lambda w: pl.BlockSpec((None, ROW_TILE, w), lambda b, j: (b, j, 0))
    return pl.pallas_call(
        _glu_kernel,
        grid=(n_batch, nlat),
        in_specs=[pl.BlockSpec((None, ROW_TILE, D_MODEL), lambda b, j: (0, j, b)),
                  pl.BlockSpec((None, ROW_TILE, D_MODEL), lambda b, j: (1, j, b)),
                  pl.BlockSpec((ROW_TILE, D_MODEL), lambda b, j: (j, b)),
                  _full((1, D_MODEL)), _full(w_a.shape), _full(w_b.shape), tm_spec(D_MODEL),
                  pl.BlockSpec((None, 6, D_MODEL), lambda b, j: (b, 0, 0)), _full((1, D_MODEL)), _full((1, D_MODEL)),
                  _full(router_p.shape)],
        out_specs=[tm_spec(D_MODEL), tm_spec(D_MODEL), tm_spec(LANE)],
        out_shape=[jax.ShapeDtypeStruct((n_batch, n_lat, D_MODEL), F32),
                   jax.ShapeDtypeStruct((n_batch, n_lat, D_MODEL), BF16),
                   jax.ShapeDtypeStruct((n_batch, n_lat, LANE), F32)],
        compiler_params=_cparams(("parallel", "parallel")),
        name="s5_glu_router",
    )(y2, y2, u2d, d_skip, w_a, w_b, xs, mod, gpost, gpre, router_p)


def _moe_kernel(hf_ref, gates_ref, x_ref, mod_ref, gpost_ref, wg_ref, wu_ref, wd_ref, o_ref, acc):
    e = pl.program_id(1)
    j = pl.program_id(2)

    @pl.when(jnp.logical_and(e == 0, j == 0))
    def _():
        acc[...] = jnp.zeros_like(acc)

    h = hf_ref[...]
    g = _bdot(h, wg_ref[...])
    u = _bdot(h, wu_ref[...])
    gates = gates_ref[...]
    lane = lax.broadcasted_iota(jnp.int32, gates.shape, 1)
    ge = jnp.sum(jnp.where(lane == e, gates, 0.0), axis=-1, keepdims=True)
    a = (g * jax.nn.sigmoid(g) * u * ge).astype(BF16)
    acc[...] += _bdot(a, wd_ref[...])

    @pl.when(jnp.logical_and(e == pl.num_programs(1) - 1, j == pl.num_programs(2) - 1))
    def _():
        o_ref[...] = x_ref[...] + mod_ref[5:6, :] * _rms(acc[...], gpost_ref[...])


def _moe(hf, gates, xs, mod, gpost, wg, wu, wd, n_lat):
    t_rows = hf.shape[0]
    n_exp, _, d_ff = wg.shape
    tm = min(1024, n_lat)
    tn = 512
    row = lambda w: pl.BlockSpec((tm, w), lambda i, e, j: (i, 0))
    return pl.pallas_call(
        _moe_kernel,
        grid=(t_rows // tm, n_exp, d_ff // tn),
        in_specs=[row(D_MODEL), row(LANE), row(D_MODEL),
                  pl.BlockSpec((None, 6, D_MODEL), lambda i, e, j: ((i * tm) // n_lat, 0, 0)),
                  pl.BlockSpec((1, D_MODEL), lambda i, e, j: (0, 0)),
                  pl.BlockSpec((None, D_MODEL, tn), lambda i, e, j: (e, 0, j)),
                  pl.BlockSpec((None, D_MODEL, tn), lambda i, e, j: (e, 0, j)),
                  pl.BlockSpec((None, tn, D_MODEL), lambda i, e, j: (e, j, 0))],
        out_specs=row(D_MODEL),
        out_shape=jax.ShapeDtypeStruct((t_rows, D_MODEL), F32),
        scratch_shapes=[pltpu.VMEM((tm, D_MODEL), F32)],
        compiler_params=_cparams(("parallel", "arbitrary", "arbitrary")),
        name="moe_dense",
    )(hf, gates, xs, mod, gpost, wg, wu, wd)


def _rope_tables(n_lat, n_ctx):
    rows = n_lat // GRID_W
    row = jnp.repeat(jnp.arange(rows, dtype=F32), GRID_W)
    col = jnp.tile(jnp.arange(GRID_W, dtype=F32), rows)
    n_freq = HEAD_DIM // 4
    inv = ROPE_THETA ** (-jnp.arange(n_freq, dtype=F32) / n_freq)
    ang = jnp.concatenate([row[:, None] * inv, col[:, None] * inv], axis=-1)
    cos, sin = jnp.cos(ang), jnp.sin(ang)
    cos_h = jnp.concatenate([cos, cos], axis=-1)
    sin_h = jnp.concatenate([-sin, sin], axis=-1)
    cos_t = jnp.concatenate([jnp.tile(cos_h, (1, N_Q_HEADS)), jnp.ones((n_ctx, ATTN_WIDTH), F32)], axis=0)
    sin_t = jnp.concatenate([jnp.tile(sin_h, (1, N_Q_HEADS)), jnp.zeros((n_ctx, ATTN_WIDTH), F32)], axis=0)
    return cos_t, sin_t


def kernel(x, c, ctx, c_ctx, ada_w, ada_b, norm_mix_pre, norm_mix_post, norm_ffn_pre, norm_ffn_post, ev_w_in, ev_hy_conv_w, ev_hy_conv_b, ev_hy_f_w1, ev_hy_f_b1, ev_hy_f_w2, ev_hy_f_b2, ev_hy_f_wout, ev_hy_freq, ev_hy_skip, ev_q_norm, ev_k_norm, ev_w_out, ev_ffn_w_gate, ev_ffn_w_up, ev_ffn_w_down, od_w_in, od_s5_lambda_re, od_s5_lambda_im, od_s5_log_step, od_s5_b_re, od_s5_b_im, od_s5_c_re, od_s5_c_im, od_s5_d, od_glu_w_a, od_glu_w_b, od_router, od_moe_w_gate, od_moe_w_up, od_moe_w_down):
    n_batch, n_lat, _ = x.shape
    n_ctx = ctx.shape[1]
    depth = ada_w.shape[0]
    assert n_batch == 8 and n_lat % ROW_TILE == 0 and n_ctx % ROW_TILE == 0 and n_lat % n_ctx == 0
    assert depth == 2
    nlat = n_lat // ROW_TILE

    cond = jnp.concatenate([c, c_ctx[None, :], jnp.zeros((16 - n_batch - 1, D_MODEL), F32)], axis=0)
    mods = _ada_params(cond, ada_w, ada_b)
    xs = jnp.concatenate([x, ctx], axis=1)
    vec = lambda a: a[None, :]

    cos_t, sin_t = _rope_tables(n_lat, n_ctx)
    hy, q, k4, v4 = _inproj_even(xs, mods[0], vec(norm_mix_pre[0]), ev_w_in[0].astype(BF16),
                                 vec(jnp.tile(ev_q_norm[0], N_Q_HEADS)), vec(jnp.tile(ev_k_norm[0], N_KV_HEADS)),
                                 cos_t, sin_t, nlat)
    y_att = _attention(q, k4, v4, n_lat)
    filt_args = (ev_hy_f_w1[0], ev_hy_f_b1[0], ev_hy_f_w2[0], ev_hy_f_b2[0], ev_hy_f_wout[0], ev_hy_freq[0])
    y_hy_lat = _hyena_mixer(hy, 0, n_lat, filt_args, ev_hy_conv_w[0], ev_hy_conv_b[0], ev_hy_skip[0])
    y_hy_ctx = _hyena_mixer(hy, n_lat // n_ctx, n_ctx, filt_args, ev_hy_conv_w[0], ev_hy_conv_b[0], ev_hy_skip[0])
    xs = _outproj(y_hy_lat, y_hy_ctx, y_att, xs, mods[0], vec(norm_mix_post[0]), ev_w_out[0].astype(BF16), nlat)
    xs = _ffn_dense(xs, mods[0], vec(norm_ffn_pre[0]), vec(norm_ffn_post[0]), ev_ffn_w_gate[0].astype(BF16),
                    ev_ffn_w_up[0].astype(BF16), ev_ffn_w_down[0].astype(BF16), nlat)

    u2d = _inproj_odd(xs, mods[1], vec(norm_mix_pre[1]), od_w_in[0].astype(BF16), nlat)
    a, w_drive, w_read = _s5_params(od_s5_lambda_re[0], od_s5_lambda_im[0], od_s5_log_step[0],
                                    od_s5_b_re[0], od_s5_b_im[0], od_s5_c_re[0], od_s5_c_im[0])
    y = _s5_scan(u2d.reshape(n_lat + n_ctx, n_batch, D_MODEL), a, w_drive, w_read, n_lat)
    x_lat, hf, gates = _glu(y.reshape(2, n_lat, n_batch * D_MODEL), u2d, vec(od_s5_d[0]), od_glu_w_a[0].astype(BF16),
                            od_glu_w_b[0].astype(BF16), xs, mods[1], vec(norm_mix_post[1]), vec(norm_ffn_pre[1]),
                            od_router[0], n_lat)
    t_rows = n_batch * n_lat
    out = _moe(hf.reshape(t_rows, D_MODEL), gates.reshape(t_rows, LANE), x_lat.reshape(t_rows, D_MODEL), mods[1],
               vec(norm_ffn_post[1]), od_moe_w_gate[0].astype(BF16), od_moe_w_up[0].astype(BF16),
               od_moe_w_down[0].astype(BF16), n_lat)
    return out.reshape(n_batch, n_lat, D_MODEL)
```

```python
import functools
import math

import jax
import jax.numpy as jnp
from jax import lax
from jax.experimental import pallas as pl
from jax.experimental.pallas import tpu as pltpu

F32 = jnp.float32
BF16 = jnp.bfloat16

D_MODEL = 1024
EPS = 1e-6
GRID_W = 64

HY_WIDTH = 512
HY_ORDER = 2
HY_IN = (HY_ORDER + 1) * HY_WIDTH
HY_BANDS = 16
HY_EMB = 2 * HY_BANDS + 1
HY_DECAY_SLOW = -math.log(1e-2) / 1.5
HY_DECAY_FAST = -math.log(1e-2) / 0.3
HEAD_DIM = 64
N_Q_HEADS = 8
N_KV_HEADS = 2
ATTN_WIDTH = N_Q_HEADS * HEAD_DIM
KV_WIDTH = N_KV_HEADS * HEAD_DIM
ROPE_THETA = 10000.0

S5_GROUP = 16
S5_GROUPS = D_MODEL // S5_GROUP
S5_STATE = 64
S5_LANES = S5_GROUPS * S5_STATE
N_EXPERTS = 8

ROW_TILE = 256
FREQ_TILE = 256
S5_CHUNK = 32
MOE_TILE = 512
MOE_FF_TILE = 512
LANE = 128
MIB = 1024 * 1024


def _cparams(sem, vmem_mib=48):
    return pltpu.CompilerParams(dimension_semantics=sem, vmem_limit_bytes=vmem_mib * MIB)


def _rms(x, gain):
    return x * lax.rsqrt(jnp.mean(x * x, axis=-1, keepdims=True) + EPS) * gain


def _bdot(a, b):
    return jnp.dot(a, b, preferred_element_type=F32)


def _full(shape):
    zeros = (0,) * len(shape)
    return pl.BlockSpec(shape, lambda *_: zeros)


def _ada_kernel(cond_ref, w_ref, b_ref, o_ref):
    c = cond_ref[...]
    s = (c * jax.nn.sigmoid(c)).astype(BF16)
    o_ref[...] = _bdot(s, w_ref[...].astype(BF16)) + b_ref[...]


def _ada_params(cond, ada_w, ada_b):
    depth, _, n6 = ada_w.shape
    rows = cond.shape[0]
    tn = 1536
    out = pl.pallas_call(
        _ada_kernel,
        grid=(depth, n6 // tn),
        in_specs=[pl.BlockSpec((rows, D_MODEL), lambda i, j: (0, 0)),
                  pl.BlockSpec((None, D_MODEL, tn), lambda i, j: (i, 0, j)),
                  pl.BlockSpec((None, 1, tn), lambda i, j: (i, 0, j))],
        out_specs=pl.BlockSpec((None, rows, tn), lambda i, j: (i, 0, j)),
        out_shape=jax.ShapeDtypeStruct((depth, rows, n6), F32),
        compiler_params=_cparams(("arbitrary", "arbitrary")),
        name="ada_params",
    )(cond, ada_w, ada_b.reshape(depth, 1, n6))
    return out.reshape(depth, rows, 6, D_MODEL)


def _rope_rotate(t):
    w = t.shape[-1]
    lane = lax.broadcasted_iota(jnp.int32, t.shape, 1)
    first = (lane & (HEAD_DIM - 1)) < HEAD_DIM // 2
    return jnp.where(first, pltpu.roll(t, w - HEAD_DIM // 2, 1), pltpu.roll(t, HEAD_DIM // 2, 1))


def _head_slots(t):
    lane = lax.broadcasted_iota(jnp.int32, t.shape, 1)
    lo = jnp.where(lane < HEAD_DIM, t, 0.0)
    hi = jnp.where(lane >= HEAD_DIM, t, 0.0)
    return jnp.concatenate([lo, pltpu.roll(lo, HEAD_DIM, 1), pltpu.roll(hi, HEAD_DIM, 1), hi], axis=-1)


def _inproj_even_kernel(x_ref, mod_ref, gain_ref, w_ref, qg_ref, kg_ref, e_ref, cos_ref, sin_ref,
                        hy_ref, q_ref, k4_ref, v4_ref):
    h = _rms(x_ref[...], gain_ref[...]) * (1.0 + mod_ref[1:2, :]) + mod_ref[0:1, :]
    p = _bdot(h.astype(BF16), w_ref[...])
    hy_ref[...] = p[:, :HY_IN].astype(hy_ref.dtype)
    q = p[:, HY_IN:HY_IN + ATTN_WIDTH]
    k = p[:, HY_IN + ATTN_WIDTH:HY_IN + ATTN_WIDTH + KV_WIDTH]
    v = p[:, HY_IN + ATTN_WIDTH + KV_WIDTH:]
    e = e_ref[...]
    qn = q * lax.rsqrt(_bdot((q * q).astype(BF16), e) + EPS) * qg_ref[...]
    kn = k * lax.rsqrt(_bdot((k * k).astype(BF16), e[:KV_WIDTH, :KV_WIDTH]) + EPS) * kg_ref[...]
    cos = cos_ref[...]
    sin = sin_ref[...]
    qr = (qn * cos + _rope_rotate(qn) * sin) * (HEAD_DIM ** -0.5)
    kr = kn * cos[:, :KV_WIDTH] + _rope_rotate(kn) * sin[:, :KV_WIDTH]
    q_ref[...] = qr.astype(q_ref.dtype)
    k4_ref[...] = _head_slots(kr).astype(k4_ref.dtype)
    v4_ref[...] = _head_slots(v).astype(v4_ref.dtype)


def _mod_spec(n_batch, nlat):
    return pl.BlockSpec((None, 6, D_MODEL), lambda b, j: (jnp.where(j >= nlat, n_batch, b), 0, 0))


def _row_spec(width):
    return pl.BlockSpec((None, ROW_TILE, width), lambda b, j: (b, j, 0))


def _inproj_even(xs, mod, gain, w_in, q_gain, k_gain, cos_t, sin_t, nlat):
    n_batch, lt, _ = xs.shape
    ntile = lt // ROW_TILE
    n_out = w_in.shape[1]
    head_avg = jnp.kron(jnp.eye(N_Q_HEADS, dtype=F32), jnp.full((HEAD_DIM, HEAD_DIM), 1.0 / HEAD_DIM, F32)).astype(BF16)
    table = pl.BlockSpec((ROW_TILE, ATTN_WIDTH), lambda b, j: (j, 0))
    outs = [jax.ShapeDtypeStruct((n_batch, lt, HY_IN), BF16)] + [jax.ShapeDtypeStruct((n_batch, lt, ATTN_WIDTH), BF16)] * 3
    return pl.pallas_call(
        _inproj_even_kernel,
        grid=(n_batch, ntile),
        in_specs=[_row_spec(D_MODEL), _mod_spec(n_batch, nlat), _full((1, D_MODEL)), _full((D_MODEL, n_out)),
                  _full((1, ATTN_WIDTH)), _full((1, KV_WIDTH)), _full((ATTN_WIDTH, ATTN_WIDTH)), table, table],
        out_specs=[_row_spec(HY_IN), _row_spec(ATTN_WIDTH), _row_spec(ATTN_WIDTH), _row_spec(ATTN_WIDTH)],
        out_shape=outs,
        compiler_params=_cparams(("parallel", "parallel")),
        name="inproj_even",
    )(xs, mod, gain, w_in, q_gain, k_gain, head_avg, cos_t, sin_t)


def _attn_kernel(q_ref, k4_ref, v4_ref, o_ref, *, n_lat, nlat_tiles):
    j = pl.program_id(1)
    n_keys = k4_ref.shape[0]

    def attend(lo):
        for g in range(N_KV_HEADS):
            for p in range(2):
                col = 256 * g + LANE * p
                qp = q_ref[:, col:col + LANE]
                acc = None
                for r in range(2):
                    slot = LANE * (2 * g + r)
                    kk = k4_ref[lo:n_keys, slot:slot + LANE]
                    vv = v4_ref[lo:n_keys, slot:slot + LANE]
                    s = lax.dot_general(qp, kk, (((1,), (1,)), ((), ())), preferred_element_type=F32)
                    e = jnp.exp(s - jnp.max(s, axis=-1, keepdims=True))
                    inv = 1.0 / jnp.sum(e, axis=-1, keepdims=True)
                    o = _bdot(e.astype(BF16), vv) * inv
                    acc = o if acc is None else acc + o
                o_ref[:, col:col + LANE] = acc.astype(o_ref.dtype)

    @pl.when(j < nlat_tiles)
    def _():
        attend(0)

    @pl.when(j >= nlat_tiles)
    def _():
        attend(n_lat)


def _attention(q, k4, v4, n_lat):
    n_batch, lt, _ = q.shape
    ntile = lt // ROW_TILE
    kv_spec = pl.BlockSpec((None, lt, ATTN_WIDTH), lambda b, j: (b, 0, 0))
    return pl.pallas_call(
        functools.partial(_attn_kernel, n_lat=n_lat, nlat_tiles=n_lat // ROW_TILE),
        grid=(n_batch, ntile),
        in_specs=[_row_spec(ATTN_WIDTH), kv_spec, kv_spec],
        out_specs=_row_spec(ATTN_WIDTH),
        out_shape=jax.ShapeDtypeStruct((n_batch, lt, ATTN_WIDTH), BF16),
        compiler_params=_cparams(("parallel", "parallel")),
        name="attention",
    )(q, k4, v4)


def _filter_kernel(z_ref, w1_ref, b1_ref, w2_ref, b2_ref, wo_ref, fr_ref, dl_ref, o_ref):
    hi = lax.Precision.HIGHEST
    z = z_ref[...]
    fr = fr_ref[...]
    h = jnp.sin(fr * (jnp.dot(z, w1_ref[...], precision=hi, preferred_element_type=F32) + b1_ref[...]))
    h = jnp.sin(fr * (jnp.dot(h, w2_ref[...], precision=hi, preferred_element_type=F32) + b2_ref[...]))
    h = jnp.dot(h, wo_ref[...], precision=hi, preferred_element_type=F32)
    decay = jnp.exp(-z[:, 0:1] * dl_ref[...])
    o_ref[...] = h * jnp.concatenate([decay] * (2 * HY_ORDER), axis=-1)


def _hyena_filters(seq, w1, b1, w2, b2, wout, freq):
    t = jnp.linspace(0.0, 1.0, seq, dtype=F32)[:, None]
    bands = jnp.linspace(1e-4, HY_BANDS - 1, HY_BANDS, dtype=F32)
    phase = (2.0 * math.pi / seq) * jnp.arange(seq, dtype=F32)[:, None] * bands
    z = jnp.concatenate([t, jnp.cos(phase), -jnp.sin(phase)], axis=-1)
    z = jnp.pad(z, ((0, 0), (0, LANE - HY_EMB)))
    w1p = jnp.pad(w1, ((0, LANE - HY_EMB), (0, 0)))
    deltas = jnp.linspace(HY_DECAY_SLOW, HY_DECAY_FAST, HY_WIDTH, dtype=F32)[None, :]
    hid = w1.shape[1]
    n_out = wout.shape[1]
    tl = ROW_TILE
    return pl.pallas_call(
        _filter_kernel,
        grid=(seq // tl,),
        in_specs=[pl.BlockSpec((tl, LANE), lambda i: (i, 0)), _full((LANE, hid)), _full((1, hid)), _full((hid, hid)),
                  _full((1, hid)), _full((hid, n_out)), _full((1, hid)), _full((1, HY_WIDTH))],
        out_specs=pl.BlockSpec((tl, n_out), lambda i: (i, 0)),
        out_shape=jax.ShapeDtypeStruct((seq, n_out), F32),
        compiler_params=_cparams(("parallel",)),
        name="hyena_filter",
    )(z, w1p, b1[None, :], w2, b2[None, :], wout, freq[None, :], deltas)


def _dft_tables(seq):
    n = 2 * seq
    f = jnp.arange(seq, dtype=jnp.int32)[:, None]
    t = jnp.arange(seq, dtype=jnp.int32)[None, :]
    ang = ((f * t) % n).astype(F32) * (2.0 * math.pi / n)
    cos = jnp.cos(ang)
    sin = jnp.where(f == 0, jnp.where(t % 2 == 0, 1.0, -1.0), jnp.sin(ang))
    nf = seq // FREQ_TILE
    fwd = jnp.concatenate([cos.reshape(nf, FREQ_TILE, seq), sin.reshape(nf, FREQ_TILE, seq)], axis=1)
    return fwd.astype(BF16), jnp.swapaxes(fwd, 1, 2).astype(BF16)


def _kfreq_kernel(hf_ref, hb_ref, f_ref, kre_ref, ks_ref):
    fi = pl.program_id(1)
    seq = hf_ref.shape[0]
    hf = hf_ref[...]
    row = lax.broadcasted_iota(jnp.int32, hf.shape, 0)
    hb = jnp.where(row == 0, 0.0, hb_ref[...])
    f = f_ref[...]
    a1 = _bdot(f, (hf + hb).astype(BF16))
    a2 = _bdot(f, (hf - hb).astype(BF16))
    tf = FREQ_TILE
    frow = lax.broadcasted_iota(jnp.int32, (tf, hf.shape[1]), 0)
    dc = jnp.logical_and(frow == 0, fi == 0)
    scale = jnp.where(dc, 1.0 / (2 * seq), 2.0 / (2 * seq))
    kre_ref[...] = a1[:tf] * scale
    ks_ref[...] = jnp.where(dc, a1[tf:], a2[tf:]) * scale


def _kfreq(hfilt, fwd_tab):
    seq = hfilt.shape[0]
    nf = seq // FREQ_TILE
    out = jax.ShapeDtypeStruct((HY_ORDER, seq, HY_WIDTH), F32)
    ospec = pl.BlockSpec((None, FREQ_TILE, HY_WIDTH), lambda o, fi: (o, fi, 0))
    return pl.pallas_call(
        _kfreq_kernel,
        grid=(HY_ORDER, nf),
        in_specs=[pl.BlockSpec((seq, HY_WIDTH), lambda o, fi: (0, o)),
                  pl.BlockSpec((seq, HY_WIDTH), lambda o, fi: (0, HY_ORDER + o)),
                  pl.BlockSpec((None, 2 * FREQ_TILE, seq), lambda o, fi: (fi, 0, 0))],
        out_specs=[ospec, ospec],
        out_shape=[out, out],
        compiler_params=_cparams(("parallel", "parallel")),
        name="hyena_kfreq",
    )(hfilt, hfilt, fwd_tab)


def _short_conv(p, w, b):
    n = p.shape[0]
    row = lax.broadcasted_iota(jnp.int32, p.shape, 0)
    prev = jnp.where(row == 0, 0.0, pltpu.roll(p, 1, 0))
    nxt = jnp.where(row == n - 1, 0.0, pltpu.roll(p, n - 1, 0))
    return b + prev * w[0:1, :] + p * w[1:2, :] + nxt * w[2:3, :]


def _hyena_kernel(vsrc_ref, gsrc_ref, cw_ref, cb_ref, skip_ref, f_ref, ft_ref, kre_ref, ks_ref, o_ref,
                  v_scr, vb_scr, acc_scr, *, conv_v):
    fi = pl.program_id(1)
    tf = FREQ_TILE

    @pl.when(fi == 0)
    def _():
        v = vsrc_ref[...].astype(F32)
        if conv_v:
            v = _short_conv(v, cw_ref[0], cb_ref[0])
        v_scr[...] = v
        vb_scr[...] = v.astype(BF16)
        acc_scr[...] = jnp.zeros_like(acc_scr)

    xf = _bdot(f_ref[...], vb_scr[...])
    xre, xs = xf[:tf], xf[tf:]
    kre, ks = kre_ref[...], ks_ref[...]
    row = lax.broadcasted_iota(jnp.int32, xre.shape, 0)
    dc = jnp.logical_and(row == 0, fi == 0)
    yre = jnp.where(dc, xre * kre, xre * kre - xs * ks)
    ys = jnp.where(dc, xs * ks, xre * ks + xs * kre)
    y = jnp.concatenate([yre, ys], axis=0).astype(BF16)
    acc_scr[...] += _bdot(ft_ref[...], y)

    @pl.when(fi == pl.num_programs(1) - 1)
    def _():
        gate = _short_conv(gsrc_ref[...].astype(F32), cw_ref[1], cb_ref[1])
        o_ref[...] = (gate * (acc_scr[...] + v_scr[...] * skip_ref[...])).astype(o_ref.dtype)


def _hyena_order(vsrc, v_blk, v_col, hy, row_blk, order, seq, conv_w, conv_b, skip, fwd_tab, inv_tab, kre, ks):
    n_batch = hy.shape[0]
    nf = seq // FREQ_TILE
    conv_v = order == 0
    cw = jnp.stack([conv_w[:, :HY_WIDTH], conv_w[:, (order + 1) * HY_WIDTH:(order + 2) * HY_WIDTH]])
    cb = jnp.stack([conv_b[None, :HY_WIDTH], conv_b[None, (order + 1) * HY_WIDTH:(order + 2) * HY_WIDTH]])
    kspec = pl.BlockSpec((None, FREQ_TILE, HY_WIDTH), lambda b, fi: (order, fi, 0))
    return pl.pallas_call(
        functools.partial(_hyena_kernel, conv_v=conv_v),
        grid=(n_batch, nf),
        in_specs=[pl.BlockSpec((None, seq, HY_WIDTH), lambda b, fi: (b, v_blk, v_col)),
                  pl.BlockSpec((None, seq, HY_WIDTH), lambda b, fi: (b, row_blk, order + 1)),
                  _full((2, 3, HY_WIDTH)), _full((2, 1, HY_WIDTH)), _full((1, HY_WIDTH)),
                  pl.BlockSpec((None, 2 * FREQ_TILE, seq), lambda b, fi: (fi, 0, 0)),
                  pl.BlockSpec((None, seq, 2 * FREQ_TILE), lambda b, fi: (fi, 0, 0)),
                  kspec, kspec],
        out_specs=pl.BlockSpec((None, seq, HY_WIDTH), lambda b, fi: (b, 0, 0)),
        out_shape=jax.ShapeDtypeStruct((n_batch, seq, HY_WIDTH), BF16),
        scratch_shapes=[pltpu.VMEM((seq, HY_WIDTH), F32), pltpu.VMEM((seq, HY_WIDTH), BF16),
                        pltpu.VMEM((seq, HY_WIDTH), F32)],
        compiler_params=_cparams(("parallel", "arbitrary")),
        name="hyena_order%d" % order,
    )(vsrc, hy, cw, cb, skip[order][None, :], fwd_tab, inv_tab, kre, ks)


def _hyena_mixer(hy, row_blk, seq, filt_args, conv_w, conv_b, skip):
    hfilt = _hyena_filters(seq, *filt_args)
    fwd_tab, inv_tab = _dft_tables(seq)
    kre, ks = _kfreq(hfilt, fwd_tab)
    v1 = _hyena_order(hy, row_blk, 0, hy, row_blk, 0, seq, conv_w, conv_b, skip, fwd_tab, inv_tab, kre, ks)
    return _hyena_order(v1, 0, 0, hy, row_blk, 1, seq, conv_w, conv_b, skip, fwd_tab, inv_tab, kre, ks)


def _outproj_kernel(yl_ref, yc_ref, ya_ref, x_ref, mod_ref, gpost_ref, w_ref, o_ref, *, nlat):
    j = pl.program_id(1)
    yh = jnp.where(j >= nlat, yc_ref[...], yl_ref[...])
    out = _bdot(yh, w_ref[:HY_WIDTH, :]) + _bdot(ya_ref[...], w_ref[HY_WIDTH:, :])
    o_ref[...] = x_ref[...] + mod_ref[2:3, :] * _rms(out, gpost_ref[...])


def _outproj(y_lat, y_ctx, y_att, xs, mod, gpost, w_out, nlat):
    n_batch, lt, _ = xs.shape
    ntile = lt // ROW_TILE
    nctx = ntile - nlat
    return pl.pallas_call(
        functools.partial(_outproj_kernel, nlat=nlat),
        grid=(n_batch, ntile),
        in_specs=[pl.BlockSpec((None, ROW_TILE, HY_WIDTH), lambda b, j: (b, jnp.minimum(j, nlat - 1), 0)),
                  pl.BlockSpec((None, ROW_TILE, HY_WIDTH), lambda b, j: (b, jnp.clip(j - nlat, 0, nctx - 1), 0)),
                  _row_spec(ATTN_WIDTH), _row_spec(D_MODEL), _mod_spec(n_batch, nlat), _full((1, D_MODEL)),
                  _full(w_out.shape)],
        out_specs=_row_spec(D_MODEL),
        out_shape=jax.ShapeDtypeStruct(xs.shape, F32),
        compiler_params=_cparams(("parallel", "parallel")),
        name="outproj_even",
    )(y_lat, y_ctx, y_att, xs, mod, gpost, w_out)


def _ffn_kernel(x_ref, mod_ref, gpre_ref, gpost_ref, wg_ref, wu_ref, wd_ref, o_ref):
    x = x_ref[...]
    h = (_rms(x, gpre_ref[...]) * (1.0 + mod_ref[4:5, :]) + mod_ref[3:4, :]).astype(BF16)
    g = _bdot(h, wg_ref[...])
    u = _bdot(h, wu_ref[...])
    a = (g * jax.nn.sigmoid(g) * u).astype(BF16)
    y = _bdot(a, wd_ref[...])
    o_ref[...] = x + mod_ref[5:6, :] * _rms(y, gpost_ref[...])


def _ffn_dense(xs, mod, gpre, gpost, wg, wu, wd, nlat):
    n_batch, lt, _ = xs.shape
    return pl.pallas_call(
        _ffn_kernel,
        grid=(n_batch, lt // ROW_TILE),
        in_specs=[_row_spec(D_MODEL), _mod_spec(n_batch, nlat), _full((1, D_MODEL)), _full((1, D_MODEL)),
                  _full(wg.shape), _full(wu.shape), _full(wd.shape)],
        out_specs=_row_spec(D_MODEL),
        out_shape=jax.ShapeDtypeStruct(xs.shape, F32),
        compiler_params=_cparams(("parallel", "parallel"), 56),
        name="ffn_dense",
    )(xs, mod, gpre, gpost, wg, wu, wd)


def _inproj_odd_kernel(x_ref, mod_ref, gain_ref, w_ref, u_ref):
    h = _rms(x_ref[...], gain_ref[...]) * (1.0 + mod_ref[1:2, :]) + mod_ref[0:1, :]
    u_ref[...] = _bdot(h.astype(BF16), w_ref[...])


def _inproj_odd(xs, mod, gain, w_in, nlat):
    n_batch, lt, _ = xs.shape
    return pl.pallas_call(
        _inproj_odd_kernel,
        grid=(n_batch, lt // ROW_TILE),
        in_specs=[_row_spec(D_MODEL), _mod_spec(n_batch, nlat), _full((1, D_MODEL)), _full(w_in.shape)],
        out_specs=pl.BlockSpec((ROW_TILE, D_MODEL), lambda b, j: (j, b)),
        out_shape=jax.ShapeDtypeStruct((lt, n_batch * D_MODEL), F32),
        compiler_params=_cparams(("parallel", "parallel")),
        name="inproj_odd",
    )(xs, mod, gain, w_in)


def _s5_param_kernel(lr_ref, li_ref, ls_ref, lrx_ref, lix_ref, lsx_ref, br_ref, bi_ref,
                     abr_ref, abi_ref, bbr_ref, bbi_ref):
    def zoh(lr_raw, li, log_step):
        lr = jnp.minimum(lr_raw, -1e-4)
        dt = jnp.exp(log_step)
        mag = jnp.exp(lr * dt)
        ab_re = mag * jnp.cos(li * dt)
        ab_im = mag * jnp.sin(li * dt)
        den = lr * lr + li * li
        nr, ni = ab_re - 1.0, ab_im
        return ab_re, ab_im, (nr * lr + ni * li) / den, (ni * lr - nr * li) / den

    ab_re, ab_im, _, _ = zoh(lr_ref[...], li_ref[...], ls_ref[...])
    abr_ref[...] = ab_re
    abi_ref[...] = ab_im
    _, _, co_re, co_im = zoh(lrx_ref[...], lix_ref[...], lsx_ref[...])
    br, bi = br_ref[...], bi_ref[...]
    bbr_ref[...] = co_re * br - co_im * bi
    bbi_ref[...] = co_re * bi + co_im * br


def _s5_params(lam_re, lam_im, log_step, b_re, b_im, c_re, c_im):
    nd, g, n = lam_re.shape
    k = S5_GROUP
    rep = lambda a: jnp.repeat(a, k, axis=1)
    ls = log_step[:, :, None]
    bt_re = jnp.swapaxes(b_re, 2, 3).reshape(nd, g * k, n)
    bt_im = jnp.swapaxes(b_im, 2, 3).reshape(nd, g * k, n)
    small = jax.ShapeDtypeStruct((nd, g, n), F32)
    big = jax.ShapeDtypeStruct((nd, g * k, n), F32)
    ab_re, ab_im, bb_re, bb_im = pl.pallas_call(
        _s5_param_kernel, out_shape=[small, small, big, big], name="s5_discretise",
    )(lam_re, lam_im, ls, rep(lam_re), rep(lam_im), rep(ls), bt_re, bt_im)
    a = jnp.stack([ab_re.reshape(nd, g * n), ab_im.reshape(nd, g * n)], axis=1)
    eye = jnp.eye(8, dtype=F32)
    nq = g // 8

    def drive_blocks(bb):
        return jnp.einsum('dqgkn,gh->dqgkhn', bb.reshape(nd, nq, 8, k, n), eye).reshape(nd, nq, 8 * k, 8 * n)

    def read_blocks(c):
        return jnp.einsum('dqgin,gh->dqhngi', c.reshape(nd, nq, 8, k, n), eye).reshape(nd, nq, 8 * n, 8 * k)

    w_drive = jnp.concatenate([drive_blocks(bb_re), drive_blocks(bb_im)], axis=-1).astype(BF16)
    w_read = jnp.stack([read_blocks(c_re), read_blocks(-c_im)], axis=2).astype(BF16)
    return a, w_drive, w_read


def _s5_scan_kernel(u_ref, a_ref, wd_ref, wr_ref, y_ref, sbuf, state, *, nctx_chunks):
    d = pl.program_id(0)
    i = pl.program_id(1)
    p_steps, n_batch, _ = u_ref.shape
    rows = p_steps * n_batch
    half = S5_LANES
    nq = wd_ref.shape[0]
    kq = wd_ref.shape[1]
    sq = wd_ref.shape[2] // 2

    @pl.when(i == 0)
    def _():
        state[...] = jnp.zeros_like(state)

    u = u_ref[...].reshape(rows, D_MODEL).astype(BF16)
    for q in range(nq):
        r = _bdot(u[:, kq * q:kq * (q + 1)], wd_ref[q])
        sbuf[:, :, sq * q:sq * (q + 1)] = r[:, :sq].reshape(p_steps, n_batch, sq)
        sbuf[:, :, half + sq * q:half + sq * (q + 1)] = r[:, sq:].reshape(p_steps, n_batch, sq)

    for q in range(nq):
        lo = sq * q
        ar = jnp.broadcast_to(a_ref[0:1, lo:lo + sq], (n_batch, sq))
        ai = jnp.broadcast_to(a_ref[1:2, lo:lo + sq], (n_batch, sq))

        def body(t, carry, lo=lo, ar=ar, ai=ai):
            sr, si = carry
            tt = jnp.where(d == 0, t, p_steps - 1 - t)
            nr = ar * sr - ai * si + sbuf[tt, :, lo:lo + sq]
            ni = ar * si + ai * sr + sbuf[tt, :, half + lo:half + lo + sq]
            sbuf[tt, :, lo:lo + sq] = nr
            sbuf[tt, :, half + lo:half + lo + sq] = ni
            return nr, ni

        sr, si = lax.fori_loop(0, p_steps, body, (state[:, lo:lo + sq], state[:, half + lo:half + lo + sq]))
        state[:, lo:lo + sq] = sr
        state[:, half + lo:half + lo + sq] = si

    @pl.when(i >= nctx_chunks)
    def _():
        s = sbuf[...].reshape(rows, 2 * half).astype(BF16)
        for q in range(nq):
            yq = (_bdot(s[:, sq * q:sq * (q + 1)], wr_ref[q, 0])
                  + _bdot(s[:, half + sq * q:half + sq * (q + 1)], wr_ref[q, 1]))
            y_ref[:, :, kq * q:kq * (q + 1)] = yq.reshape(p_steps, n_batch, kq).astype(y_ref.dtype)


def _s5_scan(u3, a, w_drive, w_read, n_lat):
    lt, n_batch, _ = u3.shape
    p = S5_CHUNK
    nchunk = lt // p
    nlatc = n_lat // p
    nctxc = nchunk - nlatc

    def u_map(d, i):
        return (jnp.where(d == 0, lax.rem(i + nlatc, nchunk), nchunk - 1 - i), 0, 0)

    def y_map(d, i):
        return (d, jnp.where(d == 0, jnp.maximum(i - nctxc, 0), jnp.minimum(nchunk - 1 - i, nlatc - 1)), 0, 0)

    return pl.pallas_call(
        functools.partial(_s5_scan_kernel, nctx_chunks=nctxc),
        grid=(2, nchunk),
        in_specs=[pl.BlockSpec((p, n_batch, D_MODEL), u_map),
                  pl.BlockSpec((None, 2, S5_LANES), lambda d, i: (d, 0, 0)),
                  pl.BlockSpec((None,) + w_drive.shape[1:], lambda d, i: (d, 0, 0, 0)),
                  pl.BlockSpec((None,) + w_read.shape[1:], lambda d, i: (d, 0, 0, 0, 0))],
        out_specs=pl.BlockSpec((None, p, n_batch, D_MODEL), y_map),
        out_shape=jax.ShapeDtypeStruct((2, n_lat, n_batch, D_MODEL), F32),
        scratch_shapes=[pltpu.VMEM((p, n_batch, 2 * S5_LANES), F32), pltpu.VMEM((n_batch, 2 * S5_LANES), F32)],
        compiler_params=_cparams(("arbitrary", "arbitrary")),
        name="s5_scan",
    )(u3, a, w_drive, w_read)


def _glu_kernel(yf_ref, yb_ref, u_ref, dskip_ref, wa_ref, wb_ref, x_ref, mod_ref, gpost_ref, gpre_ref, router_ref,
                tri_ref, xo_ref, hf_ref, rw_ref, ri_ref, cnt_ref, carry):
    @pl.when(jnp.logical_and(pl.program_id(0) == 0, pl.program_id(1) == 0))
    def _():
        carry[...] = jnp.zeros_like(carry)

    y = yf_ref[...] + yb_ref[...] + dskip_ref[...] * u_ref[...]
    z = jax.nn.gelu(y).astype(BF16)
    out = _bdot(z, wa_ref[...]) * jax.nn.sigmoid(_bdot(z, wb_ref[...]))
    xn = x_ref[...] + mod_ref[2:3, :] * _rms(out, gpost_ref[...])
    xo_ref[...] = xn
    hf = _rms(xn, gpre_ref[...]) * (1.0 + mod_ref[4:5, :]) + mod_ref[3:4, :]
    hf_ref[...] = hf
    h_hi = hf.astype(BF16)
    h_lo = (hf - h_hi.astype(F32)).astype(BF16)
    part = _bdot(h_hi, router_ref[...])
    logits = part[:, :LANE] + part[:, LANE:] + _bdot(h_lo, router_ref[:, :LANE])
    lane = lax.broadcasted_iota(jnp.int32, logits.shape, 1)
    neg = jnp.float32(-jnp.inf)
    lg = jnp.where(lane < N_EXPERTS, logits, neg)
    m1 = jnp.max(lg, axis=-1, keepdims=True)
    i1 = jnp.min(jnp.where(lg == m1, lane, LANE), axis=-1, keepdims=True)
    lg2 = jnp.where(lane == i1, neg, lg)
    m2 = jnp.max(lg2, axis=-1, keepdims=True)
    i2 = jnp.min(jnp.where(lg2 == m2, lane, LANE), axis=-1, keepdims=True)
    e2 = jnp.exp(m2 - m1)
    w1 = 1.0 / (1.0 + e2)
    rw_ref[...] = jnp.where(lane == 0, w1, jnp.where(lane == 1, e2 * w1, 0.0))
    member = jnp.where(lane == i1, 1.0, jnp.where(lane == i2, 1.0, 0.0))
    base = carry[...] + _bdot(tri_ref[...], member.astype(BF16))
    r1 = jnp.sum(jnp.where(lane == i1, base, 0.0), axis=-1, keepdims=True).astype(jnp.int32)
    r2 = jnp.sum(jnp.where(lane == i2, base, 0.0), axis=-1, keepdims=True).astype(jnp.int32)
    ri_ref[...] = jnp.where(lane == 0, i1, jnp.where(lane == 1, i2, jnp.where(lane == 2, r1, jnp.where(lane == 3, r2, 0))))
    carry[...] += jnp.sum(member, axis=0, keepdims=True)
    cnt_ref[...] = carry[...]


def _glu(y2, u2d, d_skip, w_a, w_b, xs, mod, gpost, gpre, router, n_lat):
    n_batch = xs.shape[0]
    nlat = n_lat // ROW_TILE
    router_p = jnp.pad(router, ((0, 0), (0, LANE - router.shape[1])))
    router_hi = router_p.astype(BF16)
    router_cat = jnp.concatenate([router_hi, (router_p - router_hi.astype(F32)).astype(BF16)], axis=1)
    tri = (jnp.arange(ROW_TILE)[:, None] > jnp.arange(ROW_TILE)[None, :]).astype(BF16)
    tm_spec = lambda w: pl.BlockSpec((None, ROW_TILE, w), lambda b, j: (b, j, 0))
    return pl.pallas_call(
        _glu_kernel,
        grid=(n_batch, nlat),
        in_specs=[pl.BlockSpec((None, ROW_TILE, D_MODEL), lambda b, j: (0, j, b)),
                  pl.BlockSpec((None, ROW_TILE, D_MODEL), lambda b, j: (1, j, b)),
                  pl.BlockSpec((ROW_TILE, D_MODEL), lambda b, j: (j, b)),
                  _full((1, D_MODEL)), _full(w_a.shape), _full(w_b.shape), tm_spec(D_MODEL),
                  pl.BlockSpec((None, 6, D_MODEL), lambda b, j: (b, 0, 0)), _full((1, D_MODEL)), _full((1, D_MODEL)),
                  _full(router_cat.shape), _full(tri.shape)],
        out_specs=[tm_spec(D_MODEL), tm_spec(D_MODEL), tm_spec(LANE), tm_spec(LANE), _full((1, LANE))],
        out_shape=[jax.ShapeDtypeStruct((n_batch, n_lat, D_MODEL), F32),
                   jax.ShapeDtypeStruct((n_batch, n_lat, D_MODEL), F32),
                   jax.ShapeDtypeStruct((n_batch, n_lat, LANE), F32),
                   jax.ShapeDtypeStruct((n_batch, n_lat, LANE), jnp.int32),
                   jax.ShapeDtypeStruct((1, LANE), F32)],
        scratch_shapes=[pltpu.VMEM((1, LANE), F32)],
        compiler_params=_cparams(("arbitrary", "arbitrary")),
        name="s5_glu_router",
    )(y2, y2, u2d, d_skip, w_a, w_b, xs, mod, gpost, gpre, router_cat, tri)


def _moe_plan(ri, counts, n_tiles):
    experts = jnp.arange(N_EXPERTS, dtype=jnp.int32)
    n_of = (counts[0, :N_EXPERTS].astype(jnp.int32) + MOE_TILE - 1) // MOE_TILE
    ends = jnp.cumsum(n_of)
    starts = ends - n_of
    start_of = jnp.sum(jnp.where(ri[:, 0:2, None] == experts, starts, 0), axis=-1)
    pos = start_of * MOE_TILE + ri[:, 2:4]
    n_used = ends[-1]
    tile = jnp.arange(n_tiles, dtype=jnp.int32)
    tile_expert = jnp.sum((jnp.minimum(tile, n_used - 1)[:, None] >= ends[None, :]).astype(jnp.int32), axis=1)
    return pos, tile_expert, n_used.reshape(1)


def _dispatch_kernel(pos_ref, hf_ref, zeros_ref, xs_ref, sem):
    del zeros_ref
    base = pl.program_id(0) * ROW_TILE

    def row_copy(r, k):
        return pltpu.make_async_copy(hf_ref.at[pl.ds(base + r, 1)], xs_ref.at[pl.ds(pos_ref[k, r], 1)], sem)

    def issue(r, c):
        row_copy(r, 0).start()
        row_copy(r, 1).start()
        return c

    def drain(r, c):
        row_copy(r, 0).wait()
        row_copy(r, 1).wait()
        return c

    lax.fori_loop(0, ROW_TILE, issue, 0)
    lax.fori_loop(0, ROW_TILE, drain, 0)


def _moe_group_kernel(te_ref, nu_ref, xs_ref, wg_ref, wu_ref, wd_ref, ys_ref, acc):
    del te_ref
    i = pl.program_id(0)
    j = pl.program_id(1)

    @pl.when(i < nu_ref[0])
    def _():
        @pl.when(j == 0)
        def _():
            acc[...] = jnp.zeros_like(acc)

        h = xs_ref[...].astype(BF16)
        g = _bdot(h, wg_ref[...])
        u = _bdot(h, wu_ref[...])
        a = (g * jax.nn.sigmoid(g) * u).astype(BF16)
        acc[...] += _bdot(a, wd_ref[...])

        @pl.when(j == pl.num_programs(1) - 1)
        def _():
            ys_ref[...] = acc[...]

    @pl.when(jnp.logical_and(i >= nu_ref[0], j == 0))
    def _():
        ys_ref[...] = jnp.zeros_like(ys_ref)


def _combine_kernel(pos_ref, ys_ref, rw_ref, x_ref, mod_ref, gpost_ref, o_ref, buf, sem):
    def row_copy(r, k):
        return pltpu.make_async_copy(ys_ref.at[pl.ds(pos_ref[k, r], 1)], buf.at[k, pl.ds(r, 1)], sem)

    def issue(r, c):
        row_copy(r, 0).start()
        row_copy(r, 1).start()
        return c

    def drain(r, c):
        row_copy(r, 0).wait()
        row_copy(r, 1).wait()
        return c

    lax.fori_loop(0, ROW_TILE, issue, 0)
    lax.fori_loop(0, ROW_TILE, drain, 0)
    rw = rw_ref[...]
    y = rw[:, 0:1] * buf[0] + rw[:, 1:2] * buf[1]
    o_ref[...] = x_ref[...] + mod_ref[5:6, :] * _rms(y, gpost_ref[...])


def _moe(hf, rw, ri, counts, xs, mod, gpost, wg, wu, wd, nlat):
    t_rows = hf.shape[0]
    d_ff = wg.shape[2]
    n_tiles = 2 * t_rows // MOE_TILE + N_EXPERTS
    nff = d_ff // MOE_FF_TILE
    n_row_tiles = t_rows // ROW_TILE
    pos, tile_expert, n_used = _moe_plan(ri, counts, n_tiles)
    pos_t = jnp.swapaxes(pos.reshape(n_row_tiles, ROW_TILE, 2), 1, 2)
    pos_spec = pl.BlockSpec((None, 2, ROW_TILE), lambda i: (i, 0, 0), memory_space=pltpu.SMEM)
    any_spec = pl.BlockSpec(memory_space=pl.ANY)
    sorted_rows = jax.ShapeDtypeStruct((n_tiles * MOE_TILE, D_MODEL), F32)

    xs_sorted = pl.pallas_call(
        _dispatch_kernel,
        grid=(n_row_tiles,),
        in_specs=[pos_spec, any_spec, any_spec],
        out_specs=any_spec,
        out_shape=sorted_rows,
        input_output_aliases={2: 0},
        scratch_shapes=[pltpu.SemaphoreType.DMA(())],
        compiler_params=_cparams(("arbitrary",)),
        name="moe_dispatch",
    )(pos_t, hf, jnp.zeros(sorted_rows.shape, F32))

    def valid_j(i, j, nu):
        return jnp.where(i < nu[0], j, nff - 1)

    ys_sorted = pl.pallas_call(
        _moe_group_kernel,
        grid_spec=pltpu.PrefetchScalarGridSpec(
            num_scalar_prefetch=2,
            grid=(n_tiles, nff),
            in_specs=[pl.BlockSpec((MOE_TILE, D_MODEL), lambda i, j, te, nu: (jnp.minimum(i, nu[0] - 1), 0)),
                      pl.BlockSpec((None, D_MODEL, MOE_FF_TILE), lambda i, j, te, nu: (te[i], 0, valid_j(i, j, nu))),
                      pl.BlockSpec((None, D_MODEL, MOE_FF_TILE), lambda i, j, te, nu: (te[i], 0, valid_j(i, j, nu))),
                      pl.BlockSpec((None, MOE_FF_TILE, D_MODEL), lambda i, j, te, nu: (te[i], valid_j(i, j, nu), 0))],
            out_specs=pl.BlockSpec((MOE_TILE, D_MODEL), lambda i, j, te, nu: (i, 0)),
            scratch_shapes=[pltpu.VMEM((MOE_TILE, D_MODEL), F32)]),
        out_shape=sorted_rows,
        compiler_params=_cparams(("arbitrary", "arbitrary")),
        name="moe_experts",
    )(tile_expert, n_used, xs_sorted, wg, wu, wd)

    row = lambda w: pl.BlockSpec((ROW_TILE, w), lambda i: (i, 0))
    return pl.pallas_call(
        _combine_kernel,
        grid=(n_row_tiles,),
        in_specs=[pos_spec, any_spec, row(LANE), row(D_MODEL),
                  pl.BlockSpec((None, 6, D_MODEL), lambda i: (lax.div(i, nlat), 0, 0)), _full((1, D_MODEL))],
        out_specs=row(D_MODEL),
        out_shape=jax.ShapeDtypeStruct((t_rows, D_MODEL), F32),
        scratch_shapes=[pltpu.VMEM((2, ROW_TILE, D_MODEL), F32), pltpu.SemaphoreType.DMA(())],
        compiler_params=_cparams(("arbitrary",)),
        name="moe_combine",
    )(pos_t, ys_sorted, rw, xs, mod, gpost)


def _rope_tables(n_lat, n_ctx):
    rows = n_lat // GRID_W
    row = jnp.repeat(jnp.arange(rows, dtype=F32), GRID_W)
    col = jnp.tile(jnp.arange(GRID_W, dtype=F32), rows)
    n_freq = HEAD_DIM // 4
    inv = ROPE_THETA ** (-jnp.arange(n_freq, dtype=F32) / n_freq)
    ang = jnp.concatenate([row[:, None] * inv, col[:, None] * inv], axis=-1)
    cos, sin = jnp.cos(ang), jnp.sin(ang)
    cos_h = jnp.concatenate([cos, cos], axis=-1)
    sin_h = jnp.concatenate([-sin, sin], axis=-1)
    cos_t = jnp.concatenate([jnp.tile(cos_h, (1, N_Q_HEADS)), jnp.ones((n_ctx, ATTN_WIDTH), F32)], axis=0)
    sin_t = jnp.concatenate([jnp.tile(sin_h, (1, N_Q_HEADS)), jnp.zeros((n_ctx, ATTN_WIDTH), F32)], axis=0)
    return cos_t, sin_t


def kernel(x, c, ctx, c_ctx, ada_w, ada_b, norm_mix_pre, norm_mix_post, norm_ffn_pre, norm_ffn_post, ev_w_in, ev_hy_conv_w, ev_hy_conv_b, ev_hy_f_w1, ev_hy_f_b1, ev_hy_f_w2, ev_hy_f_b2, ev_hy_f_wout, ev_hy_freq, ev_hy_skip, ev_q_norm, ev_k_norm, ev_w_out, ev_ffn_w_gate, ev_ffn_w_up, ev_ffn_w_down, od_w_in, od_s5_lambda_re, od_s5_lambda_im, od_s5_log_step, od_s5_b_re, od_s5_b_im, od_s5_c_re, od_s5_c_im, od_s5_d, od_glu_w_a, od_glu_w_b, od_router, od_moe_w_gate, od_moe_w_up, od_moe_w_down):
    n_batch, n_lat, _ = x.shape
    n_ctx = ctx.shape[1]
    depth = ada_w.shape[0]
    assert n_batch == 8 and n_lat % ROW_TILE == 0 and n_ctx % ROW_TILE == 0 and n_lat % n_ctx == 0
    assert depth == 2
    nlat = n_lat // ROW_TILE

    cond = jnp.concatenate([c, c_ctx[None, :], jnp.zeros((16 - n_batch - 1, D_MODEL), F32)], axis=0)
    mods = _ada_params(cond, ada_w, ada_b)
    xs = jnp.concatenate([x, ctx], axis=1)
    vec = lambda a: a[None, :]

    cos_t, sin_t = _rope_tables(n_lat, n_ctx)
    hy, q, k4, v4 = _inproj_even(xs, mods[0], vec(norm_mix_pre[0]), ev_w_in[0].astype(BF16),
                                 vec(jnp.tile(ev_q_norm[0], N_Q_HEADS)), vec(jnp.tile(ev_k_norm[0], N_KV_HEADS)),
                                 cos_t, sin_t, nlat)
    y_att = _attention(q, k4, v4, n_lat)
    filt_args = (ev_hy_f_w1[0], ev_hy_f_b1[0], ev_hy_f_w2[0], ev_hy_f_b2[0], ev_hy_f_wout[0], ev_hy_freq[0])
    y_hy_lat = _hyena_mixer(hy, 0, n_lat, filt_args, ev_hy_conv_w[0], ev_hy_conv_b[0], ev_hy_skip[0])
    y_hy_ctx = _hyena_mixer(hy, n_lat // n_ctx, n_ctx, filt_args, ev_hy_conv_w[0], ev_hy_conv_b[0], ev_hy_skip[0])
    xs = _outproj(y_hy_lat, y_hy_ctx, y_att, xs, mods[0], vec(norm_mix_post[0]), ev_w_out[0].astype(BF16), nlat)
    xs = _ffn_dense(xs, mods[0], vec(norm_ffn_pre[0]), vec(norm_ffn_post[0]), ev_ffn_w_gate[0].astype(BF16),
                    ev_ffn_w_up[0].astype(BF16), ev_ffn_w_down[0].astype(BF16), nlat)

    u2d = _inproj_odd(xs, mods[1], vec(norm_mix_pre[1]), od_w_in[0].astype(BF16), nlat)
    a, w_drive, w_read = _s5_params(od_s5_lambda_re[0], od_s5_lambda_im[0], od_s5_log_step[0],
                                    od_s5_b_re[0], od_s5_b_im[0], od_s5_c_re[0], od_s5_c_im[0])
    y = _s5_scan(u2d.reshape(n_lat + n_ctx, n_batch, D_MODEL), a, w_drive, w_read, n_lat)
    x_lat, hf, rw, ri, counts = _glu(y.reshape(2, n_lat, n_batch * D_MODEL), u2d, vec(od_s5_d[0]),
                                     od_glu_w_a[0].astype(BF16), od_glu_w_b[0].astype(BF16), xs, mods[1],
                                     vec(norm_mix_post[1]), vec(norm_ffn_pre[1]), od_router[0], n_lat)
    t_rows = n_batch * n_lat
    out = _moe(hf.reshape(t_rows, D_MODEL), rw.reshape(t_rows, LANE), ri.reshape(t_rows, LANE), counts,
               x_lat.reshape(t_rows, D_MODEL), mods[1], vec(norm_ffn_post[1]), od_moe_w_gate[0].astype(BF16),
               od_moe_w_up[0].astype(BF16), od_moe_w_down[0].astype(BF16), nlat)
    return out.reshape(n_batch, n_lat, D_MODEL)
```

```python
import functools
import math

import jax
import jax.numpy as jnp
from jax import lax
from jax.experimental import pallas as pl
from jax.experimental.pallas import tpu as pltpu

F32 = jnp.float32
BF16 = jnp.bfloat16

D_MODEL = 1024
EPS = 1e-6
GRID_W = 64

HY_WIDTH = 512
HY_ORDER = 2
HY_IN = (HY_ORDER + 1) * HY_WIDTH
HY_BANDS = 16
HY_EMB = 2 * HY_BANDS + 1
HY_DECAY_SLOW = -math.log(1e-2) / 1.5
HY_DECAY_FAST = -math.log(1e-2) / 0.3
HEAD_DIM = 64
N_Q_HEADS = 8
N_KV_HEADS = 2
Q_PER_KV = N_Q_HEADS // N_KV_HEADS
ATTN_WIDTH = N_Q_HEADS * HEAD_DIM
KV_WIDTH = N_KV_HEADS * HEAD_DIM
ROPE_THETA = 10000.0
LOG2_E = 1.0 / math.log(2.0)

S5_GROUP = 16
S5_GROUPS = D_MODEL // S5_GROUP
S5_STATE = 64
S5_LANES = S5_GROUPS * S5_STATE
N_EXPERTS = 8

ROW_TILE = 256
FREQ_TILE = 256
S5_CHUNK = 32
MOE_TILE = 512
MOE_FF_TILE = 896
LANE = 128
MIB = 1024 * 1024


def _cparams(sem, vmem_mib=48):
    return pltpu.CompilerParams(dimension_semantics=sem, vmem_limit_bytes=vmem_mib * MIB)


def _rms(x, gain):
    return x * lax.rsqrt(jnp.mean(x * x, axis=-1, keepdims=True) + EPS) * gain


def _bdot(a, b):
    return jnp.dot(a, b, preferred_element_type=F32)


def _full(shape):
    zeros = (0,) * len(shape)
    return pl.BlockSpec(shape, lambda *_: zeros)


def _ada_kernel(cond_ref, w_ref, b_ref, o_ref):
    c = cond_ref[...]
    s = (c * jax.nn.sigmoid(c)).astype(BF16)
    o_ref[...] = _bdot(s, w_ref[...].astype(BF16)) + b_ref[...]


def _ada_params(cond, ada_w, ada_b):
    depth, _, n6 = ada_w.shape
    rows = cond.shape[0]
    tn = 1536
    out = pl.pallas_call(
        _ada_kernel,
        grid=(depth, n6 // tn),
        in_specs=[pl.BlockSpec((rows, D_MODEL), lambda i, j: (0, 0)),
                  pl.BlockSpec((None, D_MODEL, tn), lambda i, j: (i, 0, j)),
                  pl.BlockSpec((None, 1, tn), lambda i, j: (i, 0, j))],
        out_specs=pl.BlockSpec((None, rows, tn), lambda i, j: (i, 0, j)),
        out_shape=jax.ShapeDtypeStruct((depth, rows, n6), F32),
        compiler_params=_cparams(("arbitrary", "arbitrary")),
        name="ada_params",
    )(cond, ada_w, ada_b.reshape(depth, 1, n6))
    return out.reshape(depth, rows, 6, D_MODEL)


def _rope_rotate(t):
    w = t.shape[-1]
    lane = lax.broadcasted_iota(jnp.int32, t.shape, 1)
    first = (lane & (HEAD_DIM - 1)) < HEAD_DIM // 2
    return jnp.where(first, pltpu.roll(t, w - HEAD_DIM // 2, 1), pltpu.roll(t, HEAD_DIM // 2, 1))


def _head_slots(t):
    lane = lax.broadcasted_iota(jnp.int32, t.shape, 1)
    lo = jnp.where(lane < HEAD_DIM, t, 0.0)
    hi = jnp.where(lane >= HEAD_DIM, t, 0.0)
    return jnp.concatenate([lo, pltpu.roll(lo, HEAD_DIM, 1), pltpu.roll(hi, HEAD_DIM, 1), hi], axis=-1)


def _inproj_even_kernel(x_ref, mod_ref, gain_ref, w_ref, qg_ref, kg_ref, e_ref, cos_ref, sin_ref,
                        hy_ref, q_ref, k4_ref, v4_ref):
    h = _rms(x_ref[...], gain_ref[...]) * (1.0 + mod_ref[1:2, :]) + mod_ref[0:1, :]
    p = _bdot(h.astype(BF16), w_ref[...])
    hy_ref[...] = p[:, :HY_IN].astype(hy_ref.dtype)
    q = p[:, HY_IN:HY_IN + ATTN_WIDTH]
    k = p[:, HY_IN + ATTN_WIDTH:HY_IN + ATTN_WIDTH + KV_WIDTH]
    v = p[:, HY_IN + ATTN_WIDTH + KV_WIDTH:]
    e = e_ref[...]
    qn = q * lax.rsqrt(_bdot((q * q).astype(BF16), e) + EPS) * qg_ref[...]
    kn = k * lax.rsqrt(_bdot((k * k).astype(BF16), e[:KV_WIDTH, :KV_WIDTH]) + EPS) * kg_ref[...]
    cos = cos_ref[...]
    sin = sin_ref[...]
    qr = (qn * cos + _rope_rotate(qn) * sin) * (HEAD_DIM ** -0.5 * LOG2_E)
    kr = kn * cos[:, :KV_WIDTH] + _rope_rotate(kn) * sin[:, :KV_WIDTH]
    q_ref[...] = qr.astype(q_ref.dtype)
    k4_ref[...] = _head_slots(kr).astype(k4_ref.dtype)
    v4_ref[...] = _head_slots(v).astype(v4_ref.dtype)


def _mod_spec(n_batch, nlat):
    return pl.BlockSpec((None, 6, D_MODEL), lambda b, j: (jnp.where(j >= nlat, n_batch, b), 0, 0))


def _row_spec(width):
    return pl.BlockSpec((None, ROW_TILE, width), lambda b, j: (b, j, 0))


def _inproj_even(xs, mod, gain, w_in, q_gain, k_gain, cos_t, sin_t, nlat):
    n_batch, lt, _ = xs.shape
    ntile = lt // ROW_TILE
    n_out = w_in.shape[1]
    head_avg = jnp.kron(jnp.eye(N_Q_HEADS, dtype=F32), jnp.full((HEAD_DIM, HEAD_DIM), 1.0 / HEAD_DIM, F32)).astype(BF16)
    table = pl.BlockSpec((ROW_TILE, ATTN_WIDTH), lambda b, j: (j, 0))
    outs = [jax.ShapeDtypeStruct((n_batch, lt, HY_IN), BF16)] + [jax.ShapeDtypeStruct((n_batch, lt, ATTN_WIDTH), BF16)] * 3
    return pl.pallas_call(
        _inproj_even_kernel,
        grid=(n_batch, ntile),
        in_specs=[_row_spec(D_MODEL), _mod_spec(n_batch, nlat), _full((1, D_MODEL)), _full((D_MODEL, n_out)),
                  _full((1, ATTN_WIDTH)), _full((1, KV_WIDTH)), _full((ATTN_WIDTH, ATTN_WIDTH)), table, table],
        out_specs=[_row_spec(HY_IN), _row_spec(ATTN_WIDTH), _row_spec(ATTN_WIDTH), _row_spec(ATTN_WIDTH)],
        out_shape=outs,
        compiler_params=_cparams(("parallel", "parallel")),
        name="inproj_even",
    )(xs, mod, gain, w_in, q_gain, k_gain, head_avg, cos_t, sin_t)


def _attn_kernel(q_ref, k4_ref, v4_ref, o_ref, *, n_lat, nlat_tiles):
    j = pl.program_id(1)
    n_keys = k4_ref.shape[0]

    def attend(lo):
        for blk in range(N_Q_HEADS // 2):
            g = (2 * blk) // Q_PER_KV
            qp = q_ref[:, LANE * blk:LANE * (blk + 1)]
            acc = None
            for r in range(2):
                slot = LANE * (2 * g + r)
                kk = k4_ref[lo:n_keys, slot:slot + LANE]
                vv = v4_ref[lo:n_keys, slot:slot + LANE]
                s = lax.dot_general(qp, kk, (((1,), (1,)), ((), ())), preferred_element_type=F32)
                e = jnp.exp2(s - jnp.max(s, axis=-1, keepdims=True))
                inv = 1.0 / jnp.sum(e, axis=-1, keepdims=True)
                o = _bdot(e.astype(BF16), vv) * inv
                acc = o if acc is None else acc + o
            o_ref[:, LANE * blk:LANE * (blk + 1)] = acc.astype(o_ref.dtype)

    @pl.when(j < nlat_tiles)
    def _():
        attend(0)

    @pl.when(j >= nlat_tiles)
    def _():
        attend(n_lat)


def _attention(q, k4, v4, n_lat):
    n_batch, lt, _ = q.shape
    ntile = lt // ROW_TILE
    kv_spec = pl.BlockSpec((None, lt, ATTN_WIDTH), lambda b, j: (b, 0, 0))
    return pl.pallas_call(
        functools.partial(_attn_kernel, n_lat=n_lat, nlat_tiles=n_lat // ROW_TILE),
        grid=(n_batch, ntile),
        in_specs=[_row_spec(ATTN_WIDTH), kv_spec, kv_spec],
        out_specs=_row_spec(ATTN_WIDTH),
        out_shape=jax.ShapeDtypeStruct((n_batch, lt, ATTN_WIDTH), BF16),
        compiler_params=_cparams(("parallel", "parallel")),
        name="attention",
    )(q, k4, v4)


def _filter_kernel(z_ref, w1_ref, b1_ref, w2_ref, b2_ref, wo_ref, fr_ref, dl_ref, o_ref):
    hi = lax.Precision.HIGHEST
    z = z_ref[...]
    fr = fr_ref[...]
    h = jnp.sin(fr * (jnp.dot(z, w1_ref[...], precision=hi, preferred_element_type=F32) + b1_ref[...]))
    h = jnp.sin(fr * (jnp.dot(h, w2_ref[...], precision=hi, preferred_element_type=F32) + b2_ref[...]))
    h = jnp.dot(h, wo_ref[...], precision=hi, preferred_element_type=F32)
    decay = jnp.exp(-z[:, 0:1] * dl_ref[...])
    o_ref[...] = h * jnp.concatenate([decay] * (2 * HY_ORDER), axis=-1)


def _hyena_filters(seq, w1, b1, w2, b2, wout, freq):
    t = jnp.linspace(0.0, 1.0, seq, dtype=F32)[:, None]
    bands = jnp.linspace(1e-4, HY_BANDS - 1, HY_BANDS, dtype=F32)
    phase = (2.0 * math.pi / seq) * jnp.arange(seq, dtype=F32)[:, None] * bands
    z = jnp.concatenate([t, jnp.cos(phase), -jnp.sin(phase)], axis=-1)
    z = jnp.pad(z, ((0, 0), (0, LANE - HY_EMB)))
    w1p = jnp.pad(w1, ((0, LANE - HY_EMB), (0, 0)))
    deltas = jnp.linspace(HY_DECAY_SLOW, HY_DECAY_FAST, HY_WIDTH, dtype=F32)[None, :]
    hid = w1.shape[1]
    n_out = wout.shape[1]
    tl = ROW_TILE
    return pl.pallas_call(
        _filter_kernel,
        grid=(seq // tl,),
        in_specs=[pl.BlockSpec((tl, LANE), lambda i: (i, 0)), _full((LANE, hid)), _full((1, hid)), _full((hid, hid)),
                  _full((1, hid)), _full((hid, n_out)), _full((1, hid)), _full((1, HY_WIDTH))],
        out_specs=pl.BlockSpec((tl, n_out), lambda i: (i, 0)),
        out_shape=jax.ShapeDtypeStruct((seq, n_out), F32),
        compiler_params=_cparams(("parallel",)),
        name="hyena_filter",
    )(z, w1p, b1[None, :], w2, b2[None, :], wout, freq[None, :], deltas)


def _dft_tables(seq):
    n = 2 * seq
    f = jnp.arange(seq, dtype=jnp.int32)[:, None]
    t = jnp.arange(seq, dtype=jnp.int32)[None, :]
    ang = ((f * t) % n).astype(F32) * (2.0 * math.pi / n)
    cos = jnp.cos(ang)
    sin = jnp.where(f == 0, jnp.where(t % 2 == 0, 1.0, -1.0), jnp.sin(ang))
    nf = seq // FREQ_TILE
    fwd = jnp.concatenate([cos.reshape(nf, FREQ_TILE, seq), sin.reshape(nf, FREQ_TILE, seq)], axis=1)
    return fwd.astype(BF16), jnp.swapaxes(fwd, 1, 2).astype(BF16)


def _kfreq_kernel(hf_ref, hb_ref, f_ref, kre_ref, ks_ref):
    fi = pl.program_id(1)
    seq = hf_ref.shape[0]
    hf = hf_ref[...]
    row = lax.broadcasted_iota(jnp.int32, hf.shape, 0)
    hb = jnp.where(row == 0, 0.0, hb_ref[...])
    f = f_ref[...]
    a1 = _bdot(f, (hf + hb).astype(BF16))
    a2 = _bdot(f, (hf - hb).astype(BF16))
    tf = FREQ_TILE
    frow = lax.broadcasted_iota(jnp.int32, (tf, hf.shape[1]), 0)
    dc = jnp.logical_and(frow == 0, fi == 0)
    scale = jnp.where(dc, 1.0 / (2 * seq), 2.0 / (2 * seq))
    kre_ref[...] = a1[:tf] * scale
    ks_ref[...] = jnp.where(dc, a1[tf:], a2[tf:]) * scale


def _kfreq(hfilt, fwd_tab):
    seq = hfilt.shape[0]
    nf = seq // FREQ_TILE
    out = jax.ShapeDtypeStruct((HY_ORDER, seq, HY_WIDTH), F32)
    ospec = pl.BlockSpec((None, FREQ_TILE, HY_WIDTH), lambda o, fi: (o, fi, 0))
    return pl.pallas_call(
        _kfreq_kernel,
        grid=(HY_ORDER, nf),
        in_specs=[pl.BlockSpec((seq, HY_WIDTH), lambda o, fi: (0, o)),
                  pl.BlockSpec((seq, HY_WIDTH), lambda o, fi: (0, HY_ORDER + o)),
                  pl.BlockSpec((None, 2 * FREQ_TILE, seq), lambda o, fi: (fi, 0, 0))],
        out_specs=[ospec, ospec],
        out_shape=[out, out],
        compiler_params=_cparams(("parallel", "parallel")),
        name="hyena_kfreq",
    )(hfilt, hfilt, fwd_tab)


def _short_conv(p, w, b):
    n = p.shape[0]
    row = lax.broadcasted_iota(jnp.int32, p.shape, 0)
    prev = jnp.where(row == 0, 0.0, pltpu.roll(p, 1, 0))
    nxt = jnp.where(row == n - 1, 0.0, pltpu.roll(p, n - 1, 0))
    return b + prev * w[0:1, :] + p * w[1:2, :] + nxt * w[2:3, :]


def _hyena_kernel(vsrc_ref, gsrc_ref, cw_ref, cb_ref, skip_ref, f_ref, ft_ref, kre_ref, ks_ref, o_ref,
                  v_scr, vb_scr, acc_scr, *, conv_v):
    fi = pl.program_id(1)
    tf = FREQ_TILE

    @pl.when(fi == 0)
    def _():
        v = vsrc_ref[...].astype(F32)
        if conv_v:
            v = _short_conv(v, cw_ref[0], cb_ref[0])
        v_scr[...] = v
        vb_scr[...] = v.astype(BF16)
        acc_scr[...] = jnp.zeros_like(acc_scr)

    xf = _bdot(f_ref[...], vb_scr[...])
    xre, xs = xf[:tf], xf[tf:]
    kre, ks = kre_ref[...], ks_ref[...]
    row = lax.broadcasted_iota(jnp.int32, xre.shape, 0)
    dc = jnp.logical_and(row == 0, fi == 0)
    yre = jnp.where(dc, xre * kre, xre * kre - xs * ks)
    ys = jnp.where(dc, xs * ks, xre * ks + xs * kre)
    y = jnp.concatenate([yre, ys], axis=0).astype(BF16)
    acc_scr[...] += _bdot(ft_ref[...], y)

    @pl.when(fi == pl.num_programs(1) - 1)
    def _():
        gate = _short_conv(gsrc_ref[...].astype(F32), cw_ref[1], cb_ref[1])
        o_ref[...] = (gate * (acc_scr[...] + v_scr[...] * skip_ref[...])).astype(o_ref.dtype)


def _hyena_order(vsrc, v_blk, v_col, hy, row_blk, order, seq, conv_w, conv_b, skip, fwd_tab, inv_tab, kre, ks):
    n_batch = hy.shape[0]
    nf = seq // FREQ_TILE
    conv_v = order == 0
    cw = jnp.stack([conv_w[:, :HY_WIDTH], conv_w[:, (order + 1) * HY_WIDTH:(order + 2) * HY_WIDTH]])
    cb = jnp.stack([conv_b[None, :HY_WIDTH], conv_b[None, (order + 1) * HY_WIDTH:(order + 2) * HY_WIDTH]])
    kspec = pl.BlockSpec((None, FREQ_TILE, HY_WIDTH), lambda b, fi: (order, fi, 0))
    return pl.pallas_call(
        functools.partial(_hyena_kernel, conv_v=conv_v),
        grid=(n_batch, nf),
        in_specs=[pl.BlockSpec((None, seq, HY_WIDTH), lambda b, fi: (b, v_blk, v_col)),
                  pl.BlockSpec((None, seq, HY_WIDTH), lambda b, fi: (b, row_blk, order + 1)),
                  _full((2, 3, HY_WIDTH)), _full((2, 1, HY_WIDTH)), _full((1, HY_WIDTH)),
                  pl.BlockSpec((None, 2 * FREQ_TILE, seq), lambda b, fi: (fi, 0, 0)),
                  pl.BlockSpec((None, seq, 2 * FREQ_TILE), lambda b, fi: (fi, 0, 0)),
                  kspec, kspec],
        out_specs=pl.BlockSpec((None, seq, HY_WIDTH), lambda b, fi: (b, 0, 0)),
        out_shape=jax.ShapeDtypeStruct((n_batch, seq, HY_WIDTH), BF16),
        scratch_shapes=[pltpu.VMEM((seq, HY_WIDTH), F32), pltpu.VMEM((seq, HY_WIDTH), BF16),
                        pltpu.VMEM((seq, HY_WIDTH), F32)],
        compiler_params=_cparams(("parallel", "arbitrary")),
        name="hyena_order%d" % order,
    )(vsrc, hy, cw, cb, skip[order][None, :], fwd_tab, inv_tab, kre, ks)


def _hyena_mixer(hy, row_blk, seq, filt_args, conv_w, conv_b, skip):
    hfilt = _hyena_filters(seq, *filt_args)
    fwd_tab, inv_tab = _dft_tables(seq)
    kre, ks = _kfreq(hfilt, fwd_tab)
    v1 = _hyena_order(hy, row_blk, 0, hy, row_blk, 0, seq, conv_w, conv_b, skip, fwd_tab, inv_tab, kre, ks)
    return _hyena_order(v1, 0, 0, hy, row_blk, 1, seq, conv_w, conv_b, skip, fwd_tab, inv_tab, kre, ks)


def _outproj_kernel(yl_ref, yc_ref, ya_ref, x_ref, mod_ref, gpost_ref, w_ref, o_ref, *, nlat):
    j = pl.program_id(1)
    yh = jnp.where(j >= nlat, yc_ref[...], yl_ref[...])
    out = _bdot(yh, w_ref[:HY_WIDTH, :]) + _bdot(ya_ref[...], w_ref[HY_WIDTH:, :])
    o_ref[...] = x_ref[...] + mod_ref[2:3, :] * _rms(out, gpost_ref[...])


def _outproj(y_lat, y_ctx, y_att, xs, mod, gpost, w_out, nlat):
    n_batch, lt, _ = xs.shape
    ntile = lt // ROW_TILE
    nctx = ntile - nlat
    return pl.pallas_call(
        functools.partial(_outproj_kernel, nlat=nlat),
        grid=(n_batch, ntile),
        in_specs=[pl.BlockSpec((None, ROW_TILE, HY_WIDTH), lambda b, j: (b, jnp.minimum(j, nlat - 1), 0)),
                  pl.BlockSpec((None, ROW_TILE, HY_WIDTH), lambda b, j: (b, jnp.clip(j - nlat, 0, nctx - 1), 0)),
                  _row_spec(ATTN_WIDTH), _row_spec(D_MODEL), _mod_spec(n_batch, nlat), _full((1, D_MODEL)),
                  _full(w_out.shape)],
        out_specs=_row_spec(D_MODEL),
        out_shape=jax.ShapeDtypeStruct(xs.shape, F32),
        compiler_params=_cparams(("parallel", "parallel")),
        name="outproj_even",
    )(y_lat, y_ctx, y_att, xs, mod, gpost, w_out)


def _ffn_kernel(x_ref, mod_ref, gpre_ref, gpost_ref, wg_ref, wu_ref, wd_ref, o_ref):
    x = x_ref[...]
    h = (_rms(x, gpre_ref[...]) * (1.0 + mod_ref[4:5, :]) + mod_ref[3:4, :]).astype(BF16)
    g = _bdot(h, wg_ref[...])
    u = _bdot(h, wu_ref[...])
    a = (g * jax.nn.sigmoid(g) * u).astype(BF16)
    y = _bdot(a, wd_ref[...])
    o_ref[...] = x + mod_ref[5:6, :] * _rms(y, gpost_ref[...])


def _ffn_dense(xs, mod, gpre, gpost, wg, wu, wd, nlat):
    n_batch, lt, _ = xs.shape
    return pl.pallas_call(
        _ffn_kernel,
        grid=(n_batch, lt // ROW_TILE),
        in_specs=[_row_spec(D_MODEL), _mod_spec(n_batch, nlat), _full((1, D_MODEL)), _full((1, D_MODEL)),
                  _full(wg.shape), _full(wu.shape), _full(wd.shape)],
        out_specs=_row_spec(D_MODEL),
        out_shape=jax.ShapeDtypeStruct(xs.shape, F32),
        compiler_params=_cparams(("parallel", "parallel"), 56),
        name="ffn_dense",
    )(xs, mod, gpre, gpost, wg, wu, wd)


def _mod_rows(mod_ref, k, n_batch, is_ctx):
    return jnp.where(is_ctx, mod_ref[n_batch:n_batch + 1, k:k + 1, :], mod_ref[0:n_batch, k:k + 1, :])


def _row_order_swap(n_outer, n_inner):
    n = n_outer * n_inner
    dst = jnp.arange(n)
    src = (dst % n_outer) * n_inner + dst // n_outer
    return (src[:, None] == jnp.arange(n)[None, :]).astype(BF16)


def _inproj_odd_kernel(x_ref, mod_ref, gain_ref, swap_ref, w_ref, u_ref, *, n_lat_steps):
    n_batch, steps, _ = x_ref.shape
    is_ctx = pl.program_id(0) >= n_lat_steps
    h = (_rms(x_ref[...], gain_ref[...]) * (1.0 + _mod_rows(mod_ref, 1, n_batch, is_ctx))
         + _mod_rows(mod_ref, 0, n_batch, is_ctx))
    h = h.reshape(n_batch * steps, D_MODEL).astype(BF16)
    h = _bdot(swap_ref[...], h).astype(BF16)
    u_ref[...] = _bdot(h, w_ref[...]).reshape(steps, n_batch, D_MODEL)


def _inproj_odd(xs, mod, gain, w_in, n_lat):
    n_batch, lt, _ = xs.shape
    p = S5_CHUNK
    swap = _row_order_swap(n_batch, p)
    return pl.pallas_call(
        functools.partial(_inproj_odd_kernel, n_lat_steps=n_lat // p),
        grid=(lt // p,),
        in_specs=[pl.BlockSpec((n_batch, p, D_MODEL), lambda i: (0, i, 0)), _full(mod.shape), _full((1, D_MODEL)),
                  _full(swap.shape), _full(w_in.shape)],
        out_specs=pl.BlockSpec((p, n_batch, D_MODEL), lambda i: (i, 0, 0)),
        out_shape=jax.ShapeDtypeStruct((lt, n_batch, D_MODEL), F32),
        compiler_params=_cparams(("parallel",)),
        name="inproj_odd",
    )(xs, mod, gain, swap, w_in)


def _s5_param_kernel(lr_ref, li_ref, ls_ref, lrx_ref, lix_ref, lsx_ref, br_ref, bi_ref,
                     abr_ref, abi_ref, bbr_ref, bbi_ref):
    def zoh(lr_raw, li, log_step):
        lr = jnp.minimum(lr_raw, -1e-4)
        dt = jnp.exp(log_step)
        mag = jnp.exp(lr * dt)
        ab_re = mag * jnp.cos(li * dt)
        ab_im = mag * jnp.sin(li * dt)
        den = lr * lr + li * li
        nr, ni = ab_re - 1.0, ab_im
        return ab_re, ab_im, (nr * lr + ni * li) / den, (ni * lr - nr * li) / den

    ab_re, ab_im, _, _ = zoh(lr_ref[...], li_ref[...], ls_ref[...])
    abr_ref[...] = ab_re
    abi_ref[...] = ab_im
    _, _, co_re, co_im = zoh(lrx_ref[...], lix_ref[...], lsx_ref[...])
    br, bi = br_ref[...], bi_ref[...]
    bbr_ref[...] = co_re * br - co_im * bi
    bbi_ref[...] = co_re * bi + co_im * br


def _s5_params(lam_re, lam_im, log_step, b_re, b_im, c_re, c_im):
    nd, g, n = lam_re.shape
    k = S5_GROUP
    rep = lambda a: jnp.repeat(a, k, axis=1)
    ls = log_step[:, :, None]
    bt_re = jnp.swapaxes(b_re, 2, 3).reshape(nd, g * k, n)
    bt_im = jnp.swapaxes(b_im, 2, 3).reshape(nd, g * k, n)
    small = jax.ShapeDtypeStruct((nd, g, n), F32)
    big = jax.ShapeDtypeStruct((nd, g * k, n), F32)
    ab_re, ab_im, bb_re, bb_im = pl.pallas_call(
        _s5_param_kernel, out_shape=[small, small, big, big], name="s5_discretise",
    )(lam_re, lam_im, ls, rep(lam_re), rep(lam_im), rep(ls), bt_re, bt_im)
    a = jnp.stack([ab_re.reshape(nd, g * n), ab_im.reshape(nd, g * n)], axis=1)
    eye = jnp.eye(8, dtype=F32)
    nq = g // 8

    def drive_blocks(bb):
        return jnp.einsum('dqgkn,gh->dqgkhn', bb.reshape(nd, nq, 8, k, n), eye).reshape(nd, nq, 8 * k, 8 * n)

    def read_blocks(c):
        return jnp.einsum('dqgin,gh->dqhngi', c.reshape(nd, nq, 8, k, n), eye).reshape(nd, nq, 8 * n, 8 * k)

    w_drive = jnp.concatenate([drive_blocks(bb_re), drive_blocks(bb_im)], axis=-1).astype(BF16)
    w_read = jnp.stack([read_blocks(c_re), read_blocks(-c_im)], axis=2).astype(BF16)
    return a, w_drive, w_read


def _s5_scan_kernel(u_ref, a_ref, wd_ref, wr_ref, y_ref, sbuf, state, *, nctx_chunks):
    d = pl.program_id(0)
    i = pl.program_id(1)
    p_steps, n_batch, _ = u_ref.shape
    rows = p_steps * n_batch
    half = S5_LANES
    nq = wd_ref.shape[0]
    kq = wd_ref.shape[1]
    sq = wd_ref.shape[2] // 2

    @pl.when(i == 0)
    def _():
        state[...] = jnp.zeros_like(state)

    u = u_ref[...].reshape(rows, D_MODEL).astype(BF16)
    for q in range(nq):
        r = _bdot(u[:, kq * q:kq * (q + 1)], wd_ref[q])
        sbuf[:, :, sq * q:sq * (q + 1)] = r[:, :sq].reshape(p_steps, n_batch, sq)
        sbuf[:, :, half + sq * q:half + sq * (q + 1)] = r[:, sq:].reshape(p_steps, n_batch, sq)

    for q in range(nq):
        lo = sq * q
        ar = jnp.broadcast_to(a_ref[0:1, lo:lo + sq], (n_batch, sq))
        ai = jnp.broadcast_to(a_ref[1:2, lo:lo + sq], (n_batch, sq))

        def body(t, carry, lo=lo, ar=ar, ai=ai):
            sr, si = carry
            tt = jnp.where(d == 0, t, p_steps - 1 - t)
            nr = ar * sr - ai * si + sbuf[tt, :, lo:lo + sq]
            ni = ar * si + ai * sr + sbuf[tt, :, half + lo:half + lo + sq]
            sbuf[tt, :, lo:lo + sq] = nr
            sbuf[tt, :, half + lo:half + lo + sq] = ni
            return nr, ni

        sr, si = lax.fori_loop(0, p_steps, body, (state[:, lo:lo + sq], state[:, half + lo:half + lo + sq]))
        state[:, lo:lo + sq] = sr
        state[:, half + lo:half + lo + sq] = si

    @pl.when(i >= nctx_chunks)
    def _():
        s = sbuf[...].reshape(rows, 2 * half).astype(BF16)
        for q in range(nq):
            yq = (_bdot(s[:, sq * q:sq * (q + 1)], wr_ref[q, 0])
                  + _bdot(s[:, half + sq * q:half + sq * (q + 1)], wr_ref[q, 1]))
            y_ref[:, :, kq * q:kq * (q + 1)] = yq.reshape(p_steps, n_batch, kq).astype(y_ref.dtype)


def _s5_scan(u3, a, w_drive, w_read, n_lat):
    lt, n_batch, _ = u3.shape
    p = S5_CHUNK
    nchunk = lt // p
    nlatc = n_lat // p
    nctxc = nchunk - nlatc

    def u_map(d, i):
        return (jnp.where(d == 0, lax.rem(i + nlatc, nchunk), nchunk - 1 - i), 0, 0)

    def y_map(d, i):
        return (d, jnp.where(d == 0, jnp.maximum(i - nctxc, 0), jnp.minimum(nchunk - 1 - i, nlatc - 1)), 0, 0)

    return pl.pallas_call(
        functools.partial(_s5_scan_kernel, nctx_chunks=nctxc),
        grid=(2, nchunk),
        in_specs=[pl.BlockSpec((p, n_batch, D_MODEL), u_map),
                  pl.BlockSpec((None, 2, S5_LANES), lambda d, i: (d, 0, 0)),
                  pl.BlockSpec((None,) + w_drive.shape[1:], lambda d, i: (d, 0, 0, 0)),
                  pl.BlockSpec((None,) + w_read.shape[1:], lambda d, i: (d, 0, 0, 0, 0))],
        out_specs=pl.BlockSpec((None, p, n_batch, D_MODEL), y_map),
        out_shape=jax.ShapeDtypeStruct((2, n_lat, n_batch, D_MODEL), F32),
        scratch_shapes=[pltpu.VMEM((p, n_batch, 2 * S5_LANES), F32), pltpu.VMEM((n_batch, 2 * S5_LANES), F32)],
        compiler_params=_cparams(("arbitrary", "arbitrary")),
        name="s5_scan",
    )(u3, a, w_drive, w_read)


def _glu_kernel(yf_ref, yb_ref, u_ref, dskip_ref, wa_ref, wb_ref, x_ref, mod_ref, gpost_ref, gpre_ref, router_ref,
                swap_ref, tri_ref, xo_ref, hf_ref, rw_ref, ri_ref, cnt_ref, carry):
    n_batch, steps, _ = x_ref.shape
    rows = n_batch * steps

    @pl.when(pl.program_id(0) == 0)
    def _():
        carry[...] = jnp.zeros_like(carry)

    y = (yf_ref[...] + yb_ref[...] + dskip_ref[...] * u_ref[...]).reshape(rows, D_MODEL)
    z = jax.nn.gelu(y).astype(BF16)
    z = _bdot(swap_ref[...], z).astype(BF16)
    out = _bdot(z, wa_ref[...]) * jax.nn.sigmoid(_bdot(z, wb_ref[...]))
    out = out.reshape(n_batch, steps, D_MODEL)
    mod = lambda k: mod_ref[0:n_batch, k:k + 1, :]
    xn = x_ref[...] + mod(2) * _rms(out, gpost_ref[...])
    xo_ref[...] = xn
    hf = _rms(xn, gpre_ref[...]) * (1.0 + mod(4)) + mod(3)
    hf_ref[...] = hf
    hf = hf.reshape(rows, D_MODEL)
    h_hi = hf.astype(BF16)
    h_lo = (hf - h_hi.astype(F32)).astype(BF16)
    part = _bdot(h_hi, router_ref[...])
    logits = part[:, :LANE] + part[:, LANE:] + _bdot(h_lo, router_ref[:, :LANE])
    lane = lax.broadcasted_iota(jnp.int32, logits.shape, 1)
    neg = jnp.float32(-jnp.inf)
    lg = jnp.where(lane < N_EXPERTS, logits, neg)
    m1 = jnp.max(lg, axis=-1, keepdims=True)
    i1 = jnp.min(jnp.where(lg == m1, lane, LANE), axis=-1, keepdims=True)
    lg2 = jnp.where(lane == i1, neg, lg)
    m2 = jnp.max(lg2, axis=-1, keepdims=True)
    i2 = jnp.min(jnp.where(lg2 == m2, lane, LANE), axis=-1, keepdims=True)
    e2 = jnp.exp(m2 - m1)
    w1 = 1.0 / (1.0 + e2)
    rw_ref[...] = jnp.where(lane == 0, w1, jnp.where(lane == 1, e2 * w1, 0.0)).reshape(n_batch, steps, LANE)
    member = jnp.where(lane == i1, 1.0, jnp.where(lane == i2, 1.0, 0.0))
    base = carry[...] + _bdot(tri_ref[...], member.astype(BF16))
    r1 = jnp.sum(jnp.where(lane == i1, base, 0.0), axis=-1, keepdims=True).astype(jnp.int32)
    r2 = jnp.sum(jnp.where(lane == i2, base, 0.0), axis=-1, keepdims=True).astype(jnp.int32)
    ri = jnp.where(lane == 0, i1, jnp.where(lane == 1, i2, jnp.where(lane == 2, r1, jnp.where(lane == 3, r2, 0))))
    ri_ref[...] = ri.reshape(n_batch, steps, LANE)
    carry[...] += jnp.sum(member, axis=0, keepdims=True)
    cnt_ref[...] = carry[...]


def _glu(y, u3, d_skip, w_a, w_b, xs, mod, gpost, gpre, router, n_lat):
    n_batch = xs.shape[0]
    p = S5_CHUNK
    rows = p * n_batch
    router_p = jnp.pad(router, ((0, 0), (0, LANE - router.shape[1])))
    router_hi = router_p.astype(BF16)
    router_cat = jnp.concatenate([router_hi, (router_p - router_hi.astype(F32)).astype(BF16)], axis=1)
    tri = (jnp.arange(rows)[:, None] > jnp.arange(rows)[None, :]).astype(BF16)
    swap = _row_order_swap(p, n_batch)
    bt_spec = lambda w: pl.BlockSpec((n_batch, p, w), lambda i: (0, i, 0))
    return pl.pallas_call(
        _glu_kernel,
        grid=(n_lat // p,),
        in_specs=[pl.BlockSpec((None, p, n_batch, D_MODEL), lambda i: (0, i, 0, 0)),
                  pl.BlockSpec((None, p, n_batch, D_MODEL), lambda i: (1, i, 0, 0)),
                  pl.BlockSpec((p, n_batch, D_MODEL), lambda i: (i, 0, 0)),
                  _full((1, D_MODEL)), _full(w_a.shape), _full(w_b.shape), bt_spec(D_MODEL),
                  _full(mod.shape), _full((1, D_MODEL)), _full((1, D_MODEL)),
                  _full(router_cat.shape), _full(swap.shape), _full(tri.shape)],
        out_specs=[bt_spec(D_MODEL), bt_spec(D_MODEL), bt_spec(LANE), bt_spec(LANE), _full((1, LANE))],
        out_shape=[jax.ShapeDtypeStruct((n_batch, n_lat, D_MODEL), F32),
                   jax.ShapeDtypeStruct((n_batch, n_lat, D_MODEL), F32),
                   jax.ShapeDtypeStruct((n_batch, n_lat, LANE), F32),
                   jax.ShapeDtypeStruct((n_batch, n_lat, LANE), jnp.int32),
                   jax.ShapeDtypeStruct((1, LANE), F32)],
        scratch_shapes=[pltpu.VMEM((1, LANE), F32)],
        compiler_params=_cparams(("arbitrary",)),
        name="s5_glu_router",
    )(y, y, u3, d_skip, w_a, w_b, xs, mod, gpost, gpre, router_cat, swap, tri)


def _moe_plan(ri, counts, n_tiles):
    experts = jnp.arange(N_EXPERTS, dtype=jnp.int32)
    n_of = (counts[0, :N_EXPERTS].astype(jnp.int32) + MOE_TILE - 1) // MOE_TILE
    ends = jnp.cumsum(n_of)
    starts = ends - n_of
    start_of = jnp.sum(jnp.where(ri[:, 0:2, None] == experts, starts, 0), axis=-1)
    pos = (start_of * MOE_TILE + ri[:, 2:4]).reshape(-1)
    n_used = ends[-1]
    tile = jnp.arange(n_tiles, dtype=jnp.int32)
    tile_expert = jnp.sum((jnp.minimum(tile, n_used - 1)[:, None] >= ends[None, :]).astype(jnp.int32), axis=1)
    n_pairs = pos.shape[0]
    n_rows = n_tiles * MOE_TILE
    pair_of = jnp.full((n_rows,), -1, jnp.int32).at[pos].set(jnp.arange(n_pairs, dtype=jnp.int32))
    is_pad = pair_of < 0
    pair_of = jnp.where(is_pad, n_pairs - 1 + jnp.cumsum(is_pad.astype(jnp.int32)), pair_of)
    src_token = jnp.where(is_pad, 0, pair_of // 2)
    return src_token, pair_of, tile_expert, n_used.reshape(1)


def _moe_expert_kernel(te_ref, nu_ref, src0_ref, src_ref, dst_ref, hf_ref, wg_ref, wu_ref, wd_ref, out_ref,
                       xbuf, obuf, acc, gsem, ssem, *, rows_per_step):
    del te_ref
    i = pl.program_id(0)
    j = pl.program_id(1)
    last_j = pl.num_programs(1) - 1
    nu = nu_ref[0]
    cur = lax.rem(i, 2)
    nxt = 1 - cur
    row0 = j * rows_per_step

    def gather_row(idx_ref, row, slot):
        return pltpu.make_async_copy(hf_ref.at[pl.ds(idx_ref[0, row], 1)], xbuf.at[slot, pl.ds(row, 1)], gsem.at[slot])

    def scatter_row(row, slot):
        return pltpu.make_async_copy(obuf.at[slot, pl.ds(row, 1)], out_ref.at[pl.ds(dst_ref[0, row], 1)], ssem.at[slot])

    def whole_tile_wait(sem, slot):
        pltpu.make_async_copy(xbuf.at[slot], obuf.at[slot], sem.at[slot]).wait()

    @pl.when(jnp.logical_and(i == 0, j == 0))
    def _():
        obuf[...] = jnp.zeros_like(obuf)

        def first(r, c):
            gather_row(src0_ref, r, 0).start()
            return c

        lax.fori_loop(0, MOE_TILE, first, 0)

    @pl.when(jnp.logical_and(i <= nu, j == 0))
    def _():
        whole_tile_wait(gsem, cur)

    @pl.when(i < nu)
    def _():
        h = xbuf[cur].astype(BF16)
        g = _bdot(h, wg_ref[...])
        u = _bdot(h, wu_ref[...])
        part = _bdot((g * jax.nn.sigmoid(g) * u).astype(BF16), wd_ref[...])
        acc[...] = jnp.where(j == 0, part, acc[...] + part)
        for c in range(rows_per_step):
            gather_row(src_ref, row0 + c, nxt).start()
            scatter_row(row0 + c, nxt).start()

    @pl.when(i >= nu)
    def _():
        def tail(c, carry):
            scatter_row(row0 + c, nxt).start()
            return carry

        lax.fori_loop(0, rows_per_step, tail, 0)

    @pl.when(jnp.logical_and(i >= 1, j == last_j))
    def _():
        whole_tile_wait(ssem, cur)

    @pl.when(jnp.logical_and(i < nu, j == last_j))
    def _():
        obuf[cur] = acc[...]

    @pl.when(jnp.logical_and(i == pl.num_programs(0) - 1, j == last_j))
    def _():
        whole_tile_wait(ssem, nxt)


def _combine_kernel(y2_ref, rw_ref, x_ref, mod_ref, gpost_ref, o_ref):
    rw = rw_ref[...]
    y = rw[:, 0:1] * y2_ref[:, :D_MODEL] + rw[:, 1:2] * y2_ref[:, D_MODEL:]
    o_ref[...] = x_ref[...] + mod_ref[5:6, :] * _rms(y, gpost_ref[...])


def _moe(hf, rw, ri, counts, xs, mod, gpost, wg, wu, wd, nlat):
    t_rows = hf.shape[0]
    d_ff = wg.shape[2]
    n_tiles = 2 * t_rows // MOE_TILE + N_EXPERTS
    nff = d_ff // MOE_FF_TILE
    n_row_tiles = t_rows // ROW_TILE
    n_rows = n_tiles * MOE_TILE
    src_token, pair_of, tile_expert, n_used = _moe_plan(ri, counts, n_tiles)
    spare = jnp.arange(MOE_TILE, dtype=jnp.int32)
    src_tiles = jnp.concatenate([src_token, 0 * spare]).reshape(n_tiles + 1, 1, MOE_TILE)
    dst_tiles = jnp.concatenate([n_rows + spare, pair_of]).reshape(n_tiles + 1, 1, MOE_TILE)
    step_expert = jnp.concatenate([tile_expert, tile_expert[-1:]])
    any_spec = pl.BlockSpec(memory_space=pl.ANY)
    idx_block = (None, 1, MOE_TILE)

    def ff_blk(i, j, nu):
        return jnp.where(i < nu[0], j, nff - 1)

    y_pairs = pl.pallas_call(
        functools.partial(_moe_expert_kernel, rows_per_step=MOE_TILE // nff),
        grid_spec=pltpu.PrefetchScalarGridSpec(
            num_scalar_prefetch=2,
            grid=(n_tiles + 1, nff),
            in_specs=[pl.BlockSpec(idx_block, lambda i, j, te, nu: (0, 0, 0), memory_space=pltpu.SMEM),
                      pl.BlockSpec(idx_block, lambda i, j, te, nu: (jnp.minimum(i + 1, n_tiles), 0, 0),
                                   memory_space=pltpu.SMEM),
                      pl.BlockSpec(idx_block, lambda i, j, te, nu: (i, 0, 0), memory_space=pltpu.SMEM),
                      any_spec,
                      pl.BlockSpec((None, D_MODEL, MOE_FF_TILE), lambda i, j, te, nu: (te[i], 0, ff_blk(i, j, nu))),
                      pl.BlockSpec((None, D_MODEL, MOE_FF_TILE), lambda i, j, te, nu: (te[i], 0, ff_blk(i, j, nu))),
                      pl.BlockSpec((None, MOE_FF_TILE, D_MODEL), lambda i, j, te, nu: (te[i], ff_blk(i, j, nu), 0))],
            out_specs=any_spec,
            scratch_shapes=[pltpu.VMEM((2, MOE_TILE, D_MODEL), F32), pltpu.VMEM((2, MOE_TILE, D_MODEL), F32),
                            pltpu.VMEM((MOE_TILE, D_MODEL), F32),
                            pltpu.SemaphoreType.DMA((2,)), pltpu.SemaphoreType.DMA((2,))]),
        out_shape=jax.ShapeDtypeStruct((n_rows + MOE_TILE, D_MODEL), F32),
        compiler_params=_cparams(("arbitrary", "arbitrary")),
        name="moe_experts",
    )(step_expert, n_used, src_tiles, src_tiles, dst_tiles, hf, wg, wu, wd)

    row = lambda w: pl.BlockSpec((ROW_TILE, w), lambda i: (i, 0))
    return pl.pallas_call(
        _combine_kernel,
        grid=(n_row_tiles,),
        in_specs=[row(2 * D_MODEL), row(LANE), row(D_MODEL),
                  pl.BlockSpec((None, 6, D_MODEL), lambda i: (lax.div(i, nlat), 0, 0)), _full((1, D_MODEL))],
        out_specs=row(D_MODEL),
        out_shape=jax.ShapeDtypeStruct((t_rows, D_MODEL), F32),
        compiler_params=_cparams(("parallel",)),
        name="moe_combine",
    )(y_pairs.reshape((n_rows + MOE_TILE) // 2, 2 * D_MODEL), rw, xs, mod, gpost)


def _rope_tables(n_lat, n_ctx):
    rows = n_lat // GRID_W
    row = jnp.repeat(jnp.arange(rows, dtype=F32), GRID_W)
    col = jnp.tile(jnp.arange(GRID_W, dtype=F32), rows)
    n_freq = HEAD_DIM // 4
    inv = ROPE_THETA ** (-jnp.arange(n_freq, dtype=F32) / n_freq)
    ang = jnp.concatenate([row[:, None] * inv, col[:, None] * inv], axis=-1)
    cos, sin = jnp.cos(ang), jnp.sin(ang)
    cos_h = jnp.concatenate([cos, cos], axis=-1)
    sin_h = jnp.concatenate([-sin, sin], axis=-1)
    cos_t = jnp.concatenate([jnp.tile(cos_h, (1, N_Q_HEADS)), jnp.ones((n_ctx, ATTN_WIDTH), F32)], axis=0)
    sin_t = jnp.concatenate([jnp.tile(sin_h, (1, N_Q_HEADS)), jnp.zeros((n_ctx, ATTN_WIDTH), F32)], axis=0)
    return cos_t, sin_t


def kernel(x, c, ctx, c_ctx, ada_w, ada_b, norm_mix_pre, norm_mix_post, norm_ffn_pre, norm_ffn_post, ev_w_in, ev_hy_conv_w, ev_hy_conv_b, ev_hy_f_w1, ev_hy_f_b1, ev_hy_f_w2, ev_hy_f_b2, ev_hy_f_wout, ev_hy_freq, ev_hy_skip, ev_q_norm, ev_k_norm, ev_w_out, ev_ffn_w_gate, ev_ffn_w_up, ev_ffn_w_down, od_w_in, od_s5_lambda_re, od_s5_lambda_im, od_s5_log_step, od_s5_b_re, od_s5_b_im, od_s5_c_re, od_s5_c_im, od_s5_d, od_glu_w_a, od_glu_w_b, od_router, od_moe_w_gate, od_moe_w_up, od_moe_w_down):
    n_batch, n_lat, _ = x.shape
    n_ctx = ctx.shape[1]
    depth = ada_w.shape[0]
    assert n_batch == 8 and n_lat % ROW_TILE == 0 and n_ctx % ROW_TILE == 0 and n_lat % n_ctx == 0
    assert depth == 2
    nlat = n_lat // ROW_TILE

    cond = jnp.concatenate([c, c_ctx[None, :], jnp.zeros((16 - n_batch - 1, D_MODEL), F32)], axis=0)
    mods = _ada_params(cond, ada_w, ada_b)
    xs = jnp.concatenate([x, ctx], axis=1)
    vec = lambda a: a[None, :]

    cos_t, sin_t = _rope_tables(n_lat, n_ctx)
    hy, q, k4, v4 = _inproj_even(xs, mods[0], vec(norm_mix_pre[0]), ev_w_in[0].astype(BF16),
                                 vec(jnp.tile(ev_q_norm[0], N_Q_HEADS)), vec(jnp.tile(ev_k_norm[0], N_KV_HEADS)),
                                 cos_t, sin_t, nlat)
    y_att = _attention(q, k4, v4, n_lat)
    filt_args = (ev_hy_f_w1[0], ev_hy_f_b1[0], ev_hy_f_w2[0], ev_hy_f_b2[0], ev_hy_f_wout[0], ev_hy_freq[0])
    y_hy_lat = _hyena_mixer(hy, 0, n_lat, filt_args, ev_hy_conv_w[0], ev_hy_conv_b[0], ev_hy_skip[0])
    y_hy_ctx = _hyena_mixer(hy, n_lat // n_ctx, n_ctx, filt_args, ev_hy_conv_w[0], ev_hy_conv_b[0], ev_hy_skip[0])
    xs = _outproj(y_hy_lat, y_hy_ctx, y_att, xs, mods[0], vec(norm_mix_post[0]), ev_w_out[0].astype(BF16), nlat)
    xs = _ffn_dense(xs, mods[0], vec(norm_ffn_pre[0]), vec(norm_ffn_post[0]), ev_ffn_w_gate[0].astype(BF16),
                    ev_ffn_w_up[0].astype(BF16), ev_ffn_w_down[0].astype(BF16), nlat)

    u3 = _inproj_odd(xs, mods[1], vec(norm_mix_pre[1]), od_w_in[0].astype(BF16), n_lat)
    a, w_drive, w_read = _s5_params(od_s5_lambda_re[0], od_s5_lambda_im[0], od_s5_log_step[0],
                                    od_s5_b_re[0], od_s5_b_im[0], od_s5_c_re[0], od_s5_c_im[0])
    y = _s5_scan(u3, a, w_drive, w_read, n_lat)
    x_lat, hf, rw, ri, counts = _glu(y, u3, vec(od_s5_d[0]), od_glu_w_a[0].astype(BF16), od_glu_w_b[0].astype(BF16),
                                     xs, mods[1], vec(norm_mix_post[1]), vec(norm_ffn_pre[1]), od_router[0], n_lat)
    t_rows = n_batch * n_lat
    out = _moe(hf.reshape(t_rows, D_MODEL), rw.reshape(t_rows, LANE), ri.reshape(t_rows, LANE), counts,
               x_lat.reshape(t_rows, D_MODEL), mods[1], vec(norm_ffn_post[1]), od_moe_w_gate[0].astype(BF16),
               od_moe_w_up[0].astype(BF16), od_moe_w_down[0].astype(BF16), nlat)
    return out.reshape(n_batch, n_lat, D_MODEL)
```

```python
import functools
import math

import jax
import jax.numpy as jnp
from jax import lax
from jax.experimental import pallas as pl
from jax.experimental.pallas import tpu as pltpu

F32 = jnp.float32
BF16 = jnp.bfloat16

D_MODEL = 1024
EPS = 1e-6
GRID_W = 64

HY_WIDTH = 512
HY_ORDER = 2
HY_IN = (HY_ORDER + 1) * HY_WIDTH
HY_BANDS = 16
HY_EMB = 2 * HY_BANDS + 1
HY_DECAY_SLOW = -math.log(1e-2) / 1.5
HY_DECAY_FAST = -math.log(1e-2) / 0.3
HEAD_DIM = 64
N_Q_HEADS = 8
N_KV_HEADS = 2
Q_PER_KV = N_Q_HEADS // N_KV_HEADS
ATTN_WIDTH = N_Q_HEADS * HEAD_DIM
KV_WIDTH = N_KV_HEADS * HEAD_DIM
ROPE_THETA = 10000.0
LOG2_E = 1.0 / math.log(2.0)

S5_GROUP = 16
S5_GROUPS = D_MODEL // S5_GROUP
S5_STATE = 64
S5_LANES = S5_GROUPS * S5_STATE
N_EXPERTS = 8

ROW_TILE = 256
FREQ_TILE = 512
S5_CHUNK = 32
S5_SCAN_CHUNK = 64
MOE_TILE = 512
MOE_FF_TILE = 896
LANE = 128
MIB = 1024 * 1024


def _cparams(sem, vmem_mib=48):
    return pltpu.CompilerParams(dimension_semantics=sem, vmem_limit_bytes=vmem_mib * MIB)


def _rms(x, gain):
    return x * lax.rsqrt(jnp.mean(x * x, axis=-1, keepdims=True) + EPS) * gain


def _bdot(a, b):
    return jnp.dot(a, b, preferred_element_type=F32)


def _full(shape):
    zeros = (0,) * len(shape)
    return pl.BlockSpec(shape, lambda *_: zeros)


def _ada_kernel(cond_ref, w_ref, b_ref, o_ref):
    c = cond_ref[...]
    s = (c * jax.nn.sigmoid(c)).astype(BF16)
    o_ref[...] = _bdot(s, w_ref[...].astype(BF16)) + b_ref[...]


def _ada_params(cond, ada_w, ada_b):
    depth, _, n6 = ada_w.shape
    rows = cond.shape[0]
    tn = 1536
    out = pl.pallas_call(
        _ada_kernel,
        grid=(depth, n6 // tn),
        in_specs=[pl.BlockSpec((rows, D_MODEL), lambda i, j: (0, 0)),
                  pl.BlockSpec((None, D_MODEL, tn), lambda i, j: (i, 0, j)),
                  pl.BlockSpec((None, 1, tn), lambda i, j: (i, 0, j))],
        out_specs=pl.BlockSpec((None, rows, tn), lambda i, j: (i, 0, j)),
        out_shape=jax.ShapeDtypeStruct((depth, rows, n6), F32),
        compiler_params=_cparams(("arbitrary", "arbitrary")),
        name="ada_params",
    )(cond, ada_w, ada_b.reshape(depth, 1, n6))
    return out.reshape(depth, rows, 6, D_MODEL)


def _rope_rotate(t):
    w = t.shape[-1]
    lane = lax.broadcasted_iota(jnp.int32, t.shape, 1)
    first = (lane & (HEAD_DIM - 1)) < HEAD_DIM // 2
    return jnp.where(first, pltpu.roll(t, w - HEAD_DIM // 2, 1), pltpu.roll(t, HEAD_DIM // 2, 1))


def _head_slots(t):
    lane = lax.broadcasted_iota(jnp.int32, t.shape, 1)
    lo = jnp.where(lane < HEAD_DIM, t, 0.0)
    hi = jnp.where(lane >= HEAD_DIM, t, 0.0)
    return jnp.concatenate([lo, pltpu.roll(lo, HEAD_DIM, 1), pltpu.roll(hi, HEAD_DIM, 1), hi], axis=-1)


def _inproj_even_kernel(x_ref, c_ref, mod_ref, gain_ref, w_ref, qg_ref, kg_ref, e_ref, cos_ref, sin_ref,
                        hy_ref, q_ref, k4_ref, v4_ref, *, nlat):
    xin = jnp.where(pl.program_id(1) >= nlat, c_ref[...], x_ref[...])
    h = _rms(xin, gain_ref[...]) * (1.0 + mod_ref[1:2, :]) + mod_ref[0:1, :]
    p = _bdot(h.astype(BF16), w_ref[...])
    hy_ref[...] = p[:, :HY_IN].astype(hy_ref.dtype)
    q = p[:, HY_IN:HY_IN + ATTN_WIDTH]
    k = p[:, HY_IN + ATTN_WIDTH:HY_IN + ATTN_WIDTH + KV_WIDTH]
    v = p[:, HY_IN + ATTN_WIDTH + KV_WIDTH:]
    e = e_ref[...]
    qn = q * lax.rsqrt(_bdot((q * q).astype(BF16), e) + EPS) * qg_ref[...]
    kn = k * lax.rsqrt(_bdot((k * k).astype(BF16), e[:KV_WIDTH, :KV_WIDTH]) + EPS) * kg_ref[...]
    cos = cos_ref[...]
    sin = sin_ref[...]
    qr = (qn * cos + _rope_rotate(qn) * sin) * (HEAD_DIM ** -0.5 * LOG2_E)
    kr = kn * cos[:, :KV_WIDTH] + _rope_rotate(kn) * sin[:, :KV_WIDTH]
    q_ref[...] = qr.astype(q_ref.dtype)
    k4_ref[...] = _head_slots(kr).astype(k4_ref.dtype)
    v4_ref[...] = _head_slots(v).astype(v4_ref.dtype)


def _mod_spec(n_batch, nlat):
    return pl.BlockSpec((None, 6, D_MODEL), lambda b, j: (jnp.where(j >= nlat, n_batch, b), 0, 0))


def _row_spec(width):
    return pl.BlockSpec((None, ROW_TILE, width), lambda b, j: (b, j, 0))


def _lat_ctx_specs(nlat, nctx, width):
    return [pl.BlockSpec((None, ROW_TILE, width), lambda b, j: (b, jnp.minimum(j, nlat - 1), 0)),
            pl.BlockSpec((None, ROW_TILE, width), lambda b, j: (b, jnp.clip(j - nlat, 0, nctx - 1), 0))]


def _inproj_even(x, ctx, mod, gain, w_in, q_gain, k_gain, cos_t, sin_t, nlat):
    n_batch = x.shape[0]
    nctx = ctx.shape[1] // ROW_TILE
    ntile = nlat + nctx
    lt = ntile * ROW_TILE
    n_out = w_in.shape[1]
    head_avg = jnp.kron(jnp.eye(N_Q_HEADS, dtype=F32), jnp.full((HEAD_DIM, HEAD_DIM), 1.0 / HEAD_DIM, F32)).astype(BF16)
    table = pl.BlockSpec((ROW_TILE, ATTN_WIDTH), lambda b, j: (j, 0))
    outs = [jax.ShapeDtypeStruct((n_batch, lt, HY_IN), BF16)] + [jax.ShapeDtypeStruct((n_batch, lt, ATTN_WIDTH), BF16)] * 3
    return pl.pallas_call(
        functools.partial(_inproj_even_kernel, nlat=nlat),
        grid=(n_batch, ntile),
        in_specs=_lat_ctx_specs(nlat, nctx, D_MODEL) + [
            _mod_spec(n_batch, nlat), _full((1, D_MODEL)), _full((D_MODEL, n_out)),
            _full((1, ATTN_WIDTH)), _full((1, KV_WIDTH)), _full((ATTN_WIDTH, ATTN_WIDTH)), table, table],
        out_specs=[_row_spec(HY_IN), _row_spec(ATTN_WIDTH), _row_spec(ATTN_WIDTH), _row_spec(ATTN_WIDTH)],
        out_shape=outs,
        compiler_params=_cparams(("parallel", "parallel")),
        name="inproj_even",
    )(x, ctx, mod, gain, w_in, q_gain, k_gain, head_avg, cos_t, sin_t)


def _attn_kernel(q_ref, k4_ref, v4_ref, o_ref, *, n_lat, nlat_tiles):
    j = pl.program_id(1)
    n_keys = k4_ref.shape[0]

    def attend(lo):
        for blk in range(N_Q_HEADS // 2):
            g = (2 * blk) // Q_PER_KV
            qp = q_ref[:, LANE * blk:LANE * (blk + 1)]
            acc = None
            for r in range(2):
                slot = LANE * (2 * g + r)
                kk = k4_ref[lo:n_keys, slot:slot + LANE]
                vv = v4_ref[lo:n_keys, slot:slot + LANE]
                s = lax.dot_general(qp, kk, (((1,), (1,)), ((), ())), preferred_element_type=F32)
                e = jnp.exp2(s - jnp.max(s, axis=-1, keepdims=True))
                inv = 1.0 / jnp.sum(e, axis=-1, keepdims=True)
                o = _bdot(e.astype(BF16), vv) * inv
                acc = o if acc is None else acc + o
            o_ref[:, LANE * blk:LANE * (blk + 1)] = acc.astype(o_ref.dtype)

    @pl.when(j < nlat_tiles)
    def _():
        attend(0)

    @pl.when(j >= nlat_tiles)
    def _():
        attend(n_lat)


def _attention(q, k4, v4, n_lat):
    n_batch, lt, _ = q.shape
    ntile = lt // ROW_TILE
    kv_spec = pl.BlockSpec((None, lt, ATTN_WIDTH), lambda b, j: (b, 0, 0))
    return pl.pallas_call(
        functools.partial(_attn_kernel, n_lat=n_lat, nlat_tiles=n_lat // ROW_TILE),
        grid=(n_batch, ntile),
        in_specs=[_row_spec(ATTN_WIDTH), kv_spec, kv_spec],
        out_specs=_row_spec(ATTN_WIDTH),
        out_shape=jax.ShapeDtypeStruct((n_batch, lt, ATTN_WIDTH), BF16),
        compiler_params=_cparams(("parallel", "parallel")),
        name="attention",
    )(q, k4, v4)


def _filter_kernel(z_ref, w1_ref, b1_ref, w2_ref, b2_ref, wo_ref, fr_ref, dl_ref, o_ref):
    hi = lax.Precision.HIGHEST
    z = z_ref[...]
    fr = fr_ref[...]
    h = jnp.sin(fr * (jnp.dot(z, w1_ref[...], precision=hi, preferred_element_type=F32) + b1_ref[...]))
    h = jnp.sin(fr * (jnp.dot(h, w2_ref[...], precision=hi, preferred_element_type=F32) + b2_ref[...]))
    h = jnp.dot(h, wo_ref[...], precision=hi, preferred_element_type=F32)
    decay = jnp.exp(-z[:, 0:1] * dl_ref[...])
    o_ref[...] = h * jnp.concatenate([decay] * (2 * HY_ORDER), axis=-1)


def _hyena_filters(seq, w1, b1, w2, b2, wout, freq):
    t = jnp.linspace(0.0, 1.0, seq, dtype=F32)[:, None]
    bands = jnp.linspace(1e-4, HY_BANDS - 1, HY_BANDS, dtype=F32)
    phase = (2.0 * math.pi / seq) * jnp.arange(seq, dtype=F32)[:, None] * bands
    z = jnp.concatenate([t, jnp.cos(phase), -jnp.sin(phase)], axis=-1)
    z = jnp.pad(z, ((0, 0), (0, LANE - HY_EMB)))
    w1p = jnp.pad(w1, ((0, LANE - HY_EMB), (0, 0)))
    deltas = jnp.linspace(HY_DECAY_SLOW, HY_DECAY_FAST, HY_WIDTH, dtype=F32)[None, :]
    hid = w1.shape[1]
    n_out = wout.shape[1]
    tl = ROW_TILE
    return pl.pallas_call(
        _filter_kernel,
        grid=(seq // tl,),
        in_specs=[pl.BlockSpec((tl, LANE), lambda i: (i, 0)), _full((LANE, hid)), _full((1, hid)), _full((hid, hid)),
                  _full((1, hid)), _full((hid, n_out)), _full((1, hid)), _full((1, HY_WIDTH))],
        out_specs=pl.BlockSpec((tl, n_out), lambda i: (i, 0)),
        out_shape=jax.ShapeDtypeStruct((seq, n_out), F32),
        compiler_params=_cparams(("parallel",)),
        name="hyena_filter",
    )(z, w1p, b1[None, :], w2, b2[None, :], wout, freq[None, :], deltas)


def _freq_tile(seq):
    return min(FREQ_TILE, seq)


def _dft_tables(seq):
    n = 2 * seq
    f = jnp.arange(seq, dtype=jnp.int32)[:, None]
    t = jnp.arange(seq, dtype=jnp.int32)[None, :]
    ang = ((f * t) % n).astype(F32) * (2.0 * math.pi / n)
    cos = jnp.cos(ang)
    sin = jnp.where(f == 0, jnp.where(t % 2 == 0, 1.0, -1.0), jnp.sin(ang))
    tf = _freq_tile(seq)
    fwd = jnp.concatenate([cos.reshape(seq // tf, tf, seq), sin.reshape(seq // tf, tf, seq)], axis=1)
    return fwd.astype(BF16), jnp.swapaxes(fwd, 1, 2).astype(BF16)


def _kfreq_kernel(hf_ref, hb_ref, f_ref, kre_ref, ks_ref):
    fi = pl.program_id(1)
    seq = hf_ref.shape[0]
    hf = hf_ref[...]
    row = lax.broadcasted_iota(jnp.int32, hf.shape, 0)
    hb = jnp.where(row == 0, 0.0, hb_ref[...])
    f = f_ref[...]
    a1 = _bdot(f, (hf + hb).astype(BF16))
    a2 = _bdot(f, (hf - hb).astype(BF16))
    tf = f_ref.shape[0] // 2
    frow = lax.broadcasted_iota(jnp.int32, (tf, hf.shape[1]), 0)
    dc = jnp.logical_and(frow == 0, fi == 0)
    scale = jnp.where(dc, 1.0 / (2 * seq), 2.0 / (2 * seq))
    kre_ref[...] = a1[:tf] * scale
    ks_ref[...] = jnp.where(dc, a1[tf:], a2[tf:]) * scale


def _kfreq(hfilt, fwd_tab):
    seq = hfilt.shape[0]
    tf = _freq_tile(seq)
    nf = seq // tf
    out = jax.ShapeDtypeStruct((HY_ORDER, seq, HY_WIDTH), F32)
    ospec = pl.BlockSpec((None, tf, HY_WIDTH), lambda o, fi: (o, fi, 0))
    return pl.pallas_call(
        _kfreq_kernel,
        grid=(HY_ORDER, nf),
        in_specs=[pl.BlockSpec((seq, HY_WIDTH), lambda o, fi: (0, o)),
                  pl.BlockSpec((seq, HY_WIDTH), lambda o, fi: (0, HY_ORDER + o)),
                  pl.BlockSpec((None, 2 * tf, seq), lambda o, fi: (fi, 0, 0))],
        out_specs=[ospec, ospec],
        out_shape=[out, out],
        compiler_params=_cparams(("parallel", "parallel")),
        name="hyena_kfreq",
    )(hfilt, hfilt, fwd_tab)


def _short_conv(p, w, b):
    n = p.shape[0]
    row = lax.broadcasted_iota(jnp.int32, p.shape, 0)
    prev = jnp.where(row == 0, 0.0, pltpu.roll(p, 1, 0))
    nxt = jnp.where(row == n - 1, 0.0, pltpu.roll(p, n - 1, 0))
    return b + prev * w[0:1, :] + p * w[1:2, :] + nxt * w[2:3, :]


def _hyena_kernel(vsrc_ref, gsrc_ref, cw_ref, cb_ref, skip_ref, f_ref, ft_ref, kre_ref, ks_ref, o_ref,
                  v_scr, vb_scr, acc_scr, *, conv_v):
    fi = pl.program_id(1)
    tf = f_ref.shape[0] // 2

    @pl.when(fi == 0)
    def _():
        v = vsrc_ref[...].astype(F32)
        if conv_v:
            v = _short_conv(v, cw_ref[0], cb_ref[0])
        v_scr[...] = v
        vb_scr[...] = v.astype(BF16)
        acc_scr[...] = jnp.zeros_like(acc_scr)

    xf = _bdot(f_ref[...], vb_scr[...])
    xre, xs = xf[:tf], xf[tf:]
    kre, ks = kre_ref[...], ks_ref[...]
    row = lax.broadcasted_iota(jnp.int32, xre.shape, 0)
    dc = jnp.logical_and(row == 0, fi == 0)
    yre = jnp.where(dc, xre * kre, xre * kre - xs * ks)
    ys = jnp.where(dc, xs * ks, xre * ks + xs * kre)
    y = jnp.concatenate([yre, ys], axis=0).astype(BF16)
    acc_scr[...] += _bdot(ft_ref[...], y)

    @pl.when(fi == pl.num_programs(1) - 1)
    def _():
        gate = _short_conv(gsrc_ref[...].astype(F32), cw_ref[1], cb_ref[1])
        o_ref[...] = (gate * (acc_scr[...] + v_scr[...] * skip_ref[...])).astype(o_ref.dtype)


def _hyena_order(vsrc, v_blk, v_col, hy, row_blk, order, seq, conv_w, conv_b, skip, fwd_tab, inv_tab, kre, ks):
    n_batch = hy.shape[0]
    tf = _freq_tile(seq)
    nf = seq // tf
    conv_v = order == 0
    cw = jnp.stack([conv_w[:, :HY_WIDTH], conv_w[:, (order + 1) * HY_WIDTH:(order + 2) * HY_WIDTH]])
    cb = jnp.stack([conv_b[None, :HY_WIDTH], conv_b[None, (order + 1) * HY_WIDTH:(order + 2) * HY_WIDTH]])
    kspec = pl.BlockSpec((None, tf, HY_WIDTH), lambda b, fi: (order, fi, 0))
    return pl.pallas_call(
        functools.partial(_hyena_kernel, conv_v=conv_v),
        grid=(n_batch, nf),
        in_specs=[pl.BlockSpec((None, seq, HY_WIDTH), lambda b, fi: (b, v_blk, v_col)),
                  pl.BlockSpec((None, seq, HY_WIDTH), lambda b, fi: (b, row_blk, order + 1)),
                  _full((2, 3, HY_WIDTH)), _full((2, 1, HY_WIDTH)), _full((1, HY_WIDTH)),
                  pl.BlockSpec((None, 2 * tf, seq), lambda b, fi: (fi, 0, 0)),
                  pl.BlockSpec((None, seq, 2 * tf), lambda b, fi: (fi, 0, 0)),
                  kspec, kspec],
        out_specs=pl.BlockSpec((None, seq, HY_WIDTH), lambda b, fi: (b, 0, 0)),
        out_shape=jax.ShapeDtypeStruct((n_batch, seq, HY_WIDTH), BF16),
        scratch_shapes=[pltpu.VMEM((seq, HY_WIDTH), F32), pltpu.VMEM((seq, HY_WIDTH), BF16),
                        pltpu.VMEM((seq, HY_WIDTH), F32)],
        compiler_params=_cparams(("parallel", "arbitrary")),
        name="hyena_order%d" % order,
    )(vsrc, hy, cw, cb, skip[order][None, :], fwd_tab, inv_tab, kre, ks)


def _hyena_mixer(hy, row_blk, seq, filt_args, conv_w, conv_b, skip):
    hfilt = _hyena_filters(seq, *filt_args)
    fwd_tab, inv_tab = _dft_tables(seq)
    kre, ks = _kfreq(hfilt, fwd_tab)
    v1 = _hyena_order(hy, row_blk, 0, hy, row_blk, 0, seq, conv_w, conv_b, skip, fwd_tab, inv_tab, kre, ks)
    return _hyena_order(v1, 0, 0, hy, row_blk, 1, seq, conv_w, conv_b, skip, fwd_tab, inv_tab, kre, ks)


def _outproj_kernel(yl_ref, yc_ref, ya_ref, x_ref, c_ref, mod_ref, gpost_ref, w_ref, o_ref, *, nlat):
    is_ctx = pl.program_id(1) >= nlat
    yh = jnp.where(is_ctx, yc_ref[...], yl_ref[...])
    out = _bdot(yh, w_ref[:HY_WIDTH, :]) + _bdot(ya_ref[...], w_ref[HY_WIDTH:, :])
    o_ref[...] = jnp.where(is_ctx, c_ref[...], x_ref[...]) + mod_ref[2:3, :] * _rms(out, gpost_ref[...])


def _outproj(y_lat, y_ctx, y_att, x, ctx, mod, gpost, w_out, nlat):
    n_batch = x.shape[0]
    nctx = ctx.shape[1] // ROW_TILE
    ntile = nlat + nctx
    return pl.pallas_call(
        functools.partial(_outproj_kernel, nlat=nlat),
        grid=(n_batch, ntile),
        in_specs=_lat_ctx_specs(nlat, nctx, HY_WIDTH) + [_row_spec(ATTN_WIDTH)] + _lat_ctx_specs(nlat, nctx, D_MODEL) + [
            _mod_spec(n_batch, nlat), _full((1, D_MODEL)), _full(w_out.shape)],
        out_specs=_row_spec(D_MODEL),
        out_shape=jax.ShapeDtypeStruct((n_batch, ntile * ROW_TILE, D_MODEL), F32),
        compiler_params=_cparams(("parallel", "parallel")),
        name="outproj_even",
    )(y_lat, y_ctx, y_att, x, ctx, mod, gpost, w_out)


def _ffn_kernel(x_ref, mod_ref, gpre_ref, gpost_ref, wg_ref, wu_ref, wd_ref, o_ref):
    x = x_ref[...]
    h = (_rms(x, gpre_ref[...]) * (1.0 + mod_ref[4:5, :]) + mod_ref[3:4, :]).astype(BF16)
    g = _bdot(h, wg_ref[...])
    u = _bdot(h, wu_ref[...])
    a = (g * jax.nn.sigmoid(g) * u).astype(BF16)
    y = _bdot(a, wd_ref[...])
    o_ref[...] = x + mod_ref[5:6, :] * _rms(y, gpost_ref[...])


def _ffn_dense(xs, mod, gpre, gpost, wg, wu, wd, nlat):
    n_batch, lt, _ = xs.shape
    return pl.pallas_call(
        _ffn_kernel,
        grid=(n_batch, lt // ROW_TILE),
        in_specs=[_row_spec(D_MODEL), _mod_spec(n_batch, nlat), _full((1, D_MODEL)), _full((1, D_MODEL)),
                  _full(wg.shape), _full(wu.shape), _full(wd.shape)],
        out_specs=_row_spec(D_MODEL),
        out_shape=jax.ShapeDtypeStruct(xs.shape, F32),
        compiler_params=_cparams(("parallel", "parallel"), 56),
        name="ffn_dense",
    )(xs, mod, gpre, gpost, wg, wu, wd)


def _mod_rows(mod_ref, k, n_batch, is_ctx):
    return jnp.where(is_ctx, mod_ref[n_batch:n_batch + 1, k:k + 1, :], mod_ref[0:n_batch, k:k + 1, :])


def _row_order_swap(n_outer, n_inner):
    n = n_outer * n_inner
    dst = jnp.arange(n)
    src = (dst % n_outer) * n_inner + dst // n_outer
    return (src[:, None] == jnp.arange(n)[None, :]).astype(BF16)


def _inproj_odd_kernel(x_ref, mod_ref, gain_ref, swap_ref, w_ref, u_ref, *, n_lat_steps):
    n_batch, steps, _ = x_ref.shape
    is_ctx = pl.program_id(0) >= n_lat_steps
    h = (_rms(x_ref[...], gain_ref[...]) * (1.0 + _mod_rows(mod_ref, 1, n_batch, is_ctx))
         + _mod_rows(mod_ref, 0, n_batch, is_ctx))
    h = h.reshape(n_batch * steps, D_MODEL).astype(BF16)
    h = _bdot(swap_ref[...], h).astype(BF16)
    u_ref[...] = _bdot(h, w_ref[...]).reshape(steps, n_batch, D_MODEL)


def _inproj_odd(xs, mod, gain, w_in, n_lat):
    n_batch, lt, _ = xs.shape
    p = S5_CHUNK
    swap = _row_order_swap(n_batch, p)
    return pl.pallas_call(
        functools.partial(_inproj_odd_kernel, n_lat_steps=n_lat // p),
        grid=(lt // p,),
        in_specs=[pl.BlockSpec((n_batch, p, D_MODEL), lambda i: (0, i, 0)), _full(mod.shape), _full((1, D_MODEL)),
                  _full(swap.shape), _full(w_in.shape)],
        out_specs=pl.BlockSpec((p, n_batch, D_MODEL), lambda i: (i, 0, 0)),
        out_shape=jax.ShapeDtypeStruct((lt, n_batch, D_MODEL), F32),
        compiler_params=_cparams(("parallel",)),
        name="inproj_odd",
    )(xs, mod, gain, swap, w_in)


def _s5_param_kernel(lr_ref, li_ref, ls_ref, lrx_ref, lix_ref, lsx_ref, br_ref, bi_ref,
                     abr_ref, abi_ref, bbr_ref, bbi_ref):
    def zoh(lr_raw, li, log_step):
        lr = jnp.minimum(lr_raw, -1e-4)
        dt = jnp.exp(log_step)
        mag = jnp.exp(lr * dt)
        ab_re = mag * jnp.cos(li * dt)
        ab_im = mag * jnp.sin(li * dt)
        den = lr * lr + li * li
        nr, ni = ab_re - 1.0, ab_im
        return ab_re, ab_im, (nr * lr + ni * li) / den, (ni * lr - nr * li) / den

    ab_re, ab_im, _, _ = zoh(lr_ref[...], li_ref[...], ls_ref[...])
    abr_ref[...] = ab_re
    abi_ref[...] = ab_im
    _, _, co_re, co_im = zoh(lrx_ref[...], lix_ref[...], lsx_ref[...])
    br, bi = br_ref[...], bi_ref[...]
    bbr_ref[...] = co_re * br - co_im * bi
    bbi_ref[...] = co_re * bi + co_im * br


def _s5_params(lam_re, lam_im, log_step, b_re, b_im, c_re, c_im):
    nd, g, n = lam_re.shape
    k = S5_GROUP
    rep = lambda a: jnp.repeat(a, k, axis=1)
    ls = log_step[:, :, None]
    bt_re = jnp.swapaxes(b_re, 2, 3).reshape(nd, g * k, n)
    bt_im = jnp.swapaxes(b_im, 2, 3).reshape(nd, g * k, n)
    small = jax.ShapeDtypeStruct((nd, g, n), F32)
    big = jax.ShapeDtypeStruct((nd, g * k, n), F32)
    ab_re, ab_im, bb_re, bb_im = pl.pallas_call(
        _s5_param_kernel, out_shape=[small, small, big, big], name="s5_discretise",
    )(lam_re, lam_im, ls, rep(lam_re), rep(lam_im), rep(ls), bt_re, bt_im)
    bb_re = bb_re.reshape(nd, g, k, n)
    bb_im = bb_im.reshape(nd, g, k, n)
    cmul = lambda xr, xi, yr, yi: (xr * yr - xi * yi, xr * yi + xi * yr)
    a_re, a_im = ab_re[:, :, None, :], ab_im[:, :, None, :]
    a2_re, a2_im = cmul(a_re, a_im, a_re, a_im)
    abb_re, abb_im = cmul(a_re, a_im, bb_re, bb_im)
    ca_re, ca_im = cmul(c_re, c_im, a_re, a_im)
    ca2_re, ca2_im = cmul(c_re, c_im, a2_re, a2_im)
    real_cb = lambda xr, xi: jnp.einsum('dgin,dgkn->dgik', xr, bb_re) - jnp.einsum('dgin,dgkn->dgik', xi, bb_im)
    k0 = real_cb(c_re, c_im)
    k1 = real_cb(ca_re, ca_im)
    a2 = jnp.stack([a2_re.reshape(nd, g * n), a2_im.reshape(nd, g * n)], axis=1)
    eye = jnp.eye(8, dtype=F32)
    nq = g // 8
    blocked = lambda t: t.reshape((nd, nq, 8) + t.shape[2:])

    def in_to_state(bb):
        return jnp.einsum('dqgkn,gh->dqgkhn', blocked(bb), eye).reshape(nd, nq, 8 * k, 8 * n)

    def state_to_out(c):
        return jnp.einsum('dqgin,gh->dqhngi', blocked(c), eye).reshape(nd, nq, 8 * n, 8 * k)

    def in_to_out(m):
        return jnp.einsum('dqgik,gh->dqgkhi', blocked(m), eye).reshape(nd, nq, 8 * k, 8 * k)

    w_in = jnp.concatenate([
        jnp.concatenate([in_to_state(abb_re), in_to_state(abb_im)], axis=-1),
        jnp.concatenate([in_to_state(bb_re), in_to_state(bb_im)], axis=-1)], axis=2).astype(BF16)
    w_state = jnp.stack([
        jnp.concatenate([state_to_out(ca_re), state_to_out(ca2_re)], axis=-1),
        jnp.concatenate([state_to_out(-ca_im), state_to_out(-ca2_im)], axis=-1)], axis=2).astype(BF16)
    d0, d1 = in_to_out(k0), in_to_out(k1)
    w_dir = jnp.concatenate([jnp.concatenate([d0, d1], axis=-1),
                             jnp.concatenate([jnp.zeros_like(d0), d0], axis=-1)], axis=2).astype(BF16)
    return a2, w_in, w_state, w_dir


def _s5_scan_kernel(u_ref, a2_ref, win_ref, wst_ref, wdir_ref, y_ref, sbuf, state, *, nctx_chunks):
    d = pl.program_id(0)
    i = pl.program_id(1)
    p_steps, n_batch, _ = u_ref.shape
    npair = p_steps // 2
    rows = npair * n_batch
    half = S5_LANES
    nq = win_ref.shape[0]
    kq = win_ref.shape[1] // 2
    sq = win_ref.shape[2] // 2

    @pl.when(i == 0)
    def _():
        state[...] = jnp.zeros_like(state)

    fwd = d == 0
    u = u_ref[...].reshape(npair, 2, n_batch, D_MODEL)
    u_even = u[:, 0].reshape(rows, D_MODEL).astype(BF16)
    u_odd = u[:, 1].reshape(rows, D_MODEL).astype(BF16)
    u_1 = jnp.where(fwd, u_even, u_odd)
    u_2 = jnp.where(fwd, u_odd, u_even)
    pair_in = lambda q: jnp.concatenate([u_1[:, kq * q:kq * (q + 1)], u_2[:, kq * q:kq * (q + 1)]], axis=-1)
    for q in range(nq):
        r = _bdot(pair_in(q), win_ref[q])
        sbuf[:, :, sq * q:sq * (q + 1)] = r[:, :sq].reshape(npair, n_batch, sq)
        sbuf[:, :, half + sq * q:half + sq * (q + 1)] = r[:, sq:].reshape(npair, n_batch, sq)

    for q in range(nq):
        lo = sq * q
        ar = jnp.broadcast_to(a2_ref[0:1, lo:lo + sq], (n_batch, sq))
        ai = jnp.broadcast_to(a2_ref[1:2, lo:lo + sq], (n_batch, sq))

        def body(t, carry, lo=lo, ar=ar, ai=ai):
            sr, si = carry
            tt = jnp.where(d == 0, t, npair - 1 - t)
            nr = ar * sr - ai * si + sbuf[tt, :, lo:lo + sq]
            ni = ar * si + ai * sr + sbuf[tt, :, half + lo:half + lo + sq]
            sbuf[tt, :, lo:lo + sq] = sr
            sbuf[tt, :, half + lo:half + lo + sq] = si
            return nr, ni

        sr, si = lax.fori_loop(0, npair, body, (state[:, lo:lo + sq], state[:, half + lo:half + lo + sq]))
        state[:, lo:lo + sq] = sr
        state[:, half + lo:half + lo + sq] = si

    @pl.when(i >= nctx_chunks)
    def _():
        s = sbuf[...].reshape(rows, 2 * half).astype(BF16)
        for q in range(nq):
            yq = (_bdot(s[:, sq * q:sq * (q + 1)], wst_ref[q, 0])
                  + _bdot(s[:, half + sq * q:half + sq * (q + 1)], wst_ref[q, 1])
                  + _bdot(pair_in(q), wdir_ref[q]))
            y_1 = yq[:, :kq].reshape(npair, 1, n_batch, kq)
            y_2 = yq[:, kq:].reshape(npair, 1, n_batch, kq)
            y_pair = jnp.concatenate([jnp.where(fwd, y_1, y_2), jnp.where(fwd, y_2, y_1)], axis=1)
            y_ref[:, :, kq * q:kq * (q + 1)] = y_pair.reshape(p_steps, n_batch, kq)


def _s5_scan(u3, a2, w_in, w_state, w_dir, n_lat):
    lt, n_batch, _ = u3.shape
    p = S5_SCAN_CHUNK
    nchunk = lt // p
    nlatc = n_lat // p
    nctxc = nchunk - nlatc

    def u_map(d, i):
        return (jnp.where(d == 0, lax.rem(i + nlatc, nchunk), nchunk - 1 - i), 0, 0)

    def y_map(d, i):
        return (d, jnp.where(d == 0, jnp.maximum(i - nctxc, 0), jnp.minimum(nchunk - 1 - i, nlatc - 1)), 0, 0)

    per_dir = lambda w: pl.BlockSpec((None,) + w.shape[1:], lambda d, i: (d,) + (0,) * (w.ndim - 1))
    return pl.pallas_call(
        functools.partial(_s5_scan_kernel, nctx_chunks=nctxc),
        grid=(2, nchunk),
        in_specs=[pl.BlockSpec((p, n_batch, D_MODEL), u_map), per_dir(a2), per_dir(w_in), per_dir(w_state),
                  per_dir(w_dir)],
        out_specs=pl.BlockSpec((None, p, n_batch, D_MODEL), y_map),
        out_shape=jax.ShapeDtypeStruct((2, n_lat, n_batch, D_MODEL), F32),
        scratch_shapes=[pltpu.VMEM((p // 2, n_batch, 2 * S5_LANES), F32), pltpu.VMEM((n_batch, 2 * S5_LANES), F32)],
        compiler_params=_cparams(("arbitrary", "arbitrary")),
        name="s5_scan",
    )(u3, a2, w_in, w_state, w_dir)


def _glu_kernel(yf_ref, yb_ref, u_ref, dskip_ref, wa_ref, wb_ref, x_ref, mod_ref, gpost_ref, gpre_ref, router_ref,
                swap_ref, tri_ref, xo_ref, hf_ref, rw_ref, ri_ref, cnt_ref, carry):
    n_batch, steps, _ = x_ref.shape
    rows = n_batch * steps

    @pl.when(pl.program_id(0) == 0)
    def _():
        carry[...] = jnp.zeros_like(carry)

    y = (yf_ref[...] + yb_ref[...] + dskip_ref[...] * u_ref[...]).reshape(rows, D_MODEL)
    z = jax.nn.gelu(y).astype(BF16)
    z = _bdot(swap_ref[...], z).astype(BF16)
    out = _bdot(z, wa_ref[...]) * jax.nn.sigmoid(_bdot(z, wb_ref[...]))
    out = out.reshape(n_batch, steps, D_MODEL)
    mod = lambda k: mod_ref[0:n_batch, k:k + 1, :]
    xn = x_ref[...] + mod(2) * _rms(out, gpost_ref[...])
    xo_ref[...] = xn
    hf = _rms(xn, gpre_ref[...]) * (1.0 + mod(4)) + mod(3)
    hf_ref[...] = hf
    hf = hf.reshape(rows, D_MODEL)
    h_hi = hf.astype(BF16)
    h_lo = (hf - h_hi.astype(F32)).astype(BF16)
    part = _bdot(h_hi, router_ref[...])
    logits = part[:, :LANE] + part[:, LANE:] + _bdot(h_lo, router_ref[:, :LANE])
    lane = lax.broadcasted_iota(jnp.int32, logits.shape, 1)
    neg = jnp.float32(-jnp.inf)
    lg = jnp.where(lane < N_EXPERTS, logits, neg)
    m1 = jnp.max(lg, axis=-1, keepdims=True)
    i1 = jnp.min(jnp.where(lg == m1, lane, LANE), axis=-1, keepdims=True)
    lg2 = jnp.where(lane == i1, neg, lg)
    m2 = jnp.max(lg2, axis=-1, keepdims=True)
    i2 = jnp.min(jnp.where(lg2 == m2, lane, LANE), axis=-1, keepdims=True)
    e2 = jnp.exp(m2 - m1)
    w1 = 1.0 / (1.0 + e2)
    rw_ref[...] = jnp.where(lane == 0, w1, jnp.where(lane == 1, e2 * w1, 0.0)).reshape(n_batch, steps, LANE)
    member = jnp.where(lane == i1, 1.0, jnp.where(lane == i2, 1.0, 0.0))
    base = carry[...] + _bdot(tri_ref[...], member.astype(BF16))
    r1 = jnp.sum(jnp.where(lane == i1, base, 0.0), axis=-1, keepdims=True).astype(jnp.int32)
    r2 = jnp.sum(jnp.where(lane == i2, base, 0.0), axis=-1, keepdims=True).astype(jnp.int32)
    ri = jnp.where(lane == 0, i1, jnp.where(lane == 1, i2, jnp.where(lane == 2, r1, jnp.where(lane == 3, r2, 0))))
    ri_ref[...] = ri.reshape(n_batch, steps, LANE)
    carry[...] += jnp.sum(member, axis=0, keepdims=True)
    cnt_ref[...] = carry[...]


def _glu(y, u3, d_skip, w_a, w_b, xs, mod, gpost, gpre, router, n_lat):
    n_batch = xs.shape[0]
    p = S5_CHUNK
    rows = p * n_batch
    router_p = jnp.pad(router, ((0, 0), (0, LANE - router.shape[1])))
    router_hi = router_p.astype(BF16)
    router_cat = jnp.concatenate([router_hi, (router_p - router_hi.astype(F32)).astype(BF16)], axis=1)
    tri = (jnp.arange(rows)[:, None] > jnp.arange(rows)[None, :]).astype(BF16)
    swap = _row_order_swap(p, n_batch)
    bt_spec = lambda w: pl.BlockSpec((n_batch, p, w), lambda i: (0, i, 0))
    return pl.pallas_call(
        _glu_kernel,
        grid=(n_lat // p,),
        in_specs=[pl.BlockSpec((None, p, n_batch, D_MODEL), lambda i: (0, i, 0, 0)),
                  pl.BlockSpec((None, p, n_batch, D_MODEL), lambda i: (1, i, 0, 0)),
                  pl.BlockSpec((p, n_batch, D_MODEL), lambda i: (i, 0, 0)),
                  _full((1, D_MODEL)), _full(w_a.shape), _full(w_b.shape), bt_spec(D_MODEL),
                  _full(mod.shape), _full((1, D_MODEL)), _full((1, D_MODEL)),
                  _full(router_cat.shape), _full(swap.shape), _full(tri.shape)],
        out_specs=[bt_spec(D_MODEL), bt_spec(D_MODEL), bt_spec(LANE), bt_spec(LANE), _full((1, LANE))],
        out_shape=[jax.ShapeDtypeStruct((n_batch, n_lat, D_MODEL), F32),
                   jax.ShapeDtypeStruct((n_batch, n_lat, D_MODEL), F32),
                   jax.ShapeDtypeStruct((n_batch, n_lat, LANE), F32),
                   jax.ShapeDtypeStruct((n_batch, n_lat, LANE), jnp.int32),
                   jax.ShapeDtypeStruct((1, LANE), F32)],
        scratch_shapes=[pltpu.VMEM((1, LANE), F32)],
        compiler_params=_cparams(("arbitrary",)),
        name="s5_glu_router",
    )(y, y, u3, d_skip, w_a, w_b, xs, mod, gpost, gpre, router_cat, swap, tri)


def _moe_plan(ri, counts, n_tiles):
    experts = jnp.arange(N_EXPERTS, dtype=jnp.int32)
    n_of = (counts[0, :N_EXPERTS].astype(jnp.int32) + MOE_TILE - 1) // MOE_TILE
    ends = jnp.cumsum(n_of)
    starts = ends - n_of
    start_of = jnp.sum(jnp.where(ri[:, 0:2, None] == experts, starts, 0), axis=-1)
    n_tok = ri.shape[0]
    pos = (start_of * MOE_TILE + ri[:, 2:4]).T.reshape(-1)
    n_used = ends[-1]
    tile = jnp.arange(n_tiles, dtype=jnp.int32)
    tile_expert = jnp.sum((jnp.minimum(tile, n_used - 1)[:, None] >= ends[None, :]).astype(jnp.int32), axis=1)
    n_pairs = pos.shape[0]
    n_rows = n_tiles * MOE_TILE
    pair_of = jnp.full((n_rows,), -1, jnp.int32).at[pos].set(jnp.arange(n_pairs, dtype=jnp.int32))
    is_pad = pair_of < 0
    pair_of = jnp.where(is_pad, n_pairs - 1 + jnp.cumsum(is_pad.astype(jnp.int32)), pair_of)
    src_token = jnp.where(is_pad, 0, pair_of % n_tok)
    return src_token, pair_of, tile_expert, n_used.reshape(1)


def _moe_expert_kernel(te_ref, nu_ref, src0_ref, src_ref, dst_ref, hf_ref, wg_ref, wu_ref, wd_ref, out_ref,
                       xbuf, obuf, acc, gsem, ssem, *, rows_per_step):
    del te_ref
    i = pl.program_id(0)
    j = pl.program_id(1)
    last_j = pl.num_programs(1) - 1
    nu = nu_ref[0]
    cur = lax.rem(i, 2)
    nxt = 1 - cur
    row0 = j * rows_per_step

    def gather_row(idx_ref, row, slot):
        return pltpu.make_async_copy(hf_ref.at[pl.ds(idx_ref[0, row], 1)], xbuf.at[slot, pl.ds(row, 1)], gsem.at[slot])

    def scatter_row(row, slot):
        return pltpu.make_async_copy(obuf.at[slot, pl.ds(row, 1)], out_ref.at[pl.ds(dst_ref[0, row], 1)], ssem.at[slot])

    def whole_tile_wait(sem, slot):
        pltpu.make_async_copy(xbuf.at[slot], obuf.at[slot], sem.at[slot]).wait()

    @pl.when(jnp.logical_and(i == 0, j == 0))
    def _():
        obuf[...] = jnp.zeros_like(obuf)

        def first(r, c):
            gather_row(src0_ref, r, 0).start()
            return c

        lax.fori_loop(0, MOE_TILE, first, 0)

    @pl.when(jnp.logical_and(i <= nu, j == 0))
    def _():
        whole_tile_wait(gsem, cur)

    @pl.when(i < nu)
    def _():
        h = xbuf[cur].astype(BF16)
        g = _bdot(h, wg_ref[...])
        u = _bdot(h, wu_ref[...])
        part = _bdot((g * jax.nn.sigmoid(g) * u).astype(BF16), wd_ref[...])
        acc[...] = jnp.where(j == 0, part, acc[...] + part)
        for c in range(rows_per_step):
            gather_row(src_ref, row0 + c, nxt).start()
            scatter_row(row0 + c, nxt).start()

    @pl.when(i >= nu)
    def _():
        def tail(c, carry):
            scatter_row(row0 + c, nxt).start()
            return carry

        lax.fori_loop(0, rows_per_step, tail, 0)

    @pl.when(jnp.logical_and(i >= 1, j == last_j))
    def _():
        whole_tile_wait(ssem, cur)

    @pl.when(jnp.logical_and(i < nu, j == last_j))
    def _():
        obuf[cur] = acc[...]

    @pl.when(jnp.logical_and(i == pl.num_programs(0) - 1, j == last_j))
    def _():
        whole_tile_wait(ssem, nxt)


def _combine_kernel(ya_ref, yb_ref, rw_ref, x_ref, mod_ref, gpost_ref, o_ref):
    rw = rw_ref[...]
    y = rw[:, 0:1] * ya_ref[...] + rw[:, 1:2] * yb_ref[...]
    o_ref[...] = x_ref[...] + mod_ref[5:6, :] * _rms(y, gpost_ref[...])


def _moe(hf, rw, ri, counts, xs, mod, gpost, wg, wu, wd, nlat):
    t_rows = hf.shape[0]
    d_ff = wg.shape[2]
    n_tiles = 2 * t_rows // MOE_TILE + N_EXPERTS
    nff = d_ff // MOE_FF_TILE
    n_row_tiles = t_rows // ROW_TILE
    n_rows = n_tiles * MOE_TILE
    src_token, pair_of, tile_expert, n_used = _moe_plan(ri, counts, n_tiles)
    spare = jnp.arange(MOE_TILE, dtype=jnp.int32)
    src_tiles = jnp.concatenate([src_token, 0 * spare]).reshape(n_tiles + 1, 1, MOE_TILE)
    dst_tiles = jnp.concatenate([n_rows + spare, pair_of]).reshape(n_tiles + 1, 1, MOE_TILE)
    step_expert = jnp.concatenate([tile_expert, tile_expert[-1:]])
    any_spec = pl.BlockSpec(memory_space=pl.ANY)
    idx_block = (None, 1, MOE_TILE)

    def ff_blk(i, j, nu):
        return jnp.where(i < nu[0], j, nff - 1)

    y_pairs = pl.pallas_call(
        functools.partial(_moe_expert_kernel, rows_per_step=MOE_TILE // nff),
        grid_spec=pltpu.PrefetchScalarGridSpec(
            num_scalar_prefetch=2,
            grid=(n_tiles + 1, nff),
            in_specs=[pl.BlockSpec(idx_block, lambda i, j, te, nu: (0, 0, 0), memory_space=pltpu.SMEM),
                      pl.BlockSpec(idx_block, lambda i, j, te, nu: (jnp.minimum(i + 1, n_tiles), 0, 0),
                                   memory_space=pltpu.SMEM),
                      pl.BlockSpec(idx_block, lambda i, j, te, nu: (i, 0, 0), memory_space=pltpu.SMEM),
                      any_spec,
                      pl.BlockSpec((None, D_MODEL, MOE_FF_TILE), lambda i, j, te, nu: (te[i], 0, ff_blk(i, j, nu))),
                      pl.BlockSpec((None, D_MODEL, MOE_FF_TILE), lambda i, j, te, nu: (te[i], 0, ff_blk(i, j, nu))),
                      pl.BlockSpec((None, MOE_FF_TILE, D_MODEL), lambda i, j, te, nu: (te[i], ff_blk(i, j, nu), 0))],
            out_specs=any_spec,
            scratch_shapes=[pltpu.VMEM((2, MOE_TILE, D_MODEL), F32), pltpu.VMEM((2, MOE_TILE, D_MODEL), F32),
                            pltpu.VMEM((MOE_TILE, D_MODEL), F32),
                            pltpu.SemaphoreType.DMA((2,)), pltpu.SemaphoreType.DMA((2,))]),
        out_shape=jax.ShapeDtypeStruct((n_rows + MOE_TILE, D_MODEL), F32),
        compiler_params=_cparams(("arbitrary", "arbitrary")),
        name="moe_experts",
    )(step_expert, n_used, src_tiles, src_tiles, dst_tiles, hf, wg, wu, wd)

    row = lambda w: pl.BlockSpec((ROW_TILE, w), lambda i: (i, 0))
    return pl.pallas_call(
        _combine_kernel,
        grid=(n_row_tiles,),
        in_specs=[row(D_MODEL), pl.BlockSpec((ROW_TILE, D_MODEL), lambda i: (n_row_tiles + i, 0)), row(LANE), row(D_MODEL),
                  pl.BlockSpec((None, 6, D_MODEL), lambda i: (lax.div(i, nlat), 0, 0)), _full((1, D_MODEL))],
        out_specs=row(D_MODEL),
        out_shape=jax.ShapeDtypeStruct((t_rows, D_MODEL), F32),
        compiler_params=_cparams(("parallel",)),
        name="moe_combine",
    )(y_pairs, y_pairs, rw, xs, mod, gpost)


def _rope_tables(n_lat, n_ctx):
    rows = n_lat // GRID_W
    row = jnp.repeat(jnp.arange(rows, dtype=F32), GRID_W)
    col = jnp.tile(jnp.arange(GRID_W, dtype=F32), rows)
    n_freq = HEAD_DIM // 4
    inv = ROPE_THETA ** (-jnp.arange(n_freq, dtype=F32) / n_freq)
    ang = jnp.concatenate([row[:, None] * inv, col[:, None] * inv], axis=-1)
    cos, sin = jnp.cos(ang), jnp.sin(ang)
    cos_h = jnp.concatenate([cos, cos], axis=-1)
    sin_h = jnp.concatenate([-sin, sin], axis=-1)
    cos_t = jnp.concatenate([jnp.tile(cos_h, (1, N_Q_HEADS)), jnp.ones((n_ctx, ATTN_WIDTH), F32)], axis=0)
    sin_t = jnp.concatenate([jnp.tile(sin_h, (1, N_Q_HEADS)), jnp.zeros((n_ctx, ATTN_WIDTH), F32)], axis=0)
    return cos_t, sin_t


def kernel(x, c, ctx, c_ctx, ada_w, ada_b, norm_mix_pre, norm_mix_post, norm_ffn_pre, norm_ffn_post, ev_w_in, ev_hy_conv_w, ev_hy_conv_b, ev_hy_f_w1, ev_hy_f_b1, ev_hy_f_w2, ev_hy_f_b2, ev_hy_f_wout, ev_hy_freq, ev_hy_skip, ev_q_norm, ev_k_norm, ev_w_out, ev_ffn_w_gate, ev_ffn_w_up, ev_ffn_w_down, od_w_in, od_s5_lambda_re, od_s5_lambda_im, od_s5_log_step, od_s5_b_re, od_s5_b_im, od_s5_c_re, od_s5_c_im, od_s5_d, od_glu_w_a, od_glu_w_b, od_router, od_moe_w_gate, od_moe_w_up, od_moe_w_down):
    n_batch, n_lat, _ = x.shape
    n_ctx = ctx.shape[1]
    depth = ada_w.shape[0]
    assert n_batch == 8 and n_lat % ROW_TILE == 0 and n_ctx % ROW_TILE == 0 and n_lat % n_ctx == 0
    assert depth == 2
    nlat = n_lat // ROW_TILE

    cond = jnp.concatenate([c, c_ctx[None, :], jnp.zeros((16 - n_batch - 1, D_MODEL), F32)], axis=0)
    mods = _ada_params(cond, ada_w, ada_b)
    vec = lambda a: a[None, :]

    cos_t, sin_t = _rope_tables(n_lat, n_ctx)
    hy, q, k4, v4 = _inproj_even(x, ctx, mods[0], vec(norm_mix_pre[0]), ev_w_in[0].astype(BF16),
                                 vec(jnp.tile(ev_q_norm[0], N_Q_HEADS)), vec(jnp.tile(ev_k_norm[0], N_KV_HEADS)),
                                 cos_t, sin_t, nlat)
    y_att = _attention(q, k4, v4, n_lat)
    filt_args = (ev_hy_f_w1[0], ev_hy_f_b1[0], ev_hy_f_w2[0], ev_hy_f_b2[0], ev_hy_f_wout[0], ev_hy_freq[0])
    y_hy_lat = _hyena_mixer(hy, 0, n_lat, filt_args, ev_hy_conv_w[0], ev_hy_conv_b[0], ev_hy_skip[0])
    y_hy_ctx = _hyena_mixer(hy, n_lat // n_ctx, n_ctx, filt_args, ev_hy_conv_w[0], ev_hy_conv_b[0], ev_hy_skip[0])
    xs = _outproj(y_hy_lat, y_hy_ctx, y_att, x, ctx, mods[0], vec(norm_mix_post[0]), ev_w_out[0].astype(BF16), nlat)
    xs = _ffn_dense(xs, mods[0], vec(norm_ffn_pre[0]), vec(norm_ffn_post[0]), ev_ffn_w_gate[0].astype(BF16),
                    ev_ffn_w_up[0].astype(BF16), ev_ffn_w_down[0].astype(BF16), nlat)

    u3 = _inproj_odd(xs, mods[1], vec(norm_mix_pre[1]), od_w_in[0].astype(BF16), n_lat)
    s5_ops = _s5_params(od_s5_lambda_re[0], od_s5_lambda_im[0], od_s5_log_step[0],
                        od_s5_b_re[0], od_s5_b_im[0], od_s5_c_re[0], od_s5_c_im[0])
    y = _s5_scan(u3, *s5_ops, n_lat)
    x_lat, hf, rw, ri, counts = _glu(y, u3, vec(od_s5_d[0]), od_glu_w_a[0].astype(BF16), od_glu_w_b[0].astype(BF16),
                                     xs, mods[1], vec(norm_mix_post[1]), vec(norm_ffn_pre[1]), od_router[0], n_lat)
    t_rows = n_batch * n_lat
    out = _moe(hf.reshape(t_rows, D_MODEL), rw.reshape(t_rows, LANE), ri.reshape(t_rows, LANE), counts,
               x_lat.reshape(t_rows, D_MODEL), mods[1], vec(norm_ffn_post[1]), od_moe_w_gate[0].astype(BF16),
               od_moe_w_up[0].astype(BF16), od_moe_w_down[0].astype(BF16), nlat)
    return out.reshape(n_batch, n_lat, D_MODEL)
```

```python
import functools
import math

import jax
import jax.numpy as jnp
from jax import lax
from jax.experimental import pallas as pl
from jax.experimental.pallas import tpu as pltpu

F32 = jnp.float32
BF16 = jnp.bfloat16

D_MODEL = 1024
EPS = 1e-6
GRID_W = 64

HY_WIDTH = 512
HY_ORDER = 2
HY_IN = (HY_ORDER + 1) * HY_WIDTH
HY_BANDS = 16
HY_EMB = 2 * HY_BANDS + 1
HY_DECAY_SLOW = -math.log(1e-2) / 1.5
HY_DECAY_FAST = -math.log(1e-2) / 0.3
HEAD_DIM = 64
N_Q_HEADS = 8
N_KV_HEADS = 2
Q_PER_KV = N_Q_HEADS // N_KV_HEADS
ATTN_WIDTH = N_Q_HEADS * HEAD_DIM
KV_WIDTH = N_KV_HEADS * HEAD_DIM
ROPE_THETA = 10000.0
LOG2_E = 1.0 / math.log(2.0)

S5_GROUP = 16
S5_GROUPS = D_MODEL // S5_GROUP
S5_STATE = 64
S5_LANES = S5_GROUPS * S5_STATE
N_EXPERTS = 8

ROW_TILE = 256
FFN_SUBTILES = 2
ATTN_Q_TILE = 512
FREQ_TILE = 512
S5_CHUNK = 32
S5_SCAN_CHUNK = 64
MOE_TILE = 512
MOE_FF_TILE = 1792
LANE = 128
MIB = 1024 * 1024


def _cparams(sem, vmem_mib=48):
    return pltpu.CompilerParams(dimension_semantics=sem, vmem_limit_bytes=vmem_mib * MIB)


def _rms(x, gain):
    return x * lax.rsqrt(jnp.mean(x * x, axis=-1, keepdims=True) + EPS) * gain


def _bdot(a, b):
    return jnp.dot(a, b, preferred_element_type=F32)


def _full(shape):
    zeros = (0,) * len(shape)
    return pl.BlockSpec(shape, lambda *_: zeros)


def _resident(shape):
    zeros = (0,) * len(shape)
    return pl.BlockSpec(shape, lambda *_: zeros, pipeline_mode=pl.Buffered(1))


def _ada_kernel(cond_ref, w_ref, b_ref, o_ref):
    c = cond_ref[...]
    s = (c * jax.nn.sigmoid(c)).astype(BF16)
    o_ref[...] = _bdot(s, w_ref[...].astype(BF16)) + b_ref[...]


def _ada_params(cond, ada_w, ada_b):
    depth, _, n6 = ada_w.shape
    rows = cond.shape[0]
    tn = 1536
    out = pl.pallas_call(
        _ada_kernel,
        grid=(depth, n6 // tn),
        in_specs=[pl.BlockSpec((rows, D_MODEL), lambda i, j: (0, 0)),
                  pl.BlockSpec((None, D_MODEL, tn), lambda i, j: (i, 0, j)),
                  pl.BlockSpec((None, 1, tn), lambda i, j: (i, 0, j))],
        out_specs=pl.BlockSpec((None, rows, tn), lambda i, j: (i, 0, j)),
        out_shape=jax.ShapeDtypeStruct((depth, rows, n6), F32),
        compiler_params=_cparams(("arbitrary", "arbitrary")),
        name="ada_params",
    )(cond, ada_w, ada_b.reshape(depth, 1, n6))
    return out.reshape(depth, rows, 6, D_MODEL)


def _rope_rotate(t):
    w = t.shape[-1]
    lane = lax.broadcasted_iota(jnp.int32, t.shape, 1)
    first = (lane & (HEAD_DIM - 1)) < HEAD_DIM // 2
    return jnp.where(first, pltpu.roll(t, w - HEAD_DIM // 2, 1), pltpu.roll(t, HEAD_DIM // 2, 1))


def _head_slots(t):
    lane = lax.broadcasted_iota(jnp.int32, t.shape, 1)
    lo = jnp.where(lane < HEAD_DIM, t, 0.0)
    hi = jnp.where(lane >= HEAD_DIM, t, 0.0)
    return jnp.concatenate([lo, pltpu.roll(lo, HEAD_DIM, 1), pltpu.roll(hi, HEAD_DIM, 1), hi], axis=-1)


def _inproj_even_kernel(x_ref, c_ref, mod_ref, gain_ref, w_ref, qg_ref, kg_ref, e_ref, cos_ref, sin_ref,
                        hy_ref, q_ref, k4_ref, v4_ref, *, nlat):
    xin = jnp.where(pl.program_id(1) >= nlat, c_ref[...], x_ref[...])
    h = _rms(xin, gain_ref[...]) * (1.0 + mod_ref[1:2, :]) + mod_ref[0:1, :]
    p = _bdot(h.astype(BF16), w_ref[...])
    hy_ref[...] = p[:, :HY_IN].astype(hy_ref.dtype)
    q = p[:, HY_IN:HY_IN + ATTN_WIDTH]
    k = p[:, HY_IN + ATTN_WIDTH:HY_IN + ATTN_WIDTH + KV_WIDTH]
    v = p[:, HY_IN + ATTN_WIDTH + KV_WIDTH:]
    e = e_ref[...]
    qn = q * lax.rsqrt(_bdot((q * q).astype(BF16), e) + EPS) * qg_ref[...]
    kn = k * lax.rsqrt(_bdot((k * k).astype(BF16), e[:KV_WIDTH, :KV_WIDTH]) + EPS) * kg_ref[...]
    cos = cos_ref[...]
    sin = sin_ref[...]
    qr = (qn * cos + _rope_rotate(qn) * sin) * (HEAD_DIM ** -0.5 * LOG2_E)
    kr = kn * cos[:, :KV_WIDTH] + _rope_rotate(kn) * sin[:, :KV_WIDTH]
    q_ref[...] = qr.astype(q_ref.dtype)
    k4_ref[...] = _head_slots(kr).astype(k4_ref.dtype)
    v4_ref[...] = _head_slots(v).astype(v4_ref.dtype)


def _mod_spec(n_batch, nlat):
    return pl.BlockSpec((None, 6, D_MODEL), lambda b, j: (jnp.where(j >= nlat, n_batch, b), 0, 0))


def _row_spec(width):
    return pl.BlockSpec((None, ROW_TILE, width), lambda b, j: (b, j, 0))


def _lat_ctx_specs(nlat, nctx, width):
    return [pl.BlockSpec((None, ROW_TILE, width), lambda b, j: (b, jnp.minimum(j, nlat - 1), 0)),
            pl.BlockSpec((None, ROW_TILE, width), lambda b, j: (b, jnp.clip(j - nlat, 0, nctx - 1), 0))]


def _inproj_even(x, ctx, mod, gain, w_in, q_gain, k_gain, cos_t, sin_t, nlat):
    n_batch = x.shape[0]
    nctx = ctx.shape[1] // ROW_TILE
    ntile = nlat + nctx
    lt = ntile * ROW_TILE
    n_out = w_in.shape[1]
    head_avg = jnp.kron(jnp.eye(N_Q_HEADS, dtype=F32), jnp.full((HEAD_DIM, HEAD_DIM), 1.0 / HEAD_DIM, F32)).astype(BF16)
    table = pl.BlockSpec((ROW_TILE, ATTN_WIDTH), lambda b, j: (j, 0))
    outs = [jax.ShapeDtypeStruct((n_batch, lt, HY_IN), BF16)] + [jax.ShapeDtypeStruct((n_batch, lt, ATTN_WIDTH), BF16)] * 3
    return pl.pallas_call(
        functools.partial(_inproj_even_kernel, nlat=nlat),
        grid=(n_batch, ntile),
        in_specs=_lat_ctx_specs(nlat, nctx, D_MODEL) + [
            _mod_spec(n_batch, nlat), _full((1, D_MODEL)), _full((D_MODEL, n_out)),
            _full((1, ATTN_WIDTH)), _full((1, KV_WIDTH)), _full((ATTN_WIDTH, ATTN_WIDTH)), table, table],
        out_specs=[_row_spec(HY_IN), _row_spec(ATTN_WIDTH), _row_spec(ATTN_WIDTH), _row_spec(ATTN_WIDTH)],
        out_shape=outs,
        compiler_params=_cparams(("parallel", "parallel")),
        name="inproj_even",
    )(x, ctx, mod, gain, w_in, q_gain, k_gain, head_avg, cos_t, sin_t)


def _attn_kernel(q_ref, k4_ref, v4_ref, o_ref):
    for blk in range(N_Q_HEADS // 2):
        g = (2 * blk) // Q_PER_KV
        qp = q_ref[:, LANE * blk:LANE * (blk + 1)]
        acc = None
        for r in range(2):
            slot = LANE * (2 * g + r)
            s = lax.dot_general(qp, k4_ref[:, slot:slot + LANE], (((1,), (1,)), ((), ())), preferred_element_type=F32)
            e = jnp.exp2(s - jnp.max(s, axis=-1, keepdims=True))
            inv = 1.0 / jnp.sum(e, axis=-1, keepdims=True)
            o = _bdot(e.astype(BF16), v4_ref[:, slot:slot + LANE]) * inv
            acc = o if acc is None else acc + o
        o_ref[:, LANE * blk:LANE * (blk + 1)] = acc.astype(o_ref.dtype)


def _attention(q, k4, v4, q_rows, q_tile, q_blk0, k_rows, k_blk):
    n_batch = q.shape[0]
    kv_spec = pl.BlockSpec((None, k_rows, ATTN_WIDTH), lambda b, j: (b, k_blk, 0))
    return pl.pallas_call(
        _attn_kernel,
        grid=(n_batch, q_rows // q_tile),
        in_specs=[pl.BlockSpec((None, q_tile, ATTN_WIDTH), lambda b, j: (b, q_blk0 + j, 0)), kv_spec, kv_spec],
        out_specs=pl.BlockSpec((None, q_tile, ATTN_WIDTH), lambda b, j: (b, j, 0)),
        out_shape=jax.ShapeDtypeStruct((n_batch, q_rows, ATTN_WIDTH), BF16),
        compiler_params=_cparams(("parallel", "parallel")),
        name="attention",
    )(q, k4, v4)


def _filter_kernel(z_ref, w1_ref, b1_ref, w2_ref, b2_ref, wo_ref, fr_ref, dl_ref, o_ref):
    hi = lax.Precision.HIGHEST
    z = z_ref[...]
    fr = fr_ref[...]
    h = jnp.sin(fr * (jnp.dot(z, w1_ref[...], precision=hi, preferred_element_type=F32) + b1_ref[...]))
    h = jnp.sin(fr * (jnp.dot(h, w2_ref[...], precision=hi, preferred_element_type=F32) + b2_ref[...]))
    h = jnp.dot(h, wo_ref[...], precision=hi, preferred_element_type=F32)
    decay = jnp.exp(-z[:, 0:1] * dl_ref[...])
    o_ref[...] = h * jnp.concatenate([decay] * (2 * HY_ORDER), axis=-1)


def _hyena_filters(seq, w1, b1, w2, b2, wout, freq):
    t = jnp.linspace(0.0, 1.0, seq, dtype=F32)[:, None]
    bands = jnp.linspace(1e-4, HY_BANDS - 1, HY_BANDS, dtype=F32)
    phase = (2.0 * math.pi / seq) * jnp.arange(seq, dtype=F32)[:, None] * bands
    z = jnp.concatenate([t, jnp.cos(phase), -jnp.sin(phase)], axis=-1)
    z = jnp.pad(z, ((0, 0), (0, LANE - HY_EMB)))
    w1p = jnp.pad(w1, ((0, LANE - HY_EMB), (0, 0)))
    deltas = jnp.linspace(HY_DECAY_SLOW, HY_DECAY_FAST, HY_WIDTH, dtype=F32)[None, :]
    hid = w1.shape[1]
    n_out = wout.shape[1]
    tl = ROW_TILE
    return pl.pallas_call(
        _filter_kernel,
        grid=(seq // tl,),
        in_specs=[pl.BlockSpec((tl, LANE), lambda i: (i, 0)), _full((LANE, hid)), _full((1, hid)), _full((hid, hid)),
                  _full((1, hid)), _full((hid, n_out)), _full((1, hid)), _full((1, HY_WIDTH))],
        out_specs=pl.BlockSpec((tl, n_out), lambda i: (i, 0)),
        out_shape=jax.ShapeDtypeStruct((seq, n_out), F32),
        compiler_params=_cparams(("parallel",)),
        name="hyena_filter",
    )(z, w1p, b1[None, :], w2, b2[None, :], wout, freq[None, :], deltas)


def _freq_tile(seq):
    return min(FREQ_TILE, seq)


def _dft_tables(seq):
    n = 2 * seq
    f = jnp.arange(seq, dtype=jnp.int32)[:, None]
    t = jnp.arange(seq, dtype=jnp.int32)[None, :]
    ang = ((f * t) % n).astype(F32) * (2.0 * math.pi / n)
    cos = jnp.cos(ang)
    sin = jnp.where(f == 0, jnp.where(t % 2 == 0, 1.0, -1.0), jnp.sin(ang))
    tf = _freq_tile(seq)
    fwd = jnp.concatenate([cos.reshape(seq // tf, tf, seq), sin.reshape(seq // tf, tf, seq)], axis=1)
    return fwd.astype(BF16), jnp.swapaxes(fwd, 1, 2).astype(BF16)


def _kfreq_kernel(hf_ref, hb_ref, f_ref, kre_ref, ks_ref):
    fi = pl.program_id(1)
    seq = hf_ref.shape[0]
    hf = hf_ref[...]
    row = lax.broadcasted_iota(jnp.int32, hf.shape, 0)
    hb = jnp.where(row == 0, 0.0, hb_ref[...])
    f = f_ref[...]
    a1 = _bdot(f, (hf + hb).astype(BF16))
    a2 = _bdot(f, (hf - hb).astype(BF16))
    tf = f_ref.shape[0] // 2
    frow = lax.broadcasted_iota(jnp.int32, (tf, hf.shape[1]), 0)
    dc = jnp.logical_and(frow == 0, fi == 0)
    scale = jnp.where(dc, 1.0 / (2 * seq), 2.0 / (2 * seq))
    kre_ref[...] = a1[:tf] * scale
    ks_ref[...] = jnp.where(dc, a1[tf:], a2[tf:]) * scale


def _kfreq(hfilt, fwd_tab):
    seq = hfilt.shape[0]
    tf = _freq_tile(seq)
    nf = seq // tf
    out = jax.ShapeDtypeStruct((HY_ORDER, seq, HY_WIDTH), F32)
    ospec = pl.BlockSpec((None, tf, HY_WIDTH), lambda o, fi: (o, fi, 0))
    return pl.pallas_call(
        _kfreq_kernel,
        grid=(HY_ORDER, nf),
        in_specs=[pl.BlockSpec((seq, HY_WIDTH), lambda o, fi: (0, o)),
                  pl.BlockSpec((seq, HY_WIDTH), lambda o, fi: (0, HY_ORDER + o)),
                  pl.BlockSpec((None, 2 * tf, seq), lambda o, fi: (fi, 0, 0))],
        out_specs=[ospec, ospec],
        out_shape=[out, out],
        compiler_params=_cparams(("parallel", "parallel")),
        name="hyena_kfreq",
    )(hfilt, hfilt, fwd_tab)


def _short_conv(p, w, b):
    n = p.shape[0]
    row = lax.broadcasted_iota(jnp.int32, p.shape, 0)
    prev = jnp.where(row == 0, 0.0, pltpu.roll(p, 1, 0))
    nxt = jnp.where(row == n - 1, 0.0, pltpu.roll(p, n - 1, 0))
    return b + prev * w[0:1, :] + p * w[1:2, :] + nxt * w[2:3, :]


def _hyena_kernel(vsrc_ref, gsrc_ref, cw_ref, cb_ref, skip_ref, f_ref, ft_ref, kre_ref, ks_ref, o_ref,
                  v_scr, vb_scr, acc_scr, *, conv_v):
    fi = pl.program_id(1)
    tf = f_ref.shape[0] // 2

    @pl.when(fi == 0)
    def _():
        v = vsrc_ref[...].astype(F32)
        if conv_v:
            v = _short_conv(v, cw_ref[0], cb_ref[0])
        v_scr[...] = v
        vb_scr[...] = v.astype(BF16)
        acc_scr[...] = jnp.zeros_like(acc_scr)

    xf = _bdot(f_ref[...], vb_scr[...])
    xre, xs = xf[:tf], xf[tf:]
    kre, ks = kre_ref[...], ks_ref[...]
    row = lax.broadcasted_iota(jnp.int32, xre.shape, 0)
    dc = jnp.logical_and(row == 0, fi == 0)
    yre = jnp.where(dc, xre * kre, xre * kre - xs * ks)
    ys = jnp.where(dc, xs * ks, xre * ks + xs * kre)
    y = jnp.concatenate([yre, ys], axis=0).astype(BF16)
    acc_scr[...] += _bdot(ft_ref[...], y)

    @pl.when(fi == pl.num_programs(1) - 1)
    def _():
        gate = _short_conv(gsrc_ref[...].astype(F32), cw_ref[1], cb_ref[1])
        o_ref[...] = (gate * (acc_scr[...] + v_scr[...] * skip_ref[...])).astype(o_ref.dtype)


def _hyena_order(vsrc, v_blk, v_col, hy, row_blk, order, seq, conv_w, conv_b, skip, fwd_tab, inv_tab, kre, ks):
    n_batch = hy.shape[0]
    tf = _freq_tile(seq)
    nf = seq // tf
    conv_v = order == 0
    cw = jnp.stack([conv_w[:, :HY_WIDTH], conv_w[:, (order + 1) * HY_WIDTH:(order + 2) * HY_WIDTH]])
    cb = jnp.stack([conv_b[None, :HY_WIDTH], conv_b[None, (order + 1) * HY_WIDTH:(order + 2) * HY_WIDTH]])
    kspec = pl.BlockSpec((None, tf, HY_WIDTH), lambda b, fi: (order, fi, 0))
    return pl.pallas_call(
        functools.partial(_hyena_kernel, conv_v=conv_v),
        grid=(n_batch, nf),
        in_specs=[pl.BlockSpec((None, seq, HY_WIDTH), lambda b, fi: (b, v_blk, v_col)),
                  pl.BlockSpec((None, seq, HY_WIDTH), lambda b, fi: (b, row_blk, order + 1)),
                  _full((2, 3, HY_WIDTH)), _full((2, 1, HY_WIDTH)), _full((1, HY_WIDTH)),
                  pl.BlockSpec((None, 2 * tf, seq), lambda b, fi: (fi, 0, 0)),
                  pl.BlockSpec((None, seq, 2 * tf), lambda b, fi: (fi, 0, 0)),
                  kspec, kspec],
        out_specs=pl.BlockSpec((None, seq, HY_WIDTH), lambda b, fi: (b, 0, 0)),
        out_shape=jax.ShapeDtypeStruct((n_batch, seq, HY_WIDTH), BF16),
        scratch_shapes=[pltpu.VMEM((seq, HY_WIDTH), F32), pltpu.VMEM((seq, HY_WIDTH), BF16),
                        pltpu.VMEM((seq, HY_WIDTH), F32)],
        compiler_params=_cparams(("parallel", "arbitrary")),
        name="hyena_order%d" % order,
    )(vsrc, hy, cw, cb, skip[order][None, :], fwd_tab, inv_tab, kre, ks)


def _hyena_mixer(hy, row_blk, seq, filt_args, conv_w, conv_b, skip):
    hfilt = _hyena_filters(seq, *filt_args)
    fwd_tab, inv_tab = _dft_tables(seq)
    kre, ks = _kfreq(hfilt, fwd_tab)
    v1 = _hyena_order(hy, row_blk, 0, hy, row_blk, 0, seq, conv_w, conv_b, skip, fwd_tab, inv_tab, kre, ks)
    return _hyena_order(v1, 0, 0, hy, row_blk, 1, seq, conv_w, conv_b, skip, fwd_tab, inv_tab, kre, ks)


def _outproj_kernel(yl_ref, yc_ref, al_ref, ac_ref, x_ref, c_ref, mod_ref, gpost_ref, w_ref, o_ref, *, nlat):
    is_ctx = pl.program_id(1) >= nlat
    yh = jnp.where(is_ctx, yc_ref[...], yl_ref[...])
    ya = jnp.where(is_ctx, ac_ref[...], al_ref[...])
    out = _bdot(yh, w_ref[:HY_WIDTH, :]) + _bdot(ya, w_ref[HY_WIDTH:, :])
    o_ref[...] = jnp.where(is_ctx, c_ref[...], x_ref[...]) + mod_ref[2:3, :] * _rms(out, gpost_ref[...])


def _outproj(y_lat, y_ctx, a_lat, a_ctx, x, ctx, mod, gpost, w_out, nlat):
    n_batch = x.shape[0]
    nctx = ctx.shape[1] // ROW_TILE
    ntile = nlat + nctx
    return pl.pallas_call(
        functools.partial(_outproj_kernel, nlat=nlat),
        grid=(n_batch, ntile),
        in_specs=(_lat_ctx_specs(nlat, nctx, HY_WIDTH) + _lat_ctx_specs(nlat, nctx, ATTN_WIDTH)
                  + _lat_ctx_specs(nlat, nctx, D_MODEL)
                  + [_mod_spec(n_batch, nlat), _full((1, D_MODEL)), _full(w_out.shape)]),
        out_specs=_row_spec(D_MODEL),
        out_shape=jax.ShapeDtypeStruct((n_batch, ntile * ROW_TILE, D_MODEL), F32),
        compiler_params=_cparams(("parallel", "parallel")),
        name="outproj_even",
    )(y_lat, y_ctx, a_lat, a_ctx, x, ctx, mod, gpost, w_out)


def _ffn_kernel(x_ref, mod_ref, gpre_ref, gpost_ref, wg_ref, wu_ref, wd_ref, o_ref, *, n_batch, tiles_per_batch, nlat):
    n_sub = x_ref.shape[0] // ROW_TILE
    subs = []
    for s in range(n_sub):
        tile = pl.program_id(0) * n_sub + s
        batch = lax.div(tile, tiles_per_batch)
        mod = mod_ref[jnp.where(lax.rem(tile, tiles_per_batch) >= nlat, n_batch, batch)]
        x = x_ref[s * ROW_TILE:(s + 1) * ROW_TILE, :]
        subs.append((x, mod, (_rms(x, gpre_ref[...]) * (1.0 + mod[4:5, :]) + mod[3:4, :]).astype(BF16)))
    h = jnp.concatenate([sub[2] for sub in subs], axis=0)
    g = _bdot(h, wg_ref[...])
    u = _bdot(h, wu_ref[...])
    a = (g * jax.nn.sigmoid(g) * u).astype(BF16)
    y = _bdot(a, wd_ref[...])
    for s, (x, mod, _) in enumerate(subs):
        rows = slice(s * ROW_TILE, (s + 1) * ROW_TILE)
        o_ref[rows, :] = x + mod[5:6, :] * _rms(y[rows, :], gpost_ref[...])


def _ffn_dense(xs, mod, gpre, gpost, wg, wu, wd, nlat):
    n_batch, lt, _ = xs.shape
    tiles_per_batch = lt // ROW_TILE
    rows = FFN_SUBTILES * ROW_TILE
    n_steps = n_batch * tiles_per_batch // FFN_SUBTILES
    blk = pl.BlockSpec((rows, D_MODEL), lambda i: (i, 0))
    out = pl.pallas_call(
        functools.partial(_ffn_kernel, n_batch=n_batch, tiles_per_batch=tiles_per_batch, nlat=nlat),
        grid=(n_steps,),
        in_specs=[blk, _resident(mod.shape), _resident((1, D_MODEL)), _resident((1, D_MODEL)),
                  _resident(wg.shape), _resident(wu.shape), _resident(wd.shape)],
        out_specs=blk,
        out_shape=jax.ShapeDtypeStruct((n_batch * lt, D_MODEL), F32),
        compiler_params=_cparams(("parallel",), 56),
        name="ffn_dense",
    )(xs.reshape(n_batch * lt, D_MODEL), mod, gpre, gpost, wg, wu, wd)
    return out.reshape(xs.shape)


def _mod_rows(mod_ref, k, n_batch, is_ctx):
    return jnp.where(is_ctx, mod_ref[n_batch:n_batch + 1, k:k + 1, :], mod_ref[0:n_batch, k:k + 1, :])


def _row_order_swap(n_outer, n_inner):
    n = n_outer * n_inner
    dst = jnp.arange(n)
    src = (dst % n_outer) * n_inner + dst // n_outer
    return (src[:, None] == jnp.arange(n)[None, :]).astype(BF16)


def _inproj_odd_kernel(x_ref, mod_ref, gain_ref, swap_ref, w_ref, u_ref, *, n_lat_steps):
    n_batch, steps, _ = x_ref.shape
    is_ctx = pl.program_id(0) >= n_lat_steps
    h = (_rms(x_ref[...], gain_ref[...]) * (1.0 + _mod_rows(mod_ref, 1, n_batch, is_ctx))
         + _mod_rows(mod_ref, 0, n_batch, is_ctx))
    h = h.reshape(n_batch * steps, D_MODEL).astype(BF16)
    h = _bdot(swap_ref[...], h).astype(BF16)
    u_ref[...] = _bdot(h, w_ref[...]).reshape(steps, n_batch, D_MODEL)


def _inproj_odd(xs, mod, gain, w_in, n_lat):
    n_batch, lt, _ = xs.shape
    p = S5_CHUNK
    swap = _row_order_swap(n_batch, p)
    return pl.pallas_call(
        functools.partial(_inproj_odd_kernel, n_lat_steps=n_lat // p),
        grid=(lt // p,),
        in_specs=[pl.BlockSpec((n_batch, p, D_MODEL), lambda i: (0, i, 0)), _full(mod.shape), _full((1, D_MODEL)),
                  _full(swap.shape), _full(w_in.shape)],
        out_specs=pl.BlockSpec((p, n_batch, D_MODEL), lambda i: (i, 0, 0)),
        out_shape=jax.ShapeDtypeStruct((lt, n_batch, D_MODEL), F32),
        compiler_params=_cparams(("parallel",)),
        name="inproj_odd",
    )(xs, mod, gain, swap, w_in)


def _s5_param_kernel(lr_ref, li_ref, ls_ref, lrx_ref, lix_ref, lsx_ref, br_ref, bi_ref,
                     abr_ref, abi_ref, bbr_ref, bbi_ref):
    def zoh(lr_raw, li, log_step):
        lr = jnp.minimum(lr_raw, -1e-4)
        dt = jnp.exp(log_step)
        mag = jnp.exp(lr * dt)
        ab_re = mag * jnp.cos(li * dt)
        ab_im = mag * jnp.sin(li * dt)
        den = lr * lr + li * li
        nr, ni = ab_re - 1.0, ab_im
        return ab_re, ab_im, (nr * lr + ni * li) / den, (ni * lr - nr * li) / den

    ab_re, ab_im, _, _ = zoh(lr_ref[...], li_ref[...], ls_ref[...])
    abr_ref[...] = ab_re
    abi_ref[...] = ab_im
    _, _, co_re, co_im = zoh(lrx_ref[...], lix_ref[...], lsx_ref[...])
    br, bi = br_ref[...], bi_ref[...]
    bbr_ref[...] = co_re * br - co_im * bi
    bbi_ref[...] = co_re * bi + co_im * br


def _s5_params(lam_re, lam_im, log_step, b_re, b_im, c_re, c_im):
    nd, g, n = lam_re.shape
    k = S5_GROUP
    rep = lambda a: jnp.repeat(a, k, axis=1)
    ls = log_step[:, :, None]
    bt_re = jnp.swapaxes(b_re, 2, 3).reshape(nd, g * k, n)
    bt_im = jnp.swapaxes(b_im, 2, 3).reshape(nd, g * k, n)
    small = jax.ShapeDtypeStruct((nd, g, n), F32)
    big = jax.ShapeDtypeStruct((nd, g * k, n), F32)
    ab_re, ab_im, bb_re, bb_im = pl.pallas_call(
        _s5_param_kernel, out_shape=[small, small, big, big], name="s5_discretise",
    )(lam_re, lam_im, ls, rep(lam_re), rep(lam_im), rep(ls), bt_re, bt_im)
    bb_re = bb_re.reshape(nd, g, k, n)
    bb_im = bb_im.reshape(nd, g, k, n)
    cmul = lambda xr, xi, yr, yi: (xr * yr - xi * yi, xr * yi + xi * yr)
    a_re, a_im = ab_re[:, :, None, :], ab_im[:, :, None, :]
    a2_re, a2_im = cmul(a_re, a_im, a_re, a_im)
    abb_re, abb_im = cmul(a_re, a_im, bb_re, bb_im)
    ca_re, ca_im = cmul(c_re, c_im, a_re, a_im)
    ca2_re, ca2_im = cmul(c_re, c_im, a2_re, a2_im)
    real_cb = lambda xr, xi: jnp.einsum('dgin,dgkn->dgik', xr, bb_re) - jnp.einsum('dgin,dgkn->dgik', xi, bb_im)
    k0 = real_cb(c_re, c_im)
    k1 = real_cb(ca_re, ca_im)
    a2 = jnp.stack([a2_re.reshape(nd, g * n), a2_im.reshape(nd, g * n)], axis=1)
    eye = jnp.eye(8, dtype=F32)
    nq = g // 8
    blocked = lambda t: t.reshape((nd, nq, 8) + t.shape[2:])

    def in_to_state(bb):
        return jnp.einsum('dqgkn,gh->dqgkhn', blocked(bb), eye).reshape(nd, nq, 8 * k, 8 * n)

    def state_to_out(c):
        return jnp.einsum('dqgin,gh->dqhngi', blocked(c), eye).reshape(nd, nq, 8 * n, 8 * k)

    def in_to_out(m):
        return jnp.einsum('dqgik,gh->dqgkhi', blocked(m), eye).reshape(nd, nq, 8 * k, 8 * k)

    w_in = jnp.concatenate([
        jnp.concatenate([in_to_state(abb_re), in_to_state(abb_im)], axis=-1),
        jnp.concatenate([in_to_state(bb_re), in_to_state(bb_im)], axis=-1)], axis=2).astype(BF16)
    w_state = jnp.stack([
        jnp.concatenate([state_to_out(ca_re), state_to_out(ca2_re)], axis=-1),
        jnp.concatenate([state_to_out(-ca_im), state_to_out(-ca2_im)], axis=-1)], axis=2).astype(BF16)
    d0, d1 = in_to_out(k0), in_to_out(k1)
    w_dir = jnp.concatenate([jnp.concatenate([d0, d1], axis=-1),
                             jnp.concatenate([jnp.zeros_like(d0), d0], axis=-1)], axis=2).astype(BF16)
    return a2, w_in, w_state, w_dir


def _s5_scan_kernel(u_ref, a2_ref, win_ref, wst_ref, wdir_ref, y_ref, sbuf, state, *, nctx_chunks):
    d = pl.program_id(0)
    i = pl.program_id(1)
    p_steps, n_batch, _ = u_ref.shape
    npair = p_steps // 2
    rows = npair * n_batch
    half = S5_LANES
    nq = win_ref.shape[0]
    kq = win_ref.shape[1] // 2
    sq = win_ref.shape[2] // 2

    @pl.when(i == 0)
    def _():
        state[...] = jnp.zeros_like(state)

    fwd = d == 0
    u = u_ref[...].reshape(npair, 2, n_batch, D_MODEL)
    u_even = u[:, 0].reshape(rows, D_MODEL).astype(BF16)
    u_odd = u[:, 1].reshape(rows, D_MODEL).astype(BF16)
    u_1 = jnp.where(fwd, u_even, u_odd)
    u_2 = jnp.where(fwd, u_odd, u_even)
    pair_in = lambda q: jnp.concatenate([u_1[:, kq * q:kq * (q + 1)], u_2[:, kq * q:kq * (q + 1)]], axis=-1)
    for q in range(nq):
        r = _bdot(pair_in(q), win_ref[q])
        sbuf[:, :, sq * q:sq * (q + 1)] = r[:, :sq].reshape(npair, n_batch, sq)
        sbuf[:, :, half + sq * q:half + sq * (q + 1)] = r[:, sq:].reshape(npair, n_batch, sq)

    for q in range(nq):
        lo = sq * q
        ar = jnp.broadcast_to(a2_ref[0:1, lo:lo + sq], (n_batch, sq))
        ai = jnp.broadcast_to(a2_ref[1:2, lo:lo + sq], (n_batch, sq))

        def body(t, carry, lo=lo, ar=ar, ai=ai):
            sr, si = carry
            tt = jnp.where(d == 0, t, npair - 1 - t)
            nr = ar * sr - ai * si + sbuf[tt, :, lo:lo + sq]
            ni = ar * si + ai * sr + sbuf[tt, :, half + lo:half + lo + sq]
            sbuf[tt, :, lo:lo + sq] = sr
            sbuf[tt, :, half + lo:half + lo + sq] = si
            return nr, ni

        sr, si = lax.fori_loop(0, npair, body, (state[:, lo:lo + sq], state[:, half + lo:half + lo + sq]))
        state[:, lo:lo + sq] = sr
        state[:, half + lo:half + lo + sq] = si

    @pl.when(i >= nctx_chunks)
    def _():
        s = sbuf[...].reshape(rows, 2 * half).astype(BF16)
        for q in range(nq):
            yq = (_bdot(s[:, sq * q:sq * (q + 1)], wst_ref[q, 0])
                  + _bdot(s[:, half + sq * q:half + sq * (q + 1)], wst_ref[q, 1])
                  + _bdot(pair_in(q), wdir_ref[q]))
            y_1 = yq[:, :kq].reshape(npair, 1, n_batch, kq)
            y_2 = yq[:, kq:].reshape(npair, 1, n_batch, kq)
            y_pair = jnp.concatenate([jnp.where(fwd, y_1, y_2), jnp.where(fwd, y_2, y_1)], axis=1)
            y_ref[:, :, kq * q:kq * (q + 1)] = y_pair.reshape(p_steps, n_batch, kq)


def _s5_scan(u3, a2, w_in, w_state, w_dir, n_lat):
    lt, n_batch, _ = u3.shape
    p = S5_SCAN_CHUNK
    nchunk = lt // p
    nlatc = n_lat // p
    nctxc = nchunk - nlatc

    def u_map(d, i):
        return (jnp.where(d == 0, lax.rem(i + nlatc, nchunk), nchunk - 1 - i), 0, 0)

    def y_map(d, i):
        return (d, jnp.where(d == 0, jnp.maximum(i - nctxc, 0), jnp.minimum(nchunk - 1 - i, nlatc - 1)), 0, 0)

    per_dir = lambda w: pl.BlockSpec((None,) + w.shape[1:], lambda d, i: (d,) + (0,) * (w.ndim - 1))
    return pl.pallas_call(
        functools.partial(_s5_scan_kernel, nctx_chunks=nctxc),
        grid=(2, nchunk),
        in_specs=[pl.BlockSpec((p, n_batch, D_MODEL), u_map), per_dir(a2), per_dir(w_in), per_dir(w_state),
                  per_dir(w_dir)],
        out_specs=pl.BlockSpec((None, p, n_batch, D_MODEL), y_map),
        out_shape=jax.ShapeDtypeStruct((2, n_lat, n_batch, D_MODEL), F32),
        scratch_shapes=[pltpu.VMEM((p // 2, n_batch, 2 * S5_LANES), F32), pltpu.VMEM((n_batch, 2 * S5_LANES), F32)],
        compiler_params=_cparams(("arbitrary", "arbitrary")),
        name="s5_scan",
    )(u3, a2, w_in, w_state, w_dir)


def _glu_kernel(yf_ref, yb_ref, u_ref, dskip_ref, wa_ref, wb_ref, x_ref, mod_ref, gpost_ref, gpre_ref, router_ref,
                swap_ref, tri_ref, xo_ref, hf_ref, rw_ref, ri_ref, cnt_ref, carry):
    n_batch, steps, _ = x_ref.shape
    rows = n_batch * steps

    @pl.when(pl.program_id(0) == 0)
    def _():
        carry[...] = jnp.zeros_like(carry)

    y = (yf_ref[...] + yb_ref[...] + dskip_ref[...] * u_ref[...]).reshape(rows, D_MODEL)
    z = jax.nn.gelu(y).astype(BF16)
    z = _bdot(swap_ref[...], z).astype(BF16)
    out = _bdot(z, wa_ref[...]) * jax.nn.sigmoid(_bdot(z, wb_ref[...]))
    out = out.reshape(n_batch, steps, D_MODEL)
    mod = lambda k: mod_ref[0:n_batch, k:k + 1, :]
    xn = x_ref[...] + mod(2) * _rms(out, gpost_ref[...])
    xo_ref[...] = xn
    hf = _rms(xn, gpre_ref[...]) * (1.0 + mod(4)) + mod(3)
    hf_ref[...] = hf
    hf = hf.reshape(rows, D_MODEL)
    h_hi = hf.astype(BF16)
    h_lo = (hf - h_hi.astype(F32)).astype(BF16)
    part = _bdot(h_hi, router_ref[...])
    logits = part[:, :LANE] + part[:, LANE:] + _bdot(h_lo, router_ref[:, :LANE])
    lane = lax.broadcasted_iota(jnp.int32, logits.shape, 1)
    neg = jnp.float32(-jnp.inf)
    lg = jnp.where(lane < N_EXPERTS, logits, neg)
    m1 = jnp.max(lg, axis=-1, keepdims=True)
    i1 = jnp.min(jnp.where(lg == m1, lane, LANE), axis=-1, keepdims=True)
    lg2 = jnp.where(lane == i1, neg, lg)
    m2 = jnp.max(lg2, axis=-1, keepdims=True)
    i2 = jnp.min(jnp.where(lg2 == m2, lane, LANE), axis=-1, keepdims=True)
    e2 = jnp.exp(m2 - m1)
    w1 = 1.0 / (1.0 + e2)
    rw_ref[...] = jnp.where(lane == 0, w1, jnp.where(lane == 1, e2 * w1, 0.0)).reshape(n_batch, steps, LANE)
    member = jnp.where(lane == i1, 1.0, jnp.where(lane == i2, 1.0, 0.0))
    base = carry[...] + _bdot(tri_ref[...], member.astype(BF16))
    r1 = jnp.sum(jnp.where(lane == i1, base, 0.0), axis=-1, keepdims=True).astype(jnp.int32)
    r2 = jnp.sum(jnp.where(lane == i2, base, 0.0), axis=-1, keepdims=True).astype(jnp.int32)
    ri = jnp.where(lane == 0, i1, jnp.where(lane == 1, i2, jnp.where(lane == 2, r1, jnp.where(lane == 3, r2, 0))))
    ri_ref[...] = ri.reshape(n_batch, steps, LANE)
    carry[...] += jnp.sum(member, axis=0, keepdims=True)
    cnt_ref[...] = carry[...]


def _glu(y, u3, d_skip, w_a, w_b, xs, mod, gpost, gpre, router, n_lat):
    n_batch = xs.shape[0]
    p = S5_CHUNK
    rows = p * n_batch
    router_p = jnp.pad(router, ((0, 0), (0, LANE - router.shape[1])))
    router_hi = router_p.astype(BF16)
    router_cat = jnp.concatenate([router_hi, (router_p - router_hi.astype(F32)).astype(BF16)], axis=1)
    tri = (jnp.arange(rows)[:, None] > jnp.arange(rows)[None, :]).astype(BF16)
    swap = _row_order_swap(p, n_batch)
    bt_spec = lambda w: pl.BlockSpec((n_batch, p, w), lambda i: (0, i, 0))
    return pl.pallas_call(
        _glu_kernel,
        grid=(n_lat // p,),
        in_specs=[pl.BlockSpec((None, p, n_batch, D_MODEL), lambda i: (0, i, 0, 0)),
                  pl.BlockSpec((None, p, n_batch, D_MODEL), lambda i: (1, i, 0, 0)),
                  pl.BlockSpec((p, n_batch, D_MODEL), lambda i: (i, 0, 0)),
                  _full((1, D_MODEL)), _full(w_a.shape), _full(w_b.shape), bt_spec(D_MODEL),
                  _full(mod.shape), _full((1, D_MODEL)), _full((1, D_MODEL)),
                  _full(router_cat.shape), _full(swap.shape), _full(tri.shape)],
        out_specs=[bt_spec(D_MODEL), bt_spec(D_MODEL), bt_spec(LANE), bt_spec(LANE), _full((1, LANE))],
        out_shape=[jax.ShapeDtypeStruct((n_batch, n_lat, D_MODEL), F32),
                   jax.ShapeDtypeStruct((n_batch, n_lat, D_MODEL), F32),
                   jax.ShapeDtypeStruct((n_batch, n_lat, LANE), F32),
                   jax.ShapeDtypeStruct((n_batch, n_lat, LANE), jnp.int32),
                   jax.ShapeDtypeStruct((1, LANE), F32)],
        scratch_shapes=[pltpu.VMEM((1, LANE), F32)],
        compiler_params=_cparams(("arbitrary",)),
        name="s5_glu_router",
    )(y, y, u3, d_skip, w_a, w_b, xs, mod, gpost, gpre, router_cat, swap, tri)


def _moe_plan(ri, counts, n_tiles):
    experts = jnp.arange(N_EXPERTS, dtype=jnp.int32)
    n_of = (counts[0, :N_EXPERTS].astype(jnp.int32) + MOE_TILE - 1) // MOE_TILE
    ends = jnp.cumsum(n_of)
    starts = ends - n_of
    start_of = jnp.sum(jnp.where(ri[:, 0:2, None] == experts, starts, 0), axis=-1)
    n_tok = ri.shape[0]
    pos = (start_of * MOE_TILE + ri[:, 2:4]).T.reshape(-1)
    n_used = ends[-1]
    tile = jnp.arange(n_tiles, dtype=jnp.int32)
    tile_expert = jnp.sum((jnp.minimum(tile, n_used - 1)[:, None] >= ends[None, :]).astype(jnp.int32), axis=1)
    n_pairs = pos.shape[0]
    n_rows = n_tiles * MOE_TILE
    pair_of = jnp.full((n_rows,), -1, jnp.int32).at[pos].set(jnp.arange(n_pairs, dtype=jnp.int32))
    is_pad = pair_of < 0
    pair_of = jnp.where(is_pad, n_pairs - 1 + jnp.cumsum(is_pad.astype(jnp.int32)), pair_of)
    src_token = jnp.where(is_pad, 0, pair_of % n_tok)
    return src_token, pair_of, tile_expert, n_used.reshape(1)


def _moe_expert_kernel(te_ref, nu_ref, src0_ref, src_ref, dst_ref, hf_ref, wg_ref, wu_ref, wd_ref, out_ref,
                       xbuf, obuf, acc, gsem, ssem, *, rows_per_step):
    del te_ref
    i = pl.program_id(0)
    j = pl.program_id(1)
    last_j = pl.num_programs(1) - 1
    nu = nu_ref[0]
    cur = lax.rem(i, 2)
    nxt = 1 - cur
    row0 = j * rows_per_step

    def gather_row(idx_ref, row, slot):
        return pltpu.make_async_copy(hf_ref.at[pl.ds(idx_ref[0, row], 1)], xbuf.at[slot, pl.ds(row, 1)], gsem.at[slot])

    def scatter_row(row, slot):
        return pltpu.make_async_copy(obuf.at[slot, pl.ds(row, 1)], out_ref.at[pl.ds(dst_ref[0, row], 1)], ssem.at[slot])

    def whole_tile_wait(sem, slot):
        pltpu.make_async_copy(xbuf.at[slot], obuf.at[slot], sem.at[slot]).wait()

    @pl.when(jnp.logical_and(i == 0, j == 0))
    def _():
        obuf[...] = jnp.zeros_like(obuf)

        def first(r, c):
            gather_row(src0_ref, r, 0).start()
            return c

        lax.fori_loop(0, MOE_TILE, first, 0)

    @pl.when(jnp.logical_and(i <= nu, j == 0))
    def _():
        whole_tile_wait(gsem, cur)

    @pl.when(i < nu)
    def _():
        h = xbuf[cur].astype(BF16)
        g = _bdot(h, wg_ref[...])
        u = _bdot(h, wu_ref[...])
        part = _bdot((g * jax.nn.sigmoid(g) * u).astype(BF16), wd_ref[...])
        acc[...] = jnp.where(j == 0, part, acc[...] + part)

    for step in range(MOE_TILE // rows_per_step):
        @pl.when(jnp.logical_and(i < nu, j == step))
        def _(step=step):
            for row in range(step * rows_per_step, (step + 1) * rows_per_step):
                gather_row(src_ref, row, nxt).start()
                scatter_row(row, nxt).start()

    @pl.when(i >= nu)
    def _():
        def tail(c, carry):
            scatter_row(row0 + c, nxt).start()
            return carry

        lax.fori_loop(0, rows_per_step, tail, 0)

    @pl.when(jnp.logical_and(i >= 1, j == last_j))
    def _():
        whole_tile_wait(ssem, cur)

    @pl.when(jnp.logical_and(i < nu, j == last_j))
    def _():
        obuf[cur] = acc[...]

    @pl.when(jnp.logical_and(i == pl.num_programs(0) - 1, j == last_j))
    def _():
        whole_tile_wait(ssem, nxt)


def _combine_kernel(ya_ref, yb_ref, rw_ref, x_ref, mod_ref, gpost_ref, o_ref):
    rw = rw_ref[...]
    y = rw[:, 0:1] * ya_ref[...] + rw[:, 1:2] * yb_ref[...]
    o_ref[...] = x_ref[...] + mod_ref[5:6, :] * _rms(y, gpost_ref[...])


def _moe(hf, rw, ri, counts, xs, mod, gpost, wg, wu, wd, nlat):
    t_rows = hf.shape[0]
    d_ff = wg.shape[2]
    n_tiles = 2 * t_rows // MOE_TILE + N_EXPERTS
    nff = d_ff // MOE_FF_TILE
    n_row_tiles = t_rows // ROW_TILE
    n_rows = n_tiles * MOE_TILE
    src_token, pair_of, tile_expert, n_used = _moe_plan(ri, counts, n_tiles)
    spare = jnp.arange(MOE_TILE, dtype=jnp.int32)
    src_tiles = jnp.concatenate([src_token, 0 * spare]).reshape(n_tiles + 1, 1, MOE_TILE)
    dst_tiles = jnp.concatenate([n_rows + spare, pair_of]).reshape(n_tiles + 1, 1, MOE_TILE)
    step_expert = jnp.concatenate([tile_expert, tile_expert[-1:]])
    any_spec = pl.BlockSpec(memory_space=pl.ANY)
    idx_block = (None, 1, MOE_TILE)

    def ff_blk(i, j, nu):
        return jnp.where(i < nu[0], j, nff - 1)

    y_pairs = pl.pallas_call(
        functools.partial(_moe_expert_kernel, rows_per_step=MOE_TILE // nff),
        grid_spec=pltpu.PrefetchScalarGridSpec(
            num_scalar_prefetch=2,
            grid=(n_tiles + 1, nff),
            in_specs=[pl.BlockSpec(idx_block, lambda i, j, te, nu: (0, 0, 0), memory_space=pltpu.SMEM),
                      pl.BlockSpec(idx_block, lambda i, j, te, nu: (jnp.minimum(i + 1, n_tiles), 0, 0),
                                   memory_space=pltpu.SMEM),
                      pl.BlockSpec(idx_block, lambda i, j, te, nu: (i, 0, 0), memory_space=pltpu.SMEM),
                      any_spec,
                      pl.BlockSpec((None, D_MODEL, MOE_FF_TILE), lambda i, j, te, nu: (te[i], 0, ff_blk(i, j, nu))),
                      pl.BlockSpec((None, D_MODEL, MOE_FF_TILE), lambda i, j, te, nu: (te[i], 0, ff_blk(i, j, nu))),
                      pl.BlockSpec((None, MOE_FF_TILE, D_MODEL), lambda i, j, te, nu: (te[i], ff_blk(i, j, nu), 0))],
            out_specs=any_spec,
            scratch_shapes=[pltpu.VMEM((2, MOE_TILE, D_MODEL), F32), pltpu.VMEM((2, MOE_TILE, D_MODEL), F32),
                            pltpu.VMEM((MOE_TILE, D_MODEL), F32),
                            pltpu.SemaphoreType.DMA((2,)), pltpu.SemaphoreType.DMA((2,))]),
        out_shape=jax.ShapeDtypeStruct((n_rows + MOE_TILE, D_MODEL), F32),
        compiler_params=_cparams(("arbitrary", "arbitrary")),
        name="moe_experts",
    )(step_expert, n_used, src_tiles, src_tiles, dst_tiles, hf, wg, wu, wd)

    row = lambda w: pl.BlockSpec((ROW_TILE, w), lambda i: (i, 0))
    return pl.pallas_call(
        _combine_kernel,
        grid=(n_row_tiles,),
        in_specs=[row(D_MODEL), pl.BlockSpec((ROW_TILE, D_MODEL), lambda i: (n_row_tiles + i, 0)), row(LANE), row(D_MODEL),
                  pl.BlockSpec((None, 6, D_MODEL), lambda i: (lax.div(i, nlat), 0, 0)), _full((1, D_MODEL))],
        out_specs=row(D_MODEL),
        out_shape=jax.ShapeDtypeStruct((t_rows, D_MODEL), F32),
        compiler_params=_cparams(("parallel",)),
        name="moe_combine",
    )(y_pairs, y_pairs, rw, xs, mod, gpost)


def _rope_tables(n_lat, n_ctx):
    rows = n_lat // GRID_W
    row = jnp.repeat(jnp.arange(rows, dtype=F32), GRID_W)
    col = jnp.tile(jnp.arange(GRID_W, dtype=F32), rows)
    n_freq = HEAD_DIM // 4
    inv = ROPE_THETA ** (-jnp.arange(n_freq, dtype=F32) / n_freq)
    ang = jnp.concatenate([row[:, None] * inv, col[:, None] * inv], axis=-1)
    cos, sin = jnp.cos(ang), jnp.sin(ang)
    cos_h = jnp.concatenate([cos, cos], axis=-1)
    sin_h = jnp.concatenate([-sin, sin], axis=-1)
    cos_t = jnp.concatenate([jnp.tile(cos_h, (1, N_Q_HEADS)), jnp.ones((n_ctx, ATTN_WIDTH), F32)], axis=0)
    sin_t = jnp.concatenate([jnp.tile(sin_h, (1, N_Q_HEADS)), jnp.zeros((n_ctx, ATTN_WIDTH), F32)], axis=0)
    return cos_t, sin_t


def kernel(x, c, ctx, c_ctx, ada_w, ada_b, norm_mix_pre, norm_mix_post, norm_ffn_pre, norm_ffn_post, ev_w_in, ev_hy_conv_w, ev_hy_conv_b, ev_hy_f_w1, ev_hy_f_b1, ev_hy_f_w2, ev_hy_f_b2, ev_hy_f_wout, ev_hy_freq, ev_hy_skip, ev_q_norm, ev_k_norm, ev_w_out, ev_ffn_w_gate, ev_ffn_w_up, ev_ffn_w_down, od_w_in, od_s5_lambda_re, od_s5_lambda_im, od_s5_log_step, od_s5_b_re, od_s5_b_im, od_s5_c_re, od_s5_c_im, od_s5_d, od_glu_w_a, od_glu_w_b, od_router, od_moe_w_gate, od_moe_w_up, od_moe_w_down):
    n_batch, n_lat, _ = x.shape
    n_ctx = ctx.shape[1]
    depth = ada_w.shape[0]
    assert n_batch == 8 and n_lat % ROW_TILE == 0 and n_ctx % ROW_TILE == 0 and n_lat % n_ctx == 0
    assert depth == 2
    nlat = n_lat // ROW_TILE

    cond = jnp.concatenate([c, c_ctx[None, :], jnp.zeros((16 - n_batch - 1, D_MODEL), F32)], axis=0)
    mods = _ada_params(cond, ada_w, ada_b)
    vec = lambda a: a[None, :]

    cos_t, sin_t = _rope_tables(n_lat, n_ctx)
    hy, q, k4, v4 = _inproj_even(x, ctx, mods[0], vec(norm_mix_pre[0]), ev_w_in[0].astype(BF16),
                                 vec(jnp.tile(ev_q_norm[0], N_Q_HEADS)), vec(jnp.tile(ev_k_norm[0], N_KV_HEADS)),
                                 cos_t, sin_t, nlat)
    lt = n_lat + n_ctx
    a_lat = _attention(q, k4, v4, n_lat, min(ATTN_Q_TILE, n_lat), 0, lt, 0)
    a_ctx = _attention(q, k4, v4, n_ctx, n_ctx, n_lat // n_ctx, n_ctx, n_lat // n_ctx)
    filt_args = (ev_hy_f_w1[0], ev_hy_f_b1[0], ev_hy_f_w2[0], ev_hy_f_b2[0], ev_hy_f_wout[0], ev_hy_freq[0])
    y_hy_lat = _hyena_mixer(hy, 0, n_lat, filt_args, ev_hy_conv_w[0], ev_hy_conv_b[0], ev_hy_skip[0])
    y_hy_ctx = _hyena_mixer(hy, n_lat // n_ctx, n_ctx, filt_args, ev_hy_conv_w[0], ev_hy_conv_b[0], ev_hy_skip[0])
    xs = _outproj(y_hy_lat, y_hy_ctx, a_lat, a_ctx, x, ctx, mods[0], vec(norm_mix_post[0]), ev_w_out[0].astype(BF16), nlat)
    xs = _ffn_dense(xs, mods[0], vec(norm_ffn_pre[0]), vec(norm_ffn_post[0]), ev_ffn_w_gate[0].astype(BF16),
                    ev_ffn_w_up[0].astype(BF16), ev_ffn_w_down[0].astype(BF16), nlat)

    u3 = _inproj_odd(xs, mods[1], vec(norm_mix_pre[1]), od_w_in[0].astype(BF16), n_lat)
    s5_ops = _s5_params(od_s5_lambda_re[0], od_s5_lambda_im[0], od_s5_log_step[0],
                        od_s5_b_re[0], od_s5_b_im[0], od_s5_c_re[0], od_s5_c_im[0])
    y = _s5_scan(u3, *s5_ops, n_lat)
    x_lat, hf, rw, ri, counts = _glu(y, u3, vec(od_s5_d[0]), od_glu_w_a[0].astype(BF16), od_glu_w_b[0].astype(BF16),
                                     xs, mods[1], vec(norm_mix_post[1]), vec(norm_ffn_pre[1]), od_router[0], n_lat)
    t_rows = n_batch * n_lat
    out = _moe(hf.reshape(t_rows, D_MODEL), rw.reshape(t_rows, LANE), ri.reshape(t_rows, LANE), counts,
               x_lat.reshape(t_rows, D_MODEL), mods[1], vec(norm_ffn_post[1]), od_moe_w_gate[0].astype(BF16),
               od_moe_w_up[0].astype(BF16), od_moe_w_down[0].astype(BF16), nlat)
    return out.reshape(n_batch, n_lat, D_MODEL)
```

```python
import functools
import math

import jax
import jax.numpy as jnp
from jax import lax
from jax.experimental import pallas as pl
from jax.experimental.pallas import tpu as pltpu

F32 = jnp.float32
BF16 = jnp.bfloat16

D_MODEL = 1024
EPS = 1e-6
GRID_W = 64

HY_WIDTH = 512
HY_ORDER = 2
HY_IN = (HY_ORDER + 1) * HY_WIDTH
HY_BANDS = 16
HY_EMB = 2 * HY_BANDS + 1
HY_DECAY_SLOW = -math.log(1e-2) / 1.5
HY_DECAY_FAST = -math.log(1e-2) / 0.3
HEAD_DIM = 64
N_Q_HEADS = 8
N_KV_HEADS = 2
Q_PER_KV = N_Q_HEADS // N_KV_HEADS
ATTN_WIDTH = N_Q_HEADS * HEAD_DIM
KV_WIDTH = N_KV_HEADS * HEAD_DIM
ROPE_THETA = 10000.0
LOG2_E = 1.0 / math.log(2.0)

S5_GROUP = 16
S5_GROUPS = D_MODEL // S5_GROUP
S5_STATE = 64
S5_LANES = S5_GROUPS * S5_STATE
N_EXPERTS = 8

ROW_TILE = 256
FFN_SUBTILES = 2
ATTN_Q_TILE = 512
FREQ_TILE = 512
DFT_SPLIT = 64
S5_CHUNK = 32
S5_SCAN_CHUNK = 64
MOE_TILE = 512
MOE_FF_TILE = 1792
LANE = 128
MIB = 1024 * 1024


def _cparams(sem, vmem_mib=48):
    return pltpu.CompilerParams(dimension_semantics=sem, vmem_limit_bytes=vmem_mib * MIB)


def _rms(x, gain):
    return x * lax.rsqrt(jnp.mean(x * x, axis=-1, keepdims=True) + EPS) * gain


def _bdot(a, b):
    return jnp.dot(a, b, preferred_element_type=F32)


def _full(shape):
    zeros = (0,) * len(shape)
    return pl.BlockSpec(shape, lambda *_: zeros)


def _resident(shape):
    zeros = (0,) * len(shape)
    return pl.BlockSpec(shape, lambda *_: zeros, pipeline_mode=pl.Buffered(1))


def _ada_kernel(cond_ref, w_ref, b_ref, o_ref):
    c = cond_ref[...]
    s = (c * jax.nn.sigmoid(c)).astype(BF16)
    o_ref[...] = _bdot(s, w_ref[...].astype(BF16)) + b_ref[...]


def _ada_params(cond, ada_w, ada_b):
    depth, _, n6 = ada_w.shape
    rows = cond.shape[0]
    tn = 1536
    out = pl.pallas_call(
        _ada_kernel,
        grid=(depth, n6 // tn),
        in_specs=[pl.BlockSpec((rows, D_MODEL), lambda i, j: (0, 0)),
                  pl.BlockSpec((None, D_MODEL, tn), lambda i, j: (i, 0, j)),
                  pl.BlockSpec((None, 1, tn), lambda i, j: (i, 0, j))],
        out_specs=pl.BlockSpec((None, rows, tn), lambda i, j: (i, 0, j)),
        out_shape=jax.ShapeDtypeStruct((depth, rows, n6), F32),
        compiler_params=_cparams(("arbitrary", "arbitrary")),
        name="ada_params",
    )(cond, ada_w, ada_b.reshape(depth, 1, n6))
    return out.reshape(depth, rows, 6, D_MODEL)


def _rope_rotate(t):
    w = t.shape[-1]
    lane = lax.broadcasted_iota(jnp.int32, t.shape, 1)
    first = (lane & (HEAD_DIM - 1)) < HEAD_DIM // 2
    return jnp.where(first, pltpu.roll(t, w - HEAD_DIM // 2, 1), pltpu.roll(t, HEAD_DIM // 2, 1))


def _head_slots(t):
    lane = lax.broadcasted_iota(jnp.int32, t.shape, 1)
    lo = jnp.where(lane < HEAD_DIM, t, 0.0)
    hi = jnp.where(lane >= HEAD_DIM, t, 0.0)
    return jnp.concatenate([lo, pltpu.roll(lo, HEAD_DIM, 1), pltpu.roll(hi, HEAD_DIM, 1), hi], axis=-1)


def _inproj_even_kernel(x_ref, c_ref, mod_ref, gain_ref, w_ref, qg_ref, kg_ref, e_ref, cos_ref, sin_ref,
                        hy_ref, q_ref, k4_ref, v4_ref, *, nlat):
    xin = jnp.where(pl.program_id(1) >= nlat, c_ref[...], x_ref[...])
    h = _rms(xin, gain_ref[...]) * (1.0 + mod_ref[1:2, :]) + mod_ref[0:1, :]
    p = _bdot(h.astype(BF16), w_ref[...])
    hy_ref[...] = p[:, :HY_IN].astype(hy_ref.dtype)
    q = p[:, HY_IN:HY_IN + ATTN_WIDTH]
    k = p[:, HY_IN + ATTN_WIDTH:HY_IN + ATTN_WIDTH + KV_WIDTH]
    v = p[:, HY_IN + ATTN_WIDTH + KV_WIDTH:]
    e = e_ref[...]
    qn = q * lax.rsqrt(_bdot((q * q).astype(BF16), e) + EPS) * qg_ref[...]
    kn = k * lax.rsqrt(_bdot((k * k).astype(BF16), e[:KV_WIDTH, :KV_WIDTH]) + EPS) * kg_ref[...]
    cos = cos_ref[...]
    sin = sin_ref[...]
    qr = (qn * cos + _rope_rotate(qn) * sin) * (HEAD_DIM ** -0.5 * LOG2_E)
    kr = kn * cos[:, :KV_WIDTH] + _rope_rotate(kn) * sin[:, :KV_WIDTH]
    q_ref[...] = qr.astype(q_ref.dtype)
    k4_ref[...] = _head_slots(kr).astype(k4_ref.dtype)
    v4_ref[...] = _head_slots(v).astype(v4_ref.dtype)


def _mod_spec(n_batch, nlat):
    return pl.BlockSpec((None, 6, D_MODEL), lambda b, j: (jnp.where(j >= nlat, n_batch, b), 0, 0))


def _row_spec(width):
    return pl.BlockSpec((None, ROW_TILE, width), lambda b, j: (b, j, 0))


def _lat_ctx_specs(nlat, nctx, width):
    return [pl.BlockSpec((None, ROW_TILE, width), lambda b, j: (b, jnp.minimum(j, nlat - 1), 0)),
            pl.BlockSpec((None, ROW_TILE, width), lambda b, j: (b, jnp.clip(j - nlat, 0, nctx - 1), 0))]


def _inproj_even(x, ctx, mod, gain, w_in, q_gain, k_gain, cos_t, sin_t, nlat):
    n_batch = x.shape[0]
    nctx = ctx.shape[1] // ROW_TILE
    ntile = nlat + nctx
    lt = ntile * ROW_TILE
    n_out = w_in.shape[1]
    head_avg = jnp.kron(jnp.eye(N_Q_HEADS, dtype=F32), jnp.full((HEAD_DIM, HEAD_DIM), 1.0 / HEAD_DIM, F32)).astype(BF16)
    table = pl.BlockSpec((ROW_TILE, ATTN_WIDTH), lambda b, j: (j, 0))
    outs = [jax.ShapeDtypeStruct((n_batch, lt, HY_IN), BF16)] + [jax.ShapeDtypeStruct((n_batch, lt, ATTN_WIDTH), BF16)] * 3
    return pl.pallas_call(
        functools.partial(_inproj_even_kernel, nlat=nlat),
        grid=(n_batch, ntile),
        in_specs=_lat_ctx_specs(nlat, nctx, D_MODEL) + [
            _mod_spec(n_batch, nlat), _full((1, D_MODEL)), _full((D_MODEL, n_out)),
            _full((1, ATTN_WIDTH)), _full((1, KV_WIDTH)), _full((ATTN_WIDTH, ATTN_WIDTH)), table, table],
        out_specs=[_row_spec(HY_IN), _row_spec(ATTN_WIDTH), _row_spec(ATTN_WIDTH), _row_spec(ATTN_WIDTH)],
        out_shape=outs,
        compiler_params=_cparams(("parallel", "parallel")),
        name="inproj_even",
    )(x, ctx, mod, gain, w_in, q_gain, k_gain, head_avg, cos_t, sin_t)


def _attn_kernel(q_ref, k4_ref, v4_ref, o_ref):
    for blk in range(N_Q_HEADS // 2):
        g = (2 * blk) // Q_PER_KV
        qp = q_ref[:, LANE * blk:LANE * (blk + 1)]
        acc = None
        for r in range(2):
            slot = LANE * (2 * g + r)
            s = lax.dot_general(qp, k4_ref[:, slot:slot + LANE], (((1,), (1,)), ((), ())), preferred_element_type=F32)
            e = jnp.exp2(s - jnp.max(s, axis=-1, keepdims=True))
            inv = 1.0 / jnp.sum(e, axis=-1, keepdims=True)
            o = _bdot(e.astype(BF16), v4_ref[:, slot:slot + LANE]) * inv
            acc = o if acc is None else acc + o
        o_ref[:, LANE * blk:LANE * (blk + 1)] = acc.astype(o_ref.dtype)


def _attention(q, k4, v4, q_rows, q_tile, q_blk0, k_rows, k_blk):
    n_batch = q.shape[0]
    kv_spec = pl.BlockSpec((None, k_rows, ATTN_WIDTH), lambda b, j: (b, k_blk, 0))
    return pl.pallas_call(
        _attn_kernel,
        grid=(n_batch, q_rows // q_tile),
        in_specs=[pl.BlockSpec((None, q_tile, ATTN_WIDTH), lambda b, j: (b, q_blk0 + j, 0)), kv_spec, kv_spec],
        out_specs=pl.BlockSpec((None, q_tile, ATTN_WIDTH), lambda b, j: (b, j, 0)),
        out_shape=jax.ShapeDtypeStruct((n_batch, q_rows, ATTN_WIDTH), BF16),
        compiler_params=_cparams(("parallel", "parallel")),
        name="attention",
    )(q, k4, v4)


def _filter_kernel(z_ref, w1_ref, b1_ref, w2_ref, b2_ref, wo_ref, fr_ref, dl_ref, o_ref):
    hi = lax.Precision.HIGHEST
    z = z_ref[...]
    fr = fr_ref[...]
    h = jnp.sin(fr * (jnp.dot(z, w1_ref[...], precision=hi, preferred_element_type=F32) + b1_ref[...]))
    h = jnp.sin(fr * (jnp.dot(h, w2_ref[...], precision=hi, preferred_element_type=F32) + b2_ref[...]))
    h = jnp.dot(h, wo_ref[...], precision=hi, preferred_element_type=F32)
    decay = jnp.exp(-z[:, 0:1] * dl_ref[...])
    o_ref[...] = h * jnp.concatenate([decay] * (2 * HY_ORDER), axis=-1)


def _hyena_filters(seq, w1, b1, w2, b2, wout, freq):
    t = jnp.linspace(0.0, 1.0, seq, dtype=F32)[:, None]
    bands = jnp.linspace(1e-4, HY_BANDS - 1, HY_BANDS, dtype=F32)
    phase = (2.0 * math.pi / seq) * jnp.arange(seq, dtype=F32)[:, None] * bands
    z = jnp.concatenate([t, jnp.cos(phase), -jnp.sin(phase)], axis=-1)
    z = jnp.pad(z, ((0, 0), (0, LANE - HY_EMB)))
    w1p = jnp.pad(w1, ((0, LANE - HY_EMB), (0, 0)))
    deltas = jnp.linspace(HY_DECAY_SLOW, HY_DECAY_FAST, HY_WIDTH, dtype=F32)[None, :]
    hid = w1.shape[1]
    n_out = wout.shape[1]
    tl = ROW_TILE
    return pl.pallas_call(
        _filter_kernel,
        grid=(seq // tl,),
        in_specs=[pl.BlockSpec((tl, LANE), lambda i: (i, 0)), _full((LANE, hid)), _full((1, hid)), _full((hid, hid)),
                  _full((1, hid)), _full((hid, n_out)), _full((1, hid)), _full((1, HY_WIDTH))],
        out_specs=pl.BlockSpec((tl, n_out), lambda i: (i, 0)),
        out_shape=jax.ShapeDtypeStruct((seq, n_out), F32),
        compiler_params=_cparams(("parallel",)),
        name="hyena_filter",
    )(z, w1p, b1[None, :], w2, b2[None, :], wout, freq[None, :], deltas)


def _freq_tile(seq):
    return min(FREQ_TILE, seq)


def _dft_tables(seq):
    n = 2 * seq
    unit = 2.0 * math.pi / n
    f = jnp.arange(seq, dtype=jnp.int32)[:, None]
    t1 = jnp.arange(seq // DFT_SPLIT, dtype=jnp.int32)[None, :] * DFT_SPLIT
    t2 = jnp.arange(DFT_SPLIT, dtype=jnp.int32)[None, :]
    ang1 = ((f * t1) % n).astype(F32) * unit
    ang2 = ((f * t2) % n).astype(F32) * unit
    c1, s1 = jnp.cos(ang1)[:, :, None], jnp.sin(ang1)[:, :, None]
    c2, s2 = jnp.cos(ang2)[:, None, :], jnp.sin(ang2)[:, None, :]
    cos = (c1 * c2 - s1 * s2).reshape(seq, seq)
    sin = (s1 * c2 + c1 * s2).reshape(seq, seq)
    idx = jnp.arange(seq, dtype=jnp.int32)
    nyquist = jnp.where(idx % 2 == 0, 1.0, -1.0)
    sin_rows = jnp.where(idx[:, None] == 0, nyquist[None, :], sin)
    sin_cols = jnp.where(idx[None, :] == 0, nyquist[:, None], sin)
    tf = _freq_tile(seq)
    nf = seq // tf
    fwd = jnp.stack([cos.reshape(nf, tf, seq), sin_rows.reshape(nf, tf, seq)], axis=1).reshape(2 * seq, seq)
    inv = jnp.stack([cos.reshape(seq, nf, tf), sin_cols.reshape(seq, nf, tf)], axis=2).reshape(seq, 2 * seq)
    return fwd.astype(BF16), inv.astype(BF16)


def _kfreq_kernel(hf_ref, hb_ref, f_ref, kre_ref, ks_ref):
    fi = pl.program_id(1)
    seq = hf_ref.shape[0]
    hf = hf_ref[...]
    row = lax.broadcasted_iota(jnp.int32, hf.shape, 0)
    hb = jnp.where(row == 0, 0.0, hb_ref[...])
    f = f_ref[...]
    a1 = _bdot(f, (hf + hb).astype(BF16))
    a2 = _bdot(f, (hf - hb).astype(BF16))
    tf = f_ref.shape[0] // 2
    frow = lax.broadcasted_iota(jnp.int32, (tf, hf.shape[1]), 0)
    dc = jnp.logical_and(frow == 0, fi == 0)
    scale = jnp.where(dc, 1.0 / (2 * seq), 2.0 / (2 * seq))
    kre_ref[...] = a1[:tf] * scale
    ks_ref[...] = jnp.where(dc, a1[tf:], a2[tf:]) * scale


def _kfreq(hfilt, fwd_tab):
    seq = hfilt.shape[0]
    tf = _freq_tile(seq)
    nf = seq // tf
    out = jax.ShapeDtypeStruct((HY_ORDER, seq, HY_WIDTH), F32)
    ospec = pl.BlockSpec((None, tf, HY_WIDTH), lambda o, fi: (o, fi, 0))
    return pl.pallas_call(
        _kfreq_kernel,
        grid=(HY_ORDER, nf),
        in_specs=[pl.BlockSpec((seq, HY_WIDTH), lambda o, fi: (0, o)),
                  pl.BlockSpec((seq, HY_WIDTH), lambda o, fi: (0, HY_ORDER + o)),
                  pl.BlockSpec((2 * tf, seq), lambda o, fi: (fi, 0))],
        out_specs=[ospec, ospec],
        out_shape=[out, out],
        compiler_params=_cparams(("parallel", "parallel")),
        name="hyena_kfreq",
    )(hfilt, hfilt, fwd_tab)


def _short_conv(p, w, b):
    n = p.shape[0]
    row = lax.broadcasted_iota(jnp.int32, p.shape, 0)
    prev = jnp.where(row == 0, 0.0, pltpu.roll(p, 1, 0))
    nxt = jnp.where(row == n - 1, 0.0, pltpu.roll(p, n - 1, 0))
    return b + prev * w[0:1, :] + p * w[1:2, :] + nxt * w[2:3, :]


def _hyena_kernel(vsrc_ref, gsrc_ref, cw_ref, cb_ref, skip_ref, f_ref, ft_ref, kre_ref, ks_ref, o_ref,
                  v_scr, vb_scr, acc_scr, *, conv_v):
    fi = pl.program_id(1)
    tf = f_ref.shape[0] // 2

    @pl.when(fi == 0)
    def _():
        v = vsrc_ref[...].astype(F32)
        if conv_v:
            v = _short_conv(v, cw_ref[0], cb_ref[0])
        v_scr[...] = v
        vb_scr[...] = v.astype(BF16)
        acc_scr[...] = jnp.zeros_like(acc_scr)

    xf = _bdot(f_ref[...], vb_scr[...])
    xre, xs = xf[:tf], xf[tf:]
    kre, ks = kre_ref[...], ks_ref[...]
    row = lax.broadcasted_iota(jnp.int32, xre.shape, 0)
    dc = jnp.logical_and(row == 0, fi == 0)
    yre = jnp.where(dc, xre * kre, xre * kre - xs * ks)
    ys = jnp.where(dc, xs * ks, xre * ks + xs * kre)
    y = jnp.concatenate([yre, ys], axis=0).astype(BF16)
    acc_scr[...] += _bdot(ft_ref[...], y)

    @pl.when(fi == pl.num_programs(1) - 1)
    def _():
        gate = _short_conv(gsrc_ref[...].astype(F32), cw_ref[1], cb_ref[1])
        o_ref[...] = (gate * (acc_scr[...] + v_scr[...] * skip_ref[...])).astype(o_ref.dtype)


def _hyena_order(vsrc, v_blk, v_col, hy, row_blk, order, seq, conv_w, conv_b, skip, fwd_tab, inv_tab, kre, ks):
    n_batch = hy.shape[0]
    tf = _freq_tile(seq)
    nf = seq // tf
    conv_v = order == 0
    cw = jnp.stack([conv_w[:, :HY_WIDTH], conv_w[:, (order + 1) * HY_WIDTH:(order + 2) * HY_WIDTH]])
    cb = jnp.stack([conv_b[None, :HY_WIDTH], conv_b[None, (order + 1) * HY_WIDTH:(order + 2) * HY_WIDTH]])
    kspec = pl.BlockSpec((None, tf, HY_WIDTH), lambda b, fi: (order, fi, 0))
    return pl.pallas_call(
        functools.partial(_hyena_kernel, conv_v=conv_v),
        grid=(n_batch, nf),
        in_specs=[pl.BlockSpec((None, seq, HY_WIDTH), lambda b, fi: (b, v_blk, v_col)),
                  pl.BlockSpec((None, seq, HY_WIDTH), lambda b, fi: (b, row_blk, order + 1)),
                  _full((2, 3, HY_WIDTH)), _full((2, 1, HY_WIDTH)), _full((1, HY_WIDTH)),
                  pl.BlockSpec((2 * tf, seq), lambda b, fi: (fi, 0)),
                  pl.BlockSpec((seq, 2 * tf), lambda b, fi: (0, fi)),
                  kspec, kspec],
        out_specs=pl.BlockSpec((None, seq, HY_WIDTH), lambda b, fi: (b, 0, 0)),
        out_shape=jax.ShapeDtypeStruct((n_batch, seq, HY_WIDTH), BF16),
        scratch_shapes=[pltpu.VMEM((seq, HY_WIDTH), F32), pltpu.VMEM((seq, HY_WIDTH), BF16),
                        pltpu.VMEM((seq, HY_WIDTH), F32)],
        compiler_params=_cparams(("parallel", "arbitrary")),
        name="hyena_order%d" % order,
    )(vsrc, hy, cw, cb, skip[order][None, :], fwd_tab, inv_tab, kre, ks)


def _hyena_mixer(hy, row_blk, seq, filt_args, conv_w, conv_b, skip):
    hfilt = _hyena_filters(seq, *filt_args)
    fwd_tab, inv_tab = _dft_tables(seq)
    kre, ks = _kfreq(hfilt, fwd_tab)
    v1 = _hyena_order(hy, row_blk, 0, hy, row_blk, 0, seq, conv_w, conv_b, skip, fwd_tab, inv_tab, kre, ks)
    return _hyena_order(v1, 0, 0, hy, row_blk, 1, seq, conv_w, conv_b, skip, fwd_tab, inv_tab, kre, ks)


def _outproj_kernel(yl_ref, yc_ref, al_ref, ac_ref, x_ref, c_ref, mod_ref, gpost_ref, w_ref, o_ref, *, nlat):
    is_ctx = pl.program_id(1) >= nlat
    yh = jnp.where(is_ctx, yc_ref[...], yl_ref[...])
    ya = jnp.where(is_ctx, ac_ref[...], al_ref[...])
    out = _bdot(yh, w_ref[:HY_WIDTH, :]) + _bdot(ya, w_ref[HY_WIDTH:, :])
    o_ref[...] = jnp.where(is_ctx, c_ref[...], x_ref[...]) + mod_ref[2:3, :] * _rms(out, gpost_ref[...])


def _outproj(y_lat, y_ctx, a_lat, a_ctx, x, ctx, mod, gpost, w_out, nlat):
    n_batch = x.shape[0]
    nctx = ctx.shape[1] // ROW_TILE
    ntile = nlat + nctx
    return pl.pallas_call(
        functools.partial(_outproj_kernel, nlat=nlat),
        grid=(n_batch, ntile),
        in_specs=(_lat_ctx_specs(nlat, nctx, HY_WIDTH) + _lat_ctx_specs(nlat, nctx, ATTN_WIDTH)
                  + _lat_ctx_specs(nlat, nctx, D_MODEL)
                  + [_mod_spec(n_batch, nlat), _full((1, D_MODEL)), _full(w_out.shape)]),
        out_specs=_row_spec(D_MODEL),
        out_shape=jax.ShapeDtypeStruct((n_batch, ntile * ROW_TILE, D_MODEL), F32),
        compiler_params=_cparams(("parallel", "parallel")),
        name="outproj_even",
    )(y_lat, y_ctx, a_lat, a_ctx, x, ctx, mod, gpost, w_out)


def _ffn_kernel(x_ref, mod_ref, gpre_ref, gpost_ref, wg_ref, wu_ref, wd_ref, o_ref, *, n_batch, tiles_per_batch, nlat):
    n_sub = x_ref.shape[0] // ROW_TILE
    subs = []
    for s in range(n_sub):
        tile = pl.program_id(0) * n_sub + s
        batch = lax.div(tile, tiles_per_batch)
        mod = mod_ref[jnp.where(lax.rem(tile, tiles_per_batch) >= nlat, n_batch, batch)]
        x = x_ref[s * ROW_TILE:(s + 1) * ROW_TILE, :]
        subs.append((x, mod, (_rms(x, gpre_ref[...]) * (1.0 + mod[4:5, :]) + mod[3:4, :]).astype(BF16)))
    h = jnp.concatenate([sub[2] for sub in subs], axis=0)
    g = _bdot(h, wg_ref[...])
    u = _bdot(h, wu_ref[...])
    a = (g * jax.nn.sigmoid(g) * u).astype(BF16)
    y = _bdot(a, wd_ref[...])
    for s, (x, mod, _) in enumerate(subs):
        rows = slice(s * ROW_TILE, (s + 1) * ROW_TILE)
        o_ref[rows, :] = x + mod[5:6, :] * _rms(y[rows, :], gpost_ref[...])


def _ffn_dense(xs, mod, gpre, gpost, wg, wu, wd, nlat):
    n_batch, lt, _ = xs.shape
    tiles_per_batch = lt // ROW_TILE
    rows = FFN_SUBTILES * ROW_TILE
    n_steps = n_batch * tiles_per_batch // FFN_SUBTILES
    blk = pl.BlockSpec((rows, D_MODEL), lambda i: (i, 0))
    out = pl.pallas_call(
        functools.partial(_ffn_kernel, n_batch=n_batch, tiles_per_batch=tiles_per_batch, nlat=nlat),
        grid=(n_steps,),
        in_specs=[blk, _resident(mod.shape), _resident((1, D_MODEL)), _resident((1, D_MODEL)),
                  _resident(wg.shape), _resident(wu.shape), _resident(wd.shape)],
        out_specs=blk,
        out_shape=jax.ShapeDtypeStruct((n_batch * lt, D_MODEL), F32),
        compiler_params=_cparams(("parallel",), 56),
        name="ffn_dense",
    )(xs.reshape(n_batch * lt, D_MODEL), mod, gpre, gpost, wg, wu, wd)
    return out.reshape(xs.shape)


def _mod_rows(mod_ref, k, n_batch, is_ctx):
    return jnp.where(is_ctx, mod_ref[n_batch:n_batch + 1, k:k + 1, :], mod_ref[0:n_batch, k:k + 1, :])


def _row_order_swap(n_outer, n_inner):
    n = n_outer * n_inner
    dst = jnp.arange(n)
    src = (dst % n_outer) * n_inner + dst // n_outer
    return (src[:, None] == jnp.arange(n)[None, :]).astype(BF16)


def _inproj_odd_kernel(x_ref, mod_ref, gain_ref, swap_ref, w_ref, u_ref, *, n_lat_steps):
    n_batch, steps, _ = x_ref.shape
    is_ctx = pl.program_id(0) >= n_lat_steps
    h = (_rms(x_ref[...], gain_ref[...]) * (1.0 + _mod_rows(mod_ref, 1, n_batch, is_ctx))
         + _mod_rows(mod_ref, 0, n_batch, is_ctx))
    h = h.reshape(n_batch * steps, D_MODEL).astype(BF16)
    h = _bdot(swap_ref[...], h).astype(BF16)
    u_ref[...] = _bdot(h, w_ref[...]).reshape(steps, n_batch, D_MODEL)


def _inproj_odd(xs, mod, gain, w_in, n_lat):
    n_batch, lt, _ = xs.shape
    p = S5_CHUNK
    swap = _row_order_swap(n_batch, p)
    return pl.pallas_call(
        functools.partial(_inproj_odd_kernel, n_lat_steps=n_lat // p),
        grid=(lt // p,),
        in_specs=[pl.BlockSpec((n_batch, p, D_MODEL), lambda i: (0, i, 0)), _full(mod.shape), _full((1, D_MODEL)),
                  _full(swap.shape), _full(w_in.shape)],
        out_specs=pl.BlockSpec((p, n_batch, D_MODEL), lambda i: (i, 0, 0)),
        out_shape=jax.ShapeDtypeStruct((lt, n_batch, D_MODEL), F32),
        compiler_params=_cparams(("parallel",)),
        name="inproj_odd",
    )(xs, mod, gain, swap, w_in)


def _s5_param_kernel(lr_ref, li_ref, ls_ref, lrx_ref, lix_ref, lsx_ref, br_ref, bi_ref,
                     abr_ref, abi_ref, bbr_ref, bbi_ref):
    def zoh(lr_raw, li, log_step):
        lr = jnp.minimum(lr_raw, -1e-4)
        dt = jnp.exp(log_step)
        mag = jnp.exp(lr * dt)
        ab_re = mag * jnp.cos(li * dt)
        ab_im = mag * jnp.sin(li * dt)
        den = lr * lr + li * li
        nr, ni = ab_re - 1.0, ab_im
        return ab_re, ab_im, (nr * lr + ni * li) / den, (ni * lr - nr * li) / den

    ab_re, ab_im, _, _ = zoh(lr_ref[...], li_ref[...], ls_ref[...])
    abr_ref[...] = ab_re
    abi_ref[...] = ab_im
    _, _, co_re, co_im = zoh(lrx_ref[...], lix_ref[...], lsx_ref[...])
    br, bi = br_ref[...], bi_ref[...]
    bbr_ref[...] = co_re * br - co_im * bi
    bbi_ref[...] = co_re * bi + co_im * br


def _s5_params(lam_re, lam_im, log_step, b_re, b_im, c_re, c_im):
    nd, g, n = lam_re.shape
    k = S5_GROUP
    rep = lambda a: jnp.repeat(a, k, axis=1)
    ls = log_step[:, :, None]
    bt_re = jnp.swapaxes(b_re, 2, 3).reshape(nd, g * k, n)
    bt_im = jnp.swapaxes(b_im, 2, 3).reshape(nd, g * k, n)
    small = jax.ShapeDtypeStruct((nd, g, n), F32)
    big = jax.ShapeDtypeStruct((nd, g * k, n), F32)
    ab_re, ab_im, bb_re, bb_im = pl.pallas_call(
        _s5_param_kernel, out_shape=[small, small, big, big], name="s5_discretise",
    )(lam_re, lam_im, ls, rep(lam_re), rep(lam_im), rep(ls), bt_re, bt_im)
    bb_re = bb_re.reshape(nd, g, k, n)
    bb_im = bb_im.reshape(nd, g, k, n)
    cmul = lambda xr, xi, yr, yi: (xr * yr - xi * yi, xr * yi + xi * yr)
    a_re, a_im = ab_re[:, :, None, :], ab_im[:, :, None, :]
    a2_re, a2_im = cmul(a_re, a_im, a_re, a_im)
    abb_re, abb_im = cmul(a_re, a_im, bb_re, bb_im)
    ca_re, ca_im = cmul(c_re, c_im, a_re, a_im)
    ca2_re, ca2_im = cmul(c_re, c_im, a2_re, a2_im)
    real_cb = lambda xr, xi: jnp.einsum('dgin,dgkn->dgik', xr, bb_re) - jnp.einsum('dgin,dgkn->dgik', xi, bb_im)
    k0 = real_cb(c_re, c_im)
    k1 = real_cb(ca_re, ca_im)
    a2 = jnp.stack([a2_re.reshape(nd, g * n), a2_im.reshape(nd, g * n)], axis=1)
    eye = jnp.eye(8, dtype=F32)
    nq = g // 8
    blocked = lambda t: t.reshape((nd, nq, 8) + t.shape[2:])

    def in_to_state(bb):
        return jnp.einsum('dqgkn,gh->dqgkhn', blocked(bb), eye).reshape(nd, nq, 8 * k, 8 * n)

    def state_to_out(c):
        return jnp.einsum('dqgin,gh->dqhngi', blocked(c), eye).reshape(nd, nq, 8 * n, 8 * k)

    def in_to_out(m):
        return jnp.einsum('dqgik,gh->dqgkhi', blocked(m), eye).reshape(nd, nq, 8 * k, 8 * k)

    w_in = jnp.concatenate([
        jnp.concatenate([in_to_state(abb_re), in_to_state(abb_im)], axis=-1),
        jnp.concatenate([in_to_state(bb_re), in_to_state(bb_im)], axis=-1)], axis=2).astype(BF16)
    w_state = jnp.stack([
        jnp.concatenate([state_to_out(ca_re), state_to_out(ca2_re)], axis=-1),
        jnp.concatenate([state_to_out(-ca_im), state_to_out(-ca2_im)], axis=-1)], axis=2).astype(BF16)
    d0, d1 = in_to_out(k0), in_to_out(k1)
    w_dir = jnp.concatenate([jnp.concatenate([d0, d1], axis=-1),
                             jnp.concatenate([jnp.zeros_like(d0), d0], axis=-1)], axis=2).astype(BF16)
    return a2, w_in, w_state, w_dir


def _s5_scan_kernel(u_ref, a2_ref, win_ref, wst_ref, wdir_ref, y_ref, sbuf, state, *, nctx_chunks):
    d = pl.program_id(0)
    i = pl.program_id(1)
    p_steps, n_batch, _ = u_ref.shape
    npair = p_steps // 2
    rows = npair * n_batch
    half = S5_LANES
    nq = win_ref.shape[0]
    kq = win_ref.shape[1] // 2
    sq = win_ref.shape[2] // 2

    @pl.when(i == 0)
    def _():
        state[...] = jnp.zeros_like(state)

    fwd = d == 0
    u = u_ref[...].reshape(npair, 2, n_batch, D_MODEL)
    u_even = u[:, 0].reshape(rows, D_MODEL).astype(BF16)
    u_odd = u[:, 1].reshape(rows, D_MODEL).astype(BF16)
    u_1 = jnp.where(fwd, u_even, u_odd)
    u_2 = jnp.where(fwd, u_odd, u_even)
    pair_in = lambda q: jnp.concatenate([u_1[:, kq * q:kq * (q + 1)], u_2[:, kq * q:kq * (q + 1)]], axis=-1)
    for q in range(nq):
        r = _bdot(pair_in(q), win_ref[q])
        sbuf[:, :, sq * q:sq * (q + 1)] = r[:, :sq].reshape(npair, n_batch, sq)
        sbuf[:, :, half + sq * q:half + sq * (q + 1)] = r[:, sq:].reshape(npair, n_batch, sq)

    for q in range(nq):
        lo = sq * q
        ar = jnp.broadcast_to(a2_ref[0:1, lo:lo + sq], (n_batch, sq))
        ai = jnp.broadcast_to(a2_ref[1:2, lo:lo + sq], (n_batch, sq))

        def body(t, carry, lo=lo, ar=ar, ai=ai):
            sr, si = carry
            tt = jnp.where(d == 0, t, npair - 1 - t)
            nr = ar * sr - ai * si + sbuf[tt, :, lo:lo + sq]
            ni = ar * si + ai * sr + sbuf[tt, :, half + lo:half + lo + sq]
            sbuf[tt, :, lo:lo + sq] = sr
            sbuf[tt, :, half + lo:half + lo + sq] = si
            return nr, ni

        sr, si = lax.fori_loop(0, npair, body, (state[:, lo:lo + sq], state[:, half + lo:half + lo + sq]))
        state[:, lo:lo + sq] = sr
        state[:, half + lo:half + lo + sq] = si

    @pl.when(i >= nctx_chunks)
    def _():
        s = sbuf[...].reshape(rows, 2 * half).astype(BF16)
        for q in range(nq):
            yq = (_bdot(s[:, sq * q:sq * (q + 1)], wst_ref[q, 0])
                  + _bdot(s[:, half + sq * q:half + sq * (q + 1)], wst_ref[q, 1])
                  + _bdot(pair_in(q), wdir_ref[q]))
            y_1 = yq[:, :kq].reshape(npair, 1, n_batch, kq)
            y_2 = yq[:, kq:].reshape(npair, 1, n_batch, kq)
            y_pair = jnp.concatenate([jnp.where(fwd, y_1, y_2), jnp.where(fwd, y_2, y_1)], axis=1)
            y_ref[:, :, kq * q:kq * (q + 1)] = y_pair.reshape(p_steps, n_batch, kq)


def _s5_scan(u3, a2, w_in, w_state, w_dir, n_lat):
    lt, n_batch, _ = u3.shape
    p = S5_SCAN_CHUNK
    nchunk = lt // p
    nlatc = n_lat // p
    nctxc = nchunk - nlatc

    def u_map(d, i):
        return (jnp.where(d == 0, lax.rem(i + nlatc, nchunk), nchunk - 1 - i), 0, 0)

    def y_map(d, i):
        return (d, jnp.where(d == 0, jnp.maximum(i - nctxc, 0), jnp.minimum(nchunk - 1 - i, nlatc - 1)), 0, 0)

    per_dir = lambda w: pl.BlockSpec((None,) + w.shape[1:], lambda d, i: (d,) + (0,) * (w.ndim - 1))
    return pl.pallas_call(
        functools.partial(_s5_scan_kernel, nctx_chunks=nctxc),
        grid=(2, nchunk),
        in_specs=[pl.BlockSpec((p, n_batch, D_MODEL), u_map), per_dir(a2), per_dir(w_in), per_dir(w_state),
                  per_dir(w_dir)],
        out_specs=pl.BlockSpec((None, p, n_batch, D_MODEL), y_map),
        out_shape=jax.ShapeDtypeStruct((2, n_lat, n_batch, D_MODEL), F32),
        scratch_shapes=[pltpu.VMEM((p // 2, n_batch, 2 * S5_LANES), F32), pltpu.VMEM((n_batch, 2 * S5_LANES), F32)],
        compiler_params=_cparams(("arbitrary", "arbitrary")),
        name="s5_scan",
    )(u3, a2, w_in, w_state, w_dir)


def _glu_kernel(yf_ref, yb_ref, u_ref, dskip_ref, wa_ref, wb_ref, x_ref, mod_ref, gpost_ref, gpre_ref, router_ref,
                swap_ref, tri_ref, xo_ref, hf_ref, rw_ref, ri_ref, cnt_ref, carry):
    n_batch, steps, _ = x_ref.shape
    rows = n_batch * steps

    @pl.when(pl.program_id(0) == 0)
    def _():
        carry[...] = jnp.zeros_like(carry)

    y = (yf_ref[...] + yb_ref[...] + dskip_ref[...] * u_ref[...]).reshape(rows, D_MODEL)
    z = jax.nn.gelu(y).astype(BF16)
    z = _bdot(swap_ref[...], z).astype(BF16)
    out = _bdot(z, wa_ref[...]) * jax.nn.sigmoid(_bdot(z, wb_ref[...]))
    out = out.reshape(n_batch, steps, D_MODEL)
    mod = lambda k: mod_ref[0:n_batch, k:k + 1, :]
    xn = x_ref[...] + mod(2) * _rms(out, gpost_ref[...])
    xo_ref[...] = xn
    hf = _rms(xn, gpre_ref[...]) * (1.0 + mod(4)) + mod(3)
    hf_ref[...] = hf
    hf = hf.reshape(rows, D_MODEL)
    h_hi = hf.astype(BF16)
    h_lo = (hf - h_hi.astype(F32)).astype(BF16)
    part = _bdot(h_hi, router_ref[...])
    logits = part[:, :LANE] + part[:, LANE:] + _bdot(h_lo, router_ref[:, :LANE])
    lane = lax.broadcasted_iota(jnp.int32, logits.shape, 1)
    neg = jnp.float32(-jnp.inf)
    lg = jnp.where(lane < N_EXPERTS, logits, neg)
    m1 = jnp.max(lg, axis=-1, keepdims=True)
    i1 = jnp.min(jnp.where(lg == m1, lane, LANE), axis=-1, keepdims=True)
    lg2 = jnp.where(lane == i1, neg, lg)
    m2 = jnp.max(lg2, axis=-1, keepdims=True)
    i2 = jnp.min(jnp.where(lg2 == m2, lane, LANE), axis=-1, keepdims=True)
    e2 = jnp.exp(m2 - m1)
    w1 = 1.0 / (1.0 + e2)
    rw_ref[...] = jnp.where(lane == 0, w1, jnp.where(lane == 1, e2 * w1, 0.0)).reshape(n_batch, steps, LANE)
    member = jnp.where(lane == i1, 1.0, jnp.where(lane == i2, 1.0, 0.0))
    base = carry[...] + _bdot(tri_ref[...], member.astype(BF16))
    r1 = jnp.sum(jnp.where(lane == i1, base, 0.0), axis=-1, keepdims=True).astype(jnp.int32)
    r2 = jnp.sum(jnp.where(lane == i2, base, 0.0), axis=-1, keepdims=True).astype(jnp.int32)
    ri = jnp.where(lane == 0, i1, jnp.where(lane == 1, i2, jnp.where(lane == 2, r1, jnp.where(lane == 3, r2, 0))))
    ri_ref[...] = ri.reshape(n_batch, steps, LANE)
    carry[...] += jnp.sum(member, axis=0, keepdims=True)
    cnt_ref[...] = carry[...]


def _glu(y, u3, d_skip, w_a, w_b, xs, mod, gpost, gpre, router, n_lat):
    n_batch = xs.shape[0]
    p = S5_CHUNK
    rows = p * n_batch
    router_p = jnp.pad(router, ((0, 0), (0, LANE - router.shape[1])))
    router_hi = router_p.astype(BF16)
    router_cat = jnp.concatenate([router_hi, (router_p - router_hi.astype(F32)).astype(BF16)], axis=1)
    tri = (jnp.arange(rows)[:, None] > jnp.arange(rows)[None, :]).astype(BF16)
    swap = _row_order_swap(p, n_batch)
    bt_spec = lambda w: pl.BlockSpec((n_batch, p, w), lambda i: (0, i, 0))
    return pl.pallas_call(
        _glu_kernel,
        grid=(n_lat // p,),
        in_specs=[pl.BlockSpec((None, p, n_batch, D_MODEL), lambda i: (0, i, 0, 0)),
                  pl.BlockSpec((None, p, n_batch, D_MODEL), lambda i: (1, i, 0, 0)),
                  pl.BlockSpec((p, n_batch, D_MODEL), lambda i: (i, 0, 0)),
                  _full((1, D_MODEL)), _full(w_a.shape), _full(w_b.shape), bt_spec(D_MODEL),
                  _full(mod.shape), _full((1, D_MODEL)), _full((1, D_MODEL)),
                  _full(router_cat.shape), _full(swap.shape), _full(tri.shape)],
        out_specs=[bt_spec(D_MODEL), bt_spec(D_MODEL), bt_spec(LANE), bt_spec(LANE), _full((1, LANE))],
        out_shape=[jax.ShapeDtypeStruct((n_batch, n_lat, D_MODEL), F32),
                   jax.ShapeDtypeStruct((n_batch, n_lat, D_MODEL), F32),
                   jax.ShapeDtypeStruct((n_batch, n_lat, LANE), F32),
                   jax.ShapeDtypeStruct((n_batch, n_lat, LANE), jnp.int32),
                   jax.ShapeDtypeStruct((1, LANE), F32)],
        scratch_shapes=[pltpu.VMEM((1, LANE), F32)],
        compiler_params=_cparams(("arbitrary",)),
        name="s5_glu_router",
    )(y, y, u3, d_skip, w_a, w_b, xs, mod, gpost, gpre, router_cat, swap, tri)


def _moe_plan(ri, counts, n_tiles):
    experts = jnp.arange(N_EXPERTS, dtype=jnp.int32)
    n_of = (counts[0, :N_EXPERTS].astype(jnp.int32) + MOE_TILE - 1) // MOE_TILE
    ends = jnp.cumsum(n_of)
    starts = ends - n_of
    start_of = jnp.sum(jnp.where(ri[:, 0:2, None] == experts, starts, 0), axis=-1)
    n_tok = ri.shape[0]
    pos = (start_of * MOE_TILE + ri[:, 2:4]).T.reshape(-1)
    n_used = ends[-1]
    tile = jnp.arange(n_tiles, dtype=jnp.int32)
    tile_expert = jnp.sum((jnp.minimum(tile, n_used - 1)[:, None] >= ends[None, :]).astype(jnp.int32), axis=1)
    n_pairs = pos.shape[0]
    n_rows = n_tiles * MOE_TILE
    pair_of = jnp.full((n_rows,), -1, jnp.int32).at[pos].set(jnp.arange(n_pairs, dtype=jnp.int32))
    is_pad = pair_of < 0
    pair_of = jnp.where(is_pad, n_pairs - 1 + jnp.cumsum(is_pad.astype(jnp.int32)), pair_of)
    src_token = jnp.where(is_pad, 0, pair_of % n_tok)
    return src_token, pair_of, tile_expert, n_used.reshape(1)


def _moe_expert_kernel(te_ref, nu_ref, src0_ref, src_ref, dst_ref, hf_ref, wg_ref, wu_ref, wd_ref, out_ref,
                       xbuf, obuf, acc, gsem, ssem, *, rows_per_step):
    del te_ref
    i = pl.program_id(0)
    j = pl.program_id(1)
    last_j = pl.num_programs(1) - 1
    nu = nu_ref[0]
    cur = lax.rem(i, 2)
    nxt = 1 - cur
    row0 = j * rows_per_step

    def gather_row(idx_ref, row, slot):
        return pltpu.make_async_copy(hf_ref.at[pl.ds(idx_ref[0, row], 1)], xbuf.at[slot, pl.ds(row, 1)], gsem.at[slot])

    def scatter_row(row, slot):
        return pltpu.make_async_copy(obuf.at[slot, pl.ds(row, 1)], out_ref.at[pl.ds(dst_ref[0, row], 1)], ssem.at[slot])

    def whole_tile_wait(sem, slot):
        pltpu.make_async_copy(xbuf.at[slot], obuf.at[slot], sem.at[slot]).wait()

    @pl.when(jnp.logical_and(i == 0, j == 0))
    def _():
        obuf[...] = jnp.zeros_like(obuf)

        def first(r, c):
            gather_row(src0_ref, r, 0).start()
            return c

        lax.fori_loop(0, MOE_TILE, first, 0)

    @pl.when(jnp.logical_and(i <= nu, j == 0))
    def _():
        whole_tile_wait(gsem, cur)

    for step in range(MOE_TILE // rows_per_step):
        @pl.when(jnp.logical_and(i < nu, j == step))
        def _(step=step):
            for row in range(step * rows_per_step, (step + 1) * rows_per_step):
                gather_row(src_ref, row, nxt).start()
                scatter_row(row, nxt).start()

    @pl.when(i < nu)
    def _():
        h = xbuf[cur].astype(BF16)
        g = _bdot(h, wg_ref[...])
        u = _bdot(h, wu_ref[...])
        part = _bdot((g * jax.nn.sigmoid(g) * u).astype(BF16), wd_ref[...])
        acc[...] = jnp.where(j == 0, part, acc[...] + part)

    @pl.when(i >= nu)
    def _():
        def tail(c, carry):
            scatter_row(row0 + c, nxt).start()
            return carry

        lax.fori_loop(0, rows_per_step, tail, 0)

    @pl.when(jnp.logical_and(i >= 1, j == last_j))
    def _():
        whole_tile_wait(ssem, cur)

    @pl.when(jnp.logical_and(i < nu, j == last_j))
    def _():
        obuf[cur] = acc[...]

    @pl.when(jnp.logical_and(i == pl.num_programs(0) - 1, j == last_j))
    def _():
        whole_tile_wait(ssem, nxt)


def _combine_kernel(ya_ref, yb_ref, rw_ref, x_ref, mod_ref, gpost_ref, o_ref):
    rw = rw_ref[...]
    y = rw[:, 0:1] * ya_ref[...] + rw[:, 1:2] * yb_ref[...]
    o_ref[...] = x_ref[...] + mod_ref[5:6, :] * _rms(y, gpost_ref[...])


def _moe(hf, rw, ri, counts, xs, mod, gpost, wg, wu, wd, nlat):
    t_rows = hf.shape[0]
    d_ff = wg.shape[2]
    n_tiles = 2 * t_rows // MOE_TILE + N_EXPERTS
    nff = d_ff // MOE_FF_TILE
    n_row_tiles = t_rows // ROW_TILE
    n_rows = n_tiles * MOE_TILE
    src_token, pair_of, tile_expert, n_used = _moe_plan(ri, counts, n_tiles)
    spare = jnp.arange(MOE_TILE, dtype=jnp.int32)
    src_tiles = jnp.concatenate([src_token, 0 * spare]).reshape(n_tiles + 1, 1, MOE_TILE)
    dst_tiles = jnp.concatenate([n_rows + spare, pair_of]).reshape(n_tiles + 1, 1, MOE_TILE)
    step_expert = jnp.concatenate([tile_expert, tile_expert[-1:]])
    any_spec = pl.BlockSpec(memory_space=pl.ANY)
    idx_block = (None, 1, MOE_TILE)

    def ff_blk(i, j, nu):
        return jnp.where(i < nu[0], j, nff - 1)

    y_pairs = pl.pallas_call(
        functools.partial(_moe_expert_kernel, rows_per_step=MOE_TILE // nff),
        grid_spec=pltpu.PrefetchScalarGridSpec(
            num_scalar_prefetch=2,
            grid=(n_tiles + 1, nff),
            in_specs=[pl.BlockSpec(idx_block, lambda i, j, te, nu: (0, 0, 0), memory_space=pltpu.SMEM),
                      pl.BlockSpec(idx_block, lambda i, j, te, nu: (jnp.minimum(i + 1, n_tiles), 0, 0),
                                   memory_space=pltpu.SMEM),
                      pl.BlockSpec(idx_block, lambda i, j, te, nu: (i, 0, 0), memory_space=pltpu.SMEM),
                      any_spec,
                      pl.BlockSpec((None, D_MODEL, MOE_FF_TILE), lambda i, j, te, nu: (te[i], 0, ff_blk(i, j, nu))),
                      pl.BlockSpec((None, D_MODEL, MOE_FF_TILE), lambda i, j, te, nu: (te[i], 0, ff_blk(i, j, nu))),
                      pl.BlockSpec((None, MOE_FF_TILE, D_MODEL), lambda i, j, te, nu: (te[i], ff_blk(i, j, nu), 0))],
            out_specs=any_spec,
            scratch_shapes=[pltpu.VMEM((2, MOE_TILE, D_MODEL), F32), pltpu.VMEM((2, MOE_TILE, D_MODEL), F32),
                            pltpu.VMEM((MOE_TILE, D_MODEL), F32),
                            pltpu.SemaphoreType.DMA((2,)), pltpu.SemaphoreType.DMA((2,))]),
        out_shape=jax.ShapeDtypeStruct((n_rows + MOE_TILE, D_MODEL), F32),
        compiler_params=_cparams(("arbitrary", "arbitrary")),
        name="moe_experts",
    )(step_expert, n_used, src_tiles, src_tiles, dst_tiles, hf, wg, wu, wd)

    row = lambda w: pl.BlockSpec((ROW_TILE, w), lambda i: (i, 0))
    return pl.pallas_call(
        _combine_kernel,
        grid=(n_row_tiles,),
        in_specs=[row(D_MODEL), pl.BlockSpec((ROW_TILE, D_MODEL), lambda i: (n_row_tiles + i, 0)), row(LANE), row(D_MODEL),
                  pl.BlockSpec((None, 6, D_MODEL), lambda i: (lax.div(i, nlat), 0, 0)), _full((1, D_MODEL))],
        out_specs=row(D_MODEL),
        out_shape=jax.ShapeDtypeStruct((t_rows, D_MODEL), F32),
        compiler_params=_cparams(("parallel",)),
        name="moe_combine",
    )(y_pairs, y_pairs, rw, xs, mod, gpost)


def _rope_tables(n_lat, n_ctx):
    rows = n_lat // GRID_W
    row = jnp.repeat(jnp.arange(rows, dtype=F32), GRID_W)
    col = jnp.tile(jnp.arange(GRID_W, dtype=F32), rows)
    n_freq = HEAD_DIM // 4
    inv = ROPE_THETA ** (-jnp.arange(n_freq, dtype=F32) / n_freq)
    ang = jnp.concatenate([row[:, None] * inv, col[:, None] * inv], axis=-1)
    cos, sin = jnp.cos(ang), jnp.sin(ang)
    cos_h = jnp.concatenate([cos, cos], axis=-1)
    sin_h = jnp.concatenate([-sin, sin], axis=-1)
    cos_t = jnp.concatenate([jnp.tile(cos_h, (1, N_Q_HEADS)), jnp.ones((n_ctx, ATTN_WIDTH), F32)], axis=0)
    sin_t = jnp.concatenate([jnp.tile(sin_h, (1, N_Q_HEADS)), jnp.zeros((n_ctx, ATTN_WIDTH), F32)], axis=0)
    return cos_t, sin_t


def kernel(x, c, ctx, c_ctx, ada_w, ada_b, norm_mix_pre, norm_mix_post, norm_ffn_pre, norm_ffn_post, ev_w_in, ev_hy_conv_w, ev_hy_conv_b, ev_hy_f_w1, ev_hy_f_b1, ev_hy_f_w2, ev_hy_f_b2, ev_hy_f_wout, ev_hy_freq, ev_hy_skip, ev_q_norm, ev_k_norm, ev_w_out, ev_ffn_w_gate, ev_ffn_w_up, ev_ffn_w_down, od_w_in, od_s5_lambda_re, od_s5_lambda_im, od_s5_log_step, od_s5_b_re, od_s5_b_im, od_s5_c_re, od_s5_c_im, od_s5_d, od_glu_w_a, od_glu_w_b, od_router, od_moe_w_gate, od_moe_w_up, od_moe_w_down):
    n_batch, n_lat, _ = x.shape
    n_ctx = ctx.shape[1]
    depth = ada_w.shape[0]
    assert n_batch == 8 and n_lat % ROW_TILE == 0 and n_ctx % ROW_TILE == 0 and n_lat % n_ctx == 0
    assert depth == 2
    nlat = n_lat // ROW_TILE

    cond = jnp.concatenate([c, c_ctx[None, :], jnp.zeros((16 - n_batch - 1, D_MODEL), F32)], axis=0)
    mods = _ada_params(cond, ada_w, ada_b)
    vec = lambda a: a[None, :]

    cos_t, sin_t = _rope_tables(n_lat, n_ctx)
    hy, q, k4, v4 = _inproj_even(x, ctx, mods[0], vec(norm_mix_pre[0]), ev_w_in[0].astype(BF16),
                                 vec(jnp.tile(ev_q_norm[0], N_Q_HEADS)), vec(jnp.tile(ev_k_norm[0], N_KV_HEADS)),
                                 cos_t, sin_t, nlat)
    lt = n_lat + n_ctx
    a_lat = _attention(q, k4, v4, n_lat, min(ATTN_Q_TILE, n_lat), 0, lt, 0)
    a_ctx = _attention(q, k4, v4, n_ctx, n_ctx, n_lat // n_ctx, n_ctx, n_lat // n_ctx)
    filt_args = (ev_hy_f_w1[0], ev_hy_f_b1[0], ev_hy_f_w2[0], ev_hy_f_b2[0], ev_hy_f_wout[0], ev_hy_freq[0])
    y_hy_lat = _hyena_mixer(hy, 0, n_lat, filt_args, ev_hy_conv_w[0], ev_hy_conv_b[0], ev_hy_skip[0])
    y_hy_ctx = _hyena_mixer(hy, n_lat // n_ctx, n_ctx, filt_args, ev_hy_conv_w[0], ev_hy_conv_b[0], ev_hy_skip[0])
    xs = _outproj(y_hy_lat, y_hy_ctx, a_lat, a_ctx, x, ctx, mods[0], vec(norm_mix_post[0]), ev_w_out[0].astype(BF16), nlat)
    xs = _ffn_dense(xs, mods[0], vec(norm_ffn_pre[0]), vec(norm_ffn_post[0]), ev_ffn_w_gate[0].astype(BF16),
                    ev_ffn_w_up[0].astype(BF16), ev_ffn_w_down[0].astype(BF16), nlat)

    u3 = _inproj_odd(xs, mods[1], vec(norm_mix_pre[1]), od_w_in[0].astype(BF16), n_lat)
    s5_ops = _s5_params(od_s5_lambda_re[0], od_s5_lambda_im[0], od_s5_log_step[0],
                        od_s5_b_re[0], od_s5_b_im[0], od_s5_c_re[0], od_s5_c_im[0])
    y = _s5_scan(u3, *s5_ops, n_lat)
    x_lat, hf, rw, ri, counts = _glu(y, u3, vec(od_s5_d[0]), od_glu_w_a[0].astype(BF16), od_glu_w_b[0].astype(BF16),
                                     xs, mods[1], vec(norm_mix_post[1]), vec(norm_ffn_pre[1]), od_router[0], n_lat)
    t_rows = n_batch * n_lat
    out = _moe(hf.reshape(t_rows, D_MODEL), rw.reshape(t_rows, LANE), ri.reshape(t_rows, LANE), counts,
               x_lat.reshape(t_rows, D_MODEL), mods[1], vec(norm_ffn_post[1]), od_moe_w_gate[0].astype(BF16),
               od_moe_w_up[0].astype(BF16), od_moe_w_down[0].astype(BF16), nlat)
    return out.reshape(n_batch, n_lat, D_MODEL)
```

```python
import functools
import math

import jax
import jax.numpy as jnp
import numpy as np
from jax import lax
from jax.experimental import pallas as pl
from jax.experimental.pallas import tpu as pltpu

F32 = jnp.float32
BF16 = jnp.bfloat16

D_MODEL = 1024
EPS = 1e-6
GRID_W = 64

HY_WIDTH = 512
HY_ORDER = 2
HY_IN = (HY_ORDER + 1) * HY_WIDTH
HY_BANDS = 16
HY_EMB = 2 * HY_BANDS + 1
HY_DECAY_SLOW = -math.log(1e-2) / 1.5
HY_DECAY_FAST = -math.log(1e-2) / 0.3
HEAD_DIM = 64
N_Q_HEADS = 8
N_KV_HEADS = 2
Q_PER_KV = N_Q_HEADS // N_KV_HEADS
ATTN_WIDTH = N_Q_HEADS * HEAD_DIM
KV_WIDTH = N_KV_HEADS * HEAD_DIM
ROPE_THETA = 10000.0
LOG2_E = 1.0 / math.log(2.0)

S5_GROUP = 16
S5_GROUPS = D_MODEL // S5_GROUP
S5_STATE = 64
S5_LANES = S5_GROUPS * S5_STATE
N_EXPERTS = 8

ROW_TILE = 256
FFN_SUBTILES = 2
ATTN_Q_TILE = 512
FREQ_TILE = 512
DFT_SPLIT = 128
S5_CHUNK = 32
S5_SCAN_CHUNK = 64
MOE_TILE = 512
MOE_FF_TILE = 1792
LANE = 128
MIB = 1024 * 1024


def _cparams(sem, vmem_mib=48):
    return pltpu.CompilerParams(dimension_semantics=sem, vmem_limit_bytes=vmem_mib * MIB)


def _rms(x, gain):
    return x * lax.rsqrt(jnp.mean(x * x, axis=-1, keepdims=True) + EPS) * gain


def _bdot(a, b):
    return jnp.dot(a, b, preferred_element_type=F32)


def _full(shape):
    zeros = (0,) * len(shape)
    return pl.BlockSpec(shape, lambda *_: zeros)


def _resident(shape):
    zeros = (0,) * len(shape)
    return pl.BlockSpec(shape, lambda *_: zeros, pipeline_mode=pl.Buffered(1))


def _ada_kernel(cond_ref, w_ref, b_ref, o_ref):
    c = cond_ref[...]
    s = (c * jax.nn.sigmoid(c)).astype(BF16)
    o_ref[...] = _bdot(s, w_ref[...].astype(BF16)) + b_ref[...]


def _ada_params(cond, ada_w, ada_b):
    depth, _, n6 = ada_w.shape
    rows = cond.shape[0]
    tn = 1536
    out = pl.pallas_call(
        _ada_kernel,
        grid=(depth, n6 // tn),
        in_specs=[pl.BlockSpec((rows, D_MODEL), lambda i, j: (0, 0)),
                  pl.BlockSpec((None, D_MODEL, tn), lambda i, j: (i, 0, j)),
                  pl.BlockSpec((None, 1, tn), lambda i, j: (i, 0, j))],
        out_specs=pl.BlockSpec((None, rows, tn), lambda i, j: (i, 0, j)),
        out_shape=jax.ShapeDtypeStruct((depth, rows, n6), F32),
        compiler_params=_cparams(("arbitrary", "arbitrary")),
        name="ada_params",
    )(cond, ada_w, ada_b.reshape(depth, 1, n6))
    return out.reshape(depth, rows, 6, D_MODEL)


def _rope_rotate(t):
    w = t.shape[-1]
    lane = lax.broadcasted_iota(jnp.int32, t.shape, 1)
    first = (lane & (HEAD_DIM - 1)) < HEAD_DIM // 2
    return jnp.where(first, pltpu.roll(t, w - HEAD_DIM // 2, 1), pltpu.roll(t, HEAD_DIM // 2, 1))


def _head_slots(t):
    lane = lax.broadcasted_iota(jnp.int32, t.shape, 1)
    lo = jnp.where(lane < HEAD_DIM, t, 0.0)
    hi = jnp.where(lane >= HEAD_DIM, t, 0.0)
    return jnp.concatenate([lo, pltpu.roll(lo, HEAD_DIM, 1), pltpu.roll(hi, HEAD_DIM, 1), hi], axis=-1)


def _inproj_even_kernel(x_ref, c_ref, mod_ref, gain_ref, w_ref, qg_ref, kg_ref, e_ref, cos_ref, sin_ref,
                        hy_ref, q_ref, k4_ref, v4_ref, *, nlat):
    xin = jnp.where(pl.program_id(1) >= nlat, c_ref[...], x_ref[...])
    h = _rms(xin, gain_ref[...]) * (1.0 + mod_ref[1:2, :]) + mod_ref[0:1, :]
    p = _bdot(h.astype(BF16), w_ref[...])
    hy_ref[...] = p[:, :HY_IN].astype(hy_ref.dtype)
    q = p[:, HY_IN:HY_IN + ATTN_WIDTH]
    k = p[:, HY_IN + ATTN_WIDTH:HY_IN + ATTN_WIDTH + KV_WIDTH]
    v = p[:, HY_IN + ATTN_WIDTH + KV_WIDTH:]
    e = e_ref[...]
    qn = q * lax.rsqrt(_bdot((q * q).astype(BF16), e) + EPS) * qg_ref[...]
    kn = k * lax.rsqrt(_bdot((k * k).astype(BF16), e[:KV_WIDTH, :KV_WIDTH]) + EPS) * kg_ref[...]
    cos = cos_ref[...]
    sin = sin_ref[...]
    qr = (qn * cos + _rope_rotate(qn) * sin) * (HEAD_DIM ** -0.5 * LOG2_E)
    kr = kn * cos[:, :KV_WIDTH] + _rope_rotate(kn) * sin[:, :KV_WIDTH]
    q_ref[...] = qr.astype(q_ref.dtype)
    k4_ref[...] = _head_slots(kr).astype(k4_ref.dtype)
    v4_ref[...] = _head_slots(v).astype(v4_ref.dtype)


def _mod_spec(n_batch, nlat):
    return pl.BlockSpec((None, 6, D_MODEL), lambda b, j: (jnp.where(j >= nlat, n_batch, b), 0, 0))


def _row_spec(width):
    return pl.BlockSpec((None, ROW_TILE, width), lambda b, j: (b, j, 0))


def _lat_ctx_specs(nlat, nctx, width):
    return [pl.BlockSpec((None, ROW_TILE, width), lambda b, j: (b, jnp.minimum(j, nlat - 1), 0)),
            pl.BlockSpec((None, ROW_TILE, width), lambda b, j: (b, jnp.clip(j - nlat, 0, nctx - 1), 0))]


def _inproj_even(x, ctx, mod, gain, w_in, q_gain, k_gain, cos_t, sin_t, nlat):
    n_batch = x.shape[0]
    nctx = ctx.shape[1] // ROW_TILE
    ntile = nlat + nctx
    lt = ntile * ROW_TILE
    n_out = w_in.shape[1]
    head_avg = jnp.asarray(np.kron(np.eye(N_Q_HEADS), np.full((HEAD_DIM, HEAD_DIM), 1.0 / HEAD_DIM)), BF16)
    table = pl.BlockSpec((ROW_TILE, ATTN_WIDTH), lambda b, j: (j, 0))
    outs = [jax.ShapeDtypeStruct((n_batch, lt, HY_IN), BF16)] + [jax.ShapeDtypeStruct((n_batch, lt, ATTN_WIDTH), BF16)] * 3
    return pl.pallas_call(
        functools.partial(_inproj_even_kernel, nlat=nlat),
        grid=(n_batch, ntile),
        in_specs=_lat_ctx_specs(nlat, nctx, D_MODEL) + [
            _mod_spec(n_batch, nlat), _full((1, D_MODEL)), _full((D_MODEL, n_out)),
            _full((1, ATTN_WIDTH)), _full((1, KV_WIDTH)), _full((ATTN_WIDTH, ATTN_WIDTH)), table, table],
        out_specs=[_row_spec(HY_IN), _row_spec(ATTN_WIDTH), _row_spec(ATTN_WIDTH), _row_spec(ATTN_WIDTH)],
        out_shape=outs,
        compiler_params=_cparams(("parallel", "parallel")),
        name="inproj_even",
    )(x, ctx, mod, gain, w_in, q_gain, k_gain, head_avg, cos_t, sin_t)


def _attn_kernel(q_ref, k4_ref, v4_ref, o_ref):
    for blk in range(N_Q_HEADS // 2):
        g = (2 * blk) // Q_PER_KV
        qp = q_ref[:, LANE * blk:LANE * (blk + 1)]
        acc = None
        for r in range(2):
            slot = LANE * (2 * g + r)
            s = lax.dot_general(qp, k4_ref[:, slot:slot + LANE], (((1,), (1,)), ((), ())), preferred_element_type=F32)
            e = jnp.exp2(s - jnp.max(s, axis=-1, keepdims=True))
            inv = 1.0 / jnp.sum(e, axis=-1, keepdims=True)
            o = _bdot(e.astype(BF16), v4_ref[:, slot:slot + LANE]) * inv
            acc = o if acc is None else acc + o
        o_ref[:, LANE * blk:LANE * (blk + 1)] = acc.astype(o_ref.dtype)


def _attention(q, k4, v4, q_rows, q_tile, q_blk0, k_rows, k_blk):
    n_batch = q.shape[0]
    kv_spec = pl.BlockSpec((None, k_rows, ATTN_WIDTH), lambda b, j: (b, k_blk, 0))
    return pl.pallas_call(
        _attn_kernel,
        grid=(n_batch, q_rows // q_tile),
        in_specs=[pl.BlockSpec((None, q_tile, ATTN_WIDTH), lambda b, j: (b, q_blk0 + j, 0)), kv_spec, kv_spec],
        out_specs=pl.BlockSpec((None, q_tile, ATTN_WIDTH), lambda b, j: (b, j, 0)),
        out_shape=jax.ShapeDtypeStruct((n_batch, q_rows, ATTN_WIDTH), BF16),
        compiler_params=_cparams(("parallel", "parallel")),
        name="attention",
    )(q, k4, v4)


def _filter_kernel(z_ref, w1_ref, b1_ref, w2_ref, b2_ref, wo_ref, fr_ref, dl_ref, o_ref):
    hi = lax.Precision.HIGHEST
    z = z_ref[...]
    fr = fr_ref[...]
    h = jnp.sin(fr * (jnp.dot(z, w1_ref[...], precision=hi, preferred_element_type=F32) + b1_ref[...]))
    h = jnp.sin(fr * (jnp.dot(h, w2_ref[...], precision=hi, preferred_element_type=F32) + b2_ref[...]))
    h = jnp.dot(h, wo_ref[...], precision=hi, preferred_element_type=F32)
    decay = jnp.exp(-z[:, 0:1] * dl_ref[...])
    o_ref[...] = h * jnp.concatenate([decay] * (2 * HY_ORDER), axis=-1)


def _hyena_filters(seq, w1, b1, w2, b2, wout, freq):
    t = jnp.linspace(0.0, 1.0, seq, dtype=F32)[:, None]
    bands = jnp.linspace(1e-4, HY_BANDS - 1, HY_BANDS, dtype=F32)
    phase = (2.0 * math.pi / seq) * jnp.arange(seq, dtype=F32)[:, None] * bands
    z = jnp.concatenate([t, jnp.cos(phase), -jnp.sin(phase)], axis=-1)
    z = jnp.pad(z, ((0, 0), (0, LANE - HY_EMB)))
    w1p = jnp.pad(w1, ((0, LANE - HY_EMB), (0, 0)))
    deltas = jnp.linspace(HY_DECAY_SLOW, HY_DECAY_FAST, HY_WIDTH, dtype=F32)[None, :]
    hid = w1.shape[1]
    n_out = wout.shape[1]
    tl = ROW_TILE
    return pl.pallas_call(
        _filter_kernel,
        grid=(seq // tl,),
        in_specs=[pl.BlockSpec((tl, LANE), lambda i: (i, 0)), _full((LANE, hid)), _full((1, hid)), _full((hid, hid)),
                  _full((1, hid)), _full((hid, n_out)), _full((1, hid)), _full((1, HY_WIDTH))],
        out_specs=pl.BlockSpec((tl, n_out), lambda i: (i, 0)),
        out_shape=jax.ShapeDtypeStruct((seq, n_out), F32),
        compiler_params=_cparams(("parallel",)),
        name="hyena_filter",
    )(z, w1p, b1[None, :], w2, b2[None, :], wout, freq[None, :], deltas)


def _freq_tile(seq):
    return min(FREQ_TILE, seq)


def _dft_tables(seq):
    n = 2 * seq
    unit = 2.0 * math.pi / n
    f = jnp.arange(seq, dtype=jnp.int32)[:, None]
    t1 = jnp.arange(seq // DFT_SPLIT, dtype=jnp.int32)[None, :] * DFT_SPLIT
    t2 = jnp.arange(DFT_SPLIT, dtype=jnp.int32)[None, :]
    ang1 = ((f * t1) % n).astype(F32) * unit
    ang2 = ((f * t2) % n).astype(F32) * unit
    c1, s1 = jnp.cos(ang1)[:, :, None], jnp.sin(ang1)[:, :, None]
    c2, s2 = jnp.cos(ang2)[:, None, :], jnp.sin(ang2)[:, None, :]
    cos = (c1 * c2 - s1 * s2).reshape(seq, seq)
    sin = (s1 * c2 + c1 * s2).reshape(seq, seq)
    idx = jnp.arange(seq, dtype=jnp.int32)
    nyquist = jnp.where(idx % 2 == 0, 1.0, -1.0)
    sin_rows = jnp.where(idx[:, None] == 0, nyquist[None, :], sin)
    sin_cols = jnp.where(idx[None, :] == 0, nyquist[:, None], sin)
    tf = _freq_tile(seq)
    nf = seq // tf
    fwd = jnp.stack([cos.reshape(nf, tf, seq), sin_rows.reshape(nf, tf, seq)], axis=1).reshape(2 * seq, seq)
    inv = jnp.stack([cos.reshape(seq, nf, tf), sin_cols.reshape(seq, nf, tf)], axis=2).reshape(seq, 2 * seq)
    return fwd.astype(BF16), inv.astype(BF16)


def _kfreq_kernel(hf_ref, hb_ref, f_ref, kre_ref, ks_ref):
    fi = pl.program_id(1)
    seq = hf_ref.shape[0]
    hf = hf_ref[...]
    row = lax.broadcasted_iota(jnp.int32, hf.shape, 0)
    hb = jnp.where(row == 0, 0.0, hb_ref[...])
    f = f_ref[...]
    a1 = _bdot(f, (hf + hb).astype(BF16))
    a2 = _bdot(f, (hf - hb).astype(BF16))
    tf = f_ref.shape[0] // 2
    frow = lax.broadcasted_iota(jnp.int32, (tf, hf.shape[1]), 0)
    dc = jnp.logical_and(frow == 0, fi == 0)
    scale = jnp.where(dc, 1.0 / (2 * seq), 2.0 / (2 * seq))
    kre_ref[...] = a1[:tf] * scale
    ks_ref[...] = jnp.where(dc, a1[tf:], a2[tf:]) * scale


def _kfreq(hfilt, fwd_tab):
    seq = hfilt.shape[0]
    tf = _freq_tile(seq)
    nf = seq // tf
    out = jax.ShapeDtypeStruct((HY_ORDER, seq, HY_WIDTH), F32)
    ospec = pl.BlockSpec((None, tf, HY_WIDTH), lambda o, fi: (o, fi, 0))
    return pl.pallas_call(
        _kfreq_kernel,
        grid=(HY_ORDER, nf),
        in_specs=[pl.BlockSpec((seq, HY_WIDTH), lambda o, fi: (0, o)),
                  pl.BlockSpec((seq, HY_WIDTH), lambda o, fi: (0, HY_ORDER + o)),
                  pl.BlockSpec((2 * tf, seq), lambda o, fi: (fi, 0))],
        out_specs=[ospec, ospec],
        out_shape=[out, out],
        compiler_params=_cparams(("parallel", "parallel")),
        name="hyena_kfreq",
    )(hfilt, hfilt, fwd_tab)


def _short_conv(p, w, b):
    n = p.shape[0]
    row = lax.broadcasted_iota(jnp.int32, p.shape, 0)
    prev = jnp.where(row == 0, 0.0, pltpu.roll(p, 1, 0))
    nxt = jnp.where(row == n - 1, 0.0, pltpu.roll(p, n - 1, 0))
    return b + prev * w[0:1, :] + p * w[1:2, :] + nxt * w[2:3, :]


def _hyena_kernel(vsrc_ref, gsrc_ref, cw_ref, cb_ref, skip_ref, f_ref, ft_ref, kre_ref, ks_ref, o_ref,
                  v_scr, vb_scr, acc_scr, *, conv_v):
    fi = pl.program_id(1)
    tf = f_ref.shape[0] // 2

    @pl.when(fi == 0)
    def _():
        v = vsrc_ref[...].astype(F32)
        if conv_v:
            v = _short_conv(v, cw_ref[0], cb_ref[0])
        v_scr[...] = v
        vb_scr[...] = v.astype(BF16)
        acc_scr[...] = jnp.zeros_like(acc_scr)

    xf = _bdot(f_ref[...], vb_scr[...])
    xre, xs = xf[:tf], xf[tf:]
    kre, ks = kre_ref[...], ks_ref[...]
    row = lax.broadcasted_iota(jnp.int32, xre.shape, 0)
    dc = jnp.logical_and(row == 0, fi == 0)
    yre = jnp.where(dc, xre * kre, xre * kre - xs * ks)
    ys = jnp.where(dc, xs * ks, xre * ks + xs * kre)
    y = jnp.concatenate([yre, ys], axis=0).astype(BF16)
    acc_scr[...] += _bdot(ft_ref[...], y)

    @pl.when(fi == pl.num_programs(1) - 1)
    def _():
        gate = _short_conv(gsrc_ref[...].astype(F32), cw_ref[1], cb_ref[1])
        o_ref[...] = (gate * (acc_scr[...] + v_scr[...] * skip_ref[...])).astype(o_ref.dtype)


def _hyena_order(vsrc, v_blk, v_col, hy, row_blk, order, seq, conv_w, conv_b, skip, fwd_tab, inv_tab, kre, ks):
    n_batch = hy.shape[0]
    tf = _freq_tile(seq)
    nf = seq // tf
    conv_v = order == 0
    cw = jnp.stack([conv_w[:, :HY_WIDTH], conv_w[:, (order + 1) * HY_WIDTH:(order + 2) * HY_WIDTH]])
    cb = jnp.stack([conv_b[None, :HY_WIDTH], conv_b[None, (order + 1) * HY_WIDTH:(order + 2) * HY_WIDTH]])
    kspec = pl.BlockSpec((None, tf, HY_WIDTH), lambda b, fi: (order, fi, 0))
    return pl.pallas_call(
        functools.partial(_hyena_kernel, conv_v=conv_v),
        grid=(n_batch, nf),
        in_specs=[pl.BlockSpec((None, seq, HY_WIDTH), lambda b, fi: (b, v_blk, v_col)),
                  pl.BlockSpec((None, seq, HY_WIDTH), lambda b, fi: (b, row_blk, order + 1)),
                  _full((2, 3, HY_WIDTH)), _full((2, 1, HY_WIDTH)), _full((1, HY_WIDTH)),
                  pl.BlockSpec((2 * tf, seq), lambda b, fi: (fi, 0)),
                  pl.BlockSpec((seq, 2 * tf), lambda b, fi: (0, fi)),
                  kspec, kspec],
        out_specs=pl.BlockSpec((None, seq, HY_WIDTH), lambda b, fi: (b, 0, 0)),
        out_shape=jax.ShapeDtypeStruct((n_batch, seq, HY_WIDTH), BF16),
        scratch_shapes=[pltpu.VMEM((seq, HY_WIDTH), F32), pltpu.VMEM((seq, HY_WIDTH), BF16),
                        pltpu.VMEM((seq, HY_WIDTH), F32)],
        compiler_params=_cparams(("parallel", "arbitrary")),
        name="hyena_order%d" % order,
    )(vsrc, hy, cw, cb, skip[order][None, :], fwd_tab, inv_tab, kre, ks)


def _hyena_mixer(hy, row_blk, seq, filt_args, conv_w, conv_b, skip):
    hfilt = _hyena_filters(seq, *filt_args)
    fwd_tab, inv_tab = _dft_tables(seq)
    kre, ks = _kfreq(hfilt, fwd_tab)
    v1 = _hyena_order(hy, row_blk, 0, hy, row_blk, 0, seq, conv_w, conv_b, skip, fwd_tab, inv_tab, kre, ks)
    return _hyena_order(v1, 0, 0, hy, row_blk, 1, seq, conv_w, conv_b, skip, fwd_tab, inv_tab, kre, ks)


def _outproj_kernel(yl_ref, yc_ref, al_ref, ac_ref, x_ref, c_ref, mod_ref, gpost_ref, w_ref, o_ref, *, nlat):
    is_ctx = pl.program_id(1) >= nlat
    yh = jnp.where(is_ctx, yc_ref[...], yl_ref[...])
    ya = jnp.where(is_ctx, ac_ref[...], al_ref[...])
    out = _bdot(yh, w_ref[:HY_WIDTH, :]) + _bdot(ya, w_ref[HY_WIDTH:, :])
    o_ref[...] = jnp.where(is_ctx, c_ref[...], x_ref[...]) + mod_ref[2:3, :] * _rms(out, gpost_ref[...])


def _outproj(y_lat, y_ctx, a_lat, a_ctx, x, ctx, mod, gpost, w_out, nlat):
    n_batch = x.shape[0]
    nctx = ctx.shape[1] // ROW_TILE
    ntile = nlat + nctx
    return pl.pallas_call(
        functools.partial(_outproj_kernel, nlat=nlat),
        grid=(n_batch, ntile),
        in_specs=(_lat_ctx_specs(nlat, nctx, HY_WIDTH) + _lat_ctx_specs(nlat, nctx, ATTN_WIDTH)
                  + _lat_ctx_specs(nlat, nctx, D_MODEL)
                  + [_mod_spec(n_batch, nlat), _full((1, D_MODEL)), _full(w_out.shape)]),
        out_specs=_row_spec(D_MODEL),
        out_shape=jax.ShapeDtypeStruct((n_batch, ntile * ROW_TILE, D_MODEL), F32),
        compiler_params=_cparams(("parallel", "parallel")),
        name="outproj_even",
    )(y_lat, y_ctx, a_lat, a_ctx, x, ctx, mod, gpost, w_out)


def _ffn_kernel(x_ref, mod_ref, gpre_ref, gpost_ref, wg_ref, wu_ref, wd_ref, o_ref, *, n_batch, tiles_per_batch, nlat):
    n_sub = x_ref.shape[0] // ROW_TILE
    subs = []
    for s in range(n_sub):
        tile = pl.program_id(0) * n_sub + s
        batch = lax.div(tile, tiles_per_batch)
        mod = mod_ref[jnp.where(lax.rem(tile, tiles_per_batch) >= nlat, n_batch, batch)]
        x = x_ref[s * ROW_TILE:(s + 1) * ROW_TILE, :]
        subs.append((x, mod, (_rms(x, gpre_ref[...]) * (1.0 + mod[4:5, :]) + mod[3:4, :]).astype(BF16)))
    h = jnp.concatenate([sub[2] for sub in subs], axis=0)
    g = _bdot(h, wg_ref[...])
    u = _bdot(h, wu_ref[...])
    a = (g * jax.nn.sigmoid(g) * u).astype(BF16)
    y = _bdot(a, wd_ref[...])
    for s, (x, mod, _) in enumerate(subs):
        rows = slice(s * ROW_TILE, (s + 1) * ROW_TILE)
        o_ref[rows, :] = x + mod[5:6, :] * _rms(y[rows, :], gpost_ref[...])


def _ffn_dense(xs, mod, gpre, gpost, wg, wu, wd, nlat):
    n_batch, lt, _ = xs.shape
    tiles_per_batch = lt // ROW_TILE
    rows = FFN_SUBTILES * ROW_TILE
    n_steps = n_batch * tiles_per_batch // FFN_SUBTILES
    blk = pl.BlockSpec((rows, D_MODEL), lambda i: (i, 0))
    out = pl.pallas_call(
        functools.partial(_ffn_kernel, n_batch=n_batch, tiles_per_batch=tiles_per_batch, nlat=nlat),
        grid=(n_steps,),
        in_specs=[blk, _resident(mod.shape), _resident((1, D_MODEL)), _resident((1, D_MODEL)),
                  _resident(wg.shape), _resident(wu.shape), _resident(wd.shape)],
        out_specs=blk,
        out_shape=jax.ShapeDtypeStruct((n_batch * lt, D_MODEL), F32),
        compiler_params=_cparams(("parallel",), 56),
        name="ffn_dense",
    )(xs.reshape(n_batch * lt, D_MODEL), mod, gpre, gpost, wg, wu, wd)
    return out.reshape(xs.shape)


def _mod_rows(mod_ref, k, n_batch, is_ctx):
    return jnp.where(is_ctx, mod_ref[n_batch:n_batch + 1, k:k + 1, :], mod_ref[0:n_batch, k:k + 1, :])


def _row_order_swap(n_outer, n_inner):
    n = n_outer * n_inner
    dst = np.arange(n)
    src = (dst % n_outer) * n_inner + dst // n_outer
    return jnp.asarray(src[:, None] == np.arange(n)[None, :], BF16)


def _inproj_odd_kernel(x_ref, mod_ref, gain_ref, swap_ref, w_ref, u_ref, *, n_lat_steps):
    n_batch, steps, _ = x_ref.shape
    is_ctx = pl.program_id(0) >= n_lat_steps
    h = (_rms(x_ref[...], gain_ref[...]) * (1.0 + _mod_rows(mod_ref, 1, n_batch, is_ctx))
         + _mod_rows(mod_ref, 0, n_batch, is_ctx))
    h = h.reshape(n_batch * steps, D_MODEL).astype(BF16)
    h = _bdot(swap_ref[...], h).astype(BF16)
    u_ref[...] = _bdot(h, w_ref[...]).reshape(steps, n_batch, D_MODEL)


def _inproj_odd(xs, mod, gain, w_in, n_lat):
    n_batch, lt, _ = xs.shape
    p = S5_CHUNK
    swap = _row_order_swap(n_batch, p)
    return pl.pallas_call(
        functools.partial(_inproj_odd_kernel, n_lat_steps=n_lat // p),
        grid=(lt // p,),
        in_specs=[pl.BlockSpec((n_batch, p, D_MODEL), lambda i: (0, i, 0)), _full(mod.shape), _full((1, D_MODEL)),
                  _full(swap.shape), _full(w_in.shape)],
        out_specs=pl.BlockSpec((p, n_batch, D_MODEL), lambda i: (i, 0, 0)),
        out_shape=jax.ShapeDtypeStruct((lt, n_batch, D_MODEL), F32),
        compiler_params=_cparams(("parallel",)),
        name="inproj_odd",
    )(xs, mod, gain, swap, w_in)


def _s5_param_kernel(lr_ref, li_ref, ls_ref, lrx_ref, lix_ref, lsx_ref, br_ref, bi_ref,
                     abr_ref, abi_ref, bbr_ref, bbi_ref):
    def zoh(lr_raw, li, log_step):
        lr = jnp.minimum(lr_raw, -1e-4)
        dt = jnp.exp(log_step)
        mag = jnp.exp(lr * dt)
        ab_re = mag * jnp.cos(li * dt)
        ab_im = mag * jnp.sin(li * dt)
        den = lr * lr + li * li
        nr, ni = ab_re - 1.0, ab_im
        return ab_re, ab_im, (nr * lr + ni * li) / den, (ni * lr - nr * li) / den

    ab_re, ab_im, _, _ = zoh(lr_ref[...], li_ref[...], ls_ref[...])
    abr_ref[...] = ab_re
    abi_ref[...] = ab_im
    _, _, co_re, co_im = zoh(lrx_ref[...], lix_ref[...], lsx_ref[...])
    br, bi = br_ref[...], bi_ref[...]
    bbr_ref[...] = co_re * br - co_im * bi
    bbi_ref[...] = co_re * bi + co_im * br


def _s5_params(lam_re, lam_im, log_step, b_re, b_im, c_re, c_im):
    nd, g, n = lam_re.shape
    k = S5_GROUP
    rep = lambda a: jnp.repeat(a, k, axis=1)
    ls = log_step[:, :, None]
    bt_re = jnp.swapaxes(b_re, 2, 3).reshape(nd, g * k, n)
    bt_im = jnp.swapaxes(b_im, 2, 3).reshape(nd, g * k, n)
    small = jax.ShapeDtypeStruct((nd, g, n), F32)
    big = jax.ShapeDtypeStruct((nd, g * k, n), F32)
    ab_re, ab_im, bb_re, bb_im = pl.pallas_call(
        _s5_param_kernel, out_shape=[small, small, big, big], name="s5_discretise",
    )(lam_re, lam_im, ls, rep(lam_re), rep(lam_im), rep(ls), bt_re, bt_im)
    bb_re = bb_re.reshape(nd, g, k, n)
    bb_im = bb_im.reshape(nd, g, k, n)
    cmul = lambda xr, xi, yr, yi: (xr * yr - xi * yi, xr * yi + xi * yr)
    a_re, a_im = ab_re[:, :, None, :], ab_im[:, :, None, :]
    a2_re, a2_im = cmul(a_re, a_im, a_re, a_im)
    abb_re, abb_im = cmul(a_re, a_im, bb_re, bb_im)
    ca_re, ca_im = cmul(c_re, c_im, a_re, a_im)
    ca2_re, ca2_im = cmul(c_re, c_im, a2_re, a2_im)
    real_cb = lambda xr, xi: jnp.einsum('dgin,dgkn->dgik', xr, bb_re) - jnp.einsum('dgin,dgkn->dgik', xi, bb_im)
    k0 = real_cb(c_re, c_im)
    k1 = real_cb(ca_re, ca_im)
    a2 = jnp.stack([a2_re.reshape(nd, g * n), a2_im.reshape(nd, g * n)], axis=1)
    eye = jnp.eye(8, dtype=F32)
    nq = g // 8
    blocked = lambda t: t.reshape((nd, nq, 8) + t.shape[2:])

    def in_to_state(bb):
        return jnp.einsum('dqgkn,gh->dqgkhn', blocked(bb), eye).reshape(nd, nq, 8 * k, 8 * n)

    def state_to_out(c):
        return jnp.einsum('dqgin,gh->dqhngi', blocked(c), eye).reshape(nd, nq, 8 * n, 8 * k)

    def in_to_out(m):
        return jnp.einsum('dqgik,gh->dqgkhi', blocked(m), eye).reshape(nd, nq, 8 * k, 8 * k)

    w_in = jnp.concatenate([
        jnp.concatenate([in_to_state(abb_re), in_to_state(abb_im)], axis=-1),
        jnp.concatenate([in_to_state(bb_re), in_to_state(bb_im)], axis=-1)], axis=2).astype(BF16)
    w_state = jnp.stack([
        jnp.concatenate([state_to_out(ca_re), state_to_out(ca2_re)], axis=-1),
        jnp.concatenate([state_to_out(-ca_im), state_to_out(-ca2_im)], axis=-1)], axis=2).astype(BF16)
    d0, d1 = in_to_out(k0), in_to_out(k1)
    w_dir = jnp.concatenate([jnp.concatenate([d0, d1], axis=-1),
                             jnp.concatenate([jnp.zeros_like(d0), d0], axis=-1)], axis=2).astype(BF16)
    return a2, w_in, w_state, w_dir


def _s5_scan_kernel(u_ref, a2_ref, win_ref, wst_ref, wdir_ref, y_ref, sbuf, state, *, nctx_chunks):
    d = pl.program_id(0)
    i = pl.program_id(1)
    p_steps, n_batch, _ = u_ref.shape
    npair = p_steps // 2
    rows = npair * n_batch
    half = S5_LANES
    nq = win_ref.shape[0]
    kq = win_ref.shape[1] // 2
    sq = win_ref.shape[2] // 2

    @pl.when(i == 0)
    def _():
        state[...] = jnp.zeros_like(state)

    fwd = d == 0
    u = u_ref[...].reshape(npair, 2, n_batch, D_MODEL)
    u_even = u[:, 0].reshape(rows, D_MODEL).astype(BF16)
    u_odd = u[:, 1].reshape(rows, D_MODEL).astype(BF16)
    u_1 = jnp.where(fwd, u_even, u_odd)
    u_2 = jnp.where(fwd, u_odd, u_even)
    pair_in = lambda q: jnp.concatenate([u_1[:, kq * q:kq * (q + 1)], u_2[:, kq * q:kq * (q + 1)]], axis=-1)
    for q in range(nq):
        r = _bdot(pair_in(q), win_ref[q])
        sbuf[:, :, sq * q:sq * (q + 1)] = r[:, :sq].reshape(npair, n_batch, sq)
        sbuf[:, :, half + sq * q:half + sq * (q + 1)] = r[:, sq:].reshape(npair, n_batch, sq)

    for q in range(nq):
        lo = sq * q
        ar = jnp.broadcast_to(a2_ref[0:1, lo:lo + sq], (n_batch, sq))
        ai = jnp.broadcast_to(a2_ref[1:2, lo:lo + sq], (n_batch, sq))

        def body(t, carry, lo=lo, ar=ar, ai=ai):
            sr, si = carry
            tt = jnp.where(d == 0, t, npair - 1 - t)
            nr = ar * sr - ai * si + sbuf[tt, :, lo:lo + sq]
            ni = ar * si + ai * sr + sbuf[tt, :, half + lo:half + lo + sq]
            sbuf[tt, :, lo:lo + sq] = sr
            sbuf[tt, :, half + lo:half + lo + sq] = si
            return nr, ni

        sr, si = lax.fori_loop(0, npair, body, (state[:, lo:lo + sq], state[:, half + lo:half + lo + sq]), unroll=4)
        state[:, lo:lo + sq] = sr
        state[:, half + lo:half + lo + sq] = si

    @pl.when(i >= nctx_chunks)
    def _():
        s = sbuf[...].reshape(rows, 2 * half).astype(BF16)
        for q in range(nq):
            yq = (_bdot(s[:, sq * q:sq * (q + 1)], wst_ref[q, 0])
                  + _bdot(s[:, half + sq * q:half + sq * (q + 1)], wst_ref[q, 1])
                  + _bdot(pair_in(q), wdir_ref[q]))
            y_1 = yq[:, :kq].reshape(npair, 1, n_batch, kq)
            y_2 = yq[:, kq:].reshape(npair, 1, n_batch, kq)
            y_pair = jnp.concatenate([jnp.where(fwd, y_1, y_2), jnp.where(fwd, y_2, y_1)], axis=1)
            y_ref[:, :, kq * q:kq * (q + 1)] = y_pair.reshape(p_steps, n_batch, kq)


def _s5_scan(u3, a2, w_in, w_state, w_dir, n_lat):
    lt, n_batch, _ = u3.shape
    p = S5_SCAN_CHUNK
    nchunk = lt // p
    nlatc = n_lat // p
    nctxc = nchunk - nlatc

    def u_map(d, i):
        return (jnp.where(d == 0, lax.rem(i + nlatc, nchunk), nchunk - 1 - i), 0, 0)

    def y_map(d, i):
        return (d, jnp.where(d == 0, jnp.maximum(i - nctxc, 0), jnp.minimum(nchunk - 1 - i, nlatc - 1)), 0, 0)

    per_dir = lambda w: pl.BlockSpec((None,) + w.shape[1:], lambda d, i: (d,) + (0,) * (w.ndim - 1))
    return pl.pallas_call(
        functools.partial(_s5_scan_kernel, nctx_chunks=nctxc),
        grid=(2, nchunk),
        in_specs=[pl.BlockSpec((p, n_batch, D_MODEL), u_map), per_dir(a2), per_dir(w_in), per_dir(w_state),
                  per_dir(w_dir)],
        out_specs=pl.BlockSpec((None, p, n_batch, D_MODEL), y_map),
        out_shape=jax.ShapeDtypeStruct((2, n_lat, n_batch, D_MODEL), F32),
        scratch_shapes=[pltpu.VMEM((p // 2, n_batch, 2 * S5_LANES), F32), pltpu.VMEM((n_batch, 2 * S5_LANES), F32)],
        compiler_params=_cparams(("arbitrary", "arbitrary")),
        name="s5_scan",
    )(u3, a2, w_in, w_state, w_dir)


def _glu_kernel(yf_ref, yb_ref, u_ref, dskip_ref, wa_ref, wb_ref, x_ref, mod_ref, gpost_ref, gpre_ref, router_ref,
                swap_ref, tri_ref, xo_ref, hf_ref, rw_ref, ri_ref, cnt_ref, carry):
    n_batch, steps, _ = x_ref.shape
    rows = n_batch * steps

    @pl.when(pl.program_id(0) == 0)
    def _():
        carry[...] = jnp.zeros_like(carry)

    y = (yf_ref[...] + yb_ref[...] + dskip_ref[...] * u_ref[...]).reshape(rows, D_MODEL)
    z = jax.nn.gelu(y).astype(BF16)
    z = _bdot(swap_ref[...], z).astype(BF16)
    out = _bdot(z, wa_ref[...]) * jax.nn.sigmoid(_bdot(z, wb_ref[...]))
    out = out.reshape(n_batch, steps, D_MODEL)
    mod = lambda k: mod_ref[0:n_batch, k:k + 1, :]
    xn = x_ref[...] + mod(2) * _rms(out, gpost_ref[...])
    xo_ref[...] = xn
    hf = _rms(xn, gpre_ref[...]) * (1.0 + mod(4)) + mod(3)
    hf_ref[...] = hf
    hf = hf.reshape(rows, D_MODEL)
    h_hi = hf.astype(BF16)
    h_lo = (hf - h_hi.astype(F32)).astype(BF16)
    part = _bdot(h_hi, router_ref[...])
    logits = part[:, :LANE] + part[:, LANE:] + _bdot(h_lo, router_ref[:, :LANE])
    lane = lax.broadcasted_iota(jnp.int32, logits.shape, 1)
    neg = jnp.float32(-jnp.inf)
    lg = jnp.where(lane < N_EXPERTS, logits, neg)
    m1 = jnp.max(lg, axis=-1, keepdims=True)
    i1 = jnp.min(jnp.where(lg == m1, lane, LANE), axis=-1, keepdims=True)
    lg2 = jnp.where(lane == i1, neg, lg)
    m2 = jnp.max(lg2, axis=-1, keepdims=True)
    i2 = jnp.min(jnp.where(lg2 == m2, lane, LANE), axis=-1, keepdims=True)
    e2 = jnp.exp(m2 - m1)
    w1 = 1.0 / (1.0 + e2)
    rw_ref[...] = jnp.where(lane == 0, w1, jnp.where(lane == 1, e2 * w1, 0.0)).reshape(n_batch, steps, LANE)
    member = jnp.where(lane == i1, 1.0, jnp.where(lane == i2, 1.0, 0.0))
    base = carry[...] + _bdot(tri_ref[...], member.astype(BF16))
    r1 = jnp.sum(jnp.where(lane == i1, base, 0.0), axis=-1, keepdims=True).astype(jnp.int32)
    r2 = jnp.sum(jnp.where(lane == i2, base, 0.0), axis=-1, keepdims=True).astype(jnp.int32)
    ri = jnp.where(lane == 0, i1, jnp.where(lane == 1, i2, jnp.where(lane == 2, r1, jnp.where(lane == 3, r2, 0))))
    ri_ref[...] = ri.reshape(n_batch, steps, LANE)
    carry[...] += jnp.sum(member, axis=0, keepdims=True)
    cnt_ref[...] = carry[...]


def _glu(y, u3, d_skip, w_a, w_b, xs, mod, gpost, gpre, router, n_lat):
    n_batch = xs.shape[0]
    p = S5_CHUNK
    rows = p * n_batch
    router_p = jnp.pad(router, ((0, 0), (0, LANE - router.shape[1])))
    router_hi = router_p.astype(BF16)
    router_cat = jnp.concatenate([router_hi, (router_p - router_hi.astype(F32)).astype(BF16)], axis=1)
    tri = jnp.asarray(np.arange(rows)[:, None] > np.arange(rows)[None, :], BF16)
    swap = _row_order_swap(p, n_batch)
    bt_spec = lambda w: pl.BlockSpec((n_batch, p, w), lambda i: (0, i, 0))
    return pl.pallas_call(
        _glu_kernel,
        grid=(n_lat // p,),
        in_specs=[pl.BlockSpec((None, p, n_batch, D_MODEL), lambda i: (0, i, 0, 0)),
                  pl.BlockSpec((None, p, n_batch, D_MODEL), lambda i: (1, i, 0, 0)),
                  pl.BlockSpec((p, n_batch, D_MODEL), lambda i: (i, 0, 0)),
                  _full((1, D_MODEL)), _full(w_a.shape), _full(w_b.shape), bt_spec(D_MODEL),
                  _full(mod.shape), _full((1, D_MODEL)), _full((1, D_MODEL)),
                  _full(router_cat.shape), _full(swap.shape), _full(tri.shape)],
        out_specs=[bt_spec(D_MODEL), bt_spec(D_MODEL), bt_spec(LANE), bt_spec(LANE), _full((1, LANE))],
        out_shape=[jax.ShapeDtypeStruct((n_batch, n_lat, D_MODEL), F32),
                   jax.ShapeDtypeStruct((n_batch, n_lat, D_MODEL), F32),
                   jax.ShapeDtypeStruct((n_batch, n_lat, LANE), F32),
                   jax.ShapeDtypeStruct((n_batch, n_lat, LANE), jnp.int32),
                   jax.ShapeDtypeStruct((1, LANE), F32)],
        scratch_shapes=[pltpu.VMEM((1, LANE), F32)],
        compiler_params=_cparams(("arbitrary",)),
        name="s5_glu_router",
    )(y, y, u3, d_skip, w_a, w_b, xs, mod, gpost, gpre, router_cat, swap, tri)


def _moe_plan(ri, counts, n_tiles):
    experts = jnp.arange(N_EXPERTS, dtype=jnp.int32)
    n_of = (counts[0, :N_EXPERTS].astype(jnp.int32) + MOE_TILE - 1) // MOE_TILE
    ends = jnp.cumsum(n_of)
    starts = ends - n_of
    start_of = jnp.sum(jnp.where(ri[:, 0:2, None] == experts, starts, 0), axis=-1)
    n_tok = ri.shape[0]
    pos = (start_of * MOE_TILE + ri[:, 2:4]).T.reshape(-1)
    n_used = ends[-1]
    tile = jnp.arange(n_tiles, dtype=jnp.int32)
    tile_expert = jnp.sum((jnp.minimum(tile, n_used - 1)[:, None] >= ends[None, :]).astype(jnp.int32), axis=1)
    n_pairs = pos.shape[0]
    n_rows = n_tiles * MOE_TILE
    pair_of = jnp.full((n_rows,), -1, jnp.int32).at[pos].set(jnp.arange(n_pairs, dtype=jnp.int32),
                                                             unique_indices=True, mode='promise_in_bounds')
    is_pad = pair_of < 0
    pair_of = jnp.where(is_pad, n_pairs - 1 + jnp.cumsum(is_pad.astype(jnp.int32)), pair_of)
    src_token = jnp.where(is_pad, 0, pair_of % n_tok)
    return src_token, pair_of, tile_expert, n_used.reshape(1)


def _moe_expert_kernel(te_ref, nu_ref, src0_ref, src_ref, dst_ref, hf_ref, wg_ref, wu_ref, wd_ref, out_ref,
                       xbuf, obuf, acc, gsem, ssem, *, rows_per_step):
    del te_ref
    i = pl.program_id(0)
    j = pl.program_id(1)
    last_j = pl.num_programs(1) - 1
    nu = nu_ref[0]
    cur = lax.rem(i, 2)
    nxt = 1 - cur
    row0 = j * rows_per_step

    def gather_row(idx_ref, row, slot):
        return pltpu.make_async_copy(hf_ref.at[pl.ds(idx_ref[0, row], 1)], xbuf.at[slot, pl.ds(row, 1)], gsem.at[slot])

    def scatter_row(row, slot):
        return pltpu.make_async_copy(obuf.at[slot, pl.ds(row, 1)], out_ref.at[pl.ds(dst_ref[0, row], 1)], ssem.at[slot])

    def whole_tile_wait(sem, slot):
        pltpu.make_async_copy(xbuf.at[slot], obuf.at[slot], sem.at[slot]).wait()

    @pl.when(jnp.logical_and(i == 0, j == 0))
    def _():
        obuf[...] = jnp.zeros_like(obuf)

        def first(r, c):
            gather_row(src0_ref, r, 0).start()
            return c

        lax.fori_loop(0, MOE_TILE, first, 0)

    @pl.when(jnp.logical_and(i <= nu, j == 0))
    def _():
        whole_tile_wait(gsem, cur)

    for step in range(MOE_TILE // rows_per_step):
        @pl.when(jnp.logical_and(i < nu, j == step))
        def _(step=step):
            for row in range(step * rows_per_step, (step + 1) * rows_per_step):
                gather_row(src_ref, row, nxt).start()
                scatter_row(row, nxt).start()

    @pl.when(i < nu)
    def _():
        h = xbuf[cur].astype(BF16)
        g = _bdot(h, wg_ref[...])
        u = _bdot(h, wu_ref[...])
        part = _bdot((g * jax.nn.sigmoid(g) * u).astype(BF16), wd_ref[...])
        acc[...] = jnp.where(j == 0, part, acc[...] + part)

    @pl.when(i >= nu)
    def _():
        def tail(c, carry):
            scatter_row(row0 + c, nxt).start()
            return carry

        lax.fori_loop(0, rows_per_step, tail, 0)

    @pl.when(jnp.logical_and(i >= 1, j == last_j))
    def _():
        whole_tile_wait(ssem, cur)

    @pl.when(jnp.logical_and(i < nu, j == last_j))
    def _():
        obuf[cur] = acc[...]

    @pl.when(jnp.logical_and(i == pl.num_programs(0) - 1, j == last_j))
    def _():
        whole_tile_wait(ssem, nxt)


def _combine_kernel(ya_ref, yb_ref, rw_ref, x_ref, mod_ref, gpost_ref, o_ref):
    rw = rw_ref[...]
    y = rw[:, 0:1] * ya_ref[...] + rw[:, 1:2] * yb_ref[...]
    o_ref[...] = x_ref[...] + mod_ref[5:6, :] * _rms(y, gpost_ref[...])


def _moe(hf, rw, ri, counts, xs, mod, gpost, wg, wu, wd, nlat):
    t_rows = hf.shape[0]
    d_ff = wg.shape[2]
    n_tiles = 2 * t_rows // MOE_TILE + N_EXPERTS
    nff = d_ff // MOE_FF_TILE
    n_row_tiles = t_rows // ROW_TILE
    n_rows = n_tiles * MOE_TILE
    src_token, pair_of, tile_expert, n_used = _moe_plan(ri, counts, n_tiles)
    spare = jnp.arange(MOE_TILE, dtype=jnp.int32)
    src_tiles = jnp.concatenate([src_token, 0 * spare]).reshape(n_tiles + 1, 1, MOE_TILE)
    dst_tiles = jnp.concatenate([n_rows + spare, pair_of]).reshape(n_tiles + 1, 1, MOE_TILE)
    step_expert = jnp.concatenate([tile_expert, tile_expert[-1:]])
    any_spec = pl.BlockSpec(memory_space=pl.ANY)
    idx_block = (None, 1, MOE_TILE)

    def ff_blk(i, j, nu):
        return jnp.where(i < nu[0], j, nff - 1)

    y_pairs = pl.pallas_call(
        functools.partial(_moe_expert_kernel, rows_per_step=MOE_TILE // nff),
        grid_spec=pltpu.PrefetchScalarGridSpec(
            num_scalar_prefetch=2,
            grid=(n_tiles + 1, nff),
            in_specs=[pl.BlockSpec(idx_block, lambda i, j, te, nu: (0, 0, 0), memory_space=pltpu.SMEM),
                      pl.BlockSpec(idx_block, lambda i, j, te, nu: (jnp.minimum(i + 1, n_tiles), 0, 0),
                                   memory_space=pltpu.SMEM),
                      pl.BlockSpec(idx_block, lambda i, j, te, nu: (i, 0, 0), memory_space=pltpu.SMEM),
                      any_spec,
                      pl.BlockSpec((None, D_MODEL, MOE_FF_TILE), lambda i, j, te, nu: (te[i], 0, ff_blk(i, j, nu))),
                      pl.BlockSpec((None, D_MODEL, MOE_FF_TILE), lambda i, j, te, nu: (te[i], 0, ff_blk(i, j, nu))),
                      pl.BlockSpec((None, MOE_FF_TILE, D_MODEL), lambda i, j, te, nu: (te[i], ff_blk(i, j, nu), 0))],
            out_specs=any_spec,
            scratch_shapes=[pltpu.VMEM((2, MOE_TILE, D_MODEL), F32), pltpu.VMEM((2, MOE_TILE, D_MODEL), F32),
                            pltpu.VMEM((MOE_TILE, D_MODEL), F32),
                            pltpu.SemaphoreType.DMA((2,)), pltpu.SemaphoreType.DMA((2,))]),
        out_shape=jax.ShapeDtypeStruct((n_rows + MOE_TILE, D_MODEL), F32),
        compiler_params=_cparams(("arbitrary", "arbitrary")),
        name="moe_experts",
    )(step_expert, n_used, src_tiles, src_tiles, dst_tiles, hf, wg, wu, wd)

    row = lambda w: pl.BlockSpec((ROW_TILE, w), lambda i: (i, 0))
    return pl.pallas_call(
        _combine_kernel,
        grid=(n_row_tiles,),
        in_specs=[row(D_MODEL), pl.BlockSpec((ROW_TILE, D_MODEL), lambda i: (n_row_tiles + i, 0)), row(LANE), row(D_MODEL),
                  pl.BlockSpec((None, 6, D_MODEL), lambda i: (lax.div(i, nlat), 0, 0)), _full((1, D_MODEL))],
        out_specs=row(D_MODEL),
        out_shape=jax.ShapeDtypeStruct((t_rows, D_MODEL), F32),
        compiler_params=_cparams(("parallel",)),
        name="moe_combine",
    )(y_pairs, y_pairs, rw, xs, mod, gpost)


def _rope_tables(n_lat, n_ctx):
    rows = n_lat // GRID_W
    row = jnp.repeat(jnp.arange(rows, dtype=F32), GRID_W)
    col = jnp.tile(jnp.arange(GRID_W, dtype=F32), rows)
    n_freq = HEAD_DIM // 4
    inv = ROPE_THETA ** (-jnp.arange(n_freq, dtype=F32) / n_freq)
    ang = jnp.concatenate([row[:, None] * inv, col[:, None] * inv], axis=-1)
    cos, sin = jnp.cos(ang), jnp.sin(ang)
    cos_h = jnp.concatenate([cos, cos], axis=-1)
    sin_h = jnp.concatenate([-sin, sin], axis=-1)
    cos_t = jnp.concatenate([jnp.tile(cos_h, (1, N_Q_HEADS)), jnp.ones((n_ctx, ATTN_WIDTH), F32)], axis=0)
    sin_t = jnp.concatenate([jnp.tile(sin_h, (1, N_Q_HEADS)), jnp.zeros((n_ctx, ATTN_WIDTH), F32)], axis=0)
    return cos_t, sin_t


def kernel(x, c, ctx, c_ctx, ada_w, ada_b, norm_mix_pre, norm_mix_post, norm_ffn_pre, norm_ffn_post, ev_w_in, ev_hy_conv_w, ev_hy_conv_b, ev_hy_f_w1, ev_hy_f_b1, ev_hy_f_w2, ev_hy_f_b2, ev_hy_f_wout, ev_hy_freq, ev_hy_skip, ev_q_norm, ev_k_norm, ev_w_out, ev_ffn_w_gate, ev_ffn_w_up, ev_ffn_w_down, od_w_in, od_s5_lambda_re, od_s5_lambda_im, od_s5_log_step, od_s5_b_re, od_s5_b_im, od_s5_c_re, od_s5_c_im, od_s5_d, od_glu_w_a, od_glu_w_b, od_router, od_moe_w_gate, od_moe_w_up, od_moe_w_down):
    n_batch, n_lat, _ = x.shape
    n_ctx = ctx.shape[1]
    depth = ada_w.shape[0]
    assert n_batch == 8 and n_lat % ROW_TILE == 0 and n_ctx % ROW_TILE == 0 and n_lat % n_ctx == 0
    assert depth == 2
    nlat = n_lat // ROW_TILE

    cond = jnp.concatenate([c, c_ctx[None, :], jnp.zeros((16 - n_batch - 1, D_MODEL), F32)], axis=0)
    mods = _ada_params(cond, ada_w, ada_b)
    vec = lambda a: a[None, :]

    cos_t, sin_t = _rope_tables(n_lat, n_ctx)
    hy, q, k4, v4 = _inproj_even(x, ctx, mods[0], vec(norm_mix_pre[0]), ev_w_in[0].astype(BF16),
                                 vec(jnp.tile(ev_q_norm[0], N_Q_HEADS)), vec(jnp.tile(ev_k_norm[0], N_KV_HEADS)),
                                 cos_t, sin_t, nlat)
    lt = n_lat + n_ctx
    a_lat = _attention(q, k4, v4, n_lat, min(ATTN_Q_TILE, n_lat), 0, lt, 0)
    a_ctx = _attention(q, k4, v4, n_ctx, n_ctx, n_lat // n_ctx, n_ctx, n_lat // n_ctx)
    filt_args = (ev_hy_f_w1[0], ev_hy_f_b1[0], ev_hy_f_w2[0], ev_hy_f_b2[0], ev_hy_f_wout[0], ev_hy_freq[0])
    y_hy_lat = _hyena_mixer(hy, 0, n_lat, filt_args, ev_hy_conv_w[0], ev_hy_conv_b[0], ev_hy_skip[0])
    y_hy_ctx = _hyena_mixer(hy, n_lat // n_ctx, n_ctx, filt_args, ev_hy_conv_w[0], ev_hy_conv_b[0], ev_hy_skip[0])
    xs = _outproj(y_hy_lat, y_hy_ctx, a_lat, a_ctx, x, ctx, mods[0], vec(norm_mix_post[0]), ev_w_out[0].astype(BF16), nlat)
    xs = _ffn_dense(xs, mods[0], vec(norm_ffn_pre[0]), vec(norm_ffn_post[0]), ev_ffn_w_gate[0].astype(BF16),
                    ev_ffn_w_up[0].astype(BF16), ev_ffn_w_down[0].astype(BF16), nlat)

    u3 = _inproj_odd(xs, mods[1], vec(norm_mix_pre[1]), od_w_in[0].astype(BF16), n_lat)
    s5_ops = _s5_params(od_s5_lambda_re[0], od_s5_lambda_im[0], od_s5_log_step[0],
                        od_s5_b_re[0], od_s5_b_im[0], od_s5_c_re[0], od_s5_c_im[0])
    y = _s5_scan(u3, *s5_ops, n_lat)
    x_lat, hf, rw, ri, counts = _glu(y, u3, vec(od_s5_d[0]), od_glu_w_a[0].astype(BF16), od_glu_w_b[0].astype(BF16),
                                     xs, mods[1], vec(norm_mix_post[1]), vec(norm_ffn_pre[1]), od_router[0], n_lat)
    t_rows = n_batch * n_lat
    out = _moe(hf.reshape(t_rows, D_MODEL), rw.reshape(t_rows, LANE), ri.reshape(t_rows, LANE), counts,
               x_lat.reshape(t_rows, D_MODEL), mods[1], vec(norm_ffn_post[1]), od_moe_w_gate[0].astype(BF16),
               od_moe_w_up[0].astype(BF16), od_moe_w_down[0].astype(BF16), nlat)
    return out.reshape(n_batch, n_lat, D_MODEL)
```

```python
import functools
import math

import jax
import jax.numpy as jnp
import numpy as np
from jax import lax
from jax.experimental import pallas as pl
from jax.experimental.pallas import tpu as pltpu

F32 = jnp.float32
BF16 = jnp.bfloat16

D_MODEL = 1024
EPS = 1e-6
GRID_W = 64

HY_WIDTH = 512
HY_ORDER = 2
HY_IN = (HY_ORDER + 1) * HY_WIDTH
HY_BANDS = 16
HY_EMB = 2 * HY_BANDS + 1
HY_DECAY_SLOW = -math.log(1e-2) / 1.5
HY_DECAY_FAST = -math.log(1e-2) / 0.3
HEAD_DIM = 64
N_Q_HEADS = 8
N_KV_HEADS = 2
Q_PER_KV = N_Q_HEADS // N_KV_HEADS
ATTN_WIDTH = N_Q_HEADS * HEAD_DIM
KV_WIDTH = N_KV_HEADS * HEAD_DIM
ROPE_THETA = 10000.0
LOG2_E = 1.0 / math.log(2.0)

S5_GROUP = 16
S5_GROUPS = D_MODEL // S5_GROUP
S5_STATE = 64
S5_LANES = S5_GROUPS * S5_STATE
S5_BLOCK_GROUPS = 8
N_EXPERTS = 8

ROW_TILE = 256
FFN_SUBTILES = 2
ATTN_Q_TILE = 512
FREQ_TILE = 512
DFT_SPLIT = 128
S5_CHUNK = 32
S5_SCAN_CHUNK = 64
MOE_TILE = 512
MOE_FF_TILE = 1792
LANE = 128
MIB = 1024 * 1024


def _cparams(sem, vmem_mib=48):
    return pltpu.CompilerParams(dimension_semantics=sem, vmem_limit_bytes=vmem_mib * MIB)


def _rms(x, gain):
    return x * lax.rsqrt(jnp.mean(x * x, axis=-1, keepdims=True) + EPS) * gain


def _bdot(a, b):
    return jnp.dot(a, b, preferred_element_type=F32)


def _full(shape):
    zeros = (0,) * len(shape)
    return pl.BlockSpec(shape, lambda *_: zeros)


def _resident(shape):
    zeros = (0,) * len(shape)
    return pl.BlockSpec(shape, lambda *_: zeros, pipeline_mode=pl.Buffered(1))


def _ada_kernel(cond_ref, w_ref, b_ref, o_ref):
    c = cond_ref[...]
    s = (c * jax.nn.sigmoid(c)).astype(BF16)
    o_ref[...] = _bdot(s, w_ref[...].astype(BF16)) + b_ref[...]


def _ada_params(cond, ada_w, ada_b):
    depth, _, n6 = ada_w.shape
    rows = cond.shape[0]
    tn = 1536
    out = pl.pallas_call(
        _ada_kernel,
        grid=(depth, n6 // tn),
        in_specs=[pl.BlockSpec((rows, D_MODEL), lambda i, j: (0, 0)),
                  pl.BlockSpec((None, D_MODEL, tn), lambda i, j: (i, 0, j)),
                  pl.BlockSpec((None, 1, tn), lambda i, j: (i, 0, j))],
        out_specs=pl.BlockSpec((None, rows, tn), lambda i, j: (i, 0, j)),
        out_shape=jax.ShapeDtypeStruct((depth, rows, n6), F32),
        compiler_params=_cparams(("arbitrary", "arbitrary")),
        name="ada_params",
    )(cond, ada_w, ada_b.reshape(depth, 1, n6))
    return out.reshape(depth, rows, 6, D_MODEL)


def _rope_rotate(t):
    w = t.shape[-1]
    lane = lax.broadcasted_iota(jnp.int32, t.shape, 1)
    first = (lane & (HEAD_DIM - 1)) < HEAD_DIM // 2
    return jnp.where(first, pltpu.roll(t, w - HEAD_DIM // 2, 1), pltpu.roll(t, HEAD_DIM // 2, 1))


def _head_slots(t):
    lane = lax.broadcasted_iota(jnp.int32, t.shape, 1)
    lo = jnp.where(lane < HEAD_DIM, t, 0.0)
    hi = jnp.where(lane >= HEAD_DIM, t, 0.0)
    return jnp.concatenate([lo, pltpu.roll(lo, HEAD_DIM, 1), pltpu.roll(hi, HEAD_DIM, 1), hi], axis=-1)


def _inproj_even_kernel(x_ref, c_ref, mod_ref, gain_ref, w_ref, qg_ref, kg_ref, e_ref, cos_ref, sin_ref,
                        hy_ref, q_ref, k4_ref, v4_ref, *, nlat):
    xin = jnp.where(pl.program_id(1) >= nlat, c_ref[...], x_ref[...])
    h = _rms(xin, gain_ref[...]) * (1.0 + mod_ref[1:2, :]) + mod_ref[0:1, :]
    p = _bdot(h.astype(BF16), w_ref[...])
    hy_ref[...] = p[:, :HY_IN].astype(hy_ref.dtype)
    q = p[:, HY_IN:HY_IN + ATTN_WIDTH]
    k = p[:, HY_IN + ATTN_WIDTH:HY_IN + ATTN_WIDTH + KV_WIDTH]
    v = p[:, HY_IN + ATTN_WIDTH + KV_WIDTH:]
    e = e_ref[...]
    qn = q * lax.rsqrt(_bdot((q * q).astype(BF16), e) + EPS) * qg_ref[...]
    kn = k * lax.rsqrt(_bdot((k * k).astype(BF16), e[:KV_WIDTH, :KV_WIDTH]) + EPS) * kg_ref[...]
    cos = cos_ref[...]
    sin = sin_ref[...]
    qr = (qn * cos + _rope_rotate(qn) * sin) * (HEAD_DIM ** -0.5 * LOG2_E)
    kr = kn * cos[:, :KV_WIDTH] + _rope_rotate(kn) * sin[:, :KV_WIDTH]
    q_ref[...] = qr.astype(q_ref.dtype)
    k4_ref[...] = _head_slots(kr).astype(k4_ref.dtype)
    v4_ref[...] = _head_slots(v).astype(v4_ref.dtype)


def _mod_spec(n_batch, nlat):
    return pl.BlockSpec((None, 6, D_MODEL), lambda b, j: (jnp.where(j >= nlat, n_batch, b), 0, 0))


def _row_spec(width):
    return pl.BlockSpec((None, ROW_TILE, width), lambda b, j: (b, j, 0))


def _lat_ctx_specs(nlat, nctx, width):
    return [pl.BlockSpec((None, ROW_TILE, width), lambda b, j: (b, jnp.minimum(j, nlat - 1), 0)),
            pl.BlockSpec((None, ROW_TILE, width), lambda b, j: (b, jnp.clip(j - nlat, 0, nctx - 1), 0))]


def _inproj_even(x, ctx, mod, gain, w_in, q_gain, k_gain, cos_t, sin_t, nlat):
    n_batch = x.shape[0]
    nctx = ctx.shape[1] // ROW_TILE
    ntile = nlat + nctx
    lt = ntile * ROW_TILE
    n_out = w_in.shape[1]
    head_avg = jnp.asarray(np.kron(np.eye(N_Q_HEADS), np.full((HEAD_DIM, HEAD_DIM), 1.0 / HEAD_DIM)), BF16)
    table = pl.BlockSpec((ROW_TILE, ATTN_WIDTH), lambda b, j: (j, 0))
    outs = [jax.ShapeDtypeStruct((n_batch, lt, HY_IN), BF16)] + [jax.ShapeDtypeStruct((n_batch, lt, ATTN_WIDTH), BF16)] * 3
    return pl.pallas_call(
        functools.partial(_inproj_even_kernel, nlat=nlat),
        grid=(n_batch, ntile),
        in_specs=_lat_ctx_specs(nlat, nctx, D_MODEL) + [
            _mod_spec(n_batch, nlat), _full((1, D_MODEL)), _full((D_MODEL, n_out)),
            _full((1, ATTN_WIDTH)), _full((1, KV_WIDTH)), _full((ATTN_WIDTH, ATTN_WIDTH)), table, table],
        out_specs=[_row_spec(HY_IN), _row_spec(ATTN_WIDTH), _row_spec(ATTN_WIDTH), _row_spec(ATTN_WIDTH)],
        out_shape=outs,
        compiler_params=_cparams(("parallel", "parallel")),
        name="inproj_even",
    )(x, ctx, mod, gain, w_in, q_gain, k_gain, head_avg, cos_t, sin_t)


def _attn_kernel(q_ref, k4_ref, v4_ref, o_ref):
    for blk in range(N_Q_HEADS // 2):
        g = (2 * blk) // Q_PER_KV
        qp = q_ref[:, LANE * blk:LANE * (blk + 1)]
        acc = None
        for r in range(2):
            slot = LANE * (2 * g + r)
            s = lax.dot_general(qp, k4_ref[:, slot:slot + LANE], (((1,), (1,)), ((), ())), preferred_element_type=F32)
            e = jnp.exp2(s - jnp.max(s, axis=-1, keepdims=True))
            inv = 1.0 / jnp.sum(e, axis=-1, keepdims=True)
            o = _bdot(e.astype(BF16), v4_ref[:, slot:slot + LANE]) * inv
            acc = o if acc is None else acc + o
        o_ref[:, LANE * blk:LANE * (blk + 1)] = acc.astype(o_ref.dtype)


def _attention(q, k4, v4, q_rows, q_tile, q_blk0, k_rows, k_blk):
    n_batch = q.shape[0]
    kv_spec = pl.BlockSpec((None, k_rows, ATTN_WIDTH), lambda b, j: (b, k_blk, 0))
    return pl.pallas_call(
        _attn_kernel,
        grid=(n_batch, q_rows // q_tile),
        in_specs=[pl.BlockSpec((None, q_tile, ATTN_WIDTH), lambda b, j: (b, q_blk0 + j, 0)), kv_spec, kv_spec],
        out_specs=pl.BlockSpec((None, q_tile, ATTN_WIDTH), lambda b, j: (b, j, 0)),
        out_shape=jax.ShapeDtypeStruct((n_batch, q_rows, ATTN_WIDTH), BF16),
        compiler_params=_cparams(("parallel", "parallel")),
        name="attention",
    )(q, k4, v4)


def _filter_kernel(z_ref, w1_ref, b1_ref, w2_ref, b2_ref, wo_ref, fr_ref, dl_ref, o_ref):
    hi = lax.Precision.HIGHEST
    z = z_ref[...]
    fr = fr_ref[...]
    h = jnp.sin(fr * (jnp.dot(z, w1_ref[...], precision=hi, preferred_element_type=F32) + b1_ref[...]))
    h = jnp.sin(fr * (jnp.dot(h, w2_ref[...], precision=hi, preferred_element_type=F32) + b2_ref[...]))
    h = jnp.dot(h, wo_ref[...], precision=hi, preferred_element_type=F32)
    decay = jnp.exp(-z[:, 0:1] * dl_ref[...])
    o_ref[...] = h * jnp.concatenate([decay] * (2 * HY_ORDER), axis=-1)


def _hyena_filters(seq, w1, b1, w2, b2, wout, freq):
    t = jnp.linspace(0.0, 1.0, seq, dtype=F32)[:, None]
    bands = jnp.linspace(1e-4, HY_BANDS - 1, HY_BANDS, dtype=F32)
    phase = (2.0 * math.pi / seq) * jnp.arange(seq, dtype=F32)[:, None] * bands
    z = jnp.concatenate([t, jnp.cos(phase), -jnp.sin(phase)], axis=-1)
    z = jnp.pad(z, ((0, 0), (0, LANE - HY_EMB)))
    w1p = jnp.pad(w1, ((0, LANE - HY_EMB), (0, 0)))
    deltas = jnp.linspace(HY_DECAY_SLOW, HY_DECAY_FAST, HY_WIDTH, dtype=F32)[None, :]
    hid = w1.shape[1]
    n_out = wout.shape[1]
    tl = ROW_TILE
    return pl.pallas_call(
        _filter_kernel,
        grid=(seq // tl,),
        in_specs=[pl.BlockSpec((tl, LANE), lambda i: (i, 0)), _full((LANE, hid)), _full((1, hid)), _full((hid, hid)),
                  _full((1, hid)), _full((hid, n_out)), _full((1, hid)), _full((1, HY_WIDTH))],
        out_specs=pl.BlockSpec((tl, n_out), lambda i: (i, 0)),
        out_shape=jax.ShapeDtypeStruct((seq, n_out), F32),
        compiler_params=_cparams(("parallel",)),
        name="hyena_filter",
    )(z, w1p, b1[None, :], w2, b2[None, :], wout, freq[None, :], deltas)


def _freq_tile(seq):
    return min(FREQ_TILE, seq)


def _dft_tables(seq):
    n = 2 * seq
    unit = 2.0 * math.pi / n
    col = jnp.arange(seq, dtype=jnp.int32)[None, :]
    r1 = jnp.arange(seq // DFT_SPLIT, dtype=jnp.int32)[:, None] * DFT_SPLIT
    r2 = jnp.arange(DFT_SPLIT, dtype=jnp.int32)[:, None]
    ang1 = ((r1 * col) % n).astype(F32) * unit
    ang2 = ((r2 * col) % n).astype(F32) * unit
    c1, s1 = jnp.cos(ang1)[:, None, :], jnp.sin(ang1)[:, None, :]
    c2, s2 = jnp.cos(ang2)[None, :, :], jnp.sin(ang2)[None, :, :]
    cos = (c1 * c2 - s1 * s2).reshape(seq, seq)
    sin = (s1 * c2 + c1 * s2).reshape(seq, seq)
    idx = jnp.arange(seq, dtype=jnp.int32)
    nyquist = jnp.where(idx % 2 == 0, 1.0, -1.0)
    sin_rows = jnp.where(idx[:, None] == 0, nyquist[None, :], sin)
    sin_cols = jnp.where(idx[None, :] == 0, nyquist[:, None], sin)
    tf = _freq_tile(seq)
    nf = seq // tf
    fwd = jnp.stack([cos.reshape(nf, tf, seq), sin_rows.reshape(nf, tf, seq)], axis=1).reshape(2 * seq, seq)
    inv = jnp.stack([cos.reshape(seq, nf, tf), sin_cols.reshape(seq, nf, tf)], axis=2).reshape(seq, 2 * seq)
    return fwd.astype(BF16), inv.astype(BF16)


def _kfreq_kernel(hf_ref, hb_ref, f_ref, kre_ref, ks_ref):
    fi = pl.program_id(1)
    seq = hf_ref.shape[0]
    hf = hf_ref[...]
    row = lax.broadcasted_iota(jnp.int32, hf.shape, 0)
    hb = jnp.where(row == 0, 0.0, hb_ref[...])
    f = f_ref[...]
    a1 = _bdot(f, (hf + hb).astype(BF16))
    a2 = _bdot(f, (hf - hb).astype(BF16))
    tf = f_ref.shape[0] // 2
    frow = lax.broadcasted_iota(jnp.int32, (tf, hf.shape[1]), 0)
    dc = jnp.logical_and(frow == 0, fi == 0)
    scale = jnp.where(dc, 1.0 / (2 * seq), 2.0 / (2 * seq))
    kre_ref[...] = a1[:tf] * scale
    ks_ref[...] = jnp.where(dc, a1[tf:], a2[tf:]) * scale


def _kfreq(hfilt, fwd_tab):
    seq = hfilt.shape[0]
    tf = _freq_tile(seq)
    nf = seq // tf
    out = jax.ShapeDtypeStruct((HY_ORDER, seq, HY_WIDTH), F32)
    ospec = pl.BlockSpec((None, tf, HY_WIDTH), lambda o, fi: (o, fi, 0))
    return pl.pallas_call(
        _kfreq_kernel,
        grid=(HY_ORDER, nf),
        in_specs=[pl.BlockSpec((seq, HY_WIDTH), lambda o, fi: (0, o)),
                  pl.BlockSpec((seq, HY_WIDTH), lambda o, fi: (0, HY_ORDER + o)),
                  pl.BlockSpec((2 * tf, seq), lambda o, fi: (fi, 0))],
        out_specs=[ospec, ospec],
        out_shape=[out, out],
        compiler_params=_cparams(("parallel", "parallel")),
        name="hyena_kfreq",
    )(hfilt, hfilt, fwd_tab)


def _short_conv(p, w, b):
    n = p.shape[0]
    row = lax.broadcasted_iota(jnp.int32, p.shape, 0)
    prev = jnp.where(row == 0, 0.0, pltpu.roll(p, 1, 0))
    nxt = jnp.where(row == n - 1, 0.0, pltpu.roll(p, n - 1, 0))
    return b + prev * w[0:1, :] + p * w[1:2, :] + nxt * w[2:3, :]


def _hyena_kernel(vsrc_ref, gsrc_ref, cw_ref, cb_ref, skip_ref, f_ref, ft_ref, kre_ref, ks_ref, o_ref,
                  v_scr, vb_scr, acc_scr, *, conv_v):
    fi = pl.program_id(1)
    tf = f_ref.shape[0] // 2

    @pl.when(fi == 0)
    def _():
        v = vsrc_ref[...].astype(F32)
        if conv_v:
            v = _short_conv(v, cw_ref[0], cb_ref[0])
        v_scr[...] = v
        vb_scr[...] = v.astype(BF16)
        acc_scr[...] = jnp.zeros_like(acc_scr)

    xf = _bdot(f_ref[...], vb_scr[...])
    xre, xs = xf[:tf], xf[tf:]
    kre, ks = kre_ref[...], ks_ref[...]
    row = lax.broadcasted_iota(jnp.int32, xre.shape, 0)
    dc = jnp.logical_and(row == 0, fi == 0)
    yre = jnp.where(dc, xre * kre, xre * kre - xs * ks)
    ys = jnp.where(dc, xs * ks, xre * ks + xs * kre)
    y = jnp.concatenate([yre, ys], axis=0).astype(BF16)
    acc_scr[...] += _bdot(ft_ref[...], y)

    @pl.when(fi == pl.num_programs(1) - 1)
    def _():
        gate = _short_conv(gsrc_ref[...].astype(F32), cw_ref[1], cb_ref[1])
        o_ref[...] = (gate * (acc_scr[...] + v_scr[...] * skip_ref[...])).astype(o_ref.dtype)


def _hyena_order(vsrc, v_blk, v_col, hy, row_blk, order, seq, conv_w, conv_b, skip, fwd_tab, inv_tab, kre, ks):
    n_batch = hy.shape[0]
    tf = _freq_tile(seq)
    nf = seq // tf
    conv_v = order == 0
    cw = jnp.stack([conv_w[:, :HY_WIDTH], conv_w[:, (order + 1) * HY_WIDTH:(order + 2) * HY_WIDTH]])
    cb = jnp.stack([conv_b[None, :HY_WIDTH], conv_b[None, (order + 1) * HY_WIDTH:(order + 2) * HY_WIDTH]])
    kspec = pl.BlockSpec((None, tf, HY_WIDTH), lambda b, fi: (order, fi, 0))
    return pl.pallas_call(
        functools.partial(_hyena_kernel, conv_v=conv_v),
        grid=(n_batch, nf),
        in_specs=[pl.BlockSpec((None, seq, HY_WIDTH), lambda b, fi: (b, v_blk, v_col)),
                  pl.BlockSpec((None, seq, HY_WIDTH), lambda b, fi: (b, row_blk, order + 1)),
                  _full((2, 3, HY_WIDTH)), _full((2, 1, HY_WIDTH)), _full((1, HY_WIDTH)),
                  pl.BlockSpec((2 * tf, seq), lambda b, fi: (fi, 0)),
                  pl.BlockSpec((seq, 2 * tf), lambda b, fi: (0, fi)),
                  kspec, kspec],
        out_specs=pl.BlockSpec((None, seq, HY_WIDTH), lambda b, fi: (b, 0, 0)),
        out_shape=jax.ShapeDtypeStruct((n_batch, seq, HY_WIDTH), BF16),
        scratch_shapes=[pltpu.VMEM((seq, HY_WIDTH), F32), pltpu.VMEM((seq, HY_WIDTH), BF16),
                        pltpu.VMEM((seq, HY_WIDTH), F32)],
        compiler_params=_cparams(("parallel", "arbitrary")),
        name="hyena_order%d" % order,
    )(vsrc, hy, cw, cb, skip[order][None, :], fwd_tab, inv_tab, kre, ks)


def _hyena_mixer(hy, row_blk, seq, filt_args, conv_w, conv_b, skip):
    hfilt = _hyena_filters(seq, *filt_args)
    fwd_tab, inv_tab = _dft_tables(seq)
    kre, ks = _kfreq(hfilt, fwd_tab)
    v1 = _hyena_order(hy, row_blk, 0, hy, row_blk, 0, seq, conv_w, conv_b, skip, fwd_tab, inv_tab, kre, ks)
    return _hyena_order(v1, 0, 0, hy, row_blk, 1, seq, conv_w, conv_b, skip, fwd_tab, inv_tab, kre, ks)


def _outproj_kernel(yl_ref, yc_ref, al_ref, ac_ref, x_ref, c_ref, mod_ref, gpost_ref, w_ref, o_ref, *, nlat):
    is_ctx = pl.program_id(1) >= nlat
    yh = jnp.where(is_ctx, yc_ref[...], yl_ref[...])
    ya = jnp.where(is_ctx, ac_ref[...], al_ref[...])
    out = _bdot(yh, w_ref[:HY_WIDTH, :]) + _bdot(ya, w_ref[HY_WIDTH:, :])
    o_ref[...] = jnp.where(is_ctx, c_ref[...], x_ref[...]) + mod_ref[2:3, :] * _rms(out, gpost_ref[...])


def _outproj(y_lat, y_ctx, a_lat, a_ctx, x, ctx, mod, gpost, w_out, nlat):
    n_batch = x.shape[0]
    nctx = ctx.shape[1] // ROW_TILE
    ntile = nlat + nctx
    return pl.pallas_call(
        functools.partial(_outproj_kernel, nlat=nlat),
        grid=(n_batch, ntile),
        in_specs=(_lat_ctx_specs(nlat, nctx, HY_WIDTH) + _lat_ctx_specs(nlat, nctx, ATTN_WIDTH)
                  + _lat_ctx_specs(nlat, nctx, D_MODEL)
                  + [_mod_spec(n_batch, nlat), _full((1, D_MODEL)), _full(w_out.shape)]),
        out_specs=_row_spec(D_MODEL),
        out_shape=jax.ShapeDtypeStruct((n_batch, ntile * ROW_TILE, D_MODEL), F32),
        compiler_params=_cparams(("parallel", "parallel")),
        name="outproj_even",
    )(y_lat, y_ctx, a_lat, a_ctx, x, ctx, mod, gpost, w_out)


def _ffn_kernel(x_ref, mod_ref, gpre_ref, gpost_ref, wg_ref, wu_ref, wd_ref, o_ref, *, n_batch, tiles_per_batch, nlat):
    n_sub = x_ref.shape[0] // ROW_TILE
    subs = []
    for s in range(n_sub):
        tile = pl.program_id(0) * n_sub + s
        batch = lax.div(tile, tiles_per_batch)
        mod = mod_ref[jnp.where(lax.rem(tile, tiles_per_batch) >= nlat, n_batch, batch)]
        x = x_ref[s * ROW_TILE:(s + 1) * ROW_TILE, :]
        subs.append((x, mod, (_rms(x, gpre_ref[...]) * (1.0 + mod[4:5, :]) + mod[3:4, :]).astype(BF16)))
    h = jnp.concatenate([sub[2] for sub in subs], axis=0)
    g = _bdot(h, wg_ref[...])
    u = _bdot(h, wu_ref[...])
    a = (g * jax.nn.sigmoid(g) * u).astype(BF16)
    y = _bdot(a, wd_ref[...])
    for s, (x, mod, _) in enumerate(subs):
        rows = slice(s * ROW_TILE, (s + 1) * ROW_TILE)
        o_ref[rows, :] = x + mod[5:6, :] * _rms(y[rows, :], gpost_ref[...])


def _ffn_dense(xs, mod, gpre, gpost, wg, wu, wd, nlat):
    n_batch, lt, _ = xs.shape
    tiles_per_batch = lt // ROW_TILE
    rows = FFN_SUBTILES * ROW_TILE
    n_steps = n_batch * tiles_per_batch // FFN_SUBTILES
    blk = pl.BlockSpec((rows, D_MODEL), lambda i: (i, 0))
    out = pl.pallas_call(
        functools.partial(_ffn_kernel, n_batch=n_batch, tiles_per_batch=tiles_per_batch, nlat=nlat),
        grid=(n_steps,),
        in_specs=[blk, _resident(mod.shape), _resident((1, D_MODEL)), _resident((1, D_MODEL)),
                  _resident(wg.shape), _resident(wu.shape), _resident(wd.shape)],
        out_specs=blk,
        out_shape=jax.ShapeDtypeStruct((n_batch * lt, D_MODEL), F32),
        compiler_params=_cparams(("parallel",), 56),
        name="ffn_dense",
    )(xs.reshape(n_batch * lt, D_MODEL), mod, gpre, gpost, wg, wu, wd)
    return out.reshape(xs.shape)


def _mod_rows(mod_ref, k, n_batch, is_ctx):
    return jnp.where(is_ctx, mod_ref[n_batch:n_batch + 1, k:k + 1, :], mod_ref[0:n_batch, k:k + 1, :])


def _row_order_swap(n_outer, n_inner):
    n = n_outer * n_inner
    dst = np.arange(n)
    src = (dst % n_outer) * n_inner + dst // n_outer
    return jnp.asarray(src[:, None] == np.arange(n)[None, :], BF16)


def _inproj_odd_kernel(x_ref, mod_ref, gain_ref, swap_ref, w_ref, u_ref, *, n_lat_steps):
    n_batch, steps, _ = x_ref.shape
    is_ctx = pl.program_id(0) >= n_lat_steps
    h = (_rms(x_ref[...], gain_ref[...]) * (1.0 + _mod_rows(mod_ref, 1, n_batch, is_ctx))
         + _mod_rows(mod_ref, 0, n_batch, is_ctx))
    h = h.reshape(n_batch * steps, D_MODEL).astype(BF16)
    h = _bdot(swap_ref[...], h).astype(BF16)
    u_ref[...] = _bdot(h, w_ref[...]).reshape(steps, n_batch, D_MODEL)


def _inproj_odd(xs, mod, gain, w_in, n_lat):
    n_batch, lt, _ = xs.shape
    p = S5_CHUNK
    swap = _row_order_swap(n_batch, p)
    return pl.pallas_call(
        functools.partial(_inproj_odd_kernel, n_lat_steps=n_lat // p),
        grid=(lt // p,),
        in_specs=[pl.BlockSpec((n_batch, p, D_MODEL), lambda i: (0, i, 0)), _full(mod.shape), _full((1, D_MODEL)),
                  _full(swap.shape), _full(w_in.shape)],
        out_specs=pl.BlockSpec((p, n_batch, D_MODEL), lambda i: (i, 0, 0)),
        out_shape=jax.ShapeDtypeStruct((lt, n_batch, D_MODEL), F32),
        compiler_params=_cparams(("parallel",)),
        name="inproj_odd",
    )(xs, mod, gain, swap, w_in)


def _s5_param_kernel(lr_ref, li_ref, ls_ref, lrx_ref, lix_ref, lsx_ref, br_ref, bi_ref,
                     abr_ref, abi_ref, bbr_ref, bbi_ref):
    def zoh(lr_raw, li, log_step):
        lr = jnp.minimum(lr_raw, -1e-4)
        dt = jnp.exp(log_step)
        mag = jnp.exp(lr * dt)
        ab_re = mag * jnp.cos(li * dt)
        ab_im = mag * jnp.sin(li * dt)
        den = lr * lr + li * li
        nr, ni = ab_re - 1.0, ab_im
        return ab_re, ab_im, (nr * lr + ni * li) / den, (ni * lr - nr * li) / den

    ab_re, ab_im, _, _ = zoh(lr_ref[...], li_ref[...], ls_ref[...])
    abr_ref[...] = ab_re
    abi_ref[...] = ab_im
    _, _, co_re, co_im = zoh(lrx_ref[...], lix_ref[...], lsx_ref[...])
    br, bi = br_ref[...], bi_ref[...]
    bbr_ref[...] = co_re * br - co_im * bi
    bbi_ref[...] = co_re * bi + co_im * br


def _s5_params(lam_re, lam_im, log_step, b_re, b_im, c_re, c_im):
    nd, g, n = lam_re.shape
    k = S5_GROUP
    rep = lambda a: jnp.repeat(a, k, axis=1)
    ls = log_step[:, :, None]
    bt_re = jnp.swapaxes(b_re, 2, 3).reshape(nd, g * k, n)
    bt_im = jnp.swapaxes(b_im, 2, 3).reshape(nd, g * k, n)
    small = jax.ShapeDtypeStruct((nd, g, n), F32)
    big = jax.ShapeDtypeStruct((nd, g * k, n), F32)
    ab_re, ab_im, bb_re, bb_im = pl.pallas_call(
        _s5_param_kernel, out_shape=[small, small, big, big], name="s5_discretise",
    )(lam_re, lam_im, ls, rep(lam_re), rep(lam_im), rep(ls), bt_re, bt_im)
    bb_re = bb_re.reshape(nd, g, k, n)
    bb_im = bb_im.reshape(nd, g, k, n)
    cmul = lambda xr, xi, yr, yi: (xr * yr - xi * yi, xr * yi + xi * yr)
    a_re, a_im = ab_re[:, :, None, :], ab_im[:, :, None, :]
    a2_re, a2_im = cmul(a_re, a_im, a_re, a_im)
    abb_re, abb_im = cmul(a_re, a_im, bb_re, bb_im)
    ca_re, ca_im = cmul(c_re, c_im, a_re, a_im)
    ca2_re, ca2_im = cmul(c_re, c_im, a2_re, a2_im)
    real_cb = lambda xr, xi: jnp.einsum('dgin,dgkn->dgik', xr, bb_re) - jnp.einsum('dgin,dgkn->dgik', xi, bb_im)
    k0 = real_cb(c_re, c_im)
    k1 = real_cb(ca_re, ca_im)
    a2 = jnp.stack([a2_re.reshape(nd, g * n), a2_im.reshape(nd, g * n)], axis=1)
    nq = g // S5_BLOCK_GROUPS

    def block_diag(parts):
        m = jnp.stack(parts)
        r, c = m.shape[-2:]
        mask = np.kron(np.eye(S5_BLOCK_GROUPS), np.ones((r, c)))
        rows = m.reshape(len(parts), nd, nq, S5_BLOCK_GROUPS * r, c)
        return jnp.tile(rows, (1, 1, 1, 1, S5_BLOCK_GROUPS)) * jnp.asarray(mask, F32)

    t = lambda x: jnp.swapaxes(x, 2, 3)
    i2s = block_diag([abb_re, abb_im, bb_re, bb_im])
    w_in = jnp.concatenate([jnp.concatenate([i2s[0], i2s[1]], axis=-1),
                            jnp.concatenate([i2s[2], i2s[3]], axis=-1)], axis=-2).astype(BF16)
    s2o = block_diag([t(ca_re), t(ca2_re), t(-ca_im), t(-ca2_im)])
    w_state = jnp.stack([jnp.concatenate([s2o[0], s2o[1]], axis=-1),
                         jnp.concatenate([s2o[2], s2o[3]], axis=-1)], axis=2).astype(BF16)
    i2o = block_diag([t(k0), t(k1)])
    w_dir = jnp.concatenate([jnp.concatenate([i2o[0], i2o[1]], axis=-1),
                             jnp.concatenate([jnp.zeros_like(i2o[0]), i2o[0]], axis=-1)], axis=-2).astype(BF16)
    return a2, w_in, w_state, w_dir


def _s5_scan_kernel(u_ref, a2_ref, win_ref, wst_ref, wdir_ref, y_ref, sbuf, state, *, nctx_chunks):
    d = pl.program_id(0)
    i = pl.program_id(1)
    p_steps, n_batch, _ = u_ref.shape
    npair = p_steps // 2
    rows = npair * n_batch
    half = S5_LANES
    nq = win_ref.shape[0]
    kq = win_ref.shape[1] // 2
    sq = win_ref.shape[2] // 2

    @pl.when(i == 0)
    def _():
        state[...] = jnp.zeros_like(state)

    fwd = d == 0
    u = u_ref[...].reshape(npair, 2, n_batch, D_MODEL)
    u_even = u[:, 0].reshape(rows, D_MODEL).astype(BF16)
    u_odd = u[:, 1].reshape(rows, D_MODEL).astype(BF16)
    u_1 = jnp.where(fwd, u_even, u_odd)
    u_2 = jnp.where(fwd, u_odd, u_even)
    pair_in = lambda q: jnp.concatenate([u_1[:, kq * q:kq * (q + 1)], u_2[:, kq * q:kq * (q + 1)]], axis=-1)
    for q in range(nq):
        r = _bdot(pair_in(q), win_ref[q])
        sbuf[:, :, sq * q:sq * (q + 1)] = r[:, :sq].reshape(npair, n_batch, sq)
        sbuf[:, :, half + sq * q:half + sq * (q + 1)] = r[:, sq:].reshape(npair, n_batch, sq)

    for q in range(nq):
        lo = sq * q
        ar = jnp.broadcast_to(a2_ref[0:1, lo:lo + sq], (n_batch, sq))
        ai = jnp.broadcast_to(a2_ref[1:2, lo:lo + sq], (n_batch, sq))

        def body(t, carry, lo=lo, ar=ar, ai=ai):
            sr, si = carry
            tt = jnp.where(d == 0, t, npair - 1 - t)
            nr = ar * sr - ai * si + sbuf[tt, :, lo:lo + sq]
            ni = ar * si + ai * sr + sbuf[tt, :, half + lo:half + lo + sq]
            sbuf[tt, :, lo:lo + sq] = sr
            sbuf[tt, :, half + lo:half + lo + sq] = si
            return nr, ni

        sr, si = lax.fori_loop(0, npair, body, (state[:, lo:lo + sq], state[:, half + lo:half + lo + sq]), unroll=4)
        state[:, lo:lo + sq] = sr
        state[:, half + lo:half + lo + sq] = si

    @pl.when(i >= nctx_chunks)
    def _():
        s = sbuf[...].reshape(rows, 2 * half).astype(BF16)
        for q in range(nq):
            yq = (_bdot(s[:, sq * q:sq * (q + 1)], wst_ref[q, 0])
                  + _bdot(s[:, half + sq * q:half + sq * (q + 1)], wst_ref[q, 1])
                  + _bdot(pair_in(q), wdir_ref[q]))
            y_1 = yq[:, :kq].reshape(npair, 1, n_batch, kq)
            y_2 = yq[:, kq:].reshape(npair, 1, n_batch, kq)
            y_pair = jnp.concatenate([jnp.where(fwd, y_1, y_2), jnp.where(fwd, y_2, y_1)], axis=1)
            y_ref[:, :, kq * q:kq * (q + 1)] = y_pair.reshape(p_steps, n_batch, kq)


def _s5_scan(u3, a2, w_in, w_state, w_dir, n_lat):
    lt, n_batch, _ = u3.shape
    p = S5_SCAN_CHUNK
    nchunk = lt // p
    nlatc = n_lat // p
    nctxc = nchunk - nlatc

    def u_map(d, i):
        return (jnp.where(d == 0, lax.rem(i + nlatc, nchunk), nchunk - 1 - i), 0, 0)

    def y_map(d, i):
        return (d, jnp.where(d == 0, jnp.maximum(i - nctxc, 0), jnp.minimum(nchunk - 1 - i, nlatc - 1)), 0, 0)

    per_dir = lambda w: pl.BlockSpec((None,) + w.shape[1:], lambda d, i: (d,) + (0,) * (w.ndim - 1))
    return pl.pallas_call(
        functools.partial(_s5_scan_kernel, nctx_chunks=nctxc),
        grid=(2, nchunk),
        in_specs=[pl.BlockSpec((p, n_batch, D_MODEL), u_map), per_dir(a2), per_dir(w_in), per_dir(w_state),
                  per_dir(w_dir)],
        out_specs=pl.BlockSpec((None, p, n_batch, D_MODEL), y_map),
        out_shape=jax.ShapeDtypeStruct((2, n_lat, n_batch, D_MODEL), F32),
        scratch_shapes=[pltpu.VMEM((p // 2, n_batch, 2 * S5_LANES), F32), pltpu.VMEM((n_batch, 2 * S5_LANES), F32)],
        compiler_params=_cparams(("arbitrary", "arbitrary")),
        name="s5_scan",
    )(u3, a2, w_in, w_state, w_dir)


def _glu_kernel(yf_ref, yb_ref, u_ref, dskip_ref, wa_ref, wb_ref, x_ref, mod_ref, gpost_ref, gpre_ref, router_ref,
                swap_ref, tri_ref, xo_ref, hf_ref, rw_ref, ri_ref, cnt_ref, carry):
    n_batch, steps, _ = x_ref.shape
    rows = n_batch * steps

    @pl.when(pl.program_id(0) == 0)
    def _():
        carry[...] = jnp.zeros_like(carry)

    y = (yf_ref[...] + yb_ref[...] + dskip_ref[...] * u_ref[...]).reshape(rows, D_MODEL)
    z = jax.nn.gelu(y).astype(BF16)
    z = _bdot(swap_ref[...], z).astype(BF16)
    out = _bdot(z, wa_ref[...]) * jax.nn.sigmoid(_bdot(z, wb_ref[...]))
    out = out.reshape(n_batch, steps, D_MODEL)
    mod = lambda k: mod_ref[0:n_batch, k:k + 1, :]
    xn = x_ref[...] + mod(2) * _rms(out, gpost_ref[...])
    xo_ref[...] = xn
    hf = _rms(xn, gpre_ref[...]) * (1.0 + mod(4)) + mod(3)
    hf_ref[...] = hf
    hf = hf.reshape(rows, D_MODEL)
    h_hi = hf.astype(BF16)
    h_lo = (hf - h_hi.astype(F32)).astype(BF16)
    part = _bdot(h_hi, router_ref[...])
    logits = part[:, :LANE] + part[:, LANE:] + _bdot(h_lo, router_ref[:, :LANE])
    lane = lax.broadcasted_iota(jnp.int32, logits.shape, 1)
    neg = jnp.float32(-jnp.inf)
    lg = jnp.where(lane < N_EXPERTS, logits, neg)
    m1 = jnp.max(lg, axis=-1, keepdims=True)
    i1 = jnp.min(jnp.where(lg == m1, lane, LANE), axis=-1, keepdims=True)
    lg2 = jnp.where(lane == i1, neg, lg)
    m2 = jnp.max(lg2, axis=-1, keepdims=True)
    i2 = jnp.min(jnp.where(lg2 == m2, lane, LANE), axis=-1, keepdims=True)
    e2 = jnp.exp(m2 - m1)
    w1 = 1.0 / (1.0 + e2)
    rw_ref[...] = jnp.where(lane == 0, w1, jnp.where(lane == 1, e2 * w1, 0.0)).reshape(n_batch, steps, LANE)
    member = jnp.where(lane == i1, 1.0, jnp.where(lane == i2, 1.0, 0.0))
    base = carry[...] + _bdot(tri_ref[...], member.astype(BF16))
    r1 = jnp.sum(jnp.where(lane == i1, base, 0.0), axis=-1, keepdims=True).astype(jnp.int32)
    r2 = jnp.sum(jnp.where(lane == i2, base, 0.0), axis=-1, keepdims=True).astype(jnp.int32)
    ri = jnp.where(lane == 0, i1, jnp.where(lane == 1, i2, jnp.where(lane == 2, r1, jnp.where(lane == 3, r2, 0))))
    ri_ref[...] = ri.reshape(n_batch, steps, LANE)
    carry[...] += jnp.sum(member, axis=0, keepdims=True)
    cnt_ref[...] = carry[...]


def _glu(y, u3, d_skip, w_a, w_b, xs, mod, gpost, gpre, router, n_lat):
    n_batch = xs.shape[0]
    p = S5_CHUNK
    rows = p * n_batch
    router_p = jnp.pad(router, ((0, 0), (0, LANE - router.shape[1])))
    router_hi = router_p.astype(BF16)
    router_cat = jnp.concatenate([router_hi, (router_p - router_hi.astype(F32)).astype(BF16)], axis=1)
    tri = jnp.asarray(np.arange(rows)[:, None] > np.arange(rows)[None, :], BF16)
    swap = _row_order_swap(p, n_batch)
    bt_spec = lambda w: pl.BlockSpec((n_batch, p, w), lambda i: (0, i, 0))
    return pl.pallas_call(
        _glu_kernel,
        grid=(n_lat // p,),
        in_specs=[pl.BlockSpec((None, p, n_batch, D_MODEL), lambda i: (0, i, 0, 0)),
                  pl.BlockSpec((None, p, n_batch, D_MODEL), lambda i: (1, i, 0, 0)),
                  pl.BlockSpec((p, n_batch, D_MODEL), lambda i: (i, 0, 0)),
                  _full((1, D_MODEL)), _full(w_a.shape), _full(w_b.shape), bt_spec(D_MODEL),
                  _full(mod.shape), _full((1, D_MODEL)), _full((1, D_MODEL)),
                  _full(router_cat.shape), _full(swap.shape), _full(tri.shape)],
        out_specs=[bt_spec(D_MODEL), bt_spec(D_MODEL), bt_spec(LANE), bt_spec(LANE), _full((1, LANE))],
        out_shape=[jax.ShapeDtypeStruct((n_batch, n_lat, D_MODEL), F32),
                   jax.ShapeDtypeStruct((n_batch, n_lat, D_MODEL), F32),
                   jax.ShapeDtypeStruct((n_batch, n_lat, LANE), F32),
                   jax.ShapeDtypeStruct((n_batch, n_lat, LANE), jnp.int32),
                   jax.ShapeDtypeStruct((1, LANE), F32)],
        scratch_shapes=[pltpu.VMEM((1, LANE), F32)],
        compiler_params=_cparams(("arbitrary",)),
        name="s5_glu_router",
    )(y, y, u3, d_skip, w_a, w_b, xs, mod, gpost, gpre, router_cat, swap, tri)


def _moe_plan(ri, counts, n_tiles):
    experts = jnp.arange(N_EXPERTS, dtype=jnp.int32)
    n_of = (counts[0, :N_EXPERTS].astype(jnp.int32) + MOE_TILE - 1) // MOE_TILE
    ends = jnp.cumsum(n_of)
    starts = ends - n_of
    start_of = jnp.sum(jnp.where(ri[:, 0:2, None] == experts, starts, 0), axis=-1)
    n_tok = ri.shape[0]
    pos = (start_of * MOE_TILE + ri[:, 2:4]).T.reshape(-1)
    n_used = ends[-1]
    tile = jnp.arange(n_tiles, dtype=jnp.int32)
    tile_expert = jnp.sum((jnp.minimum(tile, n_used - 1)[:, None] >= ends[None, :]).astype(jnp.int32), axis=1)
    n_pairs = pos.shape[0]
    n_rows = n_tiles * MOE_TILE
    pair_of = jnp.full((n_rows,), -1, jnp.int32).at[pos].set(jnp.arange(n_pairs, dtype=jnp.int32),
                                                             unique_indices=True, mode='promise_in_bounds')
    is_pad = pair_of < 0
    pair_of = jnp.where(is_pad, n_pairs - 1 + jnp.cumsum(is_pad.astype(jnp.int32)), pair_of)
    src_token = jnp.where(is_pad, 0, pair_of % n_tok)
    return src_token, pair_of, tile_expert, n_used.reshape(1)


def _moe_expert_kernel(te_ref, nu_ref, src0_ref, src_ref, dst_ref, hf_ref, wg_ref, wu_ref, wd_ref, out_ref,
                       xbuf, obuf, acc, gsem, ssem, *, rows_per_step):
    del te_ref
    i = pl.program_id(0)
    j = pl.program_id(1)
    last_j = pl.num_programs(1) - 1
    nu = nu_ref[0]
    cur = lax.rem(i, 2)
    nxt = 1 - cur
    row0 = j * rows_per_step

    def gather_row(idx_ref, row, slot):
        return pltpu.make_async_copy(hf_ref.at[pl.ds(idx_ref[0, row], 1)], xbuf.at[slot, pl.ds(row, 1)], gsem.at[slot])

    def scatter_row(row, slot):
        return pltpu.make_async_copy(obuf.at[slot, pl.ds(row, 1)], out_ref.at[pl.ds(dst_ref[0, row], 1)], ssem.at[slot])

    def whole_tile_wait(sem, slot):
        pltpu.make_async_copy(xbuf.at[slot], obuf.at[slot], sem.at[slot]).wait()

    @pl.when(jnp.logical_and(i == 0, j == 0))
    def _():
        obuf[...] = jnp.zeros_like(obuf)

        def first(r, c):
            gather_row(src0_ref, r, 0).start()
            return c

        lax.fori_loop(0, MOE_TILE, first, 0)

    @pl.when(jnp.logical_and(i <= nu, j == 0))
    def _():
        whole_tile_wait(gsem, cur)

    for step in range(MOE_TILE // rows_per_step):
        @pl.when(jnp.logical_and(i < nu, j == step))
        def _(step=step):
            for row in range(step * rows_per_step, (step + 1) * rows_per_step):
                gather_row(src_ref, row, nxt).start(priority=1)
                scatter_row(row, nxt).start(priority=row % 2)

    @pl.when(i < nu)
    def _():
        h = xbuf[cur].astype(BF16)
        g = _bdot(h, wg_ref[...])
        u = _bdot(h, wu_ref[...])
        part = _bdot((g * jax.nn.sigmoid(g) * u).astype(BF16), wd_ref[...])
        acc[...] = jnp.where(j == 0, part, acc[...] + part)

    @pl.when(i >= nu)
    def _():
        def tail(c, carry):
            scatter_row(row0 + c, nxt).start()
            return carry

        lax.fori_loop(0, rows_per_step, tail, 0)

    @pl.when(jnp.logical_and(i >= 1, j == last_j))
    def _():
        whole_tile_wait(ssem, cur)

    @pl.when(jnp.logical_and(i < nu, j == last_j))
    def _():
        obuf[cur] = acc[...]

    @pl.when(jnp.logical_and(i == pl.num_programs(0) - 1, j == last_j))
    def _():
        whole_tile_wait(ssem, nxt)


def _combine_kernel(ya_ref, yb_ref, rw_ref, x_ref, mod_ref, gpost_ref, o_ref):
    rw = rw_ref[...]
    y = rw[:, 0:1] * ya_ref[...] + rw[:, 1:2] * yb_ref[...]
    o_ref[...] = x_ref[...] + mod_ref[5:6, :] * _rms(y, gpost_ref[...])


def _moe(hf, rw, ri, counts, xs, mod, gpost, wg, wu, wd, nlat):
    t_rows = hf.shape[0]
    d_ff = wg.shape[2]
    n_tiles = 2 * t_rows // MOE_TILE + N_EXPERTS
    nff = d_ff // MOE_FF_TILE
    n_row_tiles = t_rows // ROW_TILE
    n_rows = n_tiles * MOE_TILE
    src_token, pair_of, tile_expert, n_used = _moe_plan(ri, counts, n_tiles)
    spare = jnp.arange(MOE_TILE, dtype=jnp.int32)
    src_tiles = jnp.concatenate([src_token, 0 * spare]).reshape(n_tiles + 1, 1, MOE_TILE)
    dst_tiles = jnp.concatenate([n_rows + spare, pair_of]).reshape(n_tiles + 1, 1, MOE_TILE)
    step_expert = jnp.concatenate([tile_expert, tile_expert[-1:]])
    any_spec = pl.BlockSpec(memory_space=pl.ANY)
    idx_block = (None, 1, MOE_TILE)

    def ff_blk(i, j, nu):
        return jnp.where(i < nu[0], j, nff - 1)

    y_pairs = pl.pallas_call(
        functools.partial(_moe_expert_kernel, rows_per_step=MOE_TILE // nff),
        grid_spec=pltpu.PrefetchScalarGridSpec(
            num_scalar_prefetch=2,
            grid=(n_tiles + 1, nff),
            in_specs=[pl.BlockSpec(idx_block, lambda i, j, te, nu: (0, 0, 0), memory_space=pltpu.SMEM),
                      pl.BlockSpec(idx_block, lambda i, j, te, nu: (jnp.minimum(i + 1, n_tiles), 0, 0),
                                   memory_space=pltpu.SMEM),
                      pl.BlockSpec(idx_block, lambda i, j, te, nu: (i, 0, 0), memory_space=pltpu.SMEM),
                      any_spec,
                      pl.BlockSpec((None, D_MODEL, MOE_FF_TILE), lambda i, j, te, nu: (te[i], 0, ff_blk(i, j, nu))),
                      pl.BlockSpec((None, D_MODEL, MOE_FF_TILE), lambda i, j, te, nu: (te[i], 0, ff_blk(i, j, nu))),
                      pl.BlockSpec((None, MOE_FF_TILE, D_MODEL), lambda i, j, te, nu: (te[i], ff_blk(i, j, nu), 0))],
            out_specs=any_spec,
            scratch_shapes=[pltpu.VMEM((2, MOE_TILE, D_MODEL), F32), pltpu.VMEM((2, MOE_TILE, D_MODEL), F32),
                            pltpu.VMEM((MOE_TILE, D_MODEL), F32),
                            pltpu.SemaphoreType.DMA((2,)), pltpu.SemaphoreType.DMA((2,))]),
        out_shape=jax.ShapeDtypeStruct((n_rows + MOE_TILE, D_MODEL), F32),
        compiler_params=_cparams(("arbitrary", "arbitrary")),
        name="moe_experts",
    )(step_expert, n_used, src_tiles, src_tiles, dst_tiles, hf, wg, wu, wd)

    row = lambda w: pl.BlockSpec((ROW_TILE, w), lambda i: (i, 0))
    return pl.pallas_call(
        _combine_kernel,
        grid=(n_row_tiles,),
        in_specs=[row(D_MODEL), pl.BlockSpec((ROW_TILE, D_MODEL), lambda i: (n_row_tiles + i, 0)), row(LANE), row(D_MODEL),
                  pl.BlockSpec((None, 6, D_MODEL), lambda i: (lax.div(i, nlat), 0, 0)), _full((1, D_MODEL))],
        out_specs=row(D_MODEL),
        out_shape=jax.ShapeDtypeStruct((t_rows, D_MODEL), F32),
        compiler_params=_cparams(("parallel",)),
        name="moe_combine",
    )(y_pairs, y_pairs, rw, xs, mod, gpost)


def _rope_tables(n_lat, n_ctx):
    rows = n_lat // GRID_W
    row = jnp.repeat(jnp.arange(rows, dtype=F32), GRID_W)
    col = jnp.tile(jnp.arange(GRID_W, dtype=F32), rows)
    n_freq = HEAD_DIM // 4
    inv = ROPE_THETA ** (-jnp.arange(n_freq, dtype=F32) / n_freq)
    ang = jnp.concatenate([row[:, None] * inv, col[:, None] * inv], axis=-1)
    cos, sin = jnp.cos(ang), jnp.sin(ang)
    cos_h = jnp.concatenate([cos, cos], axis=-1)
    sin_h = jnp.concatenate([-sin, sin], axis=-1)
    cos_t = jnp.concatenate([jnp.tile(cos_h, (1, N_Q_HEADS)), jnp.ones((n_ctx, ATTN_WIDTH), F32)], axis=0)
    sin_t = jnp.concatenate([jnp.tile(sin_h, (1, N_Q_HEADS)), jnp.zeros((n_ctx, ATTN_WIDTH), F32)], axis=0)
    return cos_t, sin_t


def kernel(x, c, ctx, c_ctx, ada_w, ada_b, norm_mix_pre, norm_mix_post, norm_ffn_pre, norm_ffn_post, ev_w_in, ev_hy_conv_w, ev_hy_conv_b, ev_hy_f_w1, ev_hy_f_b1, ev_hy_f_w2, ev_hy_f_b2, ev_hy_f_wout, ev_hy_freq, ev_hy_skip, ev_q_norm, ev_k_norm, ev_w_out, ev_ffn_w_gate, ev_ffn_w_up, ev_ffn_w_down, od_w_in, od_s5_lambda_re, od_s5_lambda_im, od_s5_log_step, od_s5_b_re, od_s5_b_im, od_s5_c_re, od_s5_c_im, od_s5_d, od_glu_w_a, od_glu_w_b, od_router, od_moe_w_gate, od_moe_w_up, od_moe_w_down):
    n_batch, n_lat, _ = x.shape
    n_ctx = ctx.shape[1]
    depth = ada_w.shape[0]
    assert n_batch == 8 and n_lat % ROW_TILE == 0 and n_ctx % ROW_TILE == 0 and n_lat % n_ctx == 0
    assert depth == 2
    nlat = n_lat // ROW_TILE

    cond = jnp.concatenate([c, c_ctx[None, :], jnp.zeros((16 - n_batch - 1, D_MODEL), F32)], axis=0)
    mods = _ada_params(cond, ada_w, ada_b)
    vec = lambda a: a[None, :]

    cos_t, sin_t = _rope_tables(n_lat, n_ctx)
    hy, q, k4, v4 = _inproj_even(x, ctx, mods[0], vec(norm_mix_pre[0]), ev_w_in[0].astype(BF16),
                                 vec(jnp.tile(ev_q_norm[0], N_Q_HEADS)), vec(jnp.tile(ev_k_norm[0], N_KV_HEADS)),
                                 cos_t, sin_t, nlat)
    lt = n_lat + n_ctx
    a_lat = _attention(q, k4, v4, n_lat, min(ATTN_Q_TILE, n_lat), 0, lt, 0)
    a_ctx = _attention(q, k4, v4, n_ctx, n_ctx, n_lat // n_ctx, n_ctx, n_lat // n_ctx)
    filt_args = (ev_hy_f_w1[0], ev_hy_f_b1[0], ev_hy_f_w2[0], ev_hy_f_b2[0], ev_hy_f_wout[0], ev_hy_freq[0])
    y_hy_lat = _hyena_mixer(hy, 0, n_lat, filt_args, ev_hy_conv_w[0], ev_hy_conv_b[0], ev_hy_skip[0])
    y_hy_ctx = _hyena_mixer(hy, n_lat // n_ctx, n_ctx, filt_args, ev_hy_conv_w[0], ev_hy_conv_b[0], ev_hy_skip[0])
    xs = _outproj(y_hy_lat, y_hy_ctx, a_lat, a_ctx, x, ctx, mods[0], vec(norm_mix_post[0]), ev_w_out[0].astype(BF16), nlat)
    xs = _ffn_dense(xs, mods[0], vec(norm_ffn_pre[0]), vec(norm_ffn_post[0]), ev_ffn_w_gate[0].astype(BF16),
                    ev_ffn_w_up[0].astype(BF16), ev_ffn_w_down[0].astype(BF16), nlat)

    u3 = _inproj_odd(xs, mods[1], vec(norm_mix_pre[1]), od_w_in[0].astype(BF16), n_lat)
    s5_ops = _s5_params(od_s5_lambda_re[0], od_s5_lambda_im[0], od_s5_log_step[0],
                        od_s5_b_re[0], od_s5_b_im[0], od_s5_c_re[0], od_s5_c_im[0])
    y = _s5_scan(u3, *s5_ops, n_lat)
    x_lat, hf, rw, ri, counts = _glu(y, u3, vec(od_s5_d[0]), od_glu_w_a[0].astype(BF16), od_glu_w_b[0].astype(BF16),
                                     xs, mods[1], vec(norm_mix_post[1]), vec(norm_ffn_pre[1]), od_router[0], n_lat)
    t_rows = n_batch * n_lat
    out = _moe(hf.reshape(t_rows, D_MODEL), rw.reshape(t_rows, LANE), ri.reshape(t_rows, LANE), counts,
               x_lat.reshape(t_rows, D_MODEL), mods[1], vec(norm_ffn_post[1]), od_moe_w_gate[0].astype(BF16),
               od_moe_w_up[0].astype(BF16), od_moe_w_down[0].astype(BF16), nlat)
    return out.reshape(n_batch, n_lat, D_MODEL)
```

```python
import functools
import math

import jax
import jax.numpy as jnp
import numpy as np
from jax import lax
from jax.experimental import pallas as pl
from jax.experimental.pallas import tpu as pltpu

F32 = jnp.float32
BF16 = jnp.bfloat16

D_MODEL = 1024
EPS = 1e-6
GRID_W = 64

HY_WIDTH = 512
HY_ORDER = 2
HY_IN = (HY_ORDER + 1) * HY_WIDTH
HY_BANDS = 16
HY_EMB = 2 * HY_BANDS + 1
HY_DECAY_SLOW = -math.log(1e-2) / 1.5
HY_DECAY_FAST = -math.log(1e-2) / 0.3
HEAD_DIM = 64
N_Q_HEADS = 8
N_KV_HEADS = 2
Q_PER_KV = N_Q_HEADS // N_KV_HEADS
ATTN_WIDTH = N_Q_HEADS * HEAD_DIM
KV_WIDTH = N_KV_HEADS * HEAD_DIM
ROPE_THETA = 10000.0
LOG2_E = 1.0 / math.log(2.0)

S5_GROUP = 16
S5_GROUPS = D_MODEL // S5_GROUP
S5_STATE = 64
S5_LANES = S5_GROUPS * S5_STATE
S5_BLOCK_GROUPS = 8
N_EXPERTS = 8

ROW_TILE = 256
FFN_SUBTILES = 2
ATTN_Q_TILE = 512
FREQ_TILE = 512
DFT_SPLIT = 128
S5_CHUNK = 32
S5_SCAN_CHUNK = 64
MOE_TILE = 512
MOE_FF_TILE = 1792
LANE = 128
MIB = 1024 * 1024


def _cparams(sem, vmem_mib=48):
    return pltpu.CompilerParams(dimension_semantics=sem, vmem_limit_bytes=vmem_mib * MIB)


def _rms(x, gain):
    return x * lax.rsqrt(jnp.mean(x * x, axis=-1, keepdims=True) + EPS) * gain


def _bdot(a, b):
    return jnp.dot(a, b, preferred_element_type=F32)


def _full(shape):
    zeros = (0,) * len(shape)
    return pl.BlockSpec(shape, lambda *_: zeros)


def _resident(shape):
    zeros = (0,) * len(shape)
    return pl.BlockSpec(shape, lambda *_: zeros, pipeline_mode=pl.Buffered(1))


def _ada_kernel(cond_ref, w_ref, b_ref, o_ref):
    c = cond_ref[...]
    s = (c * jax.nn.sigmoid(c)).astype(BF16)
    o_ref[...] = _bdot(s, w_ref[...].astype(BF16)) + b_ref[...]


def _ada_params(cond, ada_w, ada_b):
    depth, _, n6 = ada_w.shape
    rows = cond.shape[0]
    tn = 1536
    out = pl.pallas_call(
        _ada_kernel,
        grid=(depth, n6 // tn),
        in_specs=[pl.BlockSpec((rows, D_MODEL), lambda i, j: (0, 0)),
                  pl.BlockSpec((None, D_MODEL, tn), lambda i, j: (i, 0, j)),
                  pl.BlockSpec((None, 1, tn), lambda i, j: (i, 0, j))],
        out_specs=pl.BlockSpec((None, rows, tn), lambda i, j: (i, 0, j)),
        out_shape=jax.ShapeDtypeStruct((depth, rows, n6), F32),
        compiler_params=_cparams(("arbitrary", "arbitrary")),
        name="ada_params",
    )(cond, ada_w, ada_b.reshape(depth, 1, n6))
    return out.reshape(depth, rows, 6, D_MODEL)


def _rope_rotate(t):
    w = t.shape[-1]
    lane = lax.broadcasted_iota(jnp.int32, t.shape, 1)
    first = (lane & (HEAD_DIM - 1)) < HEAD_DIM // 2
    return jnp.where(first, pltpu.roll(t, w - HEAD_DIM // 2, 1), pltpu.roll(t, HEAD_DIM // 2, 1))


def _head_slots(t):
    lane = lax.broadcasted_iota(jnp.int32, t.shape, 1)
    lo = jnp.where(lane < HEAD_DIM, t, 0.0)
    hi = jnp.where(lane >= HEAD_DIM, t, 0.0)
    return jnp.concatenate([lo, pltpu.roll(lo, HEAD_DIM, 1), pltpu.roll(hi, HEAD_DIM, 1), hi], axis=-1)


def _inproj_even_kernel(x_ref, c_ref, mod_ref, gain_ref, w_ref, qg_ref, kg_ref, e_ref, cos_ref, sin_ref,
                        hy_ref, q_ref, k4_ref, v4_ref, *, nlat):
    xin = jnp.where(pl.program_id(1) >= nlat, c_ref[...], x_ref[...])
    h = _rms(xin, gain_ref[...]) * (1.0 + mod_ref[1:2, :]) + mod_ref[0:1, :]
    p = _bdot(h.astype(BF16), w_ref[...])
    hy_ref[...] = p[:, :HY_IN].astype(hy_ref.dtype)
    q = p[:, HY_IN:HY_IN + ATTN_WIDTH]
    k = p[:, HY_IN + ATTN_WIDTH:HY_IN + ATTN_WIDTH + KV_WIDTH]
    v = p[:, HY_IN + ATTN_WIDTH + KV_WIDTH:]
    e = e_ref[...]
    qn = q * lax.rsqrt(_bdot((q * q).astype(BF16), e) + EPS) * qg_ref[...]
    kn = k * lax.rsqrt(_bdot((k * k).astype(BF16), e[:KV_WIDTH, :KV_WIDTH]) + EPS) * kg_ref[...]
    cos = cos_ref[...]
    sin = sin_ref[...]
    qr = (qn * cos + _rope_rotate(qn) * sin) * (HEAD_DIM ** -0.5 * LOG2_E)
    kr = kn * cos[:, :KV_WIDTH] + _rope_rotate(kn) * sin[:, :KV_WIDTH]
    q_ref[...] = qr.astype(q_ref.dtype)
    k4_ref[...] = _head_slots(kr).astype(k4_ref.dtype)
    v4_ref[...] = _head_slots(v).astype(v4_ref.dtype)


def _mod_spec(n_batch, nlat):
    return pl.BlockSpec((None, 6, D_MODEL), lambda b, j: (jnp.where(j >= nlat, n_batch, b), 0, 0))


def _row_spec(width):
    return pl.BlockSpec((None, ROW_TILE, width), lambda b, j: (b, j, 0))


def _lat_ctx_specs(nlat, nctx, width):
    return [pl.BlockSpec((None, ROW_TILE, width), lambda b, j: (b, jnp.minimum(j, nlat - 1), 0)),
            pl.BlockSpec((None, ROW_TILE, width), lambda b, j: (b, jnp.clip(j - nlat, 0, nctx - 1), 0))]


def _inproj_even(x, ctx, mod, gain, w_in, q_gain, k_gain, cos_t, sin_t, nlat):
    n_batch = x.shape[0]
    nctx = ctx.shape[1] // ROW_TILE
    ntile = nlat + nctx
    lt = ntile * ROW_TILE
    n_out = w_in.shape[1]
    head_avg = jnp.asarray(np.kron(np.eye(N_Q_HEADS), np.full((HEAD_DIM, HEAD_DIM), 1.0 / HEAD_DIM)), BF16)
    table = pl.BlockSpec((ROW_TILE, ATTN_WIDTH), lambda b, j: (j, 0))
    outs = [jax.ShapeDtypeStruct((n_batch, lt, HY_IN), BF16)] + [jax.ShapeDtypeStruct((n_batch, lt, ATTN_WIDTH), BF16)] * 3
    return pl.pallas_call(
        functools.partial(_inproj_even_kernel, nlat=nlat),
        grid=(n_batch, ntile),
        in_specs=_lat_ctx_specs(nlat, nctx, D_MODEL) + [
            _mod_spec(n_batch, nlat), _full((1, D_MODEL)), _full((D_MODEL, n_out)),
            _full((1, ATTN_WIDTH)), _full((1, KV_WIDTH)), _full((ATTN_WIDTH, ATTN_WIDTH)), table, table],
        out_specs=[_row_spec(HY_IN), _row_spec(ATTN_WIDTH), _row_spec(ATTN_WIDTH), _row_spec(ATTN_WIDTH)],
        out_shape=outs,
        compiler_params=_cparams(("parallel", "parallel")),
        name="inproj_even",
    )(x, ctx, mod, gain, w_in, q_gain, k_gain, head_avg, cos_t, sin_t)


def _attn_kernel(q_ref, k4_ref, v4_ref, o_ref):
    for blk in range(N_Q_HEADS // 2):
        g = (2 * blk) // Q_PER_KV
        qp = q_ref[:, LANE * blk:LANE * (blk + 1)]
        acc = None
        for r in range(2):
            slot = LANE * (2 * g + r)
            s = lax.dot_general(qp, k4_ref[:, slot:slot + LANE], (((1,), (1,)), ((), ())), preferred_element_type=F32)
            e = jnp.exp2(s - jnp.max(s, axis=-1, keepdims=True))
            inv = 1.0 / jnp.sum(e, axis=-1, keepdims=True)
            o = _bdot(e.astype(BF16), v4_ref[:, slot:slot + LANE]) * inv
            acc = o if acc is None else acc + o
        o_ref[:, LANE * blk:LANE * (blk + 1)] = acc.astype(o_ref.dtype)


def _attention(q, k4, v4, q_rows, q_tile, q_blk0, k_rows, k_blk):
    n_batch = q.shape[0]
    kv_spec = pl.BlockSpec((None, k_rows, ATTN_WIDTH), lambda b, j: (b, k_blk, 0))
    return pl.pallas_call(
        _attn_kernel,
        grid=(n_batch, q_rows // q_tile),
        in_specs=[pl.BlockSpec((None, q_tile, ATTN_WIDTH), lambda b, j: (b, q_blk0 + j, 0)), kv_spec, kv_spec],
        out_specs=pl.BlockSpec((None, q_tile, ATTN_WIDTH), lambda b, j: (b, j, 0)),
        out_shape=jax.ShapeDtypeStruct((n_batch, q_rows, ATTN_WIDTH), BF16),
        compiler_params=_cparams(("parallel", "parallel")),
        name="attention",
    )(q, k4, v4)


def _filter_kernel(z_ref, w1_ref, b1_ref, w2_ref, b2_ref, wo_ref, fr_ref, dl_ref, o_ref):
    hi = lax.Precision.HIGHEST
    z = z_ref[...]
    fr = fr_ref[...]
    h = jnp.sin(fr * (jnp.dot(z, w1_ref[...], precision=hi, preferred_element_type=F32) + b1_ref[...]))
    h = jnp.sin(fr * (jnp.dot(h, w2_ref[...], precision=hi, preferred_element_type=F32) + b2_ref[...]))
    h = jnp.dot(h, wo_ref[...], precision=hi, preferred_element_type=F32)
    decay = jnp.exp(-z[:, 0:1] * dl_ref[...])
    o_ref[...] = h * jnp.concatenate([decay] * (2 * HY_ORDER), axis=-1)


def _hyena_filters(seq, w1, b1, w2, b2, wout, freq):
    t = jnp.linspace(0.0, 1.0, seq, dtype=F32)[:, None]
    bands = jnp.linspace(1e-4, HY_BANDS - 1, HY_BANDS, dtype=F32)
    phase = (2.0 * math.pi / seq) * jnp.arange(seq, dtype=F32)[:, None] * bands
    z = jnp.concatenate([t, jnp.cos(phase), -jnp.sin(phase)], axis=-1)
    z = jnp.pad(z, ((0, 0), (0, LANE - HY_EMB)))
    w1p = jnp.pad(w1, ((0, LANE - HY_EMB), (0, 0)))
    deltas = jnp.linspace(HY_DECAY_SLOW, HY_DECAY_FAST, HY_WIDTH, dtype=F32)[None, :]
    hid = w1.shape[1]
    n_out = wout.shape[1]
    tl = ROW_TILE
    return pl.pallas_call(
        _filter_kernel,
        grid=(seq // tl,),
        in_specs=[pl.BlockSpec((tl, LANE), lambda i: (i, 0)), _full((LANE, hid)), _full((1, hid)), _full((hid, hid)),
                  _full((1, hid)), _full((hid, n_out)), _full((1, hid)), _full((1, HY_WIDTH))],
        out_specs=pl.BlockSpec((tl, n_out), lambda i: (i, 0)),
        out_shape=jax.ShapeDtypeStruct((seq, n_out), F32),
        compiler_params=_cparams(("parallel",)),
        name="hyena_filter",
    )(z, w1p, b1[None, :], w2, b2[None, :], wout, freq[None, :], deltas)


def _freq_tile(seq):
    return min(FREQ_TILE, seq)


def _dft_tables(seq):
    n = 2 * seq
    unit = 2.0 * math.pi / n
    col = jnp.arange(seq, dtype=jnp.int32)[None, :]
    r1 = jnp.arange(seq // DFT_SPLIT, dtype=jnp.int32)[:, None] * DFT_SPLIT
    r2 = jnp.arange(DFT_SPLIT, dtype=jnp.int32)[:, None]
    ang1 = ((r1 * col) % n).astype(F32) * unit
    ang2 = ((r2 * col) % n).astype(F32) * unit
    c1, s1 = jnp.cos(ang1)[:, None, :], jnp.sin(ang1)[:, None, :]
    c2, s2 = jnp.cos(ang2)[None, :, :], jnp.sin(ang2)[None, :, :]
    cos = (c1 * c2 - s1 * s2).reshape(seq, seq)
    sin = (s1 * c2 + c1 * s2).reshape(seq, seq)
    idx = jnp.arange(seq, dtype=jnp.int32)
    nyquist = jnp.where(idx % 2 == 0, 1.0, -1.0)
    sin_rows = jnp.where(idx[:, None] == 0, nyquist[None, :], sin)
    sin_cols = jnp.where(idx[None, :] == 0, nyquist[:, None], sin)
    tf = _freq_tile(seq)
    nf = seq // tf
    cos, sin_rows, sin_cols = cos.astype(BF16), sin_rows.astype(BF16), sin_cols.astype(BF16)
    tiles = [slice(i * tf, (i + 1) * tf) for i in range(nf)]
    fwd = jnp.concatenate([m[s, :] for s in tiles for m in (cos, sin_rows)], axis=0)
    inv = jnp.concatenate([m[:, s] for s in tiles for m in (cos, sin_cols)], axis=1)
    return fwd, inv


def _kfreq_kernel(hf_ref, hb_ref, f_ref, kre_ref, ks_ref):
    fi = pl.program_id(1)
    seq = hf_ref.shape[0]
    hf = hf_ref[...]
    row = lax.broadcasted_iota(jnp.int32, hf.shape, 0)
    hb = jnp.where(row == 0, 0.0, hb_ref[...])
    f = f_ref[...]
    a1 = _bdot(f, (hf + hb).astype(BF16))
    a2 = _bdot(f, (hf - hb).astype(BF16))
    tf = f_ref.shape[0] // 2
    frow = lax.broadcasted_iota(jnp.int32, (tf, hf.shape[1]), 0)
    dc = jnp.logical_and(frow == 0, fi == 0)
    scale = jnp.where(dc, 1.0 / (2 * seq), 2.0 / (2 * seq))
    kre_ref[...] = a1[:tf] * scale
    ks_ref[...] = jnp.where(dc, a1[tf:], a2[tf:]) * scale


def _kfreq(hfilt, fwd_tab):
    seq = hfilt.shape[0]
    tf = _freq_tile(seq)
    nf = seq // tf
    out = jax.ShapeDtypeStruct((HY_ORDER, seq, HY_WIDTH), F32)
    ospec = pl.BlockSpec((None, tf, HY_WIDTH), lambda o, fi: (o, fi, 0))
    return pl.pallas_call(
        _kfreq_kernel,
        grid=(HY_ORDER, nf),
        in_specs=[pl.BlockSpec((seq, HY_WIDTH), lambda o, fi: (0, o)),
                  pl.BlockSpec((seq, HY_WIDTH), lambda o, fi: (0, HY_ORDER + o)),
                  pl.BlockSpec((2 * tf, seq), lambda o, fi: (fi, 0))],
        out_specs=[ospec, ospec],
        out_shape=[out, out],
        compiler_params=_cparams(("parallel", "parallel")),
        name="hyena_kfreq",
    )(hfilt, hfilt, fwd_tab)


def _short_conv(p, w, b):
    n = p.shape[0]
    row = lax.broadcasted_iota(jnp.int32, p.shape, 0)
    prev = jnp.where(row == 0, 0.0, pltpu.roll(p, 1, 0))
    nxt = jnp.where(row == n - 1, 0.0, pltpu.roll(p, n - 1, 0))
    return b + prev * w[0:1, :] + p * w[1:2, :] + nxt * w[2:3, :]


def _hyena_kernel(vsrc_ref, gsrc_ref, cw_ref, cb_ref, skip_ref, f_ref, ft_ref, kre_ref, ks_ref, o_ref,
                  v_scr, vb_scr, acc_scr, *, conv_v):
    fi = pl.program_id(1)
    tf = f_ref.shape[0] // 2

    @pl.when(fi == 0)
    def _():
        v = vsrc_ref[...].astype(F32)
        if conv_v:
            v = _short_conv(v, cw_ref[0], cb_ref[0])
        v_scr[...] = v
        vb_scr[...] = v.astype(BF16)
        acc_scr[...] = jnp.zeros_like(acc_scr)

    xf = _bdot(f_ref[...], vb_scr[...])
    xre, xs = xf[:tf], xf[tf:]
    kre, ks = kre_ref[...], ks_ref[...]
    row = lax.broadcasted_iota(jnp.int32, xre.shape, 0)
    dc = jnp.logical_and(row == 0, fi == 0)
    yre = jnp.where(dc, xre * kre, xre * kre - xs * ks)
    ys = jnp.where(dc, xs * ks, xre * ks + xs * kre)
    y = jnp.concatenate([yre, ys], axis=0).astype(BF16)
    acc_scr[...] += _bdot(ft_ref[...], y)

    @pl.when(fi == pl.num_programs(1) - 1)
    def _():
        gate = _short_conv(gsrc_ref[...].astype(F32), cw_ref[1], cb_ref[1])
        o_ref[...] = (gate * (acc_scr[...] + v_scr[...] * skip_ref[...])).astype(o_ref.dtype)


def _hyena_order(vsrc, v_blk, v_col, hy, row_blk, order, seq, conv_w, conv_b, skip, fwd_tab, inv_tab, kre, ks):
    n_batch = hy.shape[0]
    tf = _freq_tile(seq)
    nf = seq // tf
    conv_v = order == 0
    cw = jnp.stack([conv_w[:, :HY_WIDTH], conv_w[:, (order + 1) * HY_WIDTH:(order + 2) * HY_WIDTH]])
    cb = jnp.stack([conv_b[None, :HY_WIDTH], conv_b[None, (order + 1) * HY_WIDTH:(order + 2) * HY_WIDTH]])
    kspec = pl.BlockSpec((None, tf, HY_WIDTH), lambda b, fi: (order, fi, 0))
    return pl.pallas_call(
        functools.partial(_hyena_kernel, conv_v=conv_v),
        grid=(n_batch, nf),
        in_specs=[pl.BlockSpec((None, seq, HY_WIDTH), lambda b, fi: (b, v_blk, v_col)),
                  pl.BlockSpec((None, seq, HY_WIDTH), lambda b, fi: (b, row_blk, order + 1)),
                  _full((2, 3, HY_WIDTH)), _full((2, 1, HY_WIDTH)), _full((1, HY_WIDTH)),
                  pl.BlockSpec((2 * tf, seq), lambda b, fi: (fi, 0)),
                  pl.BlockSpec((seq, 2 * tf), lambda b, fi: (0, fi)),
                  kspec, kspec],
        out_specs=pl.BlockSpec((None, seq, HY_WIDTH), lambda b, fi: (b, 0, 0)),
        out_shape=jax.ShapeDtypeStruct((n_batch, seq, HY_WIDTH), BF16),
        scratch_shapes=[pltpu.VMEM((seq, HY_WIDTH), F32), pltpu.VMEM((seq, HY_WIDTH), BF16),
                        pltpu.VMEM((seq, HY_WIDTH), F32)],
        compiler_params=_cparams(("parallel", "arbitrary")),
        name="hyena_order%d" % order,
    )(vsrc, hy, cw, cb, skip[order][None, :], fwd_tab, inv_tab, kre, ks)


def _hyena_mixer(hy, row_blk, seq, filt_args, conv_w, conv_b, skip):
    hfilt = _hyena_filters(seq, *filt_args)
    fwd_tab, inv_tab = _dft_tables(seq)
    kre, ks = _kfreq(hfilt, fwd_tab)
    v1 = _hyena_order(hy, row_blk, 0, hy, row_blk, 0, seq, conv_w, conv_b, skip, fwd_tab, inv_tab, kre, ks)
    return _hyena_order(v1, 0, 0, hy, row_blk, 1, seq, conv_w, conv_b, skip, fwd_tab, inv_tab, kre, ks)


def _outproj_kernel(yl_ref, yc_ref, al_ref, ac_ref, x_ref, c_ref, mod_ref, gpost_ref, w_ref, o_ref, *, nlat):
    is_ctx = pl.program_id(1) >= nlat
    yh = jnp.where(is_ctx, yc_ref[...], yl_ref[...])
    ya = jnp.where(is_ctx, ac_ref[...], al_ref[...])
    out = _bdot(yh, w_ref[:HY_WIDTH, :]) + _bdot(ya, w_ref[HY_WIDTH:, :])
    o_ref[...] = jnp.where(is_ctx, c_ref[...], x_ref[...]) + mod_ref[2:3, :] * _rms(out, gpost_ref[...])


def _outproj(y_lat, y_ctx, a_lat, a_ctx, x, ctx, mod, gpost, w_out, nlat):
    n_batch = x.shape[0]
    nctx = ctx.shape[1] // ROW_TILE
    ntile = nlat + nctx
    return pl.pallas_call(
        functools.partial(_outproj_kernel, nlat=nlat),
        grid=(n_batch, ntile),
        in_specs=(_lat_ctx_specs(nlat, nctx, HY_WIDTH) + _lat_ctx_specs(nlat, nctx, ATTN_WIDTH)
                  + _lat_ctx_specs(nlat, nctx, D_MODEL)
                  + [_mod_spec(n_batch, nlat), _full((1, D_MODEL)), _full(w_out.shape)]),
        out_specs=_row_spec(D_MODEL),
        out_shape=jax.ShapeDtypeStruct((n_batch, ntile * ROW_TILE, D_MODEL), F32),
        compiler_params=_cparams(("parallel", "parallel")),
        name="outproj_even",
    )(y_lat, y_ctx, a_lat, a_ctx, x, ctx, mod, gpost, w_out)


def _ffn_kernel(x_ref, mod_ref, gpre_ref, gpost_ref, wg_ref, wu_ref, wd_ref, o_ref, *, n_batch, tiles_per_batch, nlat):
    n_sub = x_ref.shape[0] // ROW_TILE
    subs = []
    for s in range(n_sub):
        tile = pl.program_id(0) * n_sub + s
        batch = lax.div(tile, tiles_per_batch)
        mod = mod_ref[jnp.where(lax.rem(tile, tiles_per_batch) >= nlat, n_batch, batch)]
        x = x_ref[s * ROW_TILE:(s + 1) * ROW_TILE, :]
        subs.append((x, mod, (_rms(x, gpre_ref[...]) * (1.0 + mod[4:5, :]) + mod[3:4, :]).astype(BF16)))
    h = jnp.concatenate([sub[2] for sub in subs], axis=0)
    g = _bdot(h, wg_ref[...])
    u = _bdot(h, wu_ref[...])
    a = (g * jax.nn.sigmoid(g) * u).astype(BF16)
    y = _bdot(a, wd_ref[...])
    for s, (x, mod, _) in enumerate(subs):
        rows = slice(s * ROW_TILE, (s + 1) * ROW_TILE)
        o_ref[rows, :] = x + mod[5:6, :] * _rms(y[rows, :], gpost_ref[...])


def _ffn_dense(xs, mod, gpre, gpost, wg, wu, wd, nlat):
    n_batch, lt, _ = xs.shape
    tiles_per_batch = lt // ROW_TILE
    rows = FFN_SUBTILES * ROW_TILE
    n_steps = n_batch * tiles_per_batch // FFN_SUBTILES
    blk = pl.BlockSpec((rows, D_MODEL), lambda i: (i, 0))
    out = pl.pallas_call(
        functools.partial(_ffn_kernel, n_batch=n_batch, tiles_per_batch=tiles_per_batch, nlat=nlat),
        grid=(n_steps,),
        in_specs=[blk, _resident(mod.shape), _resident((1, D_MODEL)), _resident((1, D_MODEL)),
                  _resident(wg.shape), _resident(wu.shape), _resident(wd.shape)],
        out_specs=blk,
        out_shape=jax.ShapeDtypeStruct((n_batch * lt, D_MODEL), F32),
        compiler_params=_cparams(("parallel",), 56),
        name="ffn_dense",
    )(xs.reshape(n_batch * lt, D_MODEL), mod, gpre, gpost, wg, wu, wd)
    return out.reshape(xs.shape)


def _mod_rows(mod_ref, k, n_batch, is_ctx):
    return jnp.where(is_ctx, mod_ref[n_batch:n_batch + 1, k:k + 1, :], mod_ref[0:n_batch, k:k + 1, :])


def _row_order_swap(n_outer, n_inner):
    n = n_outer * n_inner
    dst = np.arange(n)
    src = (dst % n_outer) * n_inner + dst // n_outer
    return jnp.asarray(src[:, None] == np.arange(n)[None, :], BF16)


def _inproj_odd_kernel(x_ref, mod_ref, gain_ref, swap_ref, w_ref, u_ref, *, n_lat_steps):
    n_batch, steps, _ = x_ref.shape
    is_ctx = pl.program_id(0) >= n_lat_steps
    h = (_rms(x_ref[...], gain_ref[...]) * (1.0 + _mod_rows(mod_ref, 1, n_batch, is_ctx))
         + _mod_rows(mod_ref, 0, n_batch, is_ctx))
    h = h.reshape(n_batch * steps, D_MODEL).astype(BF16)
    h = _bdot(swap_ref[...], h).astype(BF16)
    u_ref[...] = _bdot(h, w_ref[...]).reshape(steps, n_batch, D_MODEL)


def _inproj_odd(xs, mod, gain, w_in, n_lat):
    n_batch, lt, _ = xs.shape
    p = S5_CHUNK
    swap = _row_order_swap(n_batch, p)
    return pl.pallas_call(
        functools.partial(_inproj_odd_kernel, n_lat_steps=n_lat // p),
        grid=(lt // p,),
        in_specs=[pl.BlockSpec((n_batch, p, D_MODEL), lambda i: (0, i, 0)), _full(mod.shape), _full((1, D_MODEL)),
                  _full(swap.shape), _full(w_in.shape)],
        out_specs=pl.BlockSpec((p, n_batch, D_MODEL), lambda i: (i, 0, 0)),
        out_shape=jax.ShapeDtypeStruct((lt, n_batch, D_MODEL), F32),
        compiler_params=_cparams(("parallel",)),
        name="inproj_odd",
    )(xs, mod, gain, swap, w_in)


def _s5_param_kernel(lr_ref, li_ref, ls_ref, lrx_ref, lix_ref, lsx_ref, br_ref, bi_ref,
                     abr_ref, abi_ref, bbr_ref, bbi_ref):
    def zoh(lr_raw, li, log_step):
        lr = jnp.minimum(lr_raw, -1e-4)
        dt = jnp.exp(log_step)
        mag = jnp.exp(lr * dt)
        ab_re = mag * jnp.cos(li * dt)
        ab_im = mag * jnp.sin(li * dt)
        den = lr * lr + li * li
        nr, ni = ab_re - 1.0, ab_im
        return ab_re, ab_im, (nr * lr + ni * li) / den, (ni * lr - nr * li) / den

    ab_re, ab_im, _, _ = zoh(lr_ref[...], li_ref[...], ls_ref[...])
    abr_ref[...] = ab_re
    abi_ref[...] = ab_im
    _, _, co_re, co_im = zoh(lrx_ref[...], lix_ref[...], lsx_ref[...])
    br, bi = br_ref[...], bi_ref[...]
    bbr_ref[...] = co_re * br - co_im * bi
    bbi_ref[...] = co_re * bi + co_im * br


def _s5_params(lam_re, lam_im, log_step, b_re, b_im, c_re, c_im):
    nd, g, n = lam_re.shape
    k = S5_GROUP
    rep = lambda a: jnp.repeat(a, k, axis=1)
    ls = log_step[:, :, None]
    bt_re = jnp.swapaxes(b_re, 2, 3).reshape(nd, g * k, n)
    bt_im = jnp.swapaxes(b_im, 2, 3).reshape(nd, g * k, n)
    small = jax.ShapeDtypeStruct((nd, g, n), F32)
    big = jax.ShapeDtypeStruct((nd, g * k, n), F32)
    ab_re, ab_im, bb_re, bb_im = pl.pallas_call(
        _s5_param_kernel, out_shape=[small, small, big, big], name="s5_discretise",
    )(lam_re, lam_im, ls, rep(lam_re), rep(lam_im), rep(ls), bt_re, bt_im)
    bb_re = bb_re.reshape(nd, g, k, n)
    bb_im = bb_im.reshape(nd, g, k, n)
    cmul = lambda xr, xi, yr, yi: (xr * yr - xi * yi, xr * yi + xi * yr)
    a_re, a_im = ab_re[:, :, None, :], ab_im[:, :, None, :]
    a2_re, a2_im = cmul(a_re, a_im, a_re, a_im)
    abb_re, abb_im = cmul(a_re, a_im, bb_re, bb_im)
    ca_re, ca_im = cmul(c_re, c_im, a_re, a_im)
    ca2_re, ca2_im = cmul(c_re, c_im, a2_re, a2_im)
    real_cb = lambda xr, xi: jnp.einsum('dgin,dgkn->dgik', xr, bb_re) - jnp.einsum('dgin,dgkn->dgik', xi, bb_im)
    k0 = real_cb(c_re, c_im)
    k1 = real_cb(ca_re, ca_im)
    a2 = jnp.stack([a2_re.reshape(nd, g * n), a2_im.reshape(nd, g * n)], axis=1)
    nq = g // S5_BLOCK_GROUPS

    def block_diag(parts):
        m = jnp.stack(parts)
        r, c = m.shape[-2:]
        mask = np.kron(np.eye(S5_BLOCK_GROUPS), np.ones((r, c)))
        rows = m.reshape(len(parts), nd, nq, S5_BLOCK_GROUPS * r, c)
        return jnp.tile(rows, (1, 1, 1, 1, S5_BLOCK_GROUPS)) * jnp.asarray(mask, F32)

    t = lambda x: jnp.swapaxes(x, 2, 3)
    i2s = block_diag([abb_re, abb_im, bb_re, bb_im])
    w_in = jnp.concatenate([jnp.concatenate([i2s[0], i2s[1]], axis=-1),
                            jnp.concatenate([i2s[2], i2s[3]], axis=-1)], axis=-2).astype(BF16)
    s2o = block_diag([t(ca_re), t(ca2_re), t(-ca_im), t(-ca2_im)])
    w_state = jnp.stack([jnp.concatenate([s2o[0], s2o[1]], axis=-1),
                         jnp.concatenate([s2o[2], s2o[3]], axis=-1)], axis=2).astype(BF16)
    i2o = block_diag([t(k0), t(k1)])
    w_dir = jnp.concatenate([jnp.concatenate([i2o[0], i2o[1]], axis=-1),
                             jnp.concatenate([jnp.zeros_like(i2o[0]), i2o[0]], axis=-1)], axis=-2).astype(BF16)
    return a2, w_in, w_state, w_dir


def _s5_scan_kernel(u_ref, a2_ref, win_ref, wst_ref, wdir_ref, y_ref, sbuf, state, *, nctx_chunks):
    d = pl.program_id(0)
    i = pl.program_id(1)
    p_steps, n_batch, _ = u_ref.shape
    npair = p_steps // 2
    rows = npair * n_batch
    half = S5_LANES
    nq = win_ref.shape[0]
    kq = win_ref.shape[1] // 2
    sq = win_ref.shape[2] // 2

    @pl.when(i == 0)
    def _():
        state[...] = jnp.zeros_like(state)

    fwd = d == 0
    u = u_ref[...].reshape(npair, 2, n_batch, D_MODEL)
    u_even = u[:, 0].reshape(rows, D_MODEL).astype(BF16)
    u_odd = u[:, 1].reshape(rows, D_MODEL).astype(BF16)
    u_1 = jnp.where(fwd, u_even, u_odd)
    u_2 = jnp.where(fwd, u_odd, u_even)
    pair_in = lambda q: jnp.concatenate([u_1[:, kq * q:kq * (q + 1)], u_2[:, kq * q:kq * (q + 1)]], axis=-1)
    for q in range(nq):
        r = _bdot(pair_in(q), win_ref[q])
        sbuf[:, :, sq * q:sq * (q + 1)] = r[:, :sq].reshape(npair, n_batch, sq)
        sbuf[:, :, half + sq * q:half + sq * (q + 1)] = r[:, sq:].reshape(npair, n_batch, sq)

    for q in range(nq):
        lo = sq * q
        ar = jnp.broadcast_to(a2_ref[0:1, lo:lo + sq], (n_batch, sq))
        ai = jnp.broadcast_to(a2_ref[1:2, lo:lo + sq], (n_batch, sq))

        def body(t, carry, lo=lo, ar=ar, ai=ai):
            sr, si = carry
            tt = jnp.where(d == 0, t, npair - 1 - t)
            nr = ar * sr - ai * si + sbuf[tt, :, lo:lo + sq]
            ni = ar * si + ai * sr + sbuf[tt, :, half + lo:half + lo + sq]
            sbuf[tt, :, lo:lo + sq] = sr
            sbuf[tt, :, half + lo:half + lo + sq] = si
            return nr, ni

        sr, si = lax.fori_loop(0, npair, body, (state[:, lo:lo + sq], state[:, half + lo:half + lo + sq]), unroll=4)
        state[:, lo:lo + sq] = sr
        state[:, half + lo:half + lo + sq] = si

    @pl.when(i >= nctx_chunks)
    def _():
        s = sbuf[...].reshape(rows, 2 * half).astype(BF16)
        for q in range(nq):
            yq = (_bdot(s[:, sq * q:sq * (q + 1)], wst_ref[q, 0])
                  + _bdot(s[:, half + sq * q:half + sq * (q + 1)], wst_ref[q, 1])
                  + _bdot(pair_in(q), wdir_ref[q]))
            y_1 = yq[:, :kq].reshape(npair, 1, n_batch, kq)
            y_2 = yq[:, kq:].reshape(npair, 1, n_batch, kq)
            y_pair = jnp.concatenate([jnp.where(fwd, y_1, y_2), jnp.where(fwd, y_2, y_1)], axis=1)
            y_ref[:, :, kq * q:kq * (q + 1)] = y_pair.reshape(p_steps, n_batch, kq)


def _s5_scan(u3, a2, w_in, w_state, w_dir, n_lat):
    lt, n_batch, _ = u3.shape
    p = S5_SCAN_CHUNK
    nchunk = lt // p
    nlatc = n_lat // p
    nctxc = nchunk - nlatc

    def u_map(d, i):
        return (jnp.where(d == 0, lax.rem(i + nlatc, nchunk), nchunk - 1 - i), 0, 0)

    def y_map(d, i):
        return (d, jnp.where(d == 0, jnp.maximum(i - nctxc, 0), jnp.minimum(nchunk - 1 - i, nlatc - 1)), 0, 0)

    per_dir = lambda w: pl.BlockSpec((None,) + w.shape[1:], lambda d, i: (d,) + (0,) * (w.ndim - 1))
    return pl.pallas_call(
        functools.partial(_s5_scan_kernel, nctx_chunks=nctxc),
        grid=(2, nchunk),
        in_specs=[pl.BlockSpec((p, n_batch, D_MODEL), u_map), per_dir(a2), per_dir(w_in), per_dir(w_state),
                  per_dir(w_dir)],
        out_specs=pl.BlockSpec((None, p, n_batch, D_MODEL), y_map),
        out_shape=jax.ShapeDtypeStruct((2, n_lat, n_batch, D_MODEL), F32),
        scratch_shapes=[pltpu.VMEM((p // 2, n_batch, 2 * S5_LANES), F32), pltpu.VMEM((n_batch, 2 * S5_LANES), F32)],
        compiler_params=_cparams(("arbitrary", "arbitrary")),
        name="s5_scan",
    )(u3, a2, w_in, w_state, w_dir)


def _glu_kernel(yf_ref, yb_ref, u_ref, dskip_ref, wa_ref, wb_ref, x_ref, mod_ref, gpost_ref, gpre_ref, router_ref,
                swap_ref, tri_ref, xo_ref, hf_ref, rw_ref, ri_ref, cnt_ref, carry):
    n_batch, steps, _ = x_ref.shape
    rows = n_batch * steps

    @pl.when(pl.program_id(0) == 0)
    def _():
        carry[...] = jnp.zeros_like(carry)

    y = (yf_ref[...] + yb_ref[...] + dskip_ref[...] * u_ref[...]).reshape(rows, D_MODEL)
    z = jax.nn.gelu(y).astype(BF16)
    z = _bdot(swap_ref[...], z).astype(BF16)
    out = _bdot(z, wa_ref[...]) * jax.nn.sigmoid(_bdot(z, wb_ref[...]))
    out = out.reshape(n_batch, steps, D_MODEL)
    mod = lambda k: mod_ref[0:n_batch, k:k + 1, :]
    xn = x_ref[...] + mod(2) * _rms(out, gpost_ref[...])
    xo_ref[...] = xn
    hf = _rms(xn, gpre_ref[...]) * (1.0 + mod(4)) + mod(3)
    hf_ref[...] = hf
    hf = hf.reshape(rows, D_MODEL)
    h_hi = hf.astype(BF16)
    h_lo = (hf - h_hi.astype(F32)).astype(BF16)
    part = _bdot(h_hi, router_ref[...])
    logits = part[:, :LANE] + part[:, LANE:] + _bdot(h_lo, router_ref[:, :LANE])
    lane = lax.broadcasted_iota(jnp.int32, logits.shape, 1)
    neg = jnp.float32(-jnp.inf)
    lg = jnp.where(lane < N_EXPERTS, logits, neg)
    m1 = jnp.max(lg, axis=-1, keepdims=True)
    i1 = jnp.min(jnp.where(lg == m1, lane, LANE), axis=-1, keepdims=True)
    lg2 = jnp.where(lane == i1, neg, lg)
    m2 = jnp.max(lg2, axis=-1, keepdims=True)
    i2 = jnp.min(jnp.where(lg2 == m2, lane, LANE), axis=-1, keepdims=True)
    e2 = jnp.exp(m2 - m1)
    w1 = 1.0 / (1.0 + e2)
    rw_ref[...] = jnp.where(lane == 0, w1, jnp.where(lane == 1, e2 * w1, 0.0)).reshape(n_batch, steps, LANE)
    member = jnp.where(lane == i1, 1.0, jnp.where(lane == i2, 1.0, 0.0))
    base = carry[...] + _bdot(tri_ref[...], member.astype(BF16))
    r1 = jnp.sum(jnp.where(lane == i1, base, 0.0), axis=-1, keepdims=True).astype(jnp.int32)
    r2 = jnp.sum(jnp.where(lane == i2, base, 0.0), axis=-1, keepdims=True).astype(jnp.int32)
    ri = jnp.where(lane == 0, i1, jnp.where(lane == 1, i2, jnp.where(lane == 2, r1, jnp.where(lane == 3, r2, 0))))
    ri_ref[...] = ri.reshape(n_batch, steps, LANE)
    carry[...] += jnp.sum(member, axis=0, keepdims=True)
    cnt_ref[...] = carry[...]


def _glu(y, u3, d_skip, w_a, w_b, xs, mod, gpost, gpre, router, n_lat):
    n_batch = xs.shape[0]
    p = S5_CHUNK
    rows = p * n_batch
    router_p = jnp.pad(router, ((0, 0), (0, LANE - router.shape[1])))
    router_hi = router_p.astype(BF16)
    router_cat = jnp.concatenate([router_hi, (router_p - router_hi.astype(F32)).astype(BF16)], axis=1)
    tri = jnp.asarray(np.arange(rows)[:, None] > np.arange(rows)[None, :], BF16)
    swap = _row_order_swap(p, n_batch)
    bt_spec = lambda w: pl.BlockSpec((n_batch, p, w), lambda i: (0, i, 0))
    return pl.pallas_call(
        _glu_kernel,
        grid=(n_lat // p,),
        in_specs=[pl.BlockSpec((None, p, n_batch, D_MODEL), lambda i: (0, i, 0, 0)),
                  pl.BlockSpec((None, p, n_batch, D_MODEL), lambda i: (1, i, 0, 0)),
                  pl.BlockSpec((p, n_batch, D_MODEL), lambda i: (i, 0, 0)),
                  _full((1, D_MODEL)), _full(w_a.shape), _full(w_b.shape), bt_spec(D_MODEL),
                  _full(mod.shape), _full((1, D_MODEL)), _full((1, D_MODEL)),
                  _full(router_cat.shape), _full(swap.shape), _full(tri.shape)],
        out_specs=[bt_spec(D_MODEL), bt_spec(D_MODEL), bt_spec(LANE), bt_spec(LANE), _full((1, LANE))],
        out_shape=[jax.ShapeDtypeStruct((n_batch, n_lat, D_MODEL), F32),
                   jax.ShapeDtypeStruct((n_batch, n_lat, D_MODEL), F32),
                   jax.ShapeDtypeStruct((n_batch, n_lat, LANE), F32),
                   jax.ShapeDtypeStruct((n_batch, n_lat, LANE), jnp.int32),
                   jax.ShapeDtypeStruct((1, LANE), F32)],
        scratch_shapes=[pltpu.VMEM((1, LANE), F32)],
        compiler_params=_cparams(("arbitrary",)),
        name="s5_glu_router",
    )(y, y, u3, d_skip, w_a, w_b, xs, mod, gpost, gpre, router_cat, swap, tri)


def _moe_plan(ri, counts, n_tiles):
    experts = jnp.arange(N_EXPERTS, dtype=jnp.int32)
    n_of = (counts[0, :N_EXPERTS].astype(jnp.int32) + MOE_TILE - 1) // MOE_TILE
    ends = jnp.cumsum(n_of)
    starts = ends - n_of
    start_of = jnp.sum(jnp.where(ri[:, 0:2, None] == experts, starts, 0), axis=-1)
    n_tok = ri.shape[0]
    pos = (start_of * MOE_TILE + ri[:, 2:4]).T.reshape(-1)
    n_used = ends[-1]
    tile = jnp.arange(n_tiles, dtype=jnp.int32)
    tile_expert = jnp.sum((jnp.minimum(tile, n_used - 1)[:, None] >= ends[None, :]).astype(jnp.int32), axis=1)
    n_pairs = pos.shape[0]
    n_rows = n_tiles * MOE_TILE
    pair_of = jnp.full((n_rows,), -1, jnp.int32).at[pos].set(jnp.arange(n_pairs, dtype=jnp.int32),
                                                             unique_indices=True, mode='promise_in_bounds')
    is_pad = pair_of < 0
    pair_of = jnp.where(is_pad, n_pairs - 1 + jnp.cumsum(is_pad.astype(jnp.int32)), pair_of)
    src_token = jnp.where(is_pad, 0, pair_of % n_tok)
    return src_token, pair_of, tile_expert, n_used.reshape(1)


def _moe_expert_kernel(te_ref, nu_ref, src0_ref, src_ref, dst_ref, hf_ref, wg_ref, wu_ref, wd_ref, out_ref,
                       xbuf, obuf, acc, gsem, ssem, *, rows_per_step):
    del te_ref
    i = pl.program_id(0)
    j = pl.program_id(1)
    last_j = pl.num_programs(1) - 1
    nu = nu_ref[0]
    cur = lax.rem(i, 2)
    nxt = 1 - cur
    row0 = j * rows_per_step

    def gather_row(idx_ref, row, slot):
        return pltpu.make_async_copy(hf_ref.at[pl.ds(idx_ref[0, row], 1)], xbuf.at[slot, pl.ds(row, 1)], gsem.at[slot])

    def scatter_row(row, slot):
        return pltpu.make_async_copy(obuf.at[slot, pl.ds(row, 1)], out_ref.at[pl.ds(dst_ref[0, row], 1)], ssem.at[slot])

    def whole_tile_wait(sem, slot):
        pltpu.make_async_copy(xbuf.at[slot], obuf.at[slot], sem.at[slot]).wait()

    @pl.when(jnp.logical_and(i == 0, j == 0))
    def _():
        obuf[...] = jnp.zeros_like(obuf)

        def first(r, c):
            gather_row(src0_ref, r, 0).start()
            return c

        lax.fori_loop(0, MOE_TILE, first, 0)

    @pl.when(jnp.logical_and(i <= nu, j == 0))
    def _():
        whole_tile_wait(gsem, cur)

    for step in range(MOE_TILE // rows_per_step):
        @pl.when(jnp.logical_and(i < nu, j == step))
        def _(step=step):
            for row in range(step * rows_per_step, (step + 1) * rows_per_step):
                gather_row(src_ref, row, nxt).start()
                scatter_row(row, nxt).start(priority=row % 2)

    @pl.when(i < nu)
    def _():
        h = xbuf[cur].astype(BF16)
        g = _bdot(h, wg_ref[...])
        u = _bdot(h, wu_ref[...])
        part = _bdot((g * jax.nn.sigmoid(g) * u).astype(BF16), wd_ref[...])
        acc[...] = jnp.where(j == 0, part, acc[...] + part)

    @pl.when(i >= nu)
    def _():
        def tail(c, carry):
            scatter_row(row0 + c, nxt).start()
            return carry

        lax.fori_loop(0, rows_per_step, tail, 0)

    @pl.when(jnp.logical_and(i >= 1, j == last_j))
    def _():
        whole_tile_wait(ssem, cur)

    @pl.when(jnp.logical_and(i < nu, j == last_j))
    def _():
        obuf[cur] = acc[...]

    @pl.when(jnp.logical_and(i == pl.num_programs(0) - 1, j == last_j))
    def _():
        whole_tile_wait(ssem, nxt)


def _combine_kernel(ya_ref, yb_ref, rw_ref, x_ref, mod_ref, gpost_ref, o_ref):
    rw = rw_ref[...]
    y = rw[:, 0:1] * ya_ref[...] + rw[:, 1:2] * yb_ref[...]
    o_ref[...] = x_ref[...] + mod_ref[5:6, :] * _rms(y, gpost_ref[...])


def _moe(hf, rw, ri, counts, xs, mod, gpost, wg, wu, wd, nlat):
    t_rows = hf.shape[0]
    d_ff = wg.shape[2]
    n_tiles = 2 * t_rows // MOE_TILE + N_EXPERTS
    nff = d_ff // MOE_FF_TILE
    n_row_tiles = t_rows // ROW_TILE
    n_rows = n_tiles * MOE_TILE
    src_token, pair_of, tile_expert, n_used = _moe_plan(ri, counts, n_tiles)
    spare = jnp.arange(MOE_TILE, dtype=jnp.int32)
    src_tiles = jnp.concatenate([src_token, 0 * spare]).reshape(n_tiles + 1, 1, MOE_TILE)
    dst_tiles = jnp.concatenate([n_rows + spare, pair_of]).reshape(n_tiles + 1, 1, MOE_TILE)
    step_expert = jnp.concatenate([tile_expert, tile_expert[-1:]])
    any_spec = pl.BlockSpec(memory_space=pl.ANY)
    idx_block = (None, 1, MOE_TILE)

    def ff_blk(i, j, nu):
        return jnp.where(i < nu[0], j, nff - 1)

    y_pairs = pl.pallas_call(
        functools.partial(_moe_expert_kernel, rows_per_step=MOE_TILE // nff),
        grid_spec=pltpu.PrefetchScalarGridSpec(
            num_scalar_prefetch=2,
            grid=(n_tiles + 1, nff),
            in_specs=[pl.BlockSpec(idx_block, lambda i, j, te, nu: (0, 0, 0), memory_space=pltpu.SMEM),
                      pl.BlockSpec(idx_block, lambda i, j, te, nu: (jnp.minimum(i + 1, n_tiles), 0, 0),
                                   memory_space=pltpu.SMEM),
                      pl.BlockSpec(idx_block, lambda i, j, te, nu: (i, 0, 0), memory_space=pltpu.SMEM),
                      any_spec,
                      pl.BlockSpec((None, D_MODEL, MOE_FF_TILE), lambda i, j, te, nu: (te[i], 0, ff_blk(i, j, nu))),
                      pl.BlockSpec((None, D_MODEL, MOE_FF_TILE), lambda i, j, te, nu: (te[i], 0, ff_blk(i, j, nu))),
                      pl.BlockSpec((None, MOE_FF_TILE, D_MODEL), lambda i, j, te, nu: (te[i], ff_blk(i, j, nu), 0))],
            out_specs=any_spec,
            scratch_shapes=[pltpu.VMEM((2, MOE_TILE, D_MODEL), F32), pltpu.VMEM((2, MOE_TILE, D_MODEL), F32),
                            pltpu.VMEM((MOE_TILE, D_MODEL), F32),
                            pltpu.SemaphoreType.DMA((2,)), pltpu.SemaphoreType.DMA((2,))]),
        out_shape=jax.ShapeDtypeStruct((n_rows + MOE_TILE, D_MODEL), F32),
        compiler_params=_cparams(("arbitrary", "arbitrary")),
        name="moe_experts",
    )(step_expert, n_used, src_tiles, src_tiles, dst_tiles, hf, wg, wu, wd)

    row = lambda w: pl.BlockSpec((ROW_TILE, w), lambda i: (i, 0))
    return pl.pallas_call(
        _combine_kernel,
        grid=(n_row_tiles,),
        in_specs=[row(D_MODEL), pl.BlockSpec((ROW_TILE, D_MODEL), lambda i: (n_row_tiles + i, 0)), row(LANE), row(D_MODEL),
                  pl.BlockSpec((None, 6, D_MODEL), lambda i: (lax.div(i, nlat), 0, 0)), _full((1, D_MODEL))],
        out_specs=row(D_MODEL),
        out_shape=jax.ShapeDtypeStruct((t_rows, D_MODEL), F32),
        compiler_params=_cparams(("parallel",)),
        name="moe_combine",
    )(y_pairs, y_pairs, rw, xs, mod, gpost)


def _rope_tables(n_lat, n_ctx):
    rows = n_lat // GRID_W
    row = jnp.repeat(jnp.arange(rows, dtype=F32), GRID_W)
    col = jnp.tile(jnp.arange(GRID_W, dtype=F32), rows)
    n_freq = HEAD_DIM // 4
    inv = ROPE_THETA ** (-jnp.arange(n_freq, dtype=F32) / n_freq)
    ang = jnp.concatenate([row[:, None] * inv, col[:, None] * inv], axis=-1)
    cos, sin = jnp.cos(ang), jnp.sin(ang)
    cos_h = jnp.concatenate([cos, cos], axis=-1)
    sin_h = jnp.concatenate([-sin, sin], axis=-1)
    cos_t = jnp.concatenate([jnp.tile(cos_h, (1, N_Q_HEADS)), jnp.ones((n_ctx, ATTN_WIDTH), F32)], axis=0)
    sin_t = jnp.concatenate([jnp.tile(sin_h, (1, N_Q_HEADS)), jnp.zeros((n_ctx, ATTN_WIDTH), F32)], axis=0)
    return cos_t, sin_t


def kernel(x, c, ctx, c_ctx, ada_w, ada_b, norm_mix_pre, norm_mix_post, norm_ffn_pre, norm_ffn_post, ev_w_in, ev_hy_conv_w, ev_hy_conv_b, ev_hy_f_w1, ev_hy_f_b1, ev_hy_f_w2, ev_hy_f_b2, ev_hy_f_wout, ev_hy_freq, ev_hy_skip, ev_q_norm, ev_k_norm, ev_w_out, ev_ffn_w_gate, ev_ffn_w_up, ev_ffn_w_down, od_w_in, od_s5_lambda_re, od_s5_lambda_im, od_s5_log_step, od_s5_b_re, od_s5_b_im, od_s5_c_re, od_s5_c_im, od_s5_d, od_glu_w_a, od_glu_w_b, od_router, od_moe_w_gate, od_moe_w_up, od_moe_w_down):
    n_batch, n_lat, _ = x.shape
    n_ctx = ctx.shape[1]
    depth = ada_w.shape[0]
    assert n_batch == 8 and n_lat % ROW_TILE == 0 and n_ctx % ROW_TILE == 0 and n_lat % n_ctx == 0
    assert depth == 2
    nlat = n_lat // ROW_TILE

    cond = jnp.concatenate([c, c_ctx[None, :], jnp.zeros((16 - n_batch - 1, D_MODEL), F32)], axis=0)
    mods = _ada_params(cond, ada_w, ada_b)
    vec = lambda a: a[None, :]

    cos_t, sin_t = _rope_tables(n_lat, n_ctx)
    hy, q, k4, v4 = _inproj_even(x, ctx, mods[0], vec(norm_mix_pre[0]), ev_w_in[0].astype(BF16),
                                 vec(jnp.tile(ev_q_norm[0], N_Q_HEADS)), vec(jnp.tile(ev_k_norm[0], N_KV_HEADS)),
                                 cos_t, sin_t, nlat)
    lt = n_lat + n_ctx
    a_lat = _attention(q, k4, v4, n_lat, min(ATTN_Q_TILE, n_lat), 0, lt, 0)
    a_ctx = _attention(q, k4, v4, n_ctx, n_ctx, n_lat // n_ctx, n_ctx, n_lat // n_ctx)
    filt_args = (ev_hy_f_w1[0], ev_hy_f_b1[0], ev_hy_f_w2[0], ev_hy_f_b2[0], ev_hy_f_wout[0], ev_hy_freq[0])
    y_hy_lat = _hyena_mixer(hy, 0, n_lat, filt_args, ev_hy_conv_w[0], ev_hy_conv_b[0], ev_hy_skip[0])
    y_hy_ctx = _hyena_mixer(hy, n_lat // n_ctx, n_ctx, filt_args, ev_hy_conv_w[0], ev_hy_conv_b[0], ev_hy_skip[0])
    xs = _outproj(y_hy_lat, y_hy_ctx, a_lat, a_ctx, x, ctx, mods[0], vec(norm_mix_post[0]), ev_w_out[0].astype(BF16), nlat)
    xs = _ffn_dense(xs, mods[0], vec(norm_ffn_pre[0]), vec(norm_ffn_post[0]), ev_ffn_w_gate[0].astype(BF16),
                    ev_ffn_w_up[0].astype(BF16), ev_ffn_w_down[0].astype(BF16), nlat)

    u3 = _inproj_odd(xs, mods[1], vec(norm_mix_pre[1]), od_w_in[0].astype(BF16), n_lat)
    s5_ops = _s5_params(od_s5_lambda_re[0], od_s5_lambda_im[0], od_s5_log_step[0],
                        od_s5_b_re[0], od_s5_b_im[0], od_s5_c_re[0], od_s5_c_im[0])
    y = _s5_scan(u3, *s5_ops, n_lat)
    x_lat, hf, rw, ri, counts = _glu(y, u3, vec(od_s5_d[0]), od_glu_w_a[0].astype(BF16), od_glu_w_b[0].astype(BF16),
                                     xs, mods[1], vec(norm_mix_post[1]), vec(norm_ffn_pre[1]), od_router[0], n_lat)
    t_rows = n_batch * n_lat
    out = _moe(hf.reshape(t_rows, D_MODEL), rw.reshape(t_rows, LANE), ri.reshape(t_rows, LANE), counts,
               x_lat.reshape(t_rows, D_MODEL), mods[1], vec(norm_ffn_post[1]), od_moe_w_gate[0].astype(BF16),
               od_moe_w_up[0].astype(BF16), od_moe_w_down[0].astype(BF16), nlat)
    return out.reshape(n_batch, n_lat, D_MODEL)
```

```python
import functools
import math

import jax
import jax.numpy as jnp
import numpy as np
from jax import lax
from jax.experimental import pallas as pl
from jax.experimental.pallas import tpu as pltpu

F32 = jnp.float32
BF16 = jnp.bfloat16

D_MODEL = 1024
EPS = 1e-6
GRID_W = 64

HY_WIDTH = 512
HY_ORDER = 2
HY_IN = (HY_ORDER + 1) * HY_WIDTH
HY_BANDS = 16
HY_EMB = 2 * HY_BANDS + 1
HY_DECAY_SLOW = -math.log(1e-2) / 1.5
HY_DECAY_FAST = -math.log(1e-2) / 0.3
HEAD_DIM = 64
N_Q_HEADS = 8
N_KV_HEADS = 2
Q_PER_KV = N_Q_HEADS // N_KV_HEADS
ATTN_WIDTH = N_Q_HEADS * HEAD_DIM
KV_WIDTH = N_KV_HEADS * HEAD_DIM
ROPE_THETA = 10000.0
LOG2_E = 1.0 / math.log(2.0)

S5_GROUP = 16
S5_GROUPS = D_MODEL // S5_GROUP
S5_STATE = 64
S5_LANES = S5_GROUPS * S5_STATE
S5_BLOCK_GROUPS = 8
N_EXPERTS = 8

ROW_TILE = 256
FFN_SUBTILES = 2
ATTN_Q_TILE = 512
FREQ_TILE = 512
DFT_SPLIT = 128
S5_CHUNK = 64
S5_SCAN_CHUNK = 64
MOE_TILE = 512
MOE_FF_TILE = 1792
LANE = 128
MIB = 1024 * 1024


def _cparams(sem, vmem_mib=48):
    return pltpu.CompilerParams(dimension_semantics=sem, vmem_limit_bytes=vmem_mib * MIB)


def _rms(x, gain):
    return x * lax.rsqrt(jnp.mean(x * x, axis=-1, keepdims=True) + EPS) * gain


def _bdot(a, b):
    return jnp.dot(a, b, preferred_element_type=F32)


def _full(shape):
    zeros = (0,) * len(shape)
    return pl.BlockSpec(shape, lambda *_: zeros)


def _resident(shape):
    zeros = (0,) * len(shape)
    return pl.BlockSpec(shape, lambda *_: zeros, pipeline_mode=pl.Buffered(1))


def _ada_kernel(cond_ref, w_ref, b_ref, o_ref):
    c = cond_ref[...]
    s = (c * jax.nn.sigmoid(c)).astype(BF16)
    o_ref[...] = _bdot(s, w_ref[...].astype(BF16)) + b_ref[...]


def _ada_params(cond, ada_w, ada_b):
    depth, _, n6 = ada_w.shape
    rows = cond.shape[0]
    tn = 1536
    out = pl.pallas_call(
        _ada_kernel,
        grid=(depth, n6 // tn),
        in_specs=[pl.BlockSpec((rows, D_MODEL), lambda i, j: (0, 0)),
                  pl.BlockSpec((None, D_MODEL, tn), lambda i, j: (i, 0, j)),
                  pl.BlockSpec((None, 1, tn), lambda i, j: (i, 0, j))],
        out_specs=pl.BlockSpec((None, rows, tn), lambda i, j: (i, 0, j)),
        out_shape=jax.ShapeDtypeStruct((depth, rows, n6), F32),
        compiler_params=_cparams(("arbitrary", "arbitrary")),
        name="ada_params",
    )(cond, ada_w, ada_b.reshape(depth, 1, n6))
    return out.reshape(depth, rows, 6, D_MODEL)


def _rope_rotate(t):
    w = t.shape[-1]
    lane = lax.broadcasted_iota(jnp.int32, t.shape, 1)
    first = (lane & (HEAD_DIM - 1)) < HEAD_DIM // 2
    return jnp.where(first, pltpu.roll(t, w - HEAD_DIM // 2, 1), pltpu.roll(t, HEAD_DIM // 2, 1))


def _head_slots(t):
    lane = lax.broadcasted_iota(jnp.int32, t.shape, 1)
    lo = jnp.where(lane < HEAD_DIM, t, 0.0)
    hi = jnp.where(lane >= HEAD_DIM, t, 0.0)
    return jnp.concatenate([lo, pltpu.roll(lo, HEAD_DIM, 1), pltpu.roll(hi, HEAD_DIM, 1), hi], axis=-1)


def _inproj_even_kernel(x_ref, c_ref, mod_ref, gain_ref, w_ref, qg_ref, kg_ref, e_ref, cos_ref, sin_ref,
                        hy_ref, q_ref, k4_ref, v4_ref, *, nlat):
    xin = jnp.where(pl.program_id(1) >= nlat, c_ref[...], x_ref[...])
    h = _rms(xin, gain_ref[...]) * (1.0 + mod_ref[1:2, :]) + mod_ref[0:1, :]
    p = _bdot(h.astype(BF16), w_ref[...])
    hy_ref[...] = p[:, :HY_IN].astype(hy_ref.dtype)
    q = p[:, HY_IN:HY_IN + ATTN_WIDTH]
    k = p[:, HY_IN + ATTN_WIDTH:HY_IN + ATTN_WIDTH + KV_WIDTH]
    v = p[:, HY_IN + ATTN_WIDTH + KV_WIDTH:]
    e = e_ref[...]
    qn = q * lax.rsqrt(_bdot((q * q).astype(BF16), e) + EPS) * qg_ref[...]
    kn = k * lax.rsqrt(_bdot((k * k).astype(BF16), e[:KV_WIDTH, :KV_WIDTH]) + EPS) * kg_ref[...]
    cos = cos_ref[...]
    sin = sin_ref[...]
    qr = (qn * cos + _rope_rotate(qn) * sin) * (HEAD_DIM ** -0.5 * LOG2_E)
    kr = kn * cos[:, :KV_WIDTH] + _rope_rotate(kn) * sin[:, :KV_WIDTH]
    q_ref[...] = qr.astype(q_ref.dtype)
    k4_ref[...] = _head_slots(kr).astype(k4_ref.dtype)
    v4_ref[...] = _head_slots(v).astype(v4_ref.dtype)


def _mod_spec(n_batch, nlat):
    return pl.BlockSpec((None, 6, D_MODEL), lambda b, j: (jnp.where(j >= nlat, n_batch, b), 0, 0))


def _row_spec(width):
    return pl.BlockSpec((None, ROW_TILE, width), lambda b, j: (b, j, 0))


def _lat_ctx_specs(nlat, nctx, width):
    return [pl.BlockSpec((None, ROW_TILE, width), lambda b, j: (b, jnp.minimum(j, nlat - 1), 0)),
            pl.BlockSpec((None, ROW_TILE, width), lambda b, j: (b, jnp.clip(j - nlat, 0, nctx - 1), 0))]


def _inproj_even(x, ctx, mod, gain, w_in, q_gain, k_gain, cos_t, sin_t, nlat):
    n_batch = x.shape[0]
    nctx = ctx.shape[1] // ROW_TILE
    ntile = nlat + nctx
    lt = ntile * ROW_TILE
    n_out = w_in.shape[1]
    head_avg = jnp.asarray(np.kron(np.eye(N_Q_HEADS), np.full((HEAD_DIM, HEAD_DIM), 1.0 / HEAD_DIM)), BF16)
    table = pl.BlockSpec((ROW_TILE, ATTN_WIDTH), lambda b, j: (j, 0))
    outs = [jax.ShapeDtypeStruct((n_batch, lt, HY_IN), BF16)] + [jax.ShapeDtypeStruct((n_batch, lt, ATTN_WIDTH), BF16)] * 3
    return pl.pallas_call(
        functools.partial(_inproj_even_kernel, nlat=nlat),
        grid=(n_batch, ntile),
        in_specs=_lat_ctx_specs(nlat, nctx, D_MODEL) + [
            _mod_spec(n_batch, nlat), _full((1, D_MODEL)), _full((D_MODEL, n_out)),
            _full((1, ATTN_WIDTH)), _full((1, KV_WIDTH)), _full((ATTN_WIDTH, ATTN_WIDTH)), table, table],
        out_specs=[_row_spec(HY_IN), _row_spec(ATTN_WIDTH), _row_spec(ATTN_WIDTH), _row_spec(ATTN_WIDTH)],
        out_shape=outs,
        compiler_params=_cparams(("parallel", "parallel")),
        name="inproj_even",
    )(x, ctx, mod, gain, w_in, q_gain, k_gain, head_avg, cos_t, sin_t)


def _attn_kernel(q_ref, k4_ref, v4_ref, o_ref):
    for blk in range(N_Q_HEADS // 2):
        g = (2 * blk) // Q_PER_KV
        qp = q_ref[:, LANE * blk:LANE * (blk + 1)]
        acc = None
        for r in range(2):
            slot = LANE * (2 * g + r)
            s = lax.dot_general(qp, k4_ref[:, slot:slot + LANE], (((1,), (1,)), ((), ())), preferred_element_type=F32)
            e = jnp.exp2(s - jnp.max(s, axis=-1, keepdims=True))
            inv = 1.0 / jnp.sum(e, axis=-1, keepdims=True)
            o = _bdot(e.astype(BF16), v4_ref[:, slot:slot + LANE]) * inv
            acc = o if acc is None else acc + o
        o_ref[:, LANE * blk:LANE * (blk + 1)] = acc.astype(o_ref.dtype)


def _attention(q, k4, v4, q_rows, q_tile, q_blk0, k_rows, k_blk):
    n_batch = q.shape[0]
    kv_spec = pl.BlockSpec((None, k_rows, ATTN_WIDTH), lambda b, j: (b, k_blk, 0))
    return pl.pallas_call(
        _attn_kernel,
        grid=(n_batch, q_rows // q_tile),
        in_specs=[pl.BlockSpec((None, q_tile, ATTN_WIDTH), lambda b, j: (b, q_blk0 + j, 0)), kv_spec, kv_spec],
        out_specs=pl.BlockSpec((None, q_tile, ATTN_WIDTH), lambda b, j: (b, j, 0)),
        out_shape=jax.ShapeDtypeStruct((n_batch, q_rows, ATTN_WIDTH), BF16),
        compiler_params=_cparams(("parallel", "parallel")),
        name="attention",
    )(q, k4, v4)


def _filter_kernel(z_ref, w1_ref, b1_ref, w2_ref, b2_ref, wo_ref, fr_ref, dl_ref, o_ref):
    hi = lax.Precision.HIGHEST
    z = z_ref[...]
    fr = fr_ref[...]
    h = jnp.sin(fr * (jnp.dot(z, w1_ref[...], precision=hi, preferred_element_type=F32) + b1_ref[...]))
    h = jnp.sin(fr * (jnp.dot(h, w2_ref[...], precision=hi, preferred_element_type=F32) + b2_ref[...]))
    h = jnp.dot(h, wo_ref[...], precision=hi, preferred_element_type=F32)
    decay = jnp.exp(-z[:, 0:1] * dl_ref[...])
    o_ref[...] = h * jnp.concatenate([decay] * (2 * HY_ORDER), axis=-1)


def _hyena_filters(seq, w1, b1, w2, b2, wout, freq):
    t = jnp.linspace(0.0, 1.0, seq, dtype=F32)[:, None]
    bands = jnp.linspace(1e-4, HY_BANDS - 1, HY_BANDS, dtype=F32)
    phase = (2.0 * math.pi / seq) * jnp.arange(seq, dtype=F32)[:, None] * bands
    z = jnp.concatenate([t, jnp.cos(phase), -jnp.sin(phase)], axis=-1)
    z = jnp.pad(z, ((0, 0), (0, LANE - HY_EMB)))
    w1p = jnp.pad(w1, ((0, LANE - HY_EMB), (0, 0)))
    deltas = jnp.linspace(HY_DECAY_SLOW, HY_DECAY_FAST, HY_WIDTH, dtype=F32)[None, :]
    hid = w1.shape[1]
    n_out = wout.shape[1]
    tl = ROW_TILE
    return pl.pallas_call(
        _filter_kernel,
        grid=(seq // tl,),
        in_specs=[pl.BlockSpec((tl, LANE), lambda i: (i, 0)), _full((LANE, hid)), _full((1, hid)), _full((hid, hid)),
                  _full((1, hid)), _full((hid, n_out)), _full((1, hid)), _full((1, HY_WIDTH))],
        out_specs=pl.BlockSpec((tl, n_out), lambda i: (i, 0)),
        out_shape=jax.ShapeDtypeStruct((seq, n_out), F32),
        compiler_params=_cparams(("parallel",)),
        name="hyena_filter",
    )(z, w1p, b1[None, :], w2, b2[None, :], wout, freq[None, :], deltas)


def _freq_tile(seq):
    return min(FREQ_TILE, seq)


def _dft_tables(seq):
    n = 2 * seq
    unit = 2.0 * math.pi / n
    col = jnp.arange(seq, dtype=jnp.int32)[None, :]
    r1 = jnp.arange(seq // DFT_SPLIT, dtype=jnp.int32)[:, None] * DFT_SPLIT
    r2 = jnp.arange(DFT_SPLIT, dtype=jnp.int32)[:, None]
    ang1 = ((r1 * col) % n).astype(F32) * unit
    ang2 = ((r2 * col) % n).astype(F32) * unit
    c1, s1 = jnp.cos(ang1)[:, None, :], jnp.sin(ang1)[:, None, :]
    c2, s2 = jnp.cos(ang2)[None, :, :], jnp.sin(ang2)[None, :, :]
    cos = (c1 * c2 - s1 * s2).reshape(seq, seq)
    sin = (s1 * c2 + c1 * s2).reshape(seq, seq)
    idx = jnp.arange(seq, dtype=jnp.int32)
    nyquist = jnp.where(idx % 2 == 0, 1.0, -1.0)
    sin_rows = jnp.where(idx[:, None] == 0, nyquist[None, :], sin)
    sin_cols = jnp.where(idx[None, :] == 0, nyquist[:, None], sin)
    tf = _freq_tile(seq)
    nf = seq // tf
    cos, sin_rows, sin_cols = cos.astype(BF16), sin_rows.astype(BF16), sin_cols.astype(BF16)
    tiles = [slice(i * tf, (i + 1) * tf) for i in range(nf)]
    fwd = jnp.concatenate([m[s, :] for s in tiles for m in (cos, sin_rows)], axis=0)
    inv = jnp.concatenate([m[:, s] for s in tiles for m in (cos, sin_cols)], axis=1)
    return fwd, inv


def _kfreq_kernel(hf_ref, hb_ref, f_ref, kre_ref, ks_ref):
    fi = pl.program_id(1)
    seq = hf_ref.shape[0]
    hf = hf_ref[...]
    row = lax.broadcasted_iota(jnp.int32, hf.shape, 0)
    hb = jnp.where(row == 0, 0.0, hb_ref[...])
    f = f_ref[...]
    a1 = _bdot(f, (hf + hb).astype(BF16))
    a2 = _bdot(f, (hf - hb).astype(BF16))
    tf = f_ref.shape[0] // 2
    frow = lax.broadcasted_iota(jnp.int32, (tf, hf.shape[1]), 0)
    dc = jnp.logical_and(frow == 0, fi == 0)
    scale = jnp.where(dc, 1.0 / (2 * seq), 2.0 / (2 * seq))
    kre_ref[...] = a1[:tf] * scale
    ks_ref[...] = jnp.where(dc, a1[tf:], a2[tf:]) * scale


def _kfreq(hfilt, fwd_tab):
    seq = hfilt.shape[0]
    tf = _freq_tile(seq)
    nf = seq // tf
    out = jax.ShapeDtypeStruct((HY_ORDER, seq, HY_WIDTH), F32)
    ospec = pl.BlockSpec((None, tf, HY_WIDTH), lambda o, fi: (o, fi, 0))
    return pl.pallas_call(
        _kfreq_kernel,
        grid=(HY_ORDER, nf),
        in_specs=[pl.BlockSpec((seq, HY_WIDTH), lambda o, fi: (0, o)),
                  pl.BlockSpec((seq, HY_WIDTH), lambda o, fi: (0, HY_ORDER + o)),
                  pl.BlockSpec((2 * tf, seq), lambda o, fi: (fi, 0))],
        out_specs=[ospec, ospec],
        out_shape=[out, out],
        compiler_params=_cparams(("parallel", "parallel")),
        name="hyena_kfreq",
    )(hfilt, hfilt, fwd_tab)


def _short_conv(p, w, b):
    n = p.shape[0]
    row = lax.broadcasted_iota(jnp.int32, p.shape, 0)
    prev = jnp.where(row == 0, 0.0, pltpu.roll(p, 1, 0))
    nxt = jnp.where(row == n - 1, 0.0, pltpu.roll(p, n - 1, 0))
    return b + prev * w[0:1, :] + p * w[1:2, :] + nxt * w[2:3, :]


def _hyena_kernel(vsrc_ref, gsrc_ref, cw_ref, cb_ref, skip_ref, f_ref, ft_ref, kre_ref, ks_ref, o_ref,
                  v_scr, vb_scr, acc_scr, *, conv_v):
    fi = pl.program_id(1)
    tf = f_ref.shape[0] // 2

    @pl.when(fi == 0)
    def _():
        v = vsrc_ref[...].astype(F32)
        if conv_v:
            v = _short_conv(v, cw_ref[0], cb_ref[0])
        v_scr[...] = v
        vb_scr[...] = v.astype(BF16)
        acc_scr[...] = jnp.zeros_like(acc_scr)

    xf = _bdot(f_ref[...], vb_scr[...])
    xre, xs = xf[:tf], xf[tf:]
    kre, ks = kre_ref[...], ks_ref[...]
    row = lax.broadcasted_iota(jnp.int32, xre.shape, 0)
    dc = jnp.logical_and(row == 0, fi == 0)
    yre = jnp.where(dc, xre * kre, xre * kre - xs * ks)
    ys = jnp.where(dc, xs * ks, xre * ks + xs * kre)
    y = jnp.concatenate([yre, ys], axis=0).astype(BF16)
    acc_scr[...] += _bdot(ft_ref[...], y)

    @pl.when(fi == pl.num_programs(1) - 1)
    def _():
        gate = _short_conv(gsrc_ref[...].astype(F32), cw_ref[1], cb_ref[1])
        o_ref[...] = (gate * (acc_scr[...] + v_scr[...] * skip_ref[...])).astype(o_ref.dtype)


def _hyena_order(vsrc, v_blk, v_col, hy, row_blk, order, seq, conv_w, conv_b, skip, fwd_tab, inv_tab, kre, ks):
    n_batch = hy.shape[0]
    tf = _freq_tile(seq)
    nf = seq // tf
    conv_v = order == 0
    cw = jnp.stack([conv_w[:, :HY_WIDTH], conv_w[:, (order + 1) * HY_WIDTH:(order + 2) * HY_WIDTH]])
    cb = jnp.stack([conv_b[None, :HY_WIDTH], conv_b[None, (order + 1) * HY_WIDTH:(order + 2) * HY_WIDTH]])
    kspec = pl.BlockSpec((None, tf, HY_WIDTH), lambda b, fi: (order, fi, 0))
    return pl.pallas_call(
        functools.partial(_hyena_kernel, conv_v=conv_v),
        grid=(n_batch, nf),
        in_specs=[pl.BlockSpec((None, seq, HY_WIDTH), lambda b, fi: (b, v_blk, v_col)),
                  pl.BlockSpec((None, seq, HY_WIDTH), lambda b, fi: (b, row_blk, order + 1)),
                  _full((2, 3, HY_WIDTH)), _full((2, 1, HY_WIDTH)), _full((1, HY_WIDTH)),
                  pl.BlockSpec((2 * tf, seq), lambda b, fi: (fi, 0)),
                  pl.BlockSpec((seq, 2 * tf), lambda b, fi: (0, fi)),
                  kspec, kspec],
        out_specs=pl.BlockSpec((None, seq, HY_WIDTH), lambda b, fi: (b, 0, 0)),
        out_shape=jax.ShapeDtypeStruct((n_batch, seq, HY_WIDTH), BF16),
        scratch_shapes=[pltpu.VMEM((seq, HY_WIDTH), F32), pltpu.VMEM((seq, HY_WIDTH), BF16),
                        pltpu.VMEM((seq, HY_WIDTH), F32)],
        compiler_params=_cparams(("parallel", "arbitrary")),
        name="hyena_order%d" % order,
    )(vsrc, hy, cw, cb, skip[order][None, :], fwd_tab, inv_tab, kre, ks)


def _hyena_mixer(hy, row_blk, seq, filt_args, conv_w, conv_b, skip):
    hfilt = _hyena_filters(seq, *filt_args)
    fwd_tab, inv_tab = _dft_tables(seq)
    kre, ks = _kfreq(hfilt, fwd_tab)
    v1 = _hyena_order(hy, row_blk, 0, hy, row_blk, 0, seq, conv_w, conv_b, skip, fwd_tab, inv_tab, kre, ks)
    return _hyena_order(v1, 0, 0, hy, row_blk, 1, seq, conv_w, conv_b, skip, fwd_tab, inv_tab, kre, ks)


def _outproj_kernel(yl_ref, yc_ref, al_ref, ac_ref, x_ref, c_ref, mod_ref, gpost_ref, w_ref, o_ref, *, nlat):
    is_ctx = pl.program_id(1) >= nlat
    yh = jnp.where(is_ctx, yc_ref[...], yl_ref[...])
    ya = jnp.where(is_ctx, ac_ref[...], al_ref[...])
    out = _bdot(yh, w_ref[:HY_WIDTH, :]) + _bdot(ya, w_ref[HY_WIDTH:, :])
    o_ref[...] = jnp.where(is_ctx, c_ref[...], x_ref[...]) + mod_ref[2:3, :] * _rms(out, gpost_ref[...])


def _outproj(y_lat, y_ctx, a_lat, a_ctx, x, ctx, mod, gpost, w_out, nlat):
    n_batch = x.shape[0]
    nctx = ctx.shape[1] // ROW_TILE
    ntile = nlat + nctx
    return pl.pallas_call(
        functools.partial(_outproj_kernel, nlat=nlat),
        grid=(n_batch, ntile),
        in_specs=(_lat_ctx_specs(nlat, nctx, HY_WIDTH) + _lat_ctx_specs(nlat, nctx, ATTN_WIDTH)
                  + _lat_ctx_specs(nlat, nctx, D_MODEL)
                  + [_mod_spec(n_batch, nlat), _full((1, D_MODEL)), _full(w_out.shape)]),
        out_specs=_row_spec(D_MODEL),
        out_shape=jax.ShapeDtypeStruct((n_batch, ntile * ROW_TILE, D_MODEL), F32),
        compiler_params=_cparams(("parallel", "parallel")),
        name="outproj_even",
    )(y_lat, y_ctx, a_lat, a_ctx, x, ctx, mod, gpost, w_out)


def _ffn_kernel(x_ref, mod_ref, gpre_ref, gpost_ref, wg_ref, wu_ref, wd_ref, o_ref, *, n_batch, tiles_per_batch, nlat):
    n_sub = x_ref.shape[0] // ROW_TILE
    subs = []
    for s in range(n_sub):
        tile = pl.program_id(0) * n_sub + s
        batch = lax.div(tile, tiles_per_batch)
        mod = mod_ref[jnp.where(lax.rem(tile, tiles_per_batch) >= nlat, n_batch, batch)]
        x = x_ref[s * ROW_TILE:(s + 1) * ROW_TILE, :]
        subs.append((x, mod, (_rms(x, gpre_ref[...]) * (1.0 + mod[4:5, :]) + mod[3:4, :]).astype(BF16)))
    h = jnp.concatenate([sub[2] for sub in subs], axis=0)
    g = _bdot(h, wg_ref[...])
    u = _bdot(h, wu_ref[...])
    a = (g * jax.nn.sigmoid(g) * u).astype(BF16)
    y = _bdot(a, wd_ref[...])
    for s, (x, mod, _) in enumerate(subs):
        rows = slice(s * ROW_TILE, (s + 1) * ROW_TILE)
        o_ref[rows, :] = x + mod[5:6, :] * _rms(y[rows, :], gpost_ref[...])


def _ffn_dense(xs, mod, gpre, gpost, wg, wu, wd, nlat):
    n_batch, lt, _ = xs.shape
    tiles_per_batch = lt // ROW_TILE
    rows = FFN_SUBTILES * ROW_TILE
    n_steps = n_batch * tiles_per_batch // FFN_SUBTILES
    blk = pl.BlockSpec((rows, D_MODEL), lambda i: (i, 0))
    out = pl.pallas_call(
        functools.partial(_ffn_kernel, n_batch=n_batch, tiles_per_batch=tiles_per_batch, nlat=nlat),
        grid=(n_steps,),
        in_specs=[blk, _resident(mod.shape), _resident((1, D_MODEL)), _resident((1, D_MODEL)),
                  _resident(wg.shape), _resident(wu.shape), _resident(wd.shape)],
        out_specs=blk,
        out_shape=jax.ShapeDtypeStruct((n_batch * lt, D_MODEL), F32),
        compiler_params=_cparams(("parallel",), 56),
        name="ffn_dense",
    )(xs.reshape(n_batch * lt, D_MODEL), mod, gpre, gpost, wg, wu, wd)
    return out.reshape(xs.shape)


def _mod_rows(mod_ref, k, n_batch, is_ctx):
    return jnp.where(is_ctx, mod_ref[n_batch:n_batch + 1, k:k + 1, :], mod_ref[0:n_batch, k:k + 1, :])


def _row_order_swap(n_outer, n_inner):
    n = n_outer * n_inner
    dst = np.arange(n)
    src = (dst % n_outer) * n_inner + dst // n_outer
    return jnp.asarray(src[:, None] == np.arange(n)[None, :], BF16)


def _inproj_odd_kernel(x_ref, mod_ref, gain_ref, swap_ref, w_ref, u_ref, *, n_lat_steps):
    n_batch, steps, _ = x_ref.shape
    is_ctx = pl.program_id(0) >= n_lat_steps
    h = (_rms(x_ref[...], gain_ref[...]) * (1.0 + _mod_rows(mod_ref, 1, n_batch, is_ctx))
         + _mod_rows(mod_ref, 0, n_batch, is_ctx))
    h = h.reshape(n_batch * steps, D_MODEL).astype(BF16)
    h = _bdot(swap_ref[...], h).astype(BF16)
    u_ref[...] = _bdot(h, w_ref[...]).reshape(steps, n_batch, D_MODEL)


def _inproj_odd(xs, mod, gain, w_in, n_lat):
    n_batch, lt, _ = xs.shape
    p = S5_CHUNK
    swap = _row_order_swap(n_batch, p)
    return pl.pallas_call(
        functools.partial(_inproj_odd_kernel, n_lat_steps=n_lat // p),
        grid=(lt // p,),
        in_specs=[pl.BlockSpec((n_batch, p, D_MODEL), lambda i: (0, i, 0)), _full(mod.shape), _full((1, D_MODEL)),
                  _full(swap.shape), _full(w_in.shape)],
        out_specs=pl.BlockSpec((p, n_batch, D_MODEL), lambda i: (i, 0, 0)),
        out_shape=jax.ShapeDtypeStruct((lt, n_batch, D_MODEL), F32),
        compiler_params=_cparams(("parallel",)),
        name="inproj_odd",
    )(xs, mod, gain, swap, w_in)


def _s5_param_kernel(lr_ref, li_ref, ls_ref, lrx_ref, lix_ref, lsx_ref, br_ref, bi_ref,
                     abr_ref, abi_ref, bbr_ref, bbi_ref):
    def zoh(lr_raw, li, log_step):
        lr = jnp.minimum(lr_raw, -1e-4)
        dt = jnp.exp(log_step)
        mag = jnp.exp(lr * dt)
        ab_re = mag * jnp.cos(li * dt)
        ab_im = mag * jnp.sin(li * dt)
        den = lr * lr + li * li
        nr, ni = ab_re - 1.0, ab_im
        return ab_re, ab_im, (nr * lr + ni * li) / den, (ni * lr - nr * li) / den

    ab_re, ab_im, _, _ = zoh(lr_ref[...], li_ref[...], ls_ref[...])
    abr_ref[...] = ab_re
    abi_ref[...] = ab_im
    _, _, co_re, co_im = zoh(lrx_ref[...], lix_ref[...], lsx_ref[...])
    br, bi = br_ref[...], bi_ref[...]
    bbr_ref[...] = co_re * br - co_im * bi
    bbi_ref[...] = co_re * bi + co_im * br


def _s5_params(lam_re, lam_im, log_step, b_re, b_im, c_re, c_im):
    nd, g, n = lam_re.shape
    k = S5_GROUP
    rep = lambda a: jnp.repeat(a, k, axis=1)
    ls = log_step[:, :, None]
    bt_re = jnp.swapaxes(b_re, 2, 3).reshape(nd, g * k, n)
    bt_im = jnp.swapaxes(b_im, 2, 3).reshape(nd, g * k, n)
    small = jax.ShapeDtypeStruct((nd, g, n), F32)
    big = jax.ShapeDtypeStruct((nd, g * k, n), F32)
    ab_re, ab_im, bb_re, bb_im = pl.pallas_call(
        _s5_param_kernel, out_shape=[small, small, big, big], name="s5_discretise",
    )(lam_re, lam_im, ls, rep(lam_re), rep(lam_im), rep(ls), bt_re, bt_im)
    bb_re = bb_re.reshape(nd, g, k, n)
    bb_im = bb_im.reshape(nd, g, k, n)
    cmul = lambda xr, xi, yr, yi: (xr * yr - xi * yi, xr * yi + xi * yr)
    a_re, a_im = ab_re[:, :, None, :], ab_im[:, :, None, :]
    a2_re, a2_im = cmul(a_re, a_im, a_re, a_im)
    abb_re, abb_im = cmul(a_re, a_im, bb_re, bb_im)
    ca_re, ca_im = cmul(c_re, c_im, a_re, a_im)
    ca2_re, ca2_im = cmul(c_re, c_im, a2_re, a2_im)
    real_cb = lambda xr, xi: jnp.einsum('dgin,dgkn->dgik', xr, bb_re) - jnp.einsum('dgin,dgkn->dgik', xi, bb_im)
    k0 = real_cb(c_re, c_im)
    k1 = real_cb(ca_re, ca_im)
    a2 = jnp.stack([a2_re.reshape(nd, g * n), a2_im.reshape(nd, g * n)], axis=1)
    nq = g // S5_BLOCK_GROUPS

    def block_diag(parts):
        m = jnp.stack(parts)
        r, c = m.shape[-2:]
        mask = np.kron(np.eye(S5_BLOCK_GROUPS), np.ones((r, c)))
        rows = m.reshape(len(parts), nd, nq, S5_BLOCK_GROUPS * r, c)
        return jnp.tile(rows, (1, 1, 1, 1, S5_BLOCK_GROUPS)) * jnp.asarray(mask, F32)

    t = lambda x: jnp.swapaxes(x, 2, 3)
    i2s = block_diag([abb_re, abb_im, bb_re, bb_im])
    w_in = jnp.concatenate([jnp.concatenate([i2s[0], i2s[1]], axis=-1),
                            jnp.concatenate([i2s[2], i2s[3]], axis=-1)], axis=-2).astype(BF16)
    s2o = block_diag([t(ca_re), t(ca2_re), t(-ca_im), t(-ca2_im)])
    w_state = jnp.stack([jnp.concatenate([s2o[0], s2o[1]], axis=-1),
                         jnp.concatenate([s2o[2], s2o[3]], axis=-1)], axis=2).astype(BF16)
    i2o = block_diag([t(k0), t(k1)])
    w_dir = jnp.concatenate([jnp.concatenate([i2o[0], i2o[1]], axis=-1),
                             jnp.concatenate([jnp.zeros_like(i2o[0]), i2o[0]], axis=-1)], axis=-2).astype(BF16)
    return a2, w_in, w_state, w_dir


def _s5_scan_kernel(u_ref, a2_ref, win_ref, wst_ref, wdir_ref, y_ref, sbuf, state, *, nctx_chunks):
    d = pl.program_id(0)
    i = pl.program_id(1)
    p_steps, n_batch, _ = u_ref.shape
    npair = p_steps // 2
    rows = npair * n_batch
    half = S5_LANES
    nq = win_ref.shape[0]
    kq = win_ref.shape[1] // 2
    sq = win_ref.shape[2] // 2

    @pl.when(i == 0)
    def _():
        state[...] = jnp.zeros_like(state)

    fwd = d == 0
    u = u_ref[...].reshape(npair, 2, n_batch, D_MODEL)
    u_even = u[:, 0].reshape(rows, D_MODEL).astype(BF16)
    u_odd = u[:, 1].reshape(rows, D_MODEL).astype(BF16)
    u_1 = jnp.where(fwd, u_even, u_odd)
    u_2 = jnp.where(fwd, u_odd, u_even)
    pair_in = lambda q: jnp.concatenate([u_1[:, kq * q:kq * (q + 1)], u_2[:, kq * q:kq * (q + 1)]], axis=-1)
    for q in range(nq):
        r = _bdot(pair_in(q), win_ref[q])
        sbuf[:, :, sq * q:sq * (q + 1)] = r[:, :sq].reshape(npair, n_batch, sq)
        sbuf[:, :, half + sq * q:half + sq * (q + 1)] = r[:, sq:].reshape(npair, n_batch, sq)

    for q in range(nq):
        lo = sq * q
        ar = jnp.broadcast_to(a2_ref[0:1, lo:lo + sq], (n_batch, sq))
        ai = jnp.broadcast_to(a2_ref[1:2, lo:lo + sq], (n_batch, sq))

        def body(t, carry, lo=lo, ar=ar, ai=ai):
            sr, si = carry
            tt = jnp.where(d == 0, t, npair - 1 - t)
            nr = ar * sr - ai * si + sbuf[tt, :, lo:lo + sq]
            ni = ar * si + ai * sr + sbuf[tt, :, half + lo:half + lo + sq]
            sbuf[tt, :, lo:lo + sq] = sr
            sbuf[tt, :, half + lo:half + lo + sq] = si
            return nr, ni

        sr, si = lax.fori_loop(0, npair, body, (state[:, lo:lo + sq], state[:, half + lo:half + lo + sq]), unroll=4)
        state[:, lo:lo + sq] = sr
        state[:, half + lo:half + lo + sq] = si

    @pl.when(i >= nctx_chunks)
    def _():
        s = sbuf[...].reshape(rows, 2 * half).astype(BF16)
        for q in range(nq):
            yq = (_bdot(s[:, sq * q:sq * (q + 1)], wst_ref[q, 0])
                  + _bdot(s[:, half + sq * q:half + sq * (q + 1)], wst_ref[q, 1])
                  + _bdot(pair_in(q), wdir_ref[q]))
            y_1 = yq[:, :kq].reshape(npair, 1, n_batch, kq)
            y_2 = yq[:, kq:].reshape(npair, 1, n_batch, kq)
            y_pair = jnp.concatenate([jnp.where(fwd, y_1, y_2), jnp.where(fwd, y_2, y_1)], axis=1)
            y_ref[:, :, kq * q:kq * (q + 1)] = y_pair.reshape(p_steps, n_batch, kq)


def _s5_scan(u3, a2, w_in, w_state, w_dir, n_lat):
    lt, n_batch, _ = u3.shape
    p = S5_SCAN_CHUNK
    nchunk = lt // p
    nlatc = n_lat // p
    nctxc = nchunk - nlatc

    def u_map(d, i):
        return (jnp.where(d == 0, lax.rem(i + nlatc, nchunk), nchunk - 1 - i), 0, 0)

    def y_map(d, i):
        return (d, jnp.where(d == 0, jnp.maximum(i - nctxc, 0), jnp.minimum(nchunk - 1 - i, nlatc - 1)), 0, 0)

    per_dir = lambda w: pl.BlockSpec((None,) + w.shape[1:], lambda d, i: (d,) + (0,) * (w.ndim - 1))
    return pl.pallas_call(
        functools.partial(_s5_scan_kernel, nctx_chunks=nctxc),
        grid=(2, nchunk),
        in_specs=[pl.BlockSpec((p, n_batch, D_MODEL), u_map), per_dir(a2), per_dir(w_in), per_dir(w_state),
                  per_dir(w_dir)],
        out_specs=pl.BlockSpec((None, p, n_batch, D_MODEL), y_map),
        out_shape=jax.ShapeDtypeStruct((2, n_lat, n_batch, D_MODEL), F32),
        scratch_shapes=[pltpu.VMEM((p // 2, n_batch, 2 * S5_LANES), F32), pltpu.VMEM((n_batch, 2 * S5_LANES), F32)],
        compiler_params=_cparams(("arbitrary", "arbitrary")),
        name="s5_scan",
    )(u3, a2, w_in, w_state, w_dir)


def _glu_kernel(yf_ref, yb_ref, u_ref, dskip_ref, wa_ref, wb_ref, x_ref, mod_ref, gpost_ref, gpre_ref, router_ref,
                swap_ref, tri_ref, xo_ref, hf_ref, rw_ref, ri_ref, cnt_ref, carry):
    n_batch, steps, _ = x_ref.shape
    rows = n_batch * steps

    @pl.when(pl.program_id(0) == 0)
    def _():
        carry[...] = jnp.zeros_like(carry)

    y = (yf_ref[...] + yb_ref[...] + dskip_ref[...] * u_ref[...]).reshape(rows, D_MODEL)
    z = jax.nn.gelu(y).astype(BF16)
    z = _bdot(swap_ref[...], z).astype(BF16)
    out = _bdot(z, wa_ref[...]) * jax.nn.sigmoid(_bdot(z, wb_ref[...]))
    out = out.reshape(n_batch, steps, D_MODEL)
    mod = lambda k: mod_ref[0:n_batch, k:k + 1, :]
    xn = x_ref[...] + mod(2) * _rms(out, gpost_ref[...])
    xo_ref[...] = xn
    hf = _rms(xn, gpre_ref[...]) * (1.0 + mod(4)) + mod(3)
    hf_ref[...] = hf
    hf = hf.reshape(rows, D_MODEL)
    h_hi = hf.astype(BF16)
    h_lo = (hf - h_hi.astype(F32)).astype(BF16)
    part = _bdot(h_hi, router_ref[...])
    logits = part[:, :LANE] + part[:, LANE:] + _bdot(h_lo, router_ref[:, :LANE])
    lane = lax.broadcasted_iota(jnp.int32, logits.shape, 1)
    neg = jnp.float32(-jnp.inf)
    lg = jnp.where(lane < N_EXPERTS, logits, neg)
    m1 = jnp.max(lg, axis=-1, keepdims=True)
    i1 = jnp.min(jnp.where(lg == m1, lane, LANE), axis=-1, keepdims=True)
    lg2 = jnp.where(lane == i1, neg, lg)
    m2 = jnp.max(lg2, axis=-1, keepdims=True)
    i2 = jnp.min(jnp.where(lg2 == m2, lane, LANE), axis=-1, keepdims=True)
    e2 = jnp.exp(m2 - m1)
    w1 = 1.0 / (1.0 + e2)
    rw_ref[...] = jnp.where(lane == 0, w1, jnp.where(lane == 1, e2 * w1, 0.0)).reshape(n_batch, steps, LANE)
    member = jnp.where(lane == i1, 1.0, jnp.where(lane == i2, 1.0, 0.0))
    base = carry[...] + _bdot(tri_ref[...], member.astype(BF16))
    r1 = jnp.sum(jnp.where(lane == i1, base, 0.0), axis=-1, keepdims=True).astype(jnp.int32)
    r2 = jnp.sum(jnp.where(lane == i2, base, 0.0), axis=-1, keepdims=True).astype(jnp.int32)
    ri = jnp.where(lane == 0, i1, jnp.where(lane == 1, i2, jnp.where(lane == 2, r1, jnp.where(lane == 3, r2, 0))))
    ri_ref[...] = ri.reshape(n_batch, steps, LANE)
    carry[...] += jnp.sum(member, axis=0, keepdims=True)
    cnt_ref[...] = carry[...]


def _glu(y, u3, d_skip, w_a, w_b, xs, mod, gpost, gpre, router, n_lat):
    n_batch = xs.shape[0]
    p = S5_CHUNK
    rows = p * n_batch
    router_p = jnp.pad(router, ((0, 0), (0, LANE - router.shape[1])))
    router_hi = router_p.astype(BF16)
    router_cat = jnp.concatenate([router_hi, (router_p - router_hi.astype(F32)).astype(BF16)], axis=1)
    tri = jnp.asarray(np.arange(rows)[:, None] > np.arange(rows)[None, :], BF16)
    swap = _row_order_swap(p, n_batch)
    bt_spec = lambda w: pl.BlockSpec((n_batch, p, w), lambda i: (0, i, 0))
    return pl.pallas_call(
        _glu_kernel,
        grid=(n_lat // p,),
        in_specs=[pl.BlockSpec((None, p, n_batch, D_MODEL), lambda i: (0, i, 0, 0)),
                  pl.BlockSpec((None, p, n_batch, D_MODEL), lambda i: (1, i, 0, 0)),
                  pl.BlockSpec((p, n_batch, D_MODEL), lambda i: (i, 0, 0)),
                  _full((1, D_MODEL)), _full(w_a.shape), _full(w_b.shape), bt_spec(D_MODEL),
                  _full(mod.shape), _full((1, D_MODEL)), _full((1, D_MODEL)),
                  _full(router_cat.shape), _full(swap.shape), _full(tri.shape)],
        out_specs=[bt_spec(D_MODEL), bt_spec(D_MODEL), bt_spec(LANE), bt_spec(LANE), _full((1, LANE))],
        out_shape=[jax.ShapeDtypeStruct((n_batch, n_lat, D_MODEL), F32),
                   jax.ShapeDtypeStruct((n_batch, n_lat, D_MODEL), F32),
                   jax.ShapeDtypeStruct((n_batch, n_lat, LANE), F32),
                   jax.ShapeDtypeStruct((n_batch, n_lat, LANE), jnp.int32),
                   jax.ShapeDtypeStruct((1, LANE), F32)],
        scratch_shapes=[pltpu.VMEM((1, LANE), F32)],
        compiler_params=_cparams(("arbitrary",)),
        name="s5_glu_router",
    )(y, y, u3, d_skip, w_a, w_b, xs, mod, gpost, gpre, router_cat, swap, tri)


def _moe_plan(ri, counts, n_tiles):
    experts = jnp.arange(N_EXPERTS, dtype=jnp.int32)
    n_of = (counts[0, :N_EXPERTS].astype(jnp.int32) + MOE_TILE - 1) // MOE_TILE
    ends = jnp.cumsum(n_of)
    starts = ends - n_of
    start_of = jnp.sum(jnp.where(ri[:, 0:2, None] == experts, starts, 0), axis=-1)
    n_tok = ri.shape[0]
    pos = (start_of * MOE_TILE + ri[:, 2:4]).T.reshape(-1)
    n_used = ends[-1]
    tile = jnp.arange(n_tiles, dtype=jnp.int32)
    tile_expert = jnp.sum((jnp.minimum(tile, n_used - 1)[:, None] >= ends[None, :]).astype(jnp.int32), axis=1)
    n_pairs = pos.shape[0]
    n_rows = n_tiles * MOE_TILE
    pair_of = jnp.full((n_rows,), -1, jnp.int32).at[pos].set(jnp.arange(n_pairs, dtype=jnp.int32),
                                                             unique_indices=True, mode='promise_in_bounds')
    is_pad = pair_of < 0
    pair_of = jnp.where(is_pad, n_pairs - 1 + jnp.cumsum(is_pad.astype(jnp.int32)), pair_of)
    src_token = jnp.where(is_pad, 0, pair_of % n_tok)
    return src_token, pair_of, tile_expert, n_used.reshape(1)


def _moe_expert_kernel(te_ref, nu_ref, src0_ref, src_ref, dst_ref, hf_ref, wg_ref, wu_ref, wd_ref, out_ref,
                       xbuf, obuf, acc, gsem, ssem, *, rows_per_step):
    del te_ref
    i = pl.program_id(0)
    j = pl.program_id(1)
    last_j = pl.num_programs(1) - 1
    nu = nu_ref[0]
    cur = lax.rem(i, 2)
    nxt = 1 - cur
    row0 = j * rows_per_step

    def gather_row(idx_ref, row, slot):
        return pltpu.make_async_copy(hf_ref.at[pl.ds(idx_ref[0, row], 1)], xbuf.at[slot, pl.ds(row, 1)], gsem.at[slot])

    def scatter_row(row, slot):
        return pltpu.make_async_copy(obuf.at[slot, pl.ds(row, 1)], out_ref.at[pl.ds(dst_ref[0, row], 1)], ssem.at[slot])

    def whole_tile_wait(sem, slot):
        pltpu.make_async_copy(xbuf.at[slot], obuf.at[slot], sem.at[slot]).wait()

    @pl.when(jnp.logical_and(i == 0, j == 0))
    def _():
        obuf[...] = jnp.zeros_like(obuf)

        def first(r, c):
            gather_row(src0_ref, r, 0).start()
            return c

        lax.fori_loop(0, MOE_TILE, first, 0)

    @pl.when(jnp.logical_and(i <= nu, j == 0))
    def _():
        whole_tile_wait(gsem, cur)

    for step in range(MOE_TILE // rows_per_step):
        @pl.when(jnp.logical_and(i < nu, j == step))
        def _(step=step):
            for row in range(step * rows_per_step, (step + 1) * rows_per_step):
                gather_row(src_ref, row, nxt).start()
                scatter_row(row, nxt).start(priority=row % 2)

    @pl.when(i < nu)
    def _():
        h = xbuf[cur].astype(BF16)
        g = _bdot(h, wg_ref[...])
        u = _bdot(h, wu_ref[...])
        part = _bdot((g * jax.nn.sigmoid(g) * u).astype(BF16), wd_ref[...])
        acc[...] = jnp.where(j == 0, part, acc[...] + part)

    @pl.when(i >= nu)
    def _():
        def tail(c, carry):
            scatter_row(row0 + c, nxt).start()
            return carry

        lax.fori_loop(0, rows_per_step, tail, 0)

    @pl.when(jnp.logical_and(i >= 1, j == last_j))
    def _():
        whole_tile_wait(ssem, cur)

    @pl.when(jnp.logical_and(i < nu, j == last_j))
    def _():
        obuf[cur] = acc[...]

    @pl.when(jnp.logical_and(i == pl.num_programs(0) - 1, j == last_j))
    def _():
        whole_tile_wait(ssem, nxt)


def _combine_kernel(ya_ref, yb_ref, rw_ref, x_ref, mod_ref, gpost_ref, o_ref):
    rw = rw_ref[...]
    y = rw[:, 0:1] * ya_ref[...] + rw[:, 1:2] * yb_ref[...]
    o_ref[...] = x_ref[...] + mod_ref[5:6, :] * _rms(y, gpost_ref[...])


def _moe(hf, rw, ri, counts, xs, mod, gpost, wg, wu, wd, nlat):
    t_rows = hf.shape[0]
    d_ff = wg.shape[2]
    n_tiles = 2 * t_rows // MOE_TILE + N_EXPERTS
    nff = d_ff // MOE_FF_TILE
    n_row_tiles = t_rows // ROW_TILE
    n_rows = n_tiles * MOE_TILE
    src_token, pair_of, tile_expert, n_used = _moe_plan(ri, counts, n_tiles)
    spare = jnp.arange(MOE_TILE, dtype=jnp.int32)
    src_tiles = jnp.concatenate([src_token, 0 * spare]).reshape(n_tiles + 1, 1, MOE_TILE)
    dst_tiles = jnp.concatenate([n_rows + spare, pair_of]).reshape(n_tiles + 1, 1, MOE_TILE)
    step_expert = jnp.concatenate([tile_expert, tile_expert[-1:]])
    any_spec = pl.BlockSpec(memory_space=pl.ANY)
    idx_block = (None, 1, MOE_TILE)

    def ff_blk(i, j, nu):
        return jnp.where(i < nu[0], j, nff - 1)

    y_pairs = pl.pallas_call(
        functools.partial(_moe_expert_kernel, rows_per_step=MOE_TILE // nff),
        grid_spec=pltpu.PrefetchScalarGridSpec(
            num_scalar_prefetch=2,
            grid=(n_tiles + 1, nff),
            in_specs=[pl.BlockSpec(idx_block, lambda i, j, te, nu: (0, 0, 0), memory_space=pltpu.SMEM),
                      pl.BlockSpec(idx_block, lambda i, j, te, nu: (jnp.minimum(i + 1, n_tiles), 0, 0),
                                   memory_space=pltpu.SMEM),
                      pl.BlockSpec(idx_block, lambda i, j, te, nu: (i, 0, 0), memory_space=pltpu.SMEM),
                      any_spec,
                      pl.BlockSpec((None, D_MODEL, MOE_FF_TILE), lambda i, j, te, nu: (te[i], 0, ff_blk(i, j, nu))),
                      pl.BlockSpec((None, D_MODEL, MOE_FF_TILE), lambda i, j, te, nu: (te[i], 0, ff_blk(i, j, nu))),
                      pl.BlockSpec((None, MOE_FF_TILE, D_MODEL), lambda i, j, te, nu: (te[i], ff_blk(i, j, nu), 0))],
            out_specs=any_spec,
            scratch_shapes=[pltpu.VMEM((2, MOE_TILE, D_MODEL), F32), pltpu.VMEM((2, MOE_TILE, D_MODEL), F32),
                            pltpu.VMEM((MOE_TILE, D_MODEL), F32),
                            pltpu.SemaphoreType.DMA((2,)), pltpu.SemaphoreType.DMA((2,))]),
        out_shape=jax.ShapeDtypeStruct((n_rows + MOE_TILE, D_MODEL), F32),
        compiler_params=_cparams(("arbitrary", "arbitrary")),
        name="moe_experts",
    )(step_expert, n_used, src_tiles, src_tiles, dst_tiles, hf, wg, wu, wd)

    row = lambda w: pl.BlockSpec((ROW_TILE, w), lambda i: (i, 0))
    return pl.pallas_call(
        _combine_kernel,
        grid=(n_row_tiles,),
        in_specs=[row(D_MODEL), pl.BlockSpec((ROW_TILE, D_MODEL), lambda i: (n_row_tiles + i, 0)), row(LANE), row(D_MODEL),
                  pl.BlockSpec((None, 6, D_MODEL), lambda i: (lax.div(i, nlat), 0, 0)), _full((1, D_MODEL))],
        out_specs=row(D_MODEL),
        out_shape=jax.ShapeDtypeStruct((t_rows, D_MODEL), F32),
        compiler_params=_cparams(("parallel",)),
        name="moe_combine",
    )(y_pairs, y_pairs, rw, xs, mod, gpost)


def _rope_tables(n_lat, n_ctx):
    rows = n_lat // GRID_W
    row = jnp.repeat(jnp.arange(rows, dtype=F32), GRID_W)
    col = jnp.tile(jnp.arange(GRID_W, dtype=F32), rows)
    n_freq = HEAD_DIM // 4
    inv = ROPE_THETA ** (-jnp.arange(n_freq, dtype=F32) / n_freq)
    ang = jnp.concatenate([row[:, None] * inv, col[:, None] * inv], axis=-1)
    cos, sin = jnp.cos(ang), jnp.sin(ang)
    cos_h = jnp.concatenate([cos, cos], axis=-1)
    sin_h = jnp.concatenate([-sin, sin], axis=-1)
    cos_t = jnp.concatenate([jnp.tile(cos_h, (1, N_Q_HEADS)), jnp.ones((n_ctx, ATTN_WIDTH), F32)], axis=0)
    sin_t = jnp.concatenate([jnp.tile(sin_h, (1, N_Q_HEADS)), jnp.zeros((n_ctx, ATTN_WIDTH), F32)], axis=0)
    return cos_t, sin_t


def kernel(x, c, ctx, c_ctx, ada_w, ada_b, norm_mix_pre, norm_mix_post, norm_ffn_pre, norm_ffn_post, ev_w_in, ev_hy_conv_w, ev_hy_conv_b, ev_hy_f_w1, ev_hy_f_b1, ev_hy_f_w2, ev_hy_f_b2, ev_hy_f_wout, ev_hy_freq, ev_hy_skip, ev_q_norm, ev_k_norm, ev_w_out, ev_ffn_w_gate, ev_ffn_w_up, ev_ffn_w_down, od_w_in, od_s5_lambda_re, od_s5_lambda_im, od_s5_log_step, od_s5_b_re, od_s5_b_im, od_s5_c_re, od_s5_c_im, od_s5_d, od_glu_w_a, od_glu_w_b, od_router, od_moe_w_gate, od_moe_w_up, od_moe_w_down):
    n_batch, n_lat, _ = x.shape
    n_ctx = ctx.shape[1]
    depth = ada_w.shape[0]
    assert n_batch == 8 and n_lat % ROW_TILE == 0 and n_ctx % ROW_TILE == 0 and n_lat % n_ctx == 0
    assert depth == 2
    nlat = n_lat // ROW_TILE

    cond = jnp.concatenate([c, c_ctx[None, :], jnp.zeros((16 - n_batch - 1, D_MODEL), F32)], axis=0)
    mods = _ada_params(cond, ada_w, ada_b)
    vec = lambda a: a[None, :]

    cos_t, sin_t = _rope_tables(n_lat, n_ctx)
    hy, q, k4, v4 = _inproj_even(x, ctx, mods[0], vec(norm_mix_pre[0]), ev_w_in[0].astype(BF16),
                                 vec(jnp.tile(ev_q_norm[0], N_Q_HEADS)), vec(jnp.tile(ev_k_norm[0], N_KV_HEADS)),
                                 cos_t, sin_t, nlat)
    lt = n_lat + n_ctx
    a_lat = _attention(q, k4, v4, n_lat, min(ATTN_Q_TILE, n_lat), 0, lt, 0)
    a_ctx = _attention(q, k4, v4, n_ctx, n_ctx, n_lat // n_ctx, n_ctx, n_lat // n_ctx)
    filt_args = (ev_hy_f_w1[0], ev_hy_f_b1[0], ev_hy_f_w2[0], ev_hy_f_b2[0], ev_hy_f_wout[0], ev_hy_freq[0])
    y_hy_lat = _hyena_mixer(hy, 0, n_lat, filt_args, ev_hy_conv_w[0], ev_hy_conv_b[0], ev_hy_skip[0])
    y_hy_ctx = _hyena_mixer(hy, n_lat // n_ctx, n_ctx, filt_args, ev_hy_conv_w[0], ev_hy_conv_b[0], ev_hy_skip[0])
    xs = _outproj(y_hy_lat, y_hy_ctx, a_lat, a_ctx, x, ctx, mods[0], vec(norm_mix_post[0]), ev_w_out[0].astype(BF16), nlat)
    xs = _ffn_dense(xs, mods[0], vec(norm_ffn_pre[0]), vec(norm_ffn_post[0]), ev_ffn_w_gate[0].astype(BF16),
                    ev_ffn_w_up[0].astype(BF16), ev_ffn_w_down[0].astype(BF16), nlat)

    u3 = _inproj_odd(xs, mods[1], vec(norm_mix_pre[1]), od_w_in[0].astype(BF16), n_lat)
    s5_ops = _s5_params(od_s5_lambda_re[0], od_s5_lambda_im[0], od_s5_log_step[0],
                        od_s5_b_re[0], od_s5_b_im[0], od_s5_c_re[0], od_s5_c_im[0])
    y = _s5_scan(u3, *s5_ops, n_lat)
    x_lat, hf, rw, ri, counts = _glu(y, u3, vec(od_s5_d[0]), od_glu_w_a[0].astype(BF16), od_glu_w_b[0].astype(BF16),
                                     xs, mods[1], vec(norm_mix_post[1]), vec(norm_ffn_pre[1]), od_router[0], n_lat)
    t_rows = n_batch * n_lat
    out = _moe(hf.reshape(t_rows, D_MODEL), rw.reshape(t_rows, LANE), ri.reshape(t_rows, LANE), counts,
               x_lat.reshape(t_rows, D_MODEL), mods[1], vec(norm_ffn_post[1]), od_moe_w_gate[0].astype(BF16),
               od_moe_w_up[0].astype(BF16), od_moe_w_down[0].astype(BF16), nlat)
    return out.reshape(n_batch, n_lat, D_MODEL)
```

```python
import functools
import math

import jax
import jax.numpy as jnp
import numpy as np
from jax import lax
from jax.experimental import pallas as pl
from jax.experimental.pallas import tpu as pltpu

F32 = jnp.float32
BF16 = jnp.bfloat16

D_MODEL = 1024
EPS = 1e-6
GRID_W = 64

HY_WIDTH = 512
HY_ORDER = 2
HY_IN = (HY_ORDER + 1) * HY_WIDTH
HY_BANDS = 16
HY_EMB = 2 * HY_BANDS + 1
HY_DECAY_SLOW = -math.log(1e-2) / 1.5
HY_DECAY_FAST = -math.log(1e-2) / 0.3
HEAD_DIM = 64
N_Q_HEADS = 8
N_KV_HEADS = 2
Q_PER_KV = N_Q_HEADS // N_KV_HEADS
ATTN_WIDTH = N_Q_HEADS * HEAD_DIM
KV_WIDTH = N_KV_HEADS * HEAD_DIM
ROPE_THETA = 10000.0
LOG2_E = 1.0 / math.log(2.0)

S5_GROUP = 16
S5_GROUPS = D_MODEL // S5_GROUP
S5_STATE = 64
S5_LANES = S5_GROUPS * S5_STATE
S5_BLOCK_GROUPS = 8
N_EXPERTS = 8

ROW_TILE = 256
FFN_SUBTILES = 2
ATTN_Q_TILE = 512
FREQ_TILE = 512
DFT_SPLIT = 128
S5_CHUNK = 64
S5_SCAN_CHUNK = 64
MOE_TILE = 512
COMBINE_TILE = 1024
MOE_FF_TILE = 1792
LANE = 128
MIB = 1024 * 1024


def _cparams(sem, vmem_mib=48):
    return pltpu.CompilerParams(dimension_semantics=sem, vmem_limit_bytes=vmem_mib * MIB)


def _rms(x, gain):
    return x * lax.rsqrt(jnp.mean(x * x, axis=-1, keepdims=True) + EPS) * gain


def _bdot(a, b):
    return jnp.dot(a, b, preferred_element_type=F32)


def _full(shape):
    zeros = (0,) * len(shape)
    return pl.BlockSpec(shape, lambda *_: zeros)


def _resident(shape):
    zeros = (0,) * len(shape)
    return pl.BlockSpec(shape, lambda *_: zeros, pipeline_mode=pl.Buffered(1))


def _ada_kernel(cond_ref, w_ref, b_ref, o_ref):
    c = cond_ref[...]
    s = (c * jax.nn.sigmoid(c)).astype(BF16)
    o_ref[...] = _bdot(s, w_ref[...].astype(BF16)) + b_ref[...]


def _ada_params(cond, ada_w, ada_b):
    depth, _, n6 = ada_w.shape
    rows = cond.shape[0]
    tn = 1536
    out = pl.pallas_call(
        _ada_kernel,
        grid=(depth, n6 // tn),
        in_specs=[pl.BlockSpec((rows, D_MODEL), lambda i, j: (0, 0)),
                  pl.BlockSpec((None, D_MODEL, tn), lambda i, j: (i, 0, j)),
                  pl.BlockSpec((None, 1, tn), lambda i, j: (i, 0, j))],
        out_specs=pl.BlockSpec((None, rows, tn), lambda i, j: (i, 0, j)),
        out_shape=jax.ShapeDtypeStruct((depth, rows, n6), F32),
        compiler_params=_cparams(("arbitrary", "arbitrary")),
        name="ada_params",
    )(cond, ada_w, ada_b.reshape(depth, 1, n6))
    return out.reshape(depth, rows, 6, D_MODEL)


def _rope_rotate(t):
    w = t.shape[-1]
    lane = lax.broadcasted_iota(jnp.int32, t.shape, 1)
    first = (lane & (HEAD_DIM - 1)) < HEAD_DIM // 2
    return jnp.where(first, pltpu.roll(t, w - HEAD_DIM // 2, 1), pltpu.roll(t, HEAD_DIM // 2, 1))


def _head_slots(t):
    lane = lax.broadcasted_iota(jnp.int32, t.shape, 1)
    lo = jnp.where(lane < HEAD_DIM, t, 0.0)
    hi = jnp.where(lane >= HEAD_DIM, t, 0.0)
    return jnp.concatenate([lo, pltpu.roll(lo, HEAD_DIM, 1), pltpu.roll(hi, HEAD_DIM, 1), hi], axis=-1)


def _inproj_even_kernel(x_ref, c_ref, mod_ref, gain_ref, w_ref, qg_ref, kg_ref, e_ref, cos_ref, sin_ref,
                        hy_ref, q_ref, k4_ref, v4_ref, *, nlat):
    xin = jnp.where(pl.program_id(1) >= nlat, c_ref[...], x_ref[...])
    h = _rms(xin, gain_ref[...]) * (1.0 + mod_ref[1:2, :]) + mod_ref[0:1, :]
    p = _bdot(h.astype(BF16), w_ref[...])
    hy_ref[...] = p[:, :HY_IN].astype(hy_ref.dtype)
    q = p[:, HY_IN:HY_IN + ATTN_WIDTH]
    k = p[:, HY_IN + ATTN_WIDTH:HY_IN + ATTN_WIDTH + KV_WIDTH]
    v = p[:, HY_IN + ATTN_WIDTH + KV_WIDTH:]
    e = e_ref[...]
    qn = q * lax.rsqrt(_bdot((q * q).astype(BF16), e) + EPS) * qg_ref[...]
    kn = k * lax.rsqrt(_bdot((k * k).astype(BF16), e[:KV_WIDTH, :KV_WIDTH]) + EPS) * kg_ref[...]
    cos = cos_ref[...]
    sin = sin_ref[...]
    qr = (qn * cos + _rope_rotate(qn) * sin) * (HEAD_DIM ** -0.5 * LOG2_E)
    kr = kn * cos[:, :KV_WIDTH] + _rope_rotate(kn) * sin[:, :KV_WIDTH]
    q_ref[...] = qr.astype(q_ref.dtype)
    k4_ref[...] = _head_slots(kr).astype(k4_ref.dtype)
    v4_ref[...] = _head_slots(v).astype(v4_ref.dtype)


def _mod_spec(n_batch, nlat):
    return pl.BlockSpec((None, 6, D_MODEL), lambda b, j: (jnp.where(j >= nlat, n_batch, b), 0, 0))


def _row_spec(width):
    return pl.BlockSpec((None, ROW_TILE, width), lambda b, j: (b, j, 0))


def _lat_ctx_specs(nlat, nctx, width):
    return [pl.BlockSpec((None, ROW_TILE, width), lambda b, j: (b, jnp.minimum(j, nlat - 1), 0)),
            pl.BlockSpec((None, ROW_TILE, width), lambda b, j: (b, jnp.clip(j - nlat, 0, nctx - 1), 0))]


def _inproj_even(x, ctx, mod, gain, w_in, q_gain, k_gain, cos_t, sin_t, nlat):
    n_batch = x.shape[0]
    nctx = ctx.shape[1] // ROW_TILE
    ntile = nlat + nctx
    lt = ntile * ROW_TILE
    n_out = w_in.shape[1]
    head_avg = jnp.asarray(np.kron(np.eye(N_Q_HEADS), np.full((HEAD_DIM, HEAD_DIM), 1.0 / HEAD_DIM)), BF16)
    table = pl.BlockSpec((ROW_TILE, ATTN_WIDTH), lambda b, j: (j, 0))
    outs = [jax.ShapeDtypeStruct((n_batch, lt, HY_IN), BF16)] + [jax.ShapeDtypeStruct((n_batch, lt, ATTN_WIDTH), BF16)] * 3
    return pl.pallas_call(
        functools.partial(_inproj_even_kernel, nlat=nlat),
        grid=(n_batch, ntile),
        in_specs=_lat_ctx_specs(nlat, nctx, D_MODEL) + [
            _mod_spec(n_batch, nlat), _full((1, D_MODEL)), _full((D_MODEL, n_out)),
            _full((1, ATTN_WIDTH)), _full((1, KV_WIDTH)), _full((ATTN_WIDTH, ATTN_WIDTH)), table, table],
        out_specs=[_row_spec(HY_IN), _row_spec(ATTN_WIDTH), _row_spec(ATTN_WIDTH), _row_spec(ATTN_WIDTH)],
        out_shape=outs,
        compiler_params=_cparams(("parallel", "parallel")),
        name="inproj_even",
    )(x, ctx, mod, gain, w_in, q_gain, k_gain, head_avg, cos_t, sin_t)


def _attn_kernel(q_ref, k4_ref, v4_ref, o_ref):
    for blk in range(N_Q_HEADS // 2):
        g = (2 * blk) // Q_PER_KV
        qp = q_ref[:, LANE * blk:LANE * (blk + 1)]
        acc = None
        for r in range(2):
            slot = LANE * (2 * g + r)
            s = lax.dot_general(qp, k4_ref[:, slot:slot + LANE], (((1,), (1,)), ((), ())), preferred_element_type=F32)
            e = jnp.exp2(s - jnp.max(s, axis=-1, keepdims=True))
            inv = 1.0 / jnp.sum(e, axis=-1, keepdims=True)
            o = _bdot(e.astype(BF16), v4_ref[:, slot:slot + LANE]) * inv
            acc = o if acc is None else acc + o
        o_ref[:, LANE * blk:LANE * (blk + 1)] = acc.astype(o_ref.dtype)


def _attention(q, k4, v4, q_rows, q_tile, q_blk0, k_rows, k_blk):
    n_batch = q.shape[0]
    kv_spec = pl.BlockSpec((None, k_rows, ATTN_WIDTH), lambda b, j: (b, k_blk, 0))
    return pl.pallas_call(
        _attn_kernel,
        grid=(n_batch, q_rows // q_tile),
        in_specs=[pl.BlockSpec((None, q_tile, ATTN_WIDTH), lambda b, j: (b, q_blk0 + j, 0)), kv_spec, kv_spec],
        out_specs=pl.BlockSpec((None, q_tile, ATTN_WIDTH), lambda b, j: (b, j, 0)),
        out_shape=jax.ShapeDtypeStruct((n_batch, q_rows, ATTN_WIDTH), BF16),
        compiler_params=_cparams(("parallel", "parallel")),
        name="attention",
    )(q, k4, v4)


def _filter_kernel(z_ref, w1_ref, b1_ref, w2_ref, b2_ref, wo_ref, fr_ref, dl_ref, o_ref):
    hi = lax.Precision.HIGHEST
    z = z_ref[...]
    fr = fr_ref[...]
    h = jnp.sin(fr * (jnp.dot(z, w1_ref[...], precision=hi, preferred_element_type=F32) + b1_ref[...]))
    h = jnp.sin(fr * (jnp.dot(h, w2_ref[...], precision=hi, preferred_element_type=F32) + b2_ref[...]))
    h = jnp.dot(h, wo_ref[...], precision=hi, preferred_element_type=F32)
    decay = jnp.exp(-z[:, 0:1] * dl_ref[...])
    o_ref[...] = h * jnp.concatenate([decay] * (2 * HY_ORDER), axis=-1)


def _hyena_filters(seq, w1, b1, w2, b2, wout, freq):
    t = jnp.linspace(0.0, 1.0, seq, dtype=F32)[:, None]
    bands = jnp.linspace(1e-4, HY_BANDS - 1, HY_BANDS, dtype=F32)
    phase = (2.0 * math.pi / seq) * jnp.arange(seq, dtype=F32)[:, None] * bands
    z = jnp.concatenate([t, jnp.cos(phase), -jnp.sin(phase)], axis=-1)
    z = jnp.pad(z, ((0, 0), (0, LANE - HY_EMB)))
    w1p = jnp.pad(w1, ((0, LANE - HY_EMB), (0, 0)))
    deltas = jnp.linspace(HY_DECAY_SLOW, HY_DECAY_FAST, HY_WIDTH, dtype=F32)[None, :]
    hid = w1.shape[1]
    n_out = wout.shape[1]
    tl = ROW_TILE
    return pl.pallas_call(
        _filter_kernel,
        grid=(seq // tl,),
        in_specs=[pl.BlockSpec((tl, LANE), lambda i: (i, 0)), _full((LANE, hid)), _full((1, hid)), _full((hid, hid)),
                  _full((1, hid)), _full((hid, n_out)), _full((1, hid)), _full((1, HY_WIDTH))],
        out_specs=pl.BlockSpec((tl, n_out), lambda i: (i, 0)),
        out_shape=jax.ShapeDtypeStruct((seq, n_out), F32),
        compiler_params=_cparams(("parallel",)),
        name="hyena_filter",
    )(z, w1p, b1[None, :], w2, b2[None, :], wout, freq[None, :], deltas)


def _freq_tile(seq):
    return min(FREQ_TILE, seq)


def _dft_tables(seq):
    n = 2 * seq
    unit = 2.0 * math.pi / n
    col = jnp.arange(seq, dtype=jnp.int32)[None, :]
    r1 = jnp.arange(seq // DFT_SPLIT, dtype=jnp.int32)[:, None] * DFT_SPLIT
    r2 = jnp.arange(DFT_SPLIT, dtype=jnp.int32)[:, None]
    ang1 = ((r1 * col) % n).astype(F32) * unit
    ang2 = ((r2 * col) % n).astype(F32) * unit
    c1, s1 = jnp.cos(ang1)[:, None, :], jnp.sin(ang1)[:, None, :]
    c2, s2 = jnp.cos(ang2)[None, :, :], jnp.sin(ang2)[None, :, :]
    cos = (c1 * c2 - s1 * s2).reshape(seq, seq)
    sin = (s1 * c2 + c1 * s2).reshape(seq, seq)
    idx = jnp.arange(seq, dtype=jnp.int32)
    nyquist = jnp.where(idx % 2 == 0, 1.0, -1.0)
    sin_rows = jnp.where(idx[:, None] == 0, nyquist[None, :], sin)
    sin_cols = jnp.where(idx[None, :] == 0, nyquist[:, None], sin)
    tf = _freq_tile(seq)
    nf = seq // tf
    cos, sin_rows, sin_cols = cos.astype(BF16), sin_rows.astype(BF16), sin_cols.astype(BF16)
    tiles = [slice(i * tf, (i + 1) * tf) for i in range(nf)]
    fwd = jnp.concatenate([m[s, :] for s in tiles for m in (cos, sin_rows)], axis=0)
    inv = jnp.concatenate([m[:, s] for s in tiles for m in (cos, sin_cols)], axis=1)
    return fwd, inv


def _kfreq_kernel(hf_ref, hb_ref, f_ref, kre_ref, ks_ref):
    fi = pl.program_id(1)
    seq = hf_ref.shape[0]
    hf = hf_ref[...]
    row = lax.broadcasted_iota(jnp.int32, hf.shape, 0)
    hb = jnp.where(row == 0, 0.0, hb_ref[...])
    f = f_ref[...]
    a1 = _bdot(f, (hf + hb).astype(BF16))
    a2 = _bdot(f, (hf - hb).astype(BF16))
    tf = f_ref.shape[0] // 2
    frow = lax.broadcasted_iota(jnp.int32, (tf, hf.shape[1]), 0)
    dc = jnp.logical_and(frow == 0, fi == 0)
    scale = jnp.where(dc, 1.0 / (2 * seq), 2.0 / (2 * seq))
    kre_ref[...] = a1[:tf] * scale
    ks_ref[...] = jnp.where(dc, a1[tf:], a2[tf:]) * scale


def _kfreq(hfilt, fwd_tab):
    seq = hfilt.shape[0]
    tf = _freq_tile(seq)
    nf = seq // tf
    out = jax.ShapeDtypeStruct((HY_ORDER, seq, HY_WIDTH), F32)
    ospec = pl.BlockSpec((None, tf, HY_WIDTH), lambda o, fi: (o, fi, 0))
    return pl.pallas_call(
        _kfreq_kernel,
        grid=(HY_ORDER, nf),
        in_specs=[pl.BlockSpec((seq, HY_WIDTH), lambda o, fi: (0, o)),
                  pl.BlockSpec((seq, HY_WIDTH), lambda o, fi: (0, HY_ORDER + o)),
                  pl.BlockSpec((2 * tf, seq), lambda o, fi: (fi, 0))],
        out_specs=[ospec, ospec],
        out_shape=[out, out],
        compiler_params=_cparams(("parallel", "parallel")),
        name="hyena_kfreq",
    )(hfilt, hfilt, fwd_tab)


def _short_conv(p, w, b):
    n = p.shape[0]
    row = lax.broadcasted_iota(jnp.int32, p.shape, 0)
    prev = jnp.where(row == 0, 0.0, pltpu.roll(p, 1, 0))
    nxt = jnp.where(row == n - 1, 0.0, pltpu.roll(p, n - 1, 0))
    return b + prev * w[0:1, :] + p * w[1:2, :] + nxt * w[2:3, :]


def _hyena_kernel(vsrc_ref, gsrc_ref, cw_ref, cb_ref, skip_ref, f_ref, ft_ref, kre_ref, ks_ref, o_ref,
                  v_scr, vb_scr, acc_scr, *, conv_v):
    fi = pl.program_id(1)
    tf = f_ref.shape[0] // 2

    @pl.when(fi == 0)
    def _():
        v = vsrc_ref[...].astype(F32)
        if conv_v:
            v = _short_conv(v, cw_ref[0], cb_ref[0])
        v_scr[...] = v
        vb_scr[...] = v.astype(BF16)
        acc_scr[...] = jnp.zeros_like(acc_scr)

    xf = _bdot(f_ref[...], vb_scr[...])
    xre, xs = xf[:tf], xf[tf:]
    kre, ks = kre_ref[...], ks_ref[...]
    row = lax.broadcasted_iota(jnp.int32, xre.shape, 0)
    dc = jnp.logical_and(row == 0, fi == 0)
    yre = jnp.where(dc, xre * kre, xre * kre - xs * ks)
    ys = jnp.where(dc, xs * ks, xre * ks + xs * kre)
    y = jnp.concatenate([yre, ys], axis=0).astype(BF16)
    acc_scr[...] += _bdot(ft_ref[...], y)

    @pl.when(fi == pl.num_programs(1) - 1)
    def _():
        gate = _short_conv(gsrc_ref[...].astype(F32), cw_ref[1], cb_ref[1])
        o_ref[...] = (gate * (acc_scr[...] + v_scr[...] * skip_ref[...])).astype(o_ref.dtype)


def _hyena_order(vsrc, v_blk, v_col, hy, row_blk, order, seq, conv_w, conv_b, skip, fwd_tab, inv_tab, kre, ks):
    n_batch = hy.shape[0]
    tf = _freq_tile(seq)
    nf = seq // tf
    conv_v = order == 0
    cw = jnp.stack([conv_w[:, :HY_WIDTH], conv_w[:, (order + 1) * HY_WIDTH:(order + 2) * HY_WIDTH]])
    cb = jnp.stack([conv_b[None, :HY_WIDTH], conv_b[None, (order + 1) * HY_WIDTH:(order + 2) * HY_WIDTH]])
    kspec = pl.BlockSpec((None, tf, HY_WIDTH), lambda b, fi: (order, fi, 0))
    return pl.pallas_call(
        functools.partial(_hyena_kernel, conv_v=conv_v),
        grid=(n_batch, nf),
        in_specs=[pl.BlockSpec((None, seq, HY_WIDTH), lambda b, fi: (b, v_blk, v_col)),
                  pl.BlockSpec((None, seq, HY_WIDTH), lambda b, fi: (b, row_blk, order + 1)),
                  _full((2, 3, HY_WIDTH)), _full((2, 1, HY_WIDTH)), _full((1, HY_WIDTH)),
                  pl.BlockSpec((2 * tf, seq), lambda b, fi: (fi, 0)),
                  pl.BlockSpec((seq, 2 * tf), lambda b, fi: (0, fi)),
                  kspec, kspec],
        out_specs=pl.BlockSpec((None, seq, HY_WIDTH), lambda b, fi: (b, 0, 0)),
        out_shape=jax.ShapeDtypeStruct((n_batch, seq, HY_WIDTH), BF16),
        scratch_shapes=[pltpu.VMEM((seq, HY_WIDTH), F32), pltpu.VMEM((seq, HY_WIDTH), BF16),
                        pltpu.VMEM((seq, HY_WIDTH), F32)],
        compiler_params=_cparams(("parallel", "arbitrary")),
        name="hyena_order%d" % order,
    )(vsrc, hy, cw, cb, skip[order][None, :], fwd_tab, inv_tab, kre, ks)


def _hyena_mixer(hy, row_blk, seq, filt_args, conv_w, conv_b, skip):
    hfilt = _hyena_filters(seq, *filt_args)
    fwd_tab, inv_tab = _dft_tables(seq)
    kre, ks = _kfreq(hfilt, fwd_tab)
    v1 = _hyena_order(hy, row_blk, 0, hy, row_blk, 0, seq, conv_w, conv_b, skip, fwd_tab, inv_tab, kre, ks)
    return _hyena_order(v1, 0, 0, hy, row_blk, 1, seq, conv_w, conv_b, skip, fwd_tab, inv_tab, kre, ks)


def _outproj_kernel(yl_ref, yc_ref, al_ref, ac_ref, x_ref, c_ref, mod_ref, gpost_ref, w_ref, o_ref, *, nlat):
    is_ctx = pl.program_id(1) >= nlat
    yh = jnp.where(is_ctx, yc_ref[...], yl_ref[...])
    ya = jnp.where(is_ctx, ac_ref[...], al_ref[...])
    out = _bdot(yh, w_ref[:HY_WIDTH, :]) + _bdot(ya, w_ref[HY_WIDTH:, :])
    o_ref[...] = jnp.where(is_ctx, c_ref[...], x_ref[...]) + mod_ref[2:3, :] * _rms(out, gpost_ref[...])


def _outproj(y_lat, y_ctx, a_lat, a_ctx, x, ctx, mod, gpost, w_out, nlat):
    n_batch = x.shape[0]
    nctx = ctx.shape[1] // ROW_TILE
    ntile = nlat + nctx
    return pl.pallas_call(
        functools.partial(_outproj_kernel, nlat=nlat),
        grid=(n_batch, ntile),
        in_specs=(_lat_ctx_specs(nlat, nctx, HY_WIDTH) + _lat_ctx_specs(nlat, nctx, ATTN_WIDTH)
                  + _lat_ctx_specs(nlat, nctx, D_MODEL)
                  + [_mod_spec(n_batch, nlat), _full((1, D_MODEL)), _full(w_out.shape)]),
        out_specs=_row_spec(D_MODEL),
        out_shape=jax.ShapeDtypeStruct((n_batch, ntile * ROW_TILE, D_MODEL), F32),
        compiler_params=_cparams(("parallel", "parallel")),
        name="outproj_even",
    )(y_lat, y_ctx, a_lat, a_ctx, x, ctx, mod, gpost, w_out)


def _ffn_kernel(x_ref, mod_ref, gpre_ref, gpost_ref, wg_ref, wu_ref, wd_ref, o_ref, *, n_batch, tiles_per_batch, nlat):
    n_sub = x_ref.shape[0] // ROW_TILE
    subs = []
    for s in range(n_sub):
        tile = pl.program_id(0) * n_sub + s
        batch = lax.div(tile, tiles_per_batch)
        mod = mod_ref[jnp.where(lax.rem(tile, tiles_per_batch) >= nlat, n_batch, batch)]
        x = x_ref[s * ROW_TILE:(s + 1) * ROW_TILE, :]
        subs.append((x, mod, (_rms(x, gpre_ref[...]) * (1.0 + mod[4:5, :]) + mod[3:4, :]).astype(BF16)))
    h = jnp.concatenate([sub[2] for sub in subs], axis=0)
    g = _bdot(h, wg_ref[...])
    u = _bdot(h, wu_ref[...])
    a = (g * jax.nn.sigmoid(g) * u).astype(BF16)
    y = _bdot(a, wd_ref[...])
    for s, (x, mod, _) in enumerate(subs):
        rows = slice(s * ROW_TILE, (s + 1) * ROW_TILE)
        o_ref[rows, :] = x + mod[5:6, :] * _rms(y[rows, :], gpost_ref[...])


def _ffn_dense(xs, mod, gpre, gpost, wg, wu, wd, nlat):
    n_batch, lt, _ = xs.shape
    tiles_per_batch = lt // ROW_TILE
    rows = FFN_SUBTILES * ROW_TILE
    n_steps = n_batch * tiles_per_batch // FFN_SUBTILES
    blk = pl.BlockSpec((rows, D_MODEL), lambda i: (i, 0))
    out = pl.pallas_call(
        functools.partial(_ffn_kernel, n_batch=n_batch, tiles_per_batch=tiles_per_batch, nlat=nlat),
        grid=(n_steps,),
        in_specs=[blk, _resident(mod.shape), _resident((1, D_MODEL)), _resident((1, D_MODEL)),
                  _resident(wg.shape), _resident(wu.shape), _resident(wd.shape)],
        out_specs=blk,
        out_shape=jax.ShapeDtypeStruct((n_batch * lt, D_MODEL), F32),
        compiler_params=_cparams(("parallel",), 56),
        name="ffn_dense",
    )(xs.reshape(n_batch * lt, D_MODEL), mod, gpre, gpost, wg, wu, wd)
    return out.reshape(xs.shape)


def _mod_rows(mod_ref, k, n_batch, is_ctx):
    return jnp.where(is_ctx, mod_ref[n_batch:n_batch + 1, k:k + 1, :], mod_ref[0:n_batch, k:k + 1, :])


def _row_order_swap(n_outer, n_inner):
    n = n_outer * n_inner
    dst = np.arange(n)
    src = (dst % n_outer) * n_inner + dst // n_outer
    return jnp.asarray(src[:, None] == np.arange(n)[None, :], BF16)


def _inproj_odd_kernel(x_ref, mod_ref, gain_ref, swap_ref, w_ref, u_ref, *, n_lat_steps):
    n_batch, steps, _ = x_ref.shape
    is_ctx = pl.program_id(0) >= n_lat_steps
    h = (_rms(x_ref[...], gain_ref[...]) * (1.0 + _mod_rows(mod_ref, 1, n_batch, is_ctx))
         + _mod_rows(mod_ref, 0, n_batch, is_ctx))
    h = h.reshape(n_batch * steps, D_MODEL).astype(BF16)
    h = _bdot(swap_ref[...], h).astype(BF16)
    u_ref[...] = _bdot(h, w_ref[...]).reshape(steps, n_batch, D_MODEL)


def _inproj_odd(xs, mod, gain, w_in, n_lat):
    n_batch, lt, _ = xs.shape
    p = S5_CHUNK
    swap = _row_order_swap(n_batch, p)
    return pl.pallas_call(
        functools.partial(_inproj_odd_kernel, n_lat_steps=n_lat // p),
        grid=(lt // p,),
        in_specs=[pl.BlockSpec((n_batch, p, D_MODEL), lambda i: (0, i, 0)), _full(mod.shape), _full((1, D_MODEL)),
                  _full(swap.shape), _full(w_in.shape)],
        out_specs=pl.BlockSpec((p, n_batch, D_MODEL), lambda i: (i, 0, 0)),
        out_shape=jax.ShapeDtypeStruct((lt, n_batch, D_MODEL), F32),
        compiler_params=_cparams(("parallel",)),
        name="inproj_odd",
    )(xs, mod, gain, swap, w_in)


def _s5_param_kernel(lr_ref, li_ref, ls_ref, lrx_ref, lix_ref, lsx_ref, br_ref, bi_ref,
                     abr_ref, abi_ref, bbr_ref, bbi_ref):
    def zoh(lr_raw, li, log_step):
        lr = jnp.minimum(lr_raw, -1e-4)
        dt = jnp.exp(log_step)
        mag = jnp.exp(lr * dt)
        ab_re = mag * jnp.cos(li * dt)
        ab_im = mag * jnp.sin(li * dt)
        den = lr * lr + li * li
        nr, ni = ab_re - 1.0, ab_im
        return ab_re, ab_im, (nr * lr + ni * li) / den, (ni * lr - nr * li) / den

    ab_re, ab_im, _, _ = zoh(lr_ref[...], li_ref[...], ls_ref[...])
    abr_ref[...] = ab_re
    abi_ref[...] = ab_im
    _, _, co_re, co_im = zoh(lrx_ref[...], lix_ref[...], lsx_ref[...])
    br, bi = br_ref[...], bi_ref[...]
    bbr_ref[...] = co_re * br - co_im * bi
    bbi_ref[...] = co_re * bi + co_im * br


def _s5_params(lam_re, lam_im, log_step, b_re, b_im, c_re, c_im):
    nd, g, n = lam_re.shape
    k = S5_GROUP
    rep = lambda a: jnp.repeat(a, k, axis=1)
    ls = log_step[:, :, None]
    bt_re = jnp.swapaxes(b_re, 2, 3).reshape(nd, g * k, n)
    bt_im = jnp.swapaxes(b_im, 2, 3).reshape(nd, g * k, n)
    small = jax.ShapeDtypeStruct((nd, g, n), F32)
    big = jax.ShapeDtypeStruct((nd, g * k, n), F32)
    ab_re, ab_im, bb_re, bb_im = pl.pallas_call(
        _s5_param_kernel, out_shape=[small, small, big, big], name="s5_discretise",
    )(lam_re, lam_im, ls, rep(lam_re), rep(lam_im), rep(ls), bt_re, bt_im)
    bb_re = bb_re.reshape(nd, g, k, n)
    bb_im = bb_im.reshape(nd, g, k, n)
    cmul = lambda xr, xi, yr, yi: (xr * yr - xi * yi, xr * yi + xi * yr)
    a_re, a_im = ab_re[:, :, None, :], ab_im[:, :, None, :]
    a2_re, a2_im = cmul(a_re, a_im, a_re, a_im)
    abb_re, abb_im = cmul(a_re, a_im, bb_re, bb_im)
    ca_re, ca_im = cmul(c_re, c_im, a_re, a_im)
    ca2_re, ca2_im = cmul(c_re, c_im, a2_re, a2_im)
    real_cb = lambda xr, xi: jnp.einsum('dgin,dgkn->dgik', xr, bb_re) - jnp.einsum('dgin,dgkn->dgik', xi, bb_im)
    k0 = real_cb(c_re, c_im)
    k1 = real_cb(ca_re, ca_im)
    a2 = jnp.stack([a2_re.reshape(nd, g * n), a2_im.reshape(nd, g * n)], axis=1)
    nq = g // S5_BLOCK_GROUPS

    def block_diag(parts):
        m = jnp.stack(parts)
        r, c = m.shape[-2:]
        mask = np.kron(np.eye(S5_BLOCK_GROUPS), np.ones((r, c)))
        rows = m.reshape(len(parts), nd, nq, S5_BLOCK_GROUPS * r, c)
        return jnp.tile(rows, (1, 1, 1, 1, S5_BLOCK_GROUPS)) * jnp.asarray(mask, F32)

    t = lambda x: jnp.swapaxes(x, 2, 3)
    i2s = block_diag([abb_re, abb_im, bb_re, bb_im])
    w_in = jnp.concatenate([jnp.concatenate([i2s[0], i2s[1]], axis=-1),
                            jnp.concatenate([i2s[2], i2s[3]], axis=-1)], axis=-2).astype(BF16)
    s2o = block_diag([t(ca_re), t(ca2_re), t(-ca_im), t(-ca2_im)])
    w_state = jnp.stack([jnp.concatenate([s2o[0], s2o[1]], axis=-1),
                         jnp.concatenate([s2o[2], s2o[3]], axis=-1)], axis=2).astype(BF16)
    i2o = block_diag([t(k0), t(k1)])
    w_dir = jnp.concatenate([jnp.concatenate([i2o[0], i2o[1]], axis=-1),
                             jnp.concatenate([jnp.zeros_like(i2o[0]), i2o[0]], axis=-1)], axis=-2).astype(BF16)
    return a2, w_in, w_state, w_dir


def _s5_scan_kernel(u_ref, a2_ref, win_ref, wst_ref, wdir_ref, y_ref, sbuf, state, *, nctx_chunks):
    d = pl.program_id(0)
    i = pl.program_id(1)
    p_steps, n_batch, _ = u_ref.shape
    npair = p_steps // 2
    rows = npair * n_batch
    half = S5_LANES
    nq = win_ref.shape[0]
    kq = win_ref.shape[1] // 2
    sq = win_ref.shape[2] // 2

    @pl.when(i == 0)
    def _():
        state[...] = jnp.zeros_like(state)

    fwd = d == 0
    u = u_ref[...].reshape(npair, 2, n_batch, D_MODEL)
    u_even = u[:, 0].reshape(rows, D_MODEL).astype(BF16)
    u_odd = u[:, 1].reshape(rows, D_MODEL).astype(BF16)
    u_1 = jnp.where(fwd, u_even, u_odd)
    u_2 = jnp.where(fwd, u_odd, u_even)
    pair_in = lambda q: jnp.concatenate([u_1[:, kq * q:kq * (q + 1)], u_2[:, kq * q:kq * (q + 1)]], axis=-1)
    for q in range(nq):
        r = _bdot(pair_in(q), win_ref[q])
        sbuf[:, :, sq * q:sq * (q + 1)] = r[:, :sq].reshape(npair, n_batch, sq)
        sbuf[:, :, half + sq * q:half + sq * (q + 1)] = r[:, sq:].reshape(npair, n_batch, sq)

    for q in range(nq):
        lo = sq * q
        ar = jnp.broadcast_to(a2_ref[0:1, lo:lo + sq], (n_batch, sq))
        ai = jnp.broadcast_to(a2_ref[1:2, lo:lo + sq], (n_batch, sq))

        def body(t, carry, lo=lo, ar=ar, ai=ai):
            sr, si = carry
            tt = jnp.where(d == 0, t, npair - 1 - t)
            nr = ar * sr - ai * si + sbuf[tt, :, lo:lo + sq]
            ni = ar * si + ai * sr + sbuf[tt, :, half + lo:half + lo + sq]
            sbuf[tt, :, lo:lo + sq] = sr
            sbuf[tt, :, half + lo:half + lo + sq] = si
            return nr, ni

        sr, si = lax.fori_loop(0, npair, body, (state[:, lo:lo + sq], state[:, half + lo:half + lo + sq]), unroll=4)
        state[:, lo:lo + sq] = sr
        state[:, half + lo:half + lo + sq] = si

    @pl.when(i >= nctx_chunks)
    def _():
        s = sbuf[...].reshape(rows, 2 * half).astype(BF16)
        for q in range(nq):
            yq = (_bdot(s[:, sq * q:sq * (q + 1)], wst_ref[q, 0])
                  + _bdot(s[:, half + sq * q:half + sq * (q + 1)], wst_ref[q, 1])
                  + _bdot(pair_in(q), wdir_ref[q]))
            y_1 = yq[:, :kq].reshape(npair, 1, n_batch, kq)
            y_2 = yq[:, kq:].reshape(npair, 1, n_batch, kq)
            y_pair = jnp.concatenate([jnp.where(fwd, y_1, y_2), jnp.where(fwd, y_2, y_1)], axis=1)
            y_ref[:, :, kq * q:kq * (q + 1)] = y_pair.reshape(p_steps, n_batch, kq)


def _s5_scan(u3, a2, w_in, w_state, w_dir, n_lat):
    lt, n_batch, _ = u3.shape
    p = S5_SCAN_CHUNK
    nchunk = lt // p
    nlatc = n_lat // p
    nctxc = nchunk - nlatc

    def u_map(d, i):
        return (jnp.where(d == 0, lax.rem(i + nlatc, nchunk), nchunk - 1 - i), 0, 0)

    def y_map(d, i):
        return (d, jnp.where(d == 0, jnp.maximum(i - nctxc, 0), jnp.minimum(nchunk - 1 - i, nlatc - 1)), 0, 0)

    per_dir = lambda w: pl.BlockSpec((None,) + w.shape[1:], lambda d, i: (d,) + (0,) * (w.ndim - 1))
    return pl.pallas_call(
        functools.partial(_s5_scan_kernel, nctx_chunks=nctxc),
        grid=(2, nchunk),
        in_specs=[pl.BlockSpec((p, n_batch, D_MODEL), u_map), per_dir(a2), per_dir(w_in), per_dir(w_state),
                  per_dir(w_dir)],
        out_specs=pl.BlockSpec((None, p, n_batch, D_MODEL), y_map),
        out_shape=jax.ShapeDtypeStruct((2, n_lat, n_batch, D_MODEL), F32),
        scratch_shapes=[pltpu.VMEM((p // 2, n_batch, 2 * S5_LANES), F32), pltpu.VMEM((n_batch, 2 * S5_LANES), F32)],
        compiler_params=_cparams(("arbitrary", "arbitrary")),
        name="s5_scan",
    )(u3, a2, w_in, w_state, w_dir)


def _glu_kernel(yf_ref, yb_ref, u_ref, dskip_ref, wa_ref, wb_ref, x_ref, mod_ref, gpost_ref, gpre_ref, router_ref,
                swap_ref, tri_ref, xo_ref, hf_ref, rw_ref, ri_ref, cnt_ref, carry):
    n_batch, steps, _ = x_ref.shape
    rows = n_batch * steps

    @pl.when(pl.program_id(0) == 0)
    def _():
        carry[...] = jnp.zeros_like(carry)

    y = (yf_ref[...] + yb_ref[...] + dskip_ref[...] * u_ref[...]).reshape(rows, D_MODEL)
    z = jax.nn.gelu(y).astype(BF16)
    z = _bdot(swap_ref[...], z).astype(BF16)
    out = _bdot(z, wa_ref[...]) * jax.nn.sigmoid(_bdot(z, wb_ref[...]))
    out = out.reshape(n_batch, steps, D_MODEL)
    mod = lambda k: mod_ref[0:n_batch, k:k + 1, :]
    xn = x_ref[...] + mod(2) * _rms(out, gpost_ref[...])
    xo_ref[...] = xn
    hf = _rms(xn, gpre_ref[...]) * (1.0 + mod(4)) + mod(3)
    hf_ref[...] = hf
    hf = hf.reshape(rows, D_MODEL)
    h_hi = hf.astype(BF16)
    h_lo = (hf - h_hi.astype(F32)).astype(BF16)
    part = _bdot(h_hi, router_ref[...])
    logits = part[:, :LANE] + part[:, LANE:] + _bdot(h_lo, router_ref[:, :LANE])
    lane = lax.broadcasted_iota(jnp.int32, logits.shape, 1)
    neg = jnp.float32(-jnp.inf)
    lg = jnp.where(lane < N_EXPERTS, logits, neg)
    m1 = jnp.max(lg, axis=-1, keepdims=True)
    i1 = jnp.min(jnp.where(lg == m1, lane, LANE), axis=-1, keepdims=True)
    lg2 = jnp.where(lane == i1, neg, lg)
    m2 = jnp.max(lg2, axis=-1, keepdims=True)
    i2 = jnp.min(jnp.where(lg2 == m2, lane, LANE), axis=-1, keepdims=True)
    e2 = jnp.exp(m2 - m1)
    w1 = 1.0 / (1.0 + e2)
    rw_ref[...] = jnp.where(lane == 0, w1, jnp.where(lane == 1, e2 * w1, 0.0)).reshape(n_batch, steps, LANE)
    member = jnp.where(lane == i1, 1.0, jnp.where(lane == i2, 1.0, 0.0))
    base = carry[...] + _bdot(tri_ref[...], member.astype(BF16))
    r1 = jnp.sum(jnp.where(lane == i1, base, 0.0), axis=-1, keepdims=True).astype(jnp.int32)
    r2 = jnp.sum(jnp.where(lane == i2, base, 0.0), axis=-1, keepdims=True).astype(jnp.int32)
    ri = jnp.where(lane == 0, i1, jnp.where(lane == 1, i2, jnp.where(lane == 2, r1, jnp.where(lane == 3, r2, 0))))
    ri_ref[...] = ri.reshape(n_batch, steps, LANE)
    carry[...] += jnp.sum(member, axis=0, keepdims=True)
    cnt_ref[...] = carry[...]


def _glu(y, u3, d_skip, w_a, w_b, xs, mod, gpost, gpre, router, n_lat):
    n_batch = xs.shape[0]
    p = S5_CHUNK
    rows = p * n_batch
    router_p = jnp.pad(router, ((0, 0), (0, LANE - router.shape[1])))
    router_hi = router_p.astype(BF16)
    router_cat = jnp.concatenate([router_hi, (router_p - router_hi.astype(F32)).astype(BF16)], axis=1)
    tri = jnp.asarray(np.arange(rows)[:, None] > np.arange(rows)[None, :], BF16)
    swap = _row_order_swap(p, n_batch)
    bt_spec = lambda w: pl.BlockSpec((n_batch, p, w), lambda i: (0, i, 0))
    return pl.pallas_call(
        _glu_kernel,
        grid=(n_lat // p,),
        in_specs=[pl.BlockSpec((None, p, n_batch, D_MODEL), lambda i: (0, i, 0, 0)),
                  pl.BlockSpec((None, p, n_batch, D_MODEL), lambda i: (1, i, 0, 0)),
                  pl.BlockSpec((p, n_batch, D_MODEL), lambda i: (i, 0, 0)),
                  _full((1, D_MODEL)), _full(w_a.shape), _full(w_b.shape), bt_spec(D_MODEL),
                  _full(mod.shape), _full((1, D_MODEL)), _full((1, D_MODEL)),
                  _full(router_cat.shape), _full(swap.shape), _full(tri.shape)],
        out_specs=[bt_spec(D_MODEL), bt_spec(D_MODEL), bt_spec(LANE), bt_spec(LANE), _full((1, LANE))],
        out_shape=[jax.ShapeDtypeStruct((n_batch, n_lat, D_MODEL), F32),
                   jax.ShapeDtypeStruct((n_batch, n_lat, D_MODEL), F32),
                   jax.ShapeDtypeStruct((n_batch, n_lat, LANE), F32),
                   jax.ShapeDtypeStruct((n_batch, n_lat, LANE), jnp.int32),
                   jax.ShapeDtypeStruct((1, LANE), F32)],
        scratch_shapes=[pltpu.VMEM((1, LANE), F32)],
        compiler_params=_cparams(("arbitrary",)),
        name="s5_glu_router",
    )(y, y, u3, d_skip, w_a, w_b, xs, mod, gpost, gpre, router_cat, swap, tri)


def _moe_plan(ri, counts, n_tiles):
    experts = jnp.arange(N_EXPERTS, dtype=jnp.int32)
    n_of = (counts[0, :N_EXPERTS].astype(jnp.int32) + MOE_TILE - 1) // MOE_TILE
    ends = jnp.cumsum(n_of)
    starts = ends - n_of
    start_of = jnp.sum(jnp.where(ri[:, 0:2, None] == experts, starts, 0), axis=-1)
    n_tok = ri.shape[0]
    pos = (start_of * MOE_TILE + ri[:, 2:4]).T.reshape(-1)
    n_used = ends[-1]
    tile = jnp.arange(n_tiles, dtype=jnp.int32)
    tile_expert = jnp.sum((jnp.minimum(tile, n_used - 1)[:, None] >= ends[None, :]).astype(jnp.int32), axis=1)
    n_pairs = pos.shape[0]
    n_rows = n_tiles * MOE_TILE
    pair_of = jnp.full((n_rows,), -1, jnp.int32).at[pos].set(jnp.arange(n_pairs, dtype=jnp.int32),
                                                             unique_indices=True, mode='promise_in_bounds')
    is_pad = pair_of < 0
    pair_of = jnp.where(is_pad, n_pairs - 1 + jnp.cumsum(is_pad.astype(jnp.int32)), pair_of)
    src_token = jnp.where(is_pad, 0, pair_of % n_tok)
    return src_token, pair_of, tile_expert, n_used.reshape(1)


def _moe_expert_kernel(te_ref, nu_ref, src0_ref, src_ref, dst_ref, hf_ref, wg_ref, wu_ref, wd_ref, out_ref,
                       xbuf, obuf, acc, gsem, ssem, *, rows_per_step):
    del te_ref
    i = pl.program_id(0)
    j = pl.program_id(1)
    last_j = pl.num_programs(1) - 1
    nu = nu_ref[0]
    cur = lax.rem(i, 2)
    nxt = 1 - cur
    row0 = j * rows_per_step

    def gather_row(idx_ref, row, slot):
        return pltpu.make_async_copy(hf_ref.at[pl.ds(idx_ref[0, row], 1)], xbuf.at[slot, pl.ds(row, 1)], gsem.at[slot])

    def scatter_row(row, slot):
        return pltpu.make_async_copy(obuf.at[slot, pl.ds(row, 1)], out_ref.at[pl.ds(dst_ref[0, row], 1)], ssem.at[slot])

    def whole_tile_wait(sem, slot):
        pltpu.make_async_copy(xbuf.at[slot], obuf.at[slot], sem.at[slot]).wait()

    @pl.when(jnp.logical_and(i == 0, j == 0))
    def _():
        obuf[...] = jnp.zeros_like(obuf)

        def first(r, c):
            gather_row(src0_ref, r, 0).start()
            return c

        lax.fori_loop(0, MOE_TILE, first, 0)

    @pl.when(jnp.logical_and(i <= nu, j == 0))
    def _():
        whole_tile_wait(gsem, cur)

    for step in range(MOE_TILE // rows_per_step):
        @pl.when(jnp.logical_and(i < nu, j == step))
        def _(step=step):
            for row in range(step * rows_per_step, (step + 1) * rows_per_step):
                gather_row(src_ref, row, nxt).start()
                scatter_row(row, nxt).start(priority=row % 2)

    @pl.when(i < nu)
    def _():
        h = xbuf[cur].astype(BF16)
        g = _bdot(h, wg_ref[...])
        u = _bdot(h, wu_ref[...])
        part = _bdot((g * jax.nn.sigmoid(g) * u).astype(BF16), wd_ref[...])
        acc[...] = jnp.where(j == 0, part, acc[...] + part)

    @pl.when(i >= nu)
    def _():
        def tail(c, carry):
            scatter_row(row0 + c, nxt).start()
            return carry

        lax.fori_loop(0, rows_per_step, tail, 0)

    @pl.when(jnp.logical_and(i >= 1, j == last_j))
    def _():
        whole_tile_wait(ssem, cur)

    @pl.when(jnp.logical_and(i < nu, j == last_j))
    def _():
        obuf[cur] = acc[...]

    @pl.when(jnp.logical_and(i == pl.num_programs(0) - 1, j == last_j))
    def _():
        whole_tile_wait(ssem, nxt)


def _combine_kernel(ya_ref, yb_ref, rw_ref, x_ref, mod_ref, gpost_ref, o_ref):
    rw = rw_ref[...]
    y = rw[:, 0:1] * ya_ref[...] + rw[:, 1:2] * yb_ref[...]
    o_ref[...] = x_ref[...] + mod_ref[5:6, :] * _rms(y, gpost_ref[...])


def _moe(hf, rw, ri, counts, xs, mod, gpost, wg, wu, wd, nlat):
    t_rows = hf.shape[0]
    d_ff = wg.shape[2]
    n_tiles = 2 * t_rows // MOE_TILE + N_EXPERTS
    nff = d_ff // MOE_FF_TILE
    n_row_tiles = t_rows // ROW_TILE
    n_rows = n_tiles * MOE_TILE
    src_token, pair_of, tile_expert, n_used = _moe_plan(ri, counts, n_tiles)
    spare = jnp.arange(MOE_TILE, dtype=jnp.int32)
    src_tiles = jnp.concatenate([src_token, 0 * spare]).reshape(n_tiles + 1, 1, MOE_TILE)
    dst_tiles = jnp.concatenate([n_rows + spare, pair_of]).reshape(n_tiles + 1, 1, MOE_TILE)
    step_expert = jnp.concatenate([tile_expert, tile_expert[-1:]])
    any_spec = pl.BlockSpec(memory_space=pl.ANY)
    idx_block = (None, 1, MOE_TILE)

    def ff_blk(i, j, nu):
        return jnp.where(i < nu[0], j, nff - 1)

    y_pairs = pl.pallas_call(
        functools.partial(_moe_expert_kernel, rows_per_step=MOE_TILE // nff),
        grid_spec=pltpu.PrefetchScalarGridSpec(
            num_scalar_prefetch=2,
            grid=(n_tiles + 1, nff),
            in_specs=[pl.BlockSpec(idx_block, lambda i, j, te, nu: (0, 0, 0), memory_space=pltpu.SMEM),
                      pl.BlockSpec(idx_block, lambda i, j, te, nu: (jnp.minimum(i + 1, n_tiles), 0, 0),
                                   memory_space=pltpu.SMEM),
                      pl.BlockSpec(idx_block, lambda i, j, te, nu: (i, 0, 0), memory_space=pltpu.SMEM),
                      any_spec,
                      pl.BlockSpec((None, D_MODEL, MOE_FF_TILE), lambda i, j, te, nu: (te[i], 0, ff_blk(i, j, nu))),
                      pl.BlockSpec((None, D_MODEL, MOE_FF_TILE), lambda i, j, te, nu: (te[i], 0, ff_blk(i, j, nu))),
                      pl.BlockSpec((None, MOE_FF_TILE, D_MODEL), lambda i, j, te, nu: (te[i], ff_blk(i, j, nu), 0))],
            out_specs=any_spec,
            scratch_shapes=[pltpu.VMEM((2, MOE_TILE, D_MODEL), F32), pltpu.VMEM((2, MOE_TILE, D_MODEL), F32),
                            pltpu.VMEM((MOE_TILE, D_MODEL), F32),
                            pltpu.SemaphoreType.DMA((2,)), pltpu.SemaphoreType.DMA((2,))]),
        out_shape=jax.ShapeDtypeStruct((n_rows + MOE_TILE, D_MODEL), F32),
        compiler_params=_cparams(("arbitrary", "arbitrary")),
        name="moe_experts",
    )(step_expert, n_used, src_tiles, src_tiles, dst_tiles, hf, wg, wu, wd)

    rows = min(COMBINE_TILE, nlat * ROW_TILE)
    n_steps = t_rows // rows
    per_batch = nlat * ROW_TILE // rows
    row = lambda w: pl.BlockSpec((rows, w), lambda i: (i, 0))
    return pl.pallas_call(
        _combine_kernel,
        grid=(n_steps,),
        in_specs=[row(D_MODEL), pl.BlockSpec((rows, D_MODEL), lambda i: (n_steps + i, 0)), row(LANE), row(D_MODEL),
                  pl.BlockSpec((None, 6, D_MODEL), lambda i: (lax.div(i, per_batch), 0, 0)), _full((1, D_MODEL))],
        out_specs=row(D_MODEL),
        out_shape=jax.ShapeDtypeStruct((t_rows, D_MODEL), F32),
        compiler_params=_cparams(("parallel",)),
        name="moe_combine",
    )(y_pairs, y_pairs, rw, xs, mod, gpost)


def _rope_tables(n_lat, n_ctx):
    rows = n_lat // GRID_W
    row = jnp.repeat(jnp.arange(rows, dtype=F32), GRID_W)
    col = jnp.tile(jnp.arange(GRID_W, dtype=F32), rows)
    n_freq = HEAD_DIM // 4
    inv = ROPE_THETA ** (-jnp.arange(n_freq, dtype=F32) / n_freq)
    ang = jnp.concatenate([row[:, None] * inv, col[:, None] * inv], axis=-1)
    cos, sin = jnp.cos(ang), jnp.sin(ang)
    cos_h = jnp.concatenate([cos, cos], axis=-1)
    sin_h = jnp.concatenate([-sin, sin], axis=-1)
    cos_t = jnp.concatenate([jnp.tile(cos_h, (1, N_Q_HEADS)), jnp.ones((n_ctx, ATTN_WIDTH), F32)], axis=0)
    sin_t = jnp.concatenate([jnp.tile(sin_h, (1, N_Q_HEADS)), jnp.zeros((n_ctx, ATTN_WIDTH), F32)], axis=0)
    return cos_t, sin_t


def kernel(x, c, ctx, c_ctx, ada_w, ada_b, norm_mix_pre, norm_mix_post, norm_ffn_pre, norm_ffn_post, ev_w_in, ev_hy_conv_w, ev_hy_conv_b, ev_hy_f_w1, ev_hy_f_b1, ev_hy_f_w2, ev_hy_f_b2, ev_hy_f_wout, ev_hy_freq, ev_hy_skip, ev_q_norm, ev_k_norm, ev_w_out, ev_ffn_w_gate, ev_ffn_w_up, ev_ffn_w_down, od_w_in, od_s5_lambda_re, od_s5_lambda_im, od_s5_log_step, od_s5_b_re, od_s5_b_im, od_s5_c_re, od_s5_c_im, od_s5_d, od_glu_w_a, od_glu_w_b, od_router, od_moe_w_gate, od_moe_w_up, od_moe_w_down):
    n_batch, n_lat, _ = x.shape
    n_ctx = ctx.shape[1]
    depth = ada_w.shape[0]
    assert n_batch == 8 and n_lat % ROW_TILE == 0 and n_ctx % ROW_TILE == 0 and n_lat % n_ctx == 0
    assert depth == 2
    nlat = n_lat // ROW_TILE

    cond = jnp.concatenate([c, c_ctx[None, :], jnp.zeros((16 - n_batch - 1, D_MODEL), F32)], axis=0)
    mods = _ada_params(cond, ada_w, ada_b)
    vec = lambda a: a[None, :]

    cos_t, sin_t = _rope_tables(n_lat, n_ctx)
    hy, q, k4, v4 = _inproj_even(x, ctx, mods[0], vec(norm_mix_pre[0]), ev_w_in[0].astype(BF16),
                                 vec(jnp.tile(ev_q_norm[0], N_Q_HEADS)), vec(jnp.tile(ev_k_norm[0], N_KV_HEADS)),
                                 cos_t, sin_t, nlat)
    lt = n_lat + n_ctx
    a_lat = _attention(q, k4, v4, n_lat, min(ATTN_Q_TILE, n_lat), 0, lt, 0)
    a_ctx = _attention(q, k4, v4, n_ctx, n_ctx, n_lat // n_ctx, n_ctx, n_lat // n_ctx)
    filt_args = (ev_hy_f_w1[0], ev_hy_f_b1[0], ev_hy_f_w2[0], ev_hy_f_b2[0], ev_hy_f_wout[0], ev_hy_freq[0])
    y_hy_lat = _hyena_mixer(hy, 0, n_lat, filt_args, ev_hy_conv_w[0], ev_hy_conv_b[0], ev_hy_skip[0])
    y_hy_ctx = _hyena_mixer(hy, n_lat // n_ctx, n_ctx, filt_args, ev_hy_conv_w[0], ev_hy_conv_b[0], ev_hy_skip[0])
    xs = _outproj(y_hy_lat, y_hy_ctx, a_lat, a_ctx, x, ctx, mods[0], vec(norm_mix_post[0]), ev_w_out[0].astype(BF16), nlat)
    xs = _ffn_dense(xs, mods[0], vec(norm_ffn_pre[0]), vec(norm_ffn_post[0]), ev_ffn_w_gate[0].astype(BF16),
                    ev_ffn_w_up[0].astype(BF16), ev_ffn_w_down[0].astype(BF16), nlat)

    u3 = _inproj_odd(xs, mods[1], vec(norm_mix_pre[1]), od_w_in[0].astype(BF16), n_lat)
    s5_ops = _s5_params(od_s5_lambda_re[0], od_s5_lambda_im[0], od_s5_log_step[0],
                        od_s5_b_re[0], od_s5_b_im[0], od_s5_c_re[0], od_s5_c_im[0])
    y = _s5_scan(u3, *s5_ops, n_lat)
    x_lat, hf, rw, ri, counts = _glu(y, u3, vec(od_s5_d[0]), od_glu_w_a[0].astype(BF16), od_glu_w_b[0].astype(BF16),
                                     xs, mods[1], vec(norm_mix_post[1]), vec(norm_ffn_pre[1]), od_router[0], n_lat)
    t_rows = n_batch * n_lat
    out = _moe(hf.reshape(t_rows, D_MODEL), rw.reshape(t_rows, LANE), ri.reshape(t_rows, LANE), counts,
               x_lat.reshape(t_rows, D_MODEL), mods[1], vec(norm_ffn_post[1]), od_moe_w_gate[0].astype(BF16),
               od_moe_w_up[0].astype(BF16), od_moe_w_down[0].astype(BF16), nlat)
    return out.reshape(n_batch, n_lat, D_MODEL)
```
